```python
import jax, jax.numpy as jnp
from jax import lax
import numpy as np

D_MODEL = 2048
BATCH = 4
SEQ = 2048
DEPTH = 1
DEC_BATCH = 128
DEC_SEQ = 8
PAST_LEN = 16384
PAGE_SIZE = 128

HEAD_SIZE = 64
RWKV_WIDTH = D_MODEL // 2
RWKV_HEADS = RWKV_WIDTH // HEAD_SIZE
DECAY_LORA = D_MODEL // 32
AAA_LORA = D_MODEL // 32
GATE_LORA = D_MODEL // 16
CHUNK = 128
MLP_WIDTH = D_MODEL // 2
MLP_GROUPS = 8
MLP_GROUP_DIM = MLP_WIDTH // MLP_GROUPS
N_BRANCH = 2
D_FF = ((-(-8 * D_MODEL // 3)) + 255) // 256 * 256
C_SHIFT = 3 * RWKV_WIDTH + DECAY_LORA + AAA_LORA + GATE_LORA
C_IN = C_SHIFT + 2 * MLP_WIDTH + N_BRANCH * D_MODEL
NORM_EPS = 1e-6
GN_EPS = 64e-5
LN_EPS = 1e-5

kernel_name = "rwkv7_chunk_gmlp_gated_hybrid_step"


def rms_norm(x, g):
    xf = x.astype(jnp.float32)
    return xf * lax.rsqrt(jnp.mean(xf * xf, -1, keepdims=True) + NORM_EPS) * g.astype(jnp.float32)


def layer_norm(x, g, b):
    mu = jnp.mean(x, -1, keepdims=True)
    var = jnp.mean(jnp.square(x - mu), -1, keepdims=True)
    return (x - mu) * lax.rsqrt(var + LN_EPS) * g.astype(jnp.float32) + b.astype(jnp.float32)


def wkv7_scan(state, r, decay, k, v, kk, kka):
    def step(S, inp):
        r_t, w_t, k_t, v_t, kk_t, kka_t = inp
        sa = jnp.einsum('bhvk,bhk->bhv', S, -kk_t)
        S = S * w_t[:, :, None, :] + sa[..., None] * kka_t[:, :, None, :] + v_t[..., None] * k_t[:, :, None, :]
        o = jnp.einsum('bhvk,bhk->bhv', S, r_t)
        return S, o
    xs = tuple(jnp.swapaxes(a, 0, 1) for a in (r, decay, k, v, kk, kka))
    S, o = lax.scan(step, state, xs)
    return S, jnp.swapaxes(o, 0, 1)


def hybrid_layer(x, c, shift_state, wkv_state,
                 w_ada, b_ada, norm_mix_g, w_in, mu_shift, w0, w_decay_up, a0, w_aaa_up,
                 w_gate_up, k_k, k_a, r_k, gn_g, gn_b, ln_v_g, ln_v_b, w_spatial, b_spatial,
                 w_branch_a, w_branch_b, w_out, norm_ffn_g, w_ffn_in, w_ffn_out):
    f32 = jnp.float32
    dt = x.dtype
    B, T, _ = x.shape
    H, N = RWKV_HEADS, HEAD_SIZE
    mod = (jax.nn.silu(c.astype(f32)) @ w_ada.astype(f32) + b_ada.astype(f32)).reshape(B, 6, D_MODEL)
    shift_m, scale_m, gate_m = mod[:, 0], mod[:, 1], mod[:, 2]
    shift_f, scale_f, gate_f = mod[:, 3], mod[:, 4], mod[:, 5]

    h = (rms_norm(x, norm_mix_g) * (1.0 + scale_m[:, None]) + shift_m[:, None]).astype(dt)
    z = h @ w_in
    z_rw, z_u, z_v, z_gate = jnp.split(z, [C_SHIFT, C_SHIFT + MLP_WIDTH, C_SHIFT + 2 * MLP_WIDTH], axis=-1)

    prev = jnp.concatenate([shift_state[:, None].astype(dt), z_rw[:, :-1]], axis=1)
    zs = (z_rw + (prev - z_rw) * mu_shift.astype(dt)).astype(f32)
    new_shift = z_rw[:, -1]
    r, k, v, wd, ad, gd = jnp.split(zs, [RWKV_WIDTH, 2 * RWKV_WIDTH, 3 * RWKV_WIDTH,
                                         3 * RWKV_WIDTH + DECAY_LORA,
                                         3 * RWKV_WIDTH + DECAY_LORA + AAA_LORA], axis=-1)
    w_raw = w0.astype(f32) + jnp.tanh(wd) @ w_decay_up.astype(f32)
    decay = jnp.exp(-jnp.exp(-jax.nn.softplus(-w_raw) - 0.5))
    a = jax.nn.sigmoid(a0.astype(f32) + ad @ w_aaa_up.astype(f32))
    g = jax.nn.sigmoid(gd) @ w_gate_up.astype(f32)
    kk = (k * k_k.astype(f32)).reshape(B, T, H, N)
    kk = kk / jnp.maximum(jnp.sqrt(jnp.sum(kk * kk, -1, keepdims=True)), 1e-12)
    k = k * (1.0 + (a - 1.0) * k_a.astype(f32))
    r4, k4, v4 = r.reshape(B, T, H, N), k.reshape(B, T, H, N), v.reshape(B, T, H, N)
    a4, w4 = a.reshape(B, T, H, N), decay.reshape(B, T, H, N)
    S, o = wkv7_scan(wkv_state.astype(f32), r4, w4, k4, v4, kk, kk * a4)
    mu = jnp.mean(o, -1, keepdims=True)
    var = jnp.mean(jnp.square(o - mu), -1, keepdims=True)
    o = ((o - mu) * lax.rsqrt(var + GN_EPS)).reshape(B, T, RWKV_WIDTH) * gn_g.astype(f32) + gn_b.astype(f32)
    bonus = jnp.sum(r4 * k4 * r_k.astype(f32), -1, keepdims=True) * v4
    o_a = (o + bonus.reshape(B, T, RWKV_WIDTH)) * g
    y_a = o_a.astype(dt) @ w_branch_a

    u = jax.nn.gelu(z_u.astype(f32))
    vn = layer_norm(jax.nn.gelu(z_v.astype(f32)), ln_v_g, ln_v_b)
    n_chunks = -(-T // CHUNK)
    pad = n_chunks * CHUNK - T
    vp = jnp.pad(vn, ((0, 0), (0, pad), (0, 0))).reshape(B, n_chunks, CHUNK, MLP_GROUPS, MLP_GROUP_DIM)
    causal = jnp.tril(jnp.ones((CHUNK, CHUNK), dtype=bool))
    ws = jnp.where(causal[None], w_spatial.astype(f32), 0.0)
    mixed = jnp.einsum('gts,bcsgd->bctgd', ws, vp) + jnp.transpose(b_spatial.astype(f32))[:, :, None]
    mixed = mixed.reshape(B, n_chunks * CHUNK, MLP_WIDTH)[:, :T]
    y_b = (u * mixed).astype(dt) @ w_branch_b

    gates = jax.nn.sigmoid(z_gate.astype(f32)).reshape(B, T, N_BRANCH, D_MODEL)
    merged = gates[:, :, 0] * y_a.astype(f32) + gates[:, :, 1] * y_b.astype(f32)
    mix_out = merged.astype(dt) @ w_out
    x = x + (gate_m[:, None] * mix_out.astype(f32)).astype(dt)

    h2 = (rms_norm(x, norm_ffn_g) * (1.0 + scale_f[:, None]) + shift_f[:, None]).astype(dt)
    gt, up = jnp.split(h2 @ w_ffn_in, [D_FF], axis=-1)
    f = (jax.nn.silu(gt) * up) @ w_ffn_out
    x = x + (gate_f[:, None] * f.astype(f32)).astype(dt)
    return x, S.astype(wkv_state.dtype), new_shift, vn.astype(dt)


def setup_inputs(seed: int = 0) -> dict:
    key = jax.random.key(seed)
    ks = iter(jax.random.split(key, 48))
    nrm = lambda shape, s: jax.random.normal(next(ks), shape, jnp.float32) * s
    L, D = DEPTH, D_MODEL
    return {
        "x_prompt": nrm((BATCH, SEQ, D), 1.0),
        "x_sample": nrm((DEC_BATCH, DEC_SEQ, D), 1.0),
        "state_wkv": nrm((L, DEC_BATCH, RWKV_HEADS, HEAD_SIZE, HEAD_SIZE), 0.5),
        "state_shift": nrm((L, DEC_BATCH, C_SHIFT), 1.0),
        "c_prompt": nrm((BATCH, D), 1.0),
        "c_sample": nrm((DEC_BATCH, D), 1.0),
        "w_ada": nrm((L, D, 6 * D), 0.5 * D ** -0.5),
        "b_ada": nrm((L, 6 * D), 0.02),
        "norm_mix_g": 1.0 + nrm((L, D), 0.02),
        "w_in": nrm((L, D, C_IN), D ** -0.5),
        "mu_shift": jax.random.uniform(next(ks), (L, C_SHIFT), jnp.float32),
        "w0": jax.random.uniform(next(ks), (L, RWKV_WIDTH), jnp.float32, -6.5, -1.5),
        "w_decay_up": nrm((L, DECAY_LORA, RWKV_WIDTH), 0.1 * DECAY_LORA ** -0.5),
        "a0": nrm((L, RWKV_WIDTH), 0.1),
        "w_aaa_up": nrm((L, AAA_LORA, RWKV_WIDTH), 0.1 * AAA_LORA ** -0.5),
        "w_gate_up": nrm((L, GATE_LORA, RWKV_WIDTH), GATE_LORA ** -0.5),
        "k_k": 0.85 + nrm((L, RWKV_WIDTH), 0.05),
        "k_a": 1.0 + nrm((L, RWKV_WIDTH), 0.05),
        "r_k": nrm((L, RWKV_HEADS, HEAD_SIZE), 0.1),
        "gn_g": 1.0 + nrm((L, RWKV_WIDTH), 0.02),
        "gn_b": nrm((L, RWKV_WIDTH), 0.02),
        "ln_v_g": 1.0 + nrm((L, MLP_WIDTH), 0.02),
        "ln_v_b": nrm((L, MLP_WIDTH), 0.02),
        "w_spatial": nrm((L, MLP_GROUPS, CHUNK, CHUNK), 0.5 * CHUNK ** -0.5),
        "b_spatial": 1.0 + nrm((L, MLP_GROUPS, CHUNK), 0.02),
        "w_branch_a": nrm((L, RWKV_WIDTH, D), RWKV_WIDTH ** -0.5),
        "w_branch_b": nrm((L, MLP_WIDTH, D), MLP_WIDTH ** -0.5),
        "w_out": nrm((L, D, D), D ** -0.5),
        "norm_ffn_g": 1.0 + nrm((L, D), 0.02),
        "w_ffn_in": nrm((L, D, 2 * D_FF), D ** -0.5),
        "w_ffn_out": nrm((L, D_FF, D), D_FF ** -0.5),
        "norm_final_g": 1.0 + nrm((D,), 0.02),
    }


def reference(x_prompt, x_sample, state_wkv, state_shift, c_prompt, c_sample,
              w_ada, b_ada, norm_mix_g, w_in, mu_shift, w0, w_decay_up, a0, w_aaa_up,
              w_gate_up, k_k, k_a, r_k, gn_g, gn_b, ln_v_g, ln_v_b, w_spatial, b_spatial,
              w_branch_a, w_branch_b, w_out, norm_ffn_g, w_ffn_in, w_ffn_out, norm_final_g):
    xp, xs = x_prompt, x_sample
    Bp = x_prompt.shape[0]
    wkv_p, shift_p, wkv_s, shift_s, v_s = [], [], [], [], []
    for l in range(DEPTH):
        lw = (w_ada[l], b_ada[l], norm_mix_g[l], w_in[l], mu_shift[l], w0[l], w_decay_up[l], a0[l],
              w_aaa_up[l], w_gate_up[l], k_k[l], k_a[l], r_k[l], gn_g[l], gn_b[l], ln_v_g[l], ln_v_b[l],
              w_spatial[l], b_spatial[l], w_branch_a[l], w_branch_b[l], w_out[l], norm_ffn_g[l],
              w_ffn_in[l], w_ffn_out[l])
        zero_shift = jnp.zeros((Bp, C_SHIFT), x_prompt.dtype)
        zero_wkv = jnp.zeros((Bp, RWKV_HEADS, HEAD_SIZE, HEAD_SIZE), state_wkv.dtype)
        xp, Sp, shp, _ = hybrid_layer(xp, c_prompt, zero_shift, zero_wkv, *lw)
        xs, Ss, shs, vs = hybrid_layer(xs, c_sample, state_shift[l], state_wkv[l], *lw)
        wkv_p.append(Sp); shift_p.append(shp)
        wkv_s.append(Ss); shift_s.append(shs); v_s.append(vs)
    y_prompt = rms_norm(xp, norm_final_g).astype(x_prompt.dtype)
    y_sample = rms_norm(xs, norm_final_g).astype(x_sample.dtype)
    return (y_prompt, y_sample, jnp.stack(wkv_p), jnp.stack(shift_p),
            jnp.stack(wkv_s), jnp.stack(shift_s), jnp.stack(v_s))
```

```python
import functools
import math

import jax
import jax.numpy as jnp
from jax import lax
from jax.experimental import pallas as pl
from jax.experimental.pallas import tpu as pltpu

F32 = jnp.float32
BF16 = jnp.bfloat16

D_MODEL = 2048
HEAD = 64
RW = D_MODEL // 2
N_HEADS = RW // HEAD
DECAY_LORA = D_MODEL // 32
AAA_LORA = D_MODEL // 32
GATE_LORA = D_MODEL // 16
LORA = DECAY_LORA + AAA_LORA + GATE_LORA
CHUNK = 128
MLP_W = D_MODEL // 2
MLP_GROUPS = 8
MLP_GD = MLP_W // MLP_GROUPS
D_FF = ((-(-8 * D_MODEL // 3)) + 255) // 256 * 256
C_SHIFT = 3 * RW + LORA
C_IN = C_SHIFT + 2 * MLP_W + 2 * D_MODEL
NORM_EPS = 1e-6
GN_EPS = 64e-5
LN_EPS = 1e-5

Z_GA, Z_GB, Z_U, Z_V, Z_R, Z_K, Z_VR, Z_L = 0, 2048, 4096, 5120, 6144, 7168, 8192, 9216
Z_TN = 512
Z_W = -(-C_IN // Z_TN) * Z_TN

GRP = 256
HPG = GRP // HEAD
N_GRP = RW // GRP
WKV_ROWS = 64
VMEM_LIMIT = 56 * 1024 * 1024


def _dot(a, b):
    return jnp.dot(a, b, preferred_element_type=F32)


def _dot_nt(a, b):
    return lax.dot_general(a, b, (((1,), (1,)), ((), ())), preferred_element_type=F32)


def _dot_tn(a, b):
    return lax.dot_general(a, b, (((0,), (0,)), ((), ())), preferred_element_type=F32)


def _rms(x, g):
    return x * lax.rsqrt(jnp.mean(x * x, -1, keepdims=True) + NORM_EPS) * g


def _split2(x):
    hi = x.astype(BF16)
    lo = (x - hi.astype(F32)).astype(BF16)
    return hi, lo


def _split3(x):
    hi = x.astype(BF16)
    r1 = x - hi.astype(F32)
    mid = r1.astype(BF16)
    lo = (r1 - mid.astype(F32)).astype(BF16)
    return hi, mid, lo


def _params(sem):
    return pltpu.CompilerParams(dimension_semantics=sem, vmem_limit_bytes=VMEM_LIMIT)


def _ada_kernel(c_ref, w_ref, b_ref, o_ref):
    c = c_ref[...]
    s = c * jax.nn.sigmoid(c)
    o_ref[...] = _dot(s.astype(BF16), w_ref[...].astype(BF16)) + b_ref[...]


def _ada(c, w_ada, b_ada):
    m, n, tn = c.shape[0], w_ada.shape[1], 1024
    return pl.pallas_call(
        _ada_kernel,
        grid=(n // tn,),
        in_specs=[pl.BlockSpec((m, D_MODEL), lambda j: (0, 0)),
                  pl.BlockSpec((D_MODEL, tn), lambda j: (0, j)),
                  pl.BlockSpec((1, tn), lambda j: (0, j))],
        out_specs=pl.BlockSpec((m, tn), lambda j: (0, j)),
        out_shape=jax.ShapeDtypeStruct((m, n), F32),
        compiler_params=_params(("arbitrary",)),
        name="ada",
    )(c, w_ada, b_ada.reshape(1, n))


def _mod_spec(per_row, tm, rows_per_seq):
    if per_row:
        return pl.BlockSpec((tm, D_MODEL), lambda i, *_: (i, 0))
    return pl.BlockSpec((None, 1, D_MODEL), lambda i, *_: (i * tm // rows_per_seq, 0, 0))


def _inproj_kernel(x_ref, sc_ref, sh_ref, g_ref, w_ref, o_ref, h_ref):
    @pl.when(pl.program_id(1) == 0)
    def _():
        h = _rms(x_ref[...], g_ref[...]) * (1.0 + sc_ref[...]) + sh_ref[...]
        h_ref[...] = h.astype(BF16)

    o_ref[...] = _dot(h_ref[...], w_ref[...])


def _inproj(x, scale, shift, g, w, per_row, rows_per_seq):
    m, tm = x.shape[0], 1024
    mspec = _mod_spec(per_row, tm, rows_per_seq)
    return pl.pallas_call(
        _inproj_kernel,
        grid=(m // tm, Z_W // Z_TN),
        in_specs=[pl.BlockSpec((tm, D_MODEL), lambda i, j: (i, 0)),
                  mspec, mspec,
                  pl.BlockSpec((1, D_MODEL), lambda i, j: (0, 0)),
                  pl.BlockSpec((D_MODEL, Z_TN), lambda i, j: (0, j))],
        out_specs=pl.BlockSpec((tm, Z_TN), lambda i, j: (i, j)),
        out_shape=jax.ShapeDtypeStruct((m, Z_W), F32),
        scratch_shapes=[pltpu.VMEM((tm, D_MODEL), BF16)],
        compiler_params=_params(("arbitrary", "arbitrary")),
        name="inproj",
    )(x, scale, shift, g, w)


def _bd(w, mask):
    return jnp.where(mask, jnp.concatenate([w] * HPG, axis=0), 0.0).astype(BF16)


def _fold(m):
    out = m[0:HEAD]
    for h in range(1, HPG):
        out = out + m[h * HEAD:(h + 1) * HEAD]
    return out


def _wkv_kernel(*refs, lb, carried):
    R = WKV_ROWS
    if carried:
        (zr_ref, zk_ref, zv_ref, zl_ref,
         mur_ref, muk_ref, muv_ref, mul_ref,
         w0_ref, a0_ref, kkp_ref, kap_ref, rkp_ref, gng_ref, gnb_ref,
         wd_ref, wa_ref, wg_ref,
         oa_ref, so_ref,
         s_ref, cr_ref, ck_ref, cv_ref, cl_ref) = refs
    else:
        (zr_ref, zk_ref, zv_ref, zl_ref,
         pr_ref, pk_ref, pv_ref, pl_ref, si_ref,
         mur_ref, muk_ref, muv_ref, mul_ref,
         w0_ref, a0_ref, kkp_ref, kap_ref, rkp_ref, gng_ref, gnb_ref,
         wd_ref, wa_ref, wg_ref,
         oa_ref, so_ref) = refs
    nblk = R // lb

    if carried:
        @pl.when(pl.program_id(1) == 0)
        def _():
            s_ref[...] = jnp.zeros_like(s_ref)
            cr_ref[...] = jnp.zeros_like(cr_ref)
            ck_ref[...] = jnp.zeros_like(ck_ref)
            cv_ref[...] = jnp.zeros_like(cv_ref)
            cl_ref[...] = jnp.zeros_like(cl_ref)

    row1 = lax.broadcasted_iota(jnp.int32, (R, 1), 0)
    first = (row1 % lb) == 0

    def shift(z_ref, prev0, mu_ref):
        z = z_ref[...]
        prev = jnp.where(first, prev0, pltpu.roll(z, 1, axis=0))
        return z + (prev - z) * mu_ref[...]

    if carried:
        r = shift(zr_ref, cr_ref[...], mur_ref)
        k = shift(zk_ref, ck_ref[...], muk_ref)
        v = shift(zv_ref, cv_ref[...], muv_ref)
        l = shift(zl_ref, cl_ref[...], mul_ref)
        cr_ref[...] = zr_ref[R - 1:R, :]
        ck_ref[...] = zk_ref[R - 1:R, :]
        cv_ref[...] = zv_ref[R - 1:R, :]
        cl_ref[...] = zl_ref[R - 1:R, :]
    else:
        r = shift(zr_ref, pr_ref[...], mur_ref)
        k = shift(zk_ref, pk_ref[...], muk_ref)
        v = shift(zv_ref, pv_ref[...], muv_ref)
        l = shift(zl_ref, pl_ref[...], mul_ref)

    w_raw = w0_ref[...] + _dot(jnp.tanh(l).astype(BF16), wd_ref[...])
    logw = jax.nn.sigmoid(w_raw) * (-math.exp(-0.5))
    a = jax.nn.sigmoid(a0_ref[...] + _dot(l.astype(BF16), wa_ref[...]))
    g = _dot(jax.nn.sigmoid(l).astype(BF16), wg_ref[...])
    kk = k * kkp_ref[...]
    k2 = k * (1.0 + (a - 1.0) * kap_ref[...])

    ri = lax.broadcasted_iota(jnp.int32, (GRP, GRP), 0)
    ci = lax.broadcasted_iota(jnp.int32, (GRP, GRP), 1)
    bdm = (ri // HEAD) == (ci // HEAD)
    ones_bd = jnp.where(bdm, 1.0, 0.0).astype(BF16)
    tr = lax.broadcasted_iota(jnp.int32, (R, R), 0)
    tc = lax.broadcasted_iota(jnp.int32, (R, R), 1)
    same = (tr // lb) == (tc // lb)
    incl01 = jnp.where(same & (tc <= tr), 1.0, 0.0).astype(BF16)
    wr = lax.broadcasted_iota(jnp.int32, (R, GRP), 0)
    wc = lax.broadcasted_iota(jnp.int32, (R, GRP), 1) % HEAD
    wsame = (wr // lb) == (wc // lb)
    strict_w = wsame & (wc < wr)
    incl_w = wsame & (wc <= wr)
    eye_w = jnp.where(wc == wr, 1.0, 0.0)

    def seg_sums(xs):
        parts = []
        for x in xs:
            parts.extend(_split2(x))
        stacked = jnp.concatenate(parts, axis=0)
        cols = [_dot(stacked[:, gi * GRP:(gi + 1) * GRP], ones_bd) for gi in range(N_GRP)]
        full = jnp.concatenate(cols, axis=1)
        return [full[2 * i * R:(2 * i + 1) * R] + full[(2 * i + 1) * R:(2 * i + 2) * R]
                for i in range(len(xs))]

    ss, rk = seg_sums([kk * kk, r * k2 * rkp_ref[...]])
    kk = kk / jnp.maximum(jnp.sqrt(ss), 1e-12)
    bonus = rk * v
    av = -kk
    bv = kk * a

    def cum(m01, x):
        hi, mid, lo = _split3(x)
        return _dot(m01, hi) + _dot(m01, mid) + _dot(m01, lo)

    cl = cum(incl01, logw)
    if nblk == 1:
        cl_end = cl[R - 1:R, :]
    else:
        cl_end = cum(jnp.where(same, 1.0, 0.0).astype(BF16), logw)
    e_neg = jnp.exp(-cl)
    e_end = jnp.exp(cl_end - cl)
    at = av * jnp.exp(cl - logw)
    rt = r * jnp.exp(cl)
    bt = bv * e_neg
    kt = k2 * e_neg
    bh = bv * e_end
    kh = k2 * e_end
    p_end = jnp.exp(cl_end)

    o_parts = []
    for gi in range(N_GRP):
        sl = slice(gi * GRP, (gi + 1) * GRP)
        v_g = v[:, sl]
        lhs = jnp.concatenate([at[:, sl], rt[:, sl]], axis=0).astype(BF16)
        ms = jnp.concatenate([_bd(bt[:, sl], bdm), _bd(kt[:, sl], bdm)], axis=0)
        o1 = _dot_nt(lhs, ms)
        w_ab = jnp.where(strict_w, o1[:R, :GRP], 0.0)
        w_ak = jnp.where(strict_w, o1[:R, GRP:], 0.0)
        w_rb = jnp.where(incl_w, o1[R:, :GRP], 0.0)
        w_rk = jnp.where(incl_w, o1[R:, GRP:], 0.0)

        if carried:
            o2 = _dot_nt(lhs, s_ref[gi].astype(BF16))
            a_s, r_s = o2[:R], o2[R:]
        else:
            a_rows, r_rows = [], []
            for b in range(nblk):
                rows = slice(b * lb, (b + 1) * lb)
                lhs_b = jnp.concatenate([at[rows, sl], rt[rows, sl]], axis=0).astype(BF16)
                o2 = _dot_nt(lhs_b, _bd(si_ref[b, gi], bdm))
                a_rows.append(o2[:lb])
                r_rows.append(o2[lb:])
            a_s = jnp.concatenate(a_rows, axis=0)
            r_s = jnp.concatenate(r_rows, axis=0)

        x_w = w_ab
        t_w = eye_w + x_w
        n_lvl = max(1, int(math.log2(lb)))
        for j in range(n_lvl):
            y_bd = _bd(x_w, bdm)
            if j == 0:
                x_w = _dot(x_w.astype(BF16), y_bd)
            elif j == n_lvl - 1:
                t_w = t_w + _dot(t_w.astype(BF16), y_bd)
            else:
                res = _dot(jnp.concatenate([t_w, x_w], axis=0).astype(BF16), y_bd)
                t_w = t_w + res[:R]
                x_w = res[R:]

        o3 = _dot(jnp.concatenate([w_ak, w_rk], axis=0).astype(BF16), _bd(v_g, bdm))
        u = _dot(t_w.astype(BF16), _bd(a_s + o3[:R], bdm))
        o_parts.append(r_s + o3[R:] + _dot(w_rb.astype(BF16), _bd(u, bdm)))

        if carried:
            upd = _dot_tn(jnp.concatenate([u, v_g], axis=0).astype(BF16),
                          jnp.concatenate([bh[:, sl], kh[:, sl]], axis=0).astype(BF16))
            s_new = jnp.where(bdm, s_ref[gi] * p_end[:, sl] + upd, 0.0)
            s_ref[gi] = s_new
            so_ref[gi] = _fold(s_new)
        else:
            for b in range(nblk):
                rows = slice(b * lb, (b + 1) * lb)
                upd = _dot_tn(jnp.concatenate([u[rows], v_g[rows]], axis=0).astype(BF16),
                              jnp.concatenate([bh[rows, sl], kh[rows, sl]], axis=0).astype(BF16))
                so_ref[b, gi] = (si_ref[b, gi] * p_end[b * lb:b * lb + 1, sl]
                                 + _fold(jnp.where(bdm, upd, 0.0)))

    o = jnp.concatenate(o_parts, axis=1)
    (mu,) = seg_sums([o])
    dlt = o - mu * (1.0 / HEAD)
    (var,) = seg_sums([dlt * dlt])
    on = dlt * lax.rsqrt(var * (1.0 / HEAD) + GN_EPS) * gng_ref[...] + gnb_ref[...]
    oa_ref[...] = ((on + bonus) * g).astype(BF16)


def _wkv(z, vecs, mats, n_seq, seq_len, lb, prev_rows=None, state_in=None):
    R = WKV_ROWS
    carried = prev_rows is None
    rows = z.shape[0]
    if carried:
        steps = seq_len // R
        grid = (n_seq, steps)
        rmap = lambda c: (lambda b, t: (b * steps + t, c))
        cmap = lambda b, t: (0, 0)
        seq_per_step = 1
        smap = lambda b, t: (b, 0, 0, 0)
        sem = ("arbitrary", "arbitrary")
    else:
        grid = (rows // R,)
        rmap = lambda c: (lambda i: (i, c))
        cmap = lambda i: (0, 0)
        seq_per_step = R // lb
        smap = lambda i: (i, 0, 0, 0)
        sem = ("arbitrary",)

    zspecs = [pl.BlockSpec((R, RW), rmap(Z_R // RW)),
              pl.BlockSpec((R, RW), rmap(Z_K // RW)),
              pl.BlockSpec((R, RW), rmap(Z_VR // RW)),
              pl.BlockSpec((R, LORA), rmap(Z_L // LORA))]
    in_specs = list(zspecs)
    args = [z, z, z, z]
    if not carried:
        in_specs += [pl.BlockSpec((R, RW), rmap(0)), pl.BlockSpec((R, RW), rmap(1)),
                     pl.BlockSpec((R, RW), rmap(2)), pl.BlockSpec((R, LORA), rmap(3 * RW // LORA)),
                     pl.BlockSpec((seq_per_step, N_GRP, HEAD, GRP), smap)]
        args += [prev_rows] * 4 + [state_in]
    for a in vecs + mats:
        in_specs.append(pl.BlockSpec(a.shape, cmap))
        args.append(a)

    scratch = []
    if carried:
        scratch = [pltpu.VMEM((N_GRP, GRP, GRP), F32), pltpu.VMEM((1, RW), F32),
                   pltpu.VMEM((1, RW), F32), pltpu.VMEM((1, RW), F32), pltpu.VMEM((1, LORA), F32)]
    n_state = n_seq
    return pl.pallas_call(
        functools.partial(_wkv_kernel, lb=lb, carried=carried),
        grid=grid,
        in_specs=in_specs,
        out_specs=[pl.BlockSpec((R, RW), rmap(0)),
                   pl.BlockSpec((None, N_GRP, HEAD, GRP), smap) if carried
                   else pl.BlockSpec((seq_per_step, N_GRP, HEAD, GRP), smap)],
        out_shape=[jax.ShapeDtypeStruct((rows, RW), BF16),
                   jax.ShapeDtypeStruct((n_state, N_GRP, HEAD, GRP), F32)],
        scratch_shapes=scratch,
        compiler_params=_params(sem),
        name="wkv_carried" if carried else "wkv_blocks",
    )(*args)


def _gmlp_kernel(zu_ref, zv_ref, lng_ref, lnb_ref, ws_ref, bias_ref, ub_ref, vn_ref):
    u = jax.nn.gelu(zu_ref[...])
    vg = jax.nn.gelu(zv_ref[...])
    mu = jnp.mean(vg, -1, keepdims=True)
    var = jnp.mean(jnp.square(vg - mu), -1, keepdims=True)
    vn = (vg - mu) * lax.rsqrt(var + LN_EPS) * lng_ref[...] + lnb_ref[...]
    vn_ref[...] = vn
    tr = lax.broadcasted_iota(jnp.int32, (CHUNK, CHUNK), 0)
    tc = lax.broadcasted_iota(jnp.int32, (CHUNK, CHUNK), 1)
    causal = tc <= tr
    vb = vn.astype(BF16)
    cols = []
    for gi in range(MLP_GROUPS):
        wsg = jnp.where(causal, ws_ref[gi], 0.0).astype(BF16)
        cols.append(_dot(wsg, vb[:, gi * MLP_GD:(gi + 1) * MLP_GD]))
    mixed = jnp.concatenate(cols, axis=1) + bias_ref[...]
    ub_ref[...] = (u * mixed).astype(BF16)


def _gmlp(z, ln_g, ln_b, ws, bias):
    rows = z.shape[0]
    return pl.pallas_call(
        _gmlp_kernel,
        grid=(rows // CHUNK,),
        in_specs=[pl.BlockSpec((CHUNK, MLP_W), lambda i: (i, Z_U // MLP_W)),
                  pl.BlockSpec((CHUNK, MLP_W), lambda i: (i, Z_V // MLP_W)),
                  pl.BlockSpec((1, MLP_W), lambda i: (0, 0)),
                  pl.BlockSpec((1, MLP_W), lambda i: (0, 0)),
                  pl.BlockSpec((MLP_GROUPS, CHUNK, CHUNK), lambda i: (0, 0, 0)),
                  pl.BlockSpec((CHUNK, MLP_W), lambda i: (0, 0))],
        out_specs=[pl.BlockSpec((CHUNK, MLP_W), lambda i: (i, 0)),
                   pl.BlockSpec((CHUNK, MLP_W), lambda i: (i, 0))],
        out_shape=[jax.ShapeDtypeStruct((rows, MLP_W), BF16),
                   jax.ShapeDtypeStruct((rows, MLP_W), F32)],
        compiler_params=_params(("arbitrary",)),
        name="gmlp",
    )(z, z, ln_g, ln_b, ws, bias)


def _merge_kernel(oa_ref, ub_ref, zga_ref, zgb_ref, x_ref, gm_ref, wa_ref, wb_ref, wo_ref, o_ref):
    ya = _dot(oa_ref[...], wa_ref[...])
    yb = _dot(ub_ref[...], wb_ref[...])
    merged = jax.nn.sigmoid(zga_ref[...]) * ya + jax.nn.sigmoid(zgb_ref[...]) * yb
    mix = _dot(merged.astype(BF16), wo_ref[...])
    o_ref[...] = x_ref[...] + gm_ref[...] * mix


def _merge(oa, ub, z, x, gate_m, wa, wb, wo, per_row, rows_per_seq):
    m, tm = x.shape[0], 256
    const = lambda shape: pl.BlockSpec(shape, lambda i: (0, 0), pipeline_mode=pl.Buffered(1))
    return pl.pallas_call(
        _merge_kernel,
        grid=(m // tm,),
        in_specs=[pl.BlockSpec((tm, RW), lambda i: (i, 0)),
                  pl.BlockSpec((tm, MLP_W), lambda i: (i, 0)),
                  pl.BlockSpec((tm, D_MODEL), lambda i: (i, Z_GA // D_MODEL)),
                  pl.BlockSpec((tm, D_MODEL), lambda i: (i, Z_GB // D_MODEL)),
                  pl.BlockSpec((tm, D_MODEL), lambda i: (i, 0)),
                  _mod_spec(per_row, tm, rows_per_seq),
                  const((RW, D_MODEL)), const((MLP_W, D_MODEL)), const((D_MODEL, D_MODEL))],
        out_specs=pl.BlockSpec((tm, D_MODEL), lambda i: (i, 0)),
        out_shape=jax.ShapeDtypeStruct((m, D_MODEL), F32),
        compiler_params=_params(("arbitrary",)),
        name="merge",
    )(oa, ub, z, z, x, gate_m, wa, wb, wo)


def _ffn_kernel(x_ref, sc_ref, sh_ref, gf_ref, g_ref, gfin_ref, wg_ref, wu_ref, wo_ref,
                y_ref, h_ref, acc_ref):
    j = pl.program_id(1)

    @pl.when(j == 0)
    def _():
        h = _rms(x_ref[...], g_ref[...]) * (1.0 + sc_ref[...]) + sh_ref[...]
        h_ref[...] = h.astype(BF16)
        acc_ref[...] = jnp.zeros_like(acc_ref)

    hb = h_ref[...]
    gt = _dot(hb, wg_ref[...])
    up = _dot(hb, wu_ref[...])
    act = gt * jax.nn.sigmoid(gt) * up
    acc_ref[...] += _dot(act.astype(BF16), wo_ref[...])

    @pl.when(j == pl.num_programs(1) - 1)
    def _():
        x2 = x_ref[...] + gf_ref[...] * acc_ref[...]
        y_ref[...] = _rms(x2, gfin_ref[...])


def _ffn(x, scale, shift, gate, g, g_final, w_in, w_out, per_row, rows_per_seq):
    m, tm, tf = x.shape[0], (256 if per_row else 512), 512
    nf = D_FF // tf
    mspec = _mod_spec(per_row, tm, rows_per_seq)
    vec = pl.BlockSpec((1, D_MODEL), lambda i, j: (0, 0))
    return pl.pallas_call(
        _ffn_kernel,
        grid=(m // tm, nf),
        in_specs=[pl.BlockSpec((tm, D_MODEL), lambda i, j: (i, 0)),
                  mspec, mspec, mspec, vec, vec,
                  pl.BlockSpec((D_MODEL, tf), lambda i, j: (0, j)),
                  pl.BlockSpec((D_MODEL, tf), lambda i, j: (0, nf + j)),
                  pl.BlockSpec((tf, D_MODEL), lambda i, j: (j, 0))],
        out_specs=pl.BlockSpec((tm, D_MODEL), lambda i, j: (i, 0)),
        out_shape=jax.ShapeDtypeStruct((m, D_MODEL), F32),
        scratch_shapes=[pltpu.VMEM((tm, D_MODEL), BF16), pltpu.VMEM((tm, D_MODEL), F32)],
        compiler_params=_params(("arbitrary", "arbitrary")),
        name="ffn",
    )(x, scale, shift, gate, g, g_final, w_in, w_in, w_out)


def _to_wide(s):
    n = s.shape[0]
    return s.reshape(n, N_GRP, HPG, HEAD, HEAD).transpose(0, 1, 3, 2, 4).reshape(n, N_GRP, HEAD, GRP)


def _from_wide(s):
    n = s.shape[0]
    return s.reshape(n, N_GRP, HEAD, HPG, HEAD).transpose(0, 1, 3, 2, 4).reshape(n, N_HEADS, HEAD, HEAD)


def kernel(x_prompt, x_sample, state_wkv, state_shift, c_prompt, c_sample, w_ada, b_ada, norm_mix_g, w_in, mu_shift, w0, w_decay_up, a0, w_aaa_up, w_gate_up, k_k, k_a, r_k, gn_g, gn_b, ln_v_g, ln_v_b, w_spatial, b_spatial, w_branch_a, w_branch_b, w_out, norm_ffn_g, w_ffn_in, w_ffn_out, norm_final_g):
    assert w_ada.shape[0] == 1, "single layer"
    bp, tp, _ = x_prompt.shape
    bs, ts, _ = x_sample.shape
    assert tp % CHUNK == 0 and WKV_ROWS % ts == 0 and CHUNK % ts == 0

    n_c = bp + bs
    c_all = jnp.concatenate([c_prompt, c_sample], axis=0)
    c_all = jnp.pad(c_all, ((0, -n_c % 8), (0, 0)))
    mod = _ada(c_all, w_ada[0], b_ada[0])[:n_c].reshape(n_c, 6, D_MODEL)
    mod_p = [mod[:bp, i][:, None, :] for i in range(6)]
    mod_s = [jnp.repeat(mod[bp:, i], ts, axis=0) for i in range(6)]

    w_in_r = jnp.concatenate([w_in[0][:, C_SHIFT + 2 * MLP_W:], w_in[0][:, C_SHIFT:C_SHIFT + 2 * MLP_W],
                              w_in[0][:, :C_SHIFT]], axis=1)
    w_in_r = jnp.pad(w_in_r, ((0, 0), (0, Z_W - C_IN))).astype(BF16)
    wa_b, wb_b, wo_b = w_branch_a[0].astype(BF16), w_branch_b[0].astype(BF16), w_out[0].astype(BF16)
    wfi_b, wfo_b = w_ffn_in[0].astype(BF16), w_ffn_out[0].astype(BF16)

    row = lambda a: a.reshape(1, -1)
    mu = mu_shift[0]
    vecs = [row(mu[:RW]), row(mu[RW:2 * RW]), row(mu[2 * RW:3 * RW]), row(mu[3 * RW:]),
            row(w0[0]), row(a0[0]), row(k_k[0]), row(k_a[0]), row(r_k[0]), row(gn_g[0]), row(gn_b[0])]
    zpad = lambda w, lo: jnp.pad(w, ((lo, LORA - lo - w.shape[0]), (0, 0))).astype(BF16)
    mats = [zpad(w_decay_up[0], 0), zpad(w_aaa_up[0], DECAY_LORA),
            zpad(w_gate_up[0], DECAY_LORA + AAA_LORA)]

    bias_p = jnp.repeat(b_spatial[0].T, MLP_GD, axis=1)
    ws_p = w_spatial[0]
    reps = CHUNK // ts
    ws_s = jnp.einsum("ab,gts->gatbs", jnp.eye(reps, dtype=F32),
                      w_spatial[0][:, :ts, :ts]).reshape(MLP_GROUPS, CHUNK, CHUNK)
    bias_s = jnp.tile(bias_p[:ts], (reps, 1))

    xp = x_prompt.reshape(bp * tp, D_MODEL)
    xs = x_sample.reshape(bs * ts, D_MODEL)
    g_mix, g_ffn, g_fin = row(norm_mix_g[0]), row(norm_ffn_g[0]), row(norm_final_g)
    lng, lnb = row(ln_v_g[0]), row(ln_v_b[0])

    zp = _inproj(xp, mod_p[1], mod_p[0], g_mix, w_in_r, False, tp)
    oa_p, st_p = _wkv(zp, vecs, mats, bp, tp, WKV_ROWS)
    ub_p, _ = _gmlp(zp, lng, lnb, ws_p, bias_p)
    x1_p = _merge(oa_p, ub_p, zp, xp, mod_p[2], wa_b, wb_b, wo_b, False, tp)
    y_p = _ffn(x1_p, mod_p[4], mod_p[3], mod_p[5], g_ffn, g_fin, wfi_b, wfo_b, False, tp)

    zs = _inproj(xs, mod_s[1], mod_s[0], g_mix, w_in_r, True, ts)
    prev_rows = jnp.repeat(state_shift[0], ts, axis=0)
    oa_s, st_s = _wkv(zs, vecs, mats, bs, ts, ts, prev_rows=prev_rows, state_in=_to_wide(state_wkv[0]))
    ub_s, vn_s = _gmlp(zs, lng, lnb, ws_s, bias_s)
    x1_s = _merge(oa_s, ub_s, zs, xs, mod_s[2], wa_b, wb_b, wo_b, True, ts)
    y_s = _ffn(x1_s, mod_s[4], mod_s[3], mod_s[5], g_ffn, g_fin, wfi_b, wfo_b, True, ts)

    shift_p = zp.reshape(bp, tp, Z_W)[:, -1, Z_R:Z_R + C_SHIFT]
    shift_s = zs.reshape(bs, ts, Z_W)[:, -1, Z_R:Z_R + C_SHIFT]
    return (y_p.reshape(bp, tp, D_MODEL),
            y_s.reshape(bs, ts, D_MODEL),
            _from_wide(st_p)[None],
            shift_p[None],
            _from_wide(st_s)[None],
            shift_s[None],
            vn_s.reshape(bs, ts, MLP_W)[None])
```

```python
import functools
import math

import jax
import jax.numpy as jnp
from jax import lax
from jax.experimental import pallas as pl
from jax.experimental.pallas import tpu as pltpu

F32 = jnp.float32
BF16 = jnp.bfloat16

D_MODEL = 2048
HEAD = 64
RW = D_MODEL // 2
N_HEADS = RW // HEAD
DECAY_LORA = D_MODEL // 32
AAA_LORA = D_MODEL // 32
GATE_LORA = D_MODEL // 16
LORA = DECAY_LORA + AAA_LORA + GATE_LORA
CHUNK = 128
MLP_W = D_MODEL // 2
MLP_GROUPS = 8
MLP_GD = MLP_W // MLP_GROUPS
D_FF = ((-(-8 * D_MODEL // 3)) + 255) // 256 * 256
C_SHIFT = 3 * RW + LORA
C_IN = C_SHIFT + 2 * MLP_W + 2 * D_MODEL
NORM_EPS = 1e-6
GN_EPS = 64e-5
LN_EPS = 1e-5

Z_GA, Z_GB, Z_U, Z_V, Z_R, Z_K, Z_VR, Z_L = 0, 2048, 4096, 5120, 6144, 7168, 8192, 9216
Z_TN = 512
Z_W = -(-C_IN // Z_TN) * Z_TN

GRP = 256
HPG = GRP // HEAD
N_GRP = RW // GRP
WKV_ROWS = 64
VMEM_LIMIT = 56 * 1024 * 1024


def _dot(a, b):
    return jnp.dot(a, b, preferred_element_type=F32)


def _dot_nt(a, b):
    return lax.dot_general(a, b, (((1,), (1,)), ((), ())), preferred_element_type=F32)


def _dot_tn(a, b):
    return lax.dot_general(a, b, (((0,), (0,)), ((), ())), preferred_element_type=F32)


def _rms(x, g):
    return x * lax.rsqrt(jnp.mean(x * x, -1, keepdims=True) + NORM_EPS) * g


def _split2(x):
    hi = x.astype(BF16)
    lo = (x - hi.astype(F32)).astype(BF16)
    return hi, lo


def _split3(x):
    hi = x.astype(BF16)
    r1 = x - hi.astype(F32)
    mid = r1.astype(BF16)
    lo = (r1 - mid.astype(F32)).astype(BF16)
    return hi, mid, lo


def _params(sem):
    return pltpu.CompilerParams(dimension_semantics=sem, vmem_limit_bytes=VMEM_LIMIT)


def _ada_kernel(c_ref, w_ref, b_ref, o_ref, s_ref):
    @pl.when(pl.program_id(0) == 0)
    def _():
        c = c_ref[...]
        s_ref[...] = (c * jax.nn.sigmoid(c)).astype(BF16)

    o_ref[...] = _dot(s_ref[...], w_ref[...].astype(BF16)) + b_ref[...]


def _ada(c, w_ada, b_ada):
    m, n, tn = c.shape[0], w_ada.shape[1], 512
    return pl.pallas_call(
        _ada_kernel,
        grid=(n // tn,),
        in_specs=[pl.BlockSpec((m, D_MODEL), lambda j: (0, 0)),
                  pl.BlockSpec((D_MODEL, tn), lambda j: (0, j)),
                  pl.BlockSpec((1, tn), lambda j: (0, j))],
        out_specs=pl.BlockSpec((m, tn), lambda j: (0, j)),
        out_shape=jax.ShapeDtypeStruct((m, n), F32),
        scratch_shapes=[pltpu.VMEM((m, D_MODEL), BF16)],
        compiler_params=_params(("arbitrary",)),
        name="ada",
    )(c, w_ada, b_ada.reshape(1, n))


MOD_SHIFT_M, MOD_SCALE_M, MOD_GATE_M, MOD_SHIFT_F, MOD_SCALE_F, MOD_GATE_F = range(6)


def _mod_arg(mod, per_row):
    return mod if per_row else mod.reshape(mod.shape[0], 1, mod.shape[1])


def _mod_spec(k, per_row, tm, rows_per_seq, seq_row0):
    if per_row:
        return pl.BlockSpec((tm, D_MODEL), lambda i, *_: (i, k))
    return pl.BlockSpec((None, 1, D_MODEL), lambda i, *_: (seq_row0 + i * tm // rows_per_seq, 0, k))


ZB = 256
_N_RW_B, _N_MLP_B, _N_GATE_B = C_SHIFT // ZB, 2 * MLP_W // ZB, 2 * D_MODEL // ZB


def _w_in_block(zb):
    return jnp.where(zb < _N_GATE_B, zb + _N_RW_B + _N_MLP_B,
                     jnp.where(zb < _N_GATE_B + _N_MLP_B, zb - _N_GATE_B + _N_RW_B,
                               jnp.minimum(zb - _N_GATE_B - _N_MLP_B, _N_RW_B - 1)))


def _inproj_kernel(x_ref, sc_ref, sh_ref, g_ref, wlo_ref, whi_ref, o_ref, h_ref):
    @pl.when(pl.program_id(1) == 0)
    def _():
        h = _rms(x_ref[...], g_ref[...]) * (1.0 + sc_ref[...]) + sh_ref[...]
        h_ref[...] = h.astype(BF16)

    h = h_ref[...]
    o_ref[:, :ZB] = _dot(h, wlo_ref[...].astype(BF16))
    o_ref[:, ZB:] = _dot(h, whi_ref[...].astype(BF16))


def _inproj(x, mod, g, w, per_row, rows_per_seq, seq_row0):
    m, tm = x.shape[0], 1024
    assert Z_TN == 2 * ZB
    mspec = lambda k: _mod_spec(k, per_row, tm, rows_per_seq, seq_row0)
    marg = _mod_arg(mod, per_row)
    return pl.pallas_call(
        _inproj_kernel,
        grid=(m // tm, Z_W // Z_TN),
        in_specs=[pl.BlockSpec((tm, D_MODEL), lambda i, j: (i, 0)),
                  mspec(MOD_SCALE_M), mspec(MOD_SHIFT_M),
                  pl.BlockSpec((1, D_MODEL), lambda i, j: (0, 0)),
                  pl.BlockSpec((D_MODEL, ZB), lambda i, j: (0, _w_in_block(2 * j))),
                  pl.BlockSpec((D_MODEL, ZB), lambda i, j: (0, _w_in_block(2 * j + 1)))],
        out_specs=pl.BlockSpec((tm, Z_TN), lambda i, j: (i, j)),
        out_shape=jax.ShapeDtypeStruct((m, Z_W), F32),
        scratch_shapes=[pltpu.VMEM((tm, D_MODEL), BF16)],
        compiler_params=_params(("arbitrary", "arbitrary")),
        name="inproj",
    )(x, marg, marg, g, w, w)


def _bd(w, mask):
    return jnp.where(mask, jnp.concatenate([w] * HPG, axis=0), 0.0).astype(BF16)


def _fold(m):
    out = m[0:HEAD]
    for h in range(1, HPG):
        out = out + m[h * HEAD:(h + 1) * HEAD]
    return out


def _wkv_kernel(*refs, lb, carried):
    R = WKV_ROWS
    if carried:
        (zr_ref, zk_ref, zv_ref, zl_ref,
         mur_ref, muk_ref, muv_ref, mul_ref,
         w0_ref, a0_ref, kkp_ref, kap_ref, rkp_ref, gng_ref, gnb_ref,
         wd_ref, wa_ref, wg_ref,
         oa_ref, so_ref,
         s_ref, cr_ref, ck_ref, cv_ref, cl_ref) = refs
    else:
        (zr_ref, zk_ref, zv_ref, zl_ref,
         pr_ref, pk_ref, pv_ref, pl_ref, si_ref,
         mur_ref, muk_ref, muv_ref, mul_ref,
         w0_ref, a0_ref, kkp_ref, kap_ref, rkp_ref, gng_ref, gnb_ref,
         wd_ref, wa_ref, wg_ref,
         oa_ref, so_ref) = refs
    nblk = R // lb

    if carried:
        @pl.when(pl.program_id(1) == 0)
        def _():
            s_ref[...] = jnp.zeros_like(s_ref)
            cr_ref[...] = jnp.zeros_like(cr_ref)
            ck_ref[...] = jnp.zeros_like(ck_ref)
            cv_ref[...] = jnp.zeros_like(cv_ref)
            cl_ref[...] = jnp.zeros_like(cl_ref)

    row1 = lax.broadcasted_iota(jnp.int32, (R, 1), 0)
    first = (row1 % lb) == 0

    def shift(z_ref, prev0, mu_ref):
        z = z_ref[...]
        prev = jnp.where(first, prev0, pltpu.roll(z, 1, axis=0))
        return z + (prev - z) * mu_ref[...]

    if carried:
        r = shift(zr_ref, cr_ref[...], mur_ref)
        k = shift(zk_ref, ck_ref[...], muk_ref)
        v = shift(zv_ref, cv_ref[...], muv_ref)
        l = shift(zl_ref, cl_ref[...], mul_ref)
        cr_ref[...] = zr_ref[R - 1:R, :]
        ck_ref[...] = zk_ref[R - 1:R, :]
        cv_ref[...] = zv_ref[R - 1:R, :]
        cl_ref[...] = zl_ref[R - 1:R, :]
    else:
        r = shift(zr_ref, pr_ref[...], mur_ref)
        k = shift(zk_ref, pk_ref[...], muk_ref)
        v = shift(zv_ref, pv_ref[...], muv_ref)
        l = shift(zl_ref, pl_ref[...], mul_ref)

    w_raw = w0_ref[...] + _dot(jnp.tanh(l).astype(BF16), wd_ref[...])
    logw = jax.nn.sigmoid(w_raw) * (-math.exp(-0.5))
    a = jax.nn.sigmoid(a0_ref[...] + _dot(l.astype(BF16), wa_ref[...]))
    g = _dot(jax.nn.sigmoid(l).astype(BF16), wg_ref[...])
    kk = k * kkp_ref[...]
    k2 = k * (1.0 + (a - 1.0) * kap_ref[...])

    ri = lax.broadcasted_iota(jnp.int32, (GRP, GRP), 0)
    ci = lax.broadcasted_iota(jnp.int32, (GRP, GRP), 1)
    bdm = (ri // HEAD) == (ci // HEAD)
    ones_bd = jnp.where(bdm, 1.0, 0.0).astype(BF16)
    tr = lax.broadcasted_iota(jnp.int32, (R, R), 0)
    tc = lax.broadcasted_iota(jnp.int32, (R, R), 1)
    same = (tr // lb) == (tc // lb)
    incl01 = jnp.where(same & (tc <= tr), 1.0, 0.0).astype(BF16)
    wr = lax.broadcasted_iota(jnp.int32, (R, GRP), 0)
    wc = lax.broadcasted_iota(jnp.int32, (R, GRP), 1) % HEAD
    wsame = (wr // lb) == (wc // lb)
    strict_w = wsame & (wc < wr)
    incl_w = wsame & (wc <= wr)
    eye_w = jnp.where(wc == wr, 1.0, 0.0)

    def seg_sums(xs):
        parts = []
        for x in xs:
            parts.extend(_split2(x))
        stacked = jnp.concatenate(parts, axis=0)
        cols = [_dot(stacked[:, gi * GRP:(gi + 1) * GRP], ones_bd) for gi in range(N_GRP)]
        full = jnp.concatenate(cols, axis=1)
        return [full[2 * i * R:(2 * i + 1) * R] + full[(2 * i + 1) * R:(2 * i + 2) * R]
                for i in range(len(xs))]

    ss, rk = seg_sums([kk * kk, r * k2 * rkp_ref[...]])
    kk = kk / jnp.maximum(jnp.sqrt(ss), 1e-12)
    bonus = rk * v
    av = -kk
    bv = kk * a

    def cum(m01, x):
        hi, mid, lo = _split3(x)
        return _dot(m01, hi) + _dot(m01, mid) + _dot(m01, lo)

    cl = cum(incl01, logw)
    if nblk == 1:
        cl_end = cl[R - 1:R, :]
    else:
        cl_end = cum(jnp.where(same, 1.0, 0.0).astype(BF16), logw)
    e_neg = jnp.exp(-cl)
    e_end = jnp.exp(cl_end - cl)
    at = av * jnp.exp(cl - logw)
    rt = r * jnp.exp(cl)
    bt = bv * e_neg
    kt = k2 * e_neg
    bh = bv * e_end
    kh = k2 * e_end
    p_end = jnp.exp(cl_end)

    grps = range(N_GRP)
    sls = [slice(gi * GRP, (gi + 1) * GRP) for gi in grps]
    blks = [slice(b * lb, (b + 1) * lb) for b in range(nblk)]
    v_g = [v[:, sl] for sl in sls]
    lhs = [jnp.concatenate([at[:, sl], rt[:, sl]], axis=0).astype(BF16) for sl in sls]
    o1 = [_dot_nt(lhs[gi], jnp.concatenate([_bd(bt[:, sls[gi]], bdm), _bd(kt[:, sls[gi]], bdm)], axis=0))
          for gi in grps]
    w_ab = [jnp.where(strict_w, o[:R, :GRP], 0.0) for o in o1]
    w_ak = [jnp.where(strict_w, o[:R, GRP:], 0.0) for o in o1]
    w_rb = [jnp.where(incl_w, o[R:, :GRP], 0.0) for o in o1]
    w_rk = [jnp.where(incl_w, o[R:, GRP:], 0.0) for o in o1]

    def state_part(gi):
        if carried:
            o2 = _dot_nt(lhs[gi], s_ref[gi].astype(BF16))
            return o2[:R], o2[R:]
        a_rows, r_rows = [], []
        for b, rows in enumerate(blks):
            lhs_b = jnp.concatenate([at[rows, sls[gi]], rt[rows, sls[gi]]], axis=0).astype(BF16)
            o2 = _dot_nt(lhs_b, _bd(si_ref[b, gi], bdm))
            a_rows.append(o2[:lb])
            r_rows.append(o2[lb:])
        return jnp.concatenate(a_rows, axis=0), jnp.concatenate(r_rows, axis=0)

    x_w = list(w_ab)
    t_w = [eye_w + x for x in x_w]
    n_lvl = max(1, int(math.log2(lb)))
    a_s = r_s = o3 = None
    for j in range(n_lvl):
        for gi in grps:
            y_bd = _bd(x_w[gi], bdm)
            if j == 0:
                x_w[gi] = _dot(x_w[gi].astype(BF16), y_bd)
            elif j == n_lvl - 1:
                t_w[gi] = t_w[gi] + _dot(t_w[gi].astype(BF16), y_bd)
            else:
                res = _dot(jnp.concatenate([t_w[gi], x_w[gi]], axis=0).astype(BF16), y_bd)
                t_w[gi] = t_w[gi] + res[:R]
                x_w[gi] = res[R:]
        if j == 0:
            parts = [state_part(gi) for gi in grps]
            a_s, r_s = [p[0] for p in parts], [p[1] for p in parts]
        elif j == 1:
            o3 = [_dot(jnp.concatenate([w_ak[gi], w_rk[gi]], axis=0).astype(BF16), _bd(v_g[gi], bdm))
                  for gi in grps]

    u = [_dot(t_w[gi].astype(BF16), _bd(a_s[gi] + o3[gi][:R], bdm)) for gi in grps]
    o_parts = [r_s[gi] + o3[gi][R:] + _dot(w_rb[gi].astype(BF16), _bd(u[gi], bdm)) for gi in grps]

    for gi in grps:
        sl = sls[gi]
        if carried:
            upd = _dot_tn(jnp.concatenate([u[gi], v_g[gi]], axis=0).astype(BF16),
                          jnp.concatenate([bh[:, sl], kh[:, sl]], axis=0).astype(BF16))
            s_new = jnp.where(bdm, s_ref[gi] * p_end[:, sl] + upd, 0.0)
            s_ref[gi] = s_new
            so_ref[gi] = _fold(s_new)
        else:
            for b, rows in enumerate(blks):
                upd = _dot_tn(jnp.concatenate([u[gi][rows], v_g[gi][rows]], axis=0).astype(BF16),
                              jnp.concatenate([bh[rows, sl], kh[rows, sl]], axis=0).astype(BF16))
                so_ref[b, gi] = (si_ref[b, gi] * p_end[b * lb:b * lb + 1, sl]
                                 + _fold(jnp.where(bdm, upd, 0.0)))

    o = jnp.concatenate(o_parts, axis=1)
    (mu,) = seg_sums([o])
    dlt = o - mu * (1.0 / HEAD)
    (var,) = seg_sums([dlt * dlt])
    on = dlt * lax.rsqrt(var * (1.0 / HEAD) + GN_EPS) * gng_ref[...] + gnb_ref[...]
    oa_ref[...] = ((on + bonus) * g).astype(BF16)


def _wkv(z, vecs, mats, n_seq, seq_len, lb, prev_rows=None, state_in=None):
    R = WKV_ROWS
    carried = prev_rows is None
    rows = z.shape[0]
    if carried:
        steps = seq_len // R
        grid = (n_seq, steps)
        rmap = lambda c: (lambda b, t: (b * steps + t, c))
        cmap = lambda b, t: (0, 0)
        seq_per_step = 1
        smap = lambda b, t: (b, 0, 0, 0)
        sem = ("arbitrary", "arbitrary")
    else:
        grid = (rows // R,)
        rmap = lambda c: (lambda i: (i, c))
        cmap = lambda i: (0, 0)
        seq_per_step = R // lb
        smap = lambda i: (i, 0, 0, 0)
        sem = ("arbitrary",)

    zspecs = [pl.BlockSpec((R, RW), rmap(Z_R // RW)),
              pl.BlockSpec((R, RW), rmap(Z_K // RW)),
              pl.BlockSpec((R, RW), rmap(Z_VR // RW)),
              pl.BlockSpec((R, LORA), rmap(Z_L // LORA))]
    in_specs = list(zspecs)
    args = [z, z, z, z]
    if not carried:
        in_specs += [pl.BlockSpec((R, RW), rmap(0)), pl.BlockSpec((R, RW), rmap(1)),
                     pl.BlockSpec((R, RW), rmap(2)), pl.BlockSpec((R, LORA), rmap(3 * RW // LORA)),
                     pl.BlockSpec((seq_per_step, N_GRP, HEAD, GRP), smap)]
        args += [prev_rows] * 4 + [state_in]
    for a in vecs + mats:
        in_specs.append(pl.BlockSpec(a.shape, cmap))
        args.append(a)

    scratch = []
    if carried:
        scratch = [pltpu.VMEM((N_GRP, GRP, GRP), F32), pltpu.VMEM((1, RW), F32),
                   pltpu.VMEM((1, RW), F32), pltpu.VMEM((1, RW), F32), pltpu.VMEM((1, LORA), F32)]
    n_state = n_seq
    return pl.pallas_call(
        functools.partial(_wkv_kernel, lb=lb, carried=carried),
        grid=grid,
        in_specs=in_specs,
        out_specs=[pl.BlockSpec((R, RW), rmap(0)),
                   pl.BlockSpec((None, N_GRP, HEAD, GRP), smap) if carried
                   else pl.BlockSpec((seq_per_step, N_GRP, HEAD, GRP), smap)],
        out_shape=[jax.ShapeDtypeStruct((rows, RW), BF16),
                   jax.ShapeDtypeStruct((n_state, N_GRP, HEAD, GRP), F32)],
        scratch_shapes=scratch,
        compiler_params=_params(sem),
        name="wkv_carried" if carried else "wkv_blocks",
    )(*args)


def _gmlp_kernel(zu_ref, zv_ref, lng_ref, lnb_ref, ws_ref, bias_ref, ub_ref, vn_ref=None):
    u = jax.nn.gelu(zu_ref[...])
    vg = jax.nn.gelu(zv_ref[...])
    mu = jnp.mean(vg, -1, keepdims=True)
    var = jnp.mean(jnp.square(vg - mu), -1, keepdims=True)
    vn = (vg - mu) * lax.rsqrt(var + LN_EPS) * lng_ref[...] + lnb_ref[...]
    if vn_ref is not None:
        vn_ref[...] = vn
    tr = lax.broadcasted_iota(jnp.int32, (CHUNK, CHUNK), 0)
    tc = lax.broadcasted_iota(jnp.int32, (CHUNK, CHUNK), 1)
    causal = tc <= tr
    vb = vn.astype(BF16)
    cols = []
    for gi in range(MLP_GROUPS):
        wsg = jnp.where(causal, ws_ref[gi], 0.0).astype(BF16)
        cols.append(_dot(wsg, vb[:, gi * MLP_GD:(gi + 1) * MLP_GD]))
    mixed = jnp.concatenate(cols, axis=1) + bias_ref[...]
    ub_ref[...] = (u * mixed).astype(BF16)


def _gmlp(z, ln_g, ln_b, ws, bias, want_vn):
    rows = z.shape[0]
    n_out = 2 if want_vn else 1
    return pl.pallas_call(
        _gmlp_kernel,
        grid=(rows // CHUNK,),
        in_specs=[pl.BlockSpec((CHUNK, MLP_W), lambda i: (i, Z_U // MLP_W)),
                  pl.BlockSpec((CHUNK, MLP_W), lambda i: (i, Z_V // MLP_W)),
                  pl.BlockSpec((1, MLP_W), lambda i: (0, 0)),
                  pl.BlockSpec((1, MLP_W), lambda i: (0, 0)),
                  pl.BlockSpec((MLP_GROUPS, CHUNK, CHUNK), lambda i: (0, 0, 0)),
                  pl.BlockSpec((CHUNK, MLP_W), lambda i: (0, 0))],
        out_specs=[pl.BlockSpec((CHUNK, MLP_W), lambda i: (i, 0)),
                   pl.BlockSpec((CHUNK, MLP_W), lambda i: (i, 0))][:n_out],
        out_shape=[jax.ShapeDtypeStruct((rows, MLP_W), BF16),
                   jax.ShapeDtypeStruct((rows, MLP_W), F32)][:n_out],
        compiler_params=_params(("arbitrary",)),
        name="gmlp",
    )(z, z, ln_g, ln_b, ws, bias)


def _merge_kernel(oa_ref, ub_ref, zga_ref, zgb_ref, x_ref, gm_ref, wa_ref, wb_ref, wo_ref, o_ref):
    ya = _dot(oa_ref[...], wa_ref[...])
    yb = _dot(ub_ref[...], wb_ref[...])
    merged = jax.nn.sigmoid(zga_ref[...]) * ya + jax.nn.sigmoid(zgb_ref[...]) * yb
    mix = _dot(merged.astype(BF16), wo_ref[...])
    o_ref[...] = x_ref[...] + gm_ref[...] * mix


def _merge(oa, ub, z, x, mod, wa, wb, wo, per_row, rows_per_seq, seq_row0):
    m, tm = x.shape[0], 256
    const = lambda shape: pl.BlockSpec(shape, lambda i: (0, 0), pipeline_mode=pl.Buffered(1))
    return pl.pallas_call(
        _merge_kernel,
        grid=(m // tm,),
        in_specs=[pl.BlockSpec((tm, RW), lambda i: (i, 0)),
                  pl.BlockSpec((tm, MLP_W), lambda i: (i, 0)),
                  pl.BlockSpec((tm, D_MODEL), lambda i: (i, Z_GA // D_MODEL)),
                  pl.BlockSpec((tm, D_MODEL), lambda i: (i, Z_GB // D_MODEL)),
                  pl.BlockSpec((tm, D_MODEL), lambda i: (i, 0)),
                  _mod_spec(MOD_GATE_M, per_row, tm, rows_per_seq, seq_row0),
                  const((RW, D_MODEL)), const((MLP_W, D_MODEL)), const((D_MODEL, D_MODEL))],
        out_specs=pl.BlockSpec((tm, D_MODEL), lambda i: (i, 0)),
        out_shape=jax.ShapeDtypeStruct((m, D_MODEL), F32),
        compiler_params=_params(("arbitrary",)),
        name="merge",
    )(oa, ub, z, z, x, _mod_arg(mod, per_row), wa, wb, wo)


def _ffn_kernel(x_ref, sc_ref, sh_ref, gf_ref, g_ref, gfin_ref, wg_ref, wu_ref, wo_ref,
                y_ref, h_ref, acc_ref):
    j = pl.program_id(1)

    @pl.when(j == 0)
    def _():
        h = _rms(x_ref[...], g_ref[...]) * (1.0 + sc_ref[...]) + sh_ref[...]
        h_ref[...] = h.astype(BF16)
        acc_ref[...] = jnp.zeros_like(acc_ref)

    hb = h_ref[...]
    gt = _dot(hb, wg_ref[...])
    up = _dot(hb, wu_ref[...])
    act = gt * jax.nn.sigmoid(gt) * up
    acc_ref[...] += _dot(act.astype(BF16), wo_ref[...])

    @pl.when(j == pl.num_programs(1) - 1)
    def _():
        x2 = x_ref[...] + gf_ref[...] * acc_ref[...]
        y_ref[...] = _rms(x2, gfin_ref[...])


def _ffn(x, mod, g, g_final, w_in, w_out, per_row, rows_per_seq, seq_row0):
    m, tm, tf = x.shape[0], (256 if per_row else 512), 512
    nf = D_FF // tf
    mspec = lambda k: _mod_spec(k, per_row, tm, rows_per_seq, seq_row0)
    marg = _mod_arg(mod, per_row)
    vec = pl.BlockSpec((1, D_MODEL), lambda i, j: (0, 0))
    return pl.pallas_call(
        _ffn_kernel,
        grid=(m // tm, nf),
        in_specs=[pl.BlockSpec((tm, D_MODEL), lambda i, j: (i, 0)),
                  mspec(MOD_SCALE_F), mspec(MOD_SHIFT_F), mspec(MOD_GATE_F), vec, vec,
                  pl.BlockSpec((D_MODEL, tf), lambda i, j: (0, j)),
                  pl.BlockSpec((D_MODEL, tf), lambda i, j: (0, nf + j)),
                  pl.BlockSpec((tf, D_MODEL), lambda i, j: (j, 0))],
        out_specs=pl.BlockSpec((tm, D_MODEL), lambda i, j: (i, 0)),
        out_shape=jax.ShapeDtypeStruct((m, D_MODEL), F32),
        scratch_shapes=[pltpu.VMEM((tm, D_MODEL), BF16), pltpu.VMEM((tm, D_MODEL), F32)],
        compiler_params=_params(("arbitrary", "arbitrary")),
        name="ffn",
    )(x, marg, marg, marg, g, g_final, w_in, w_in, w_out)


def _to_wide(s):
    n = s.shape[0]
    return s.reshape(n, N_GRP, HPG, HEAD, HEAD).transpose(0, 1, 3, 2, 4).reshape(n, N_GRP, HEAD, GRP)


def _from_wide(s):
    n = s.shape[0]
    return s.reshape(n, N_GRP, HEAD, HPG, HEAD).transpose(0, 1, 3, 2, 4).reshape(n, N_HEADS, HEAD, HEAD)


def kernel(x_prompt, x_sample, state_wkv, state_shift, c_prompt, c_sample, w_ada, b_ada, norm_mix_g, w_in, mu_shift, w0, w_decay_up, a0, w_aaa_up, w_gate_up, k_k, k_a, r_k, gn_g, gn_b, ln_v_g, ln_v_b, w_spatial, b_spatial, w_branch_a, w_branch_b, w_out, norm_ffn_g, w_ffn_in, w_ffn_out, norm_final_g):
    assert w_ada.shape[0] == 1, "single layer"
    bp, tp, _ = x_prompt.shape
    bs, ts, _ = x_sample.shape
    assert tp % CHUNK == 0 and WKV_ROWS % ts == 0 and CHUNK % ts == 0

    seq_row0 = bs * ts
    c_all = jnp.concatenate([jnp.repeat(c_sample, ts, axis=0), c_prompt], axis=0)
    c_all = jnp.pad(c_all, ((0, -c_all.shape[0] % 8), (0, 0)))
    mod = _ada(c_all, w_ada[0], b_ada[0])

    wa_b, wb_b, wo_b = w_branch_a[0].astype(BF16), w_branch_b[0].astype(BF16), w_out[0].astype(BF16)
    wfi_b, wfo_b = w_ffn_in[0].astype(BF16), w_ffn_out[0].astype(BF16)

    row = lambda a: a.reshape(1, -1)
    mu = mu_shift[0]
    vecs = [row(mu[:RW]), row(mu[RW:2 * RW]), row(mu[2 * RW:3 * RW]), row(mu[3 * RW:]),
            row(w0[0]), row(a0[0]), row(k_k[0]), row(k_a[0]), row(r_k[0]), row(gn_g[0]), row(gn_b[0])]
    zpad = lambda w, lo: jnp.pad(w, ((lo, LORA - lo - w.shape[0]), (0, 0))).astype(BF16)
    mats = [zpad(w_decay_up[0], 0), zpad(w_aaa_up[0], DECAY_LORA),
            zpad(w_gate_up[0], DECAY_LORA + AAA_LORA)]

    bias_p = jnp.repeat(b_spatial[0].T, MLP_GD, axis=1)
    ws_p = w_spatial[0]
    reps = CHUNK // ts
    ws_s = jnp.einsum("ab,gts->gatbs", jnp.eye(reps, dtype=F32),
                      w_spatial[0][:, :ts, :ts]).reshape(MLP_GROUPS, CHUNK, CHUNK)
    bias_s = jnp.tile(bias_p[:ts], (reps, 1))

    xp = x_prompt.reshape(bp * tp, D_MODEL)
    xs = x_sample.reshape(bs * ts, D_MODEL)
    g_mix, g_ffn, g_fin = row(norm_mix_g[0]), row(norm_ffn_g[0]), row(norm_final_g)
    lng, lnb = row(ln_v_g[0]), row(ln_v_b[0])

    zp = _inproj(xp, mod, g_mix, w_in[0], False, tp, seq_row0)
    oa_p, st_p = _wkv(zp, vecs, mats, bp, tp, WKV_ROWS)
    (ub_p,) = _gmlp(zp, lng, lnb, ws_p, bias_p, False)
    x1_p = _merge(oa_p, ub_p, zp, xp, mod, wa_b, wb_b, wo_b, False, tp, seq_row0)
    y_p = _ffn(x1_p, mod, g_ffn, g_fin, wfi_b, wfo_b, False, tp, seq_row0)

    zs = _inproj(xs, mod, g_mix, w_in[0], True, ts, seq_row0)
    prev_rows = jnp.repeat(state_shift[0], ts, axis=0)
    oa_s, st_s = _wkv(zs, vecs, mats, bs, ts, ts, prev_rows=prev_rows, state_in=_to_wide(state_wkv[0]))
    ub_s, vn_s = _gmlp(zs, lng, lnb, ws_s, bias_s, True)
    x1_s = _merge(oa_s, ub_s, zs, xs, mod, wa_b, wb_b, wo_b, True, ts, seq_row0)
    y_s = _ffn(x1_s, mod, g_ffn, g_fin, wfi_b, wfo_b, True, ts, seq_row0)

    shift_p = zp.reshape(bp, tp, Z_W)[:, -1, Z_R:Z_R + C_SHIFT]
    shift_s = zs.reshape(bs, ts, Z_W)[:, -1, Z_R:Z_R + C_SHIFT]
    return (y_p.reshape(bp, tp, D_MODEL),
            y_s.reshape(bs, ts, D_MODEL),
            _from_wide(st_p)[None],
            shift_p[None],
            _from_wide(st_s)[None],
            shift_s[None],
            vn_s.reshape(bs, ts, MLP_W)[None])
```

```python
import functools
import math

import jax
import jax.numpy as jnp
from jax import lax
from jax.experimental import pallas as pl
from jax.experimental.pallas import tpu as pltpu

F32 = jnp.float32
BF16 = jnp.bfloat16

D_MODEL = 2048
HEAD = 64
RW = D_MODEL // 2
N_HEADS = RW // HEAD
DECAY_LORA = D_MODEL // 32
AAA_LORA = D_MODEL // 32
GATE_LORA = D_MODEL // 16
LORA = DECAY_LORA + AAA_LORA + GATE_LORA
CHUNK = 128
MLP_W = D_MODEL // 2
MLP_GROUPS = 8
MLP_GD = MLP_W // MLP_GROUPS
D_FF = ((-(-8 * D_MODEL // 3)) + 255) // 256 * 256
C_SHIFT = 3 * RW + LORA
C_IN = C_SHIFT + 2 * MLP_W + 2 * D_MODEL
NORM_EPS = 1e-6
GN_EPS = 64e-5
LN_EPS = 1e-5

Z_GA, Z_GB, Z_U, Z_V, Z_R, Z_K, Z_VR, Z_L = 0, 2048, 4096, 5120, 6144, 7168, 8192, 9216
Z_TN = 512
Z_W = -(-C_IN // Z_TN) * Z_TN

GRP = 256
HPG = GRP // HEAD
N_GRP = RW // GRP
WKV_ROWS = 64
VMEM_LIMIT = 56 * 1024 * 1024


def _dot(a, b):
    return jnp.dot(a, b, preferred_element_type=F32)


def _dot_nt(a, b):
    return lax.dot_general(a, b, (((1,), (1,)), ((), ())), preferred_element_type=F32)


def _dot_tn(a, b):
    return lax.dot_general(a, b, (((0,), (0,)), ((), ())), preferred_element_type=F32)


def _rms(x, g):
    return x * lax.rsqrt(jnp.mean(x * x, -1, keepdims=True) + NORM_EPS) * g


def _split2(x):
    hi = x.astype(BF16)
    lo = (x - hi.astype(F32)).astype(BF16)
    return hi, lo


def _split3(x):
    hi = x.astype(BF16)
    r1 = x - hi.astype(F32)
    mid = r1.astype(BF16)
    lo = (r1 - mid.astype(F32)).astype(BF16)
    return hi, mid, lo


def _params(sem):
    return pltpu.CompilerParams(dimension_semantics=sem, vmem_limit_bytes=VMEM_LIMIT)


def _ada_kernel(c_ref, w_ref, b_ref, o_ref, s_ref, e_ref, *, n_rep, rep):
    m_out, m_in = e_ref.shape

    @pl.when(pl.program_id(0) == 0)
    def _():
        c = c_ref[...]
        s_ref[...] = (c * jax.nn.sigmoid(c)).astype(BF16)
        r = lax.broadcasted_iota(jnp.int32, (m_out, m_in), 0)
        s = lax.broadcasted_iota(jnp.int32, (m_out, m_in), 1)
        src = jnp.where(r < n_rep * rep, r // rep, r - n_rep * (rep - 1))
        e_ref[...] = jnp.where(s == src, 1.0, 0.0).astype(BF16)

    mod = _dot(s_ref[...], w_ref[...].astype(BF16)) + b_ref[...]
    hi, mid, lo = _split3(mod)
    e = e_ref[...]
    o_ref[...] = _dot(e, hi) + _dot(e, mid) + _dot(e, lo)


def _ada(c, w_ada, b_ada, n_rep, rep):
    m_in, n, tn = c.shape[0], w_ada.shape[1], 512
    m_out = m_in + n_rep * (rep - 1)
    return pl.pallas_call(
        functools.partial(_ada_kernel, n_rep=n_rep, rep=rep),
        grid=(n // tn,),
        in_specs=[pl.BlockSpec((m_in, D_MODEL), lambda j: (0, 0)),
                  pl.BlockSpec((D_MODEL, tn), lambda j: (0, j)),
                  pl.BlockSpec((1, tn), lambda j: (0, j))],
        out_specs=pl.BlockSpec((m_out, tn), lambda j: (0, j)),
        out_shape=jax.ShapeDtypeStruct((m_out, n), F32),
        scratch_shapes=[pltpu.VMEM((m_in, D_MODEL), BF16), pltpu.VMEM((m_out, m_in), BF16)],
        compiler_params=_params(("arbitrary",)),
        name="ada",
    )(c, w_ada, b_ada.reshape(1, n))


MOD_SHIFT_M, MOD_SCALE_M, MOD_GATE_M, MOD_SHIFT_F, MOD_SCALE_F, MOD_GATE_F = range(6)


def _mod_spec(k, per_row, tm, rows_per_seq):
    if per_row:
        return pl.BlockSpec((tm, D_MODEL), lambda i, *_: (i, k))
    return pl.BlockSpec((None, 1, D_MODEL), lambda i, *_: (i * tm // rows_per_seq, 0, k))


ZB = 256
_N_RW_B, _N_MLP_B, _N_GATE_B = C_SHIFT // ZB, 2 * MLP_W // ZB, 2 * D_MODEL // ZB


def _w_in_block(zb):
    return jnp.where(zb < _N_GATE_B, zb + _N_RW_B + _N_MLP_B,
                     jnp.where(zb < _N_GATE_B + _N_MLP_B, zb - _N_GATE_B + _N_RW_B,
                               jnp.minimum(zb - _N_GATE_B - _N_MLP_B, _N_RW_B - 1)))


def _inproj_kernel(x_ref, sc_ref, sh_ref, g_ref, wlo_ref, whi_ref, o_ref, h_ref):
    @pl.when(pl.program_id(1) == 0)
    def _():
        h = _rms(x_ref[...], g_ref[...]) * (1.0 + sc_ref[...]) + sh_ref[...]
        h_ref[...] = h.astype(BF16)

    h = h_ref[...]
    o_ref[:, :ZB] = _dot(h, wlo_ref[...].astype(BF16))
    o_ref[:, ZB:] = _dot(h, whi_ref[...].astype(BF16))


def _inproj(x, mod, g, w, per_row, rows_per_seq):
    m, tm = x.shape[0], 1024
    assert Z_TN == 2 * ZB
    mspec = lambda k: _mod_spec(k, per_row, tm, rows_per_seq)
    return pl.pallas_call(
        _inproj_kernel,
        grid=(m // tm, Z_W // Z_TN),
        in_specs=[pl.BlockSpec((tm, D_MODEL), lambda i, j: (i, 0)),
                  mspec(MOD_SCALE_M), mspec(MOD_SHIFT_M),
                  pl.BlockSpec((1, D_MODEL), lambda i, j: (0, 0)),
                  pl.BlockSpec((D_MODEL, ZB), lambda i, j: (0, _w_in_block(2 * j))),
                  pl.BlockSpec((D_MODEL, ZB), lambda i, j: (0, _w_in_block(2 * j + 1)))],
        out_specs=pl.BlockSpec((tm, Z_TN), lambda i, j: (i, j)),
        out_shape=jax.ShapeDtypeStruct((m, Z_W), F32),
        scratch_shapes=[pltpu.VMEM((tm, D_MODEL), BF16)],
        compiler_params=_params(("arbitrary", "arbitrary")),
        name="inproj",
    )(x, mod, mod, g, w, w)


def _bd(w, mask):
    return jnp.where(mask, jnp.concatenate([w] * HPG, axis=0), 0.0).astype(BF16)


def _bd_heads(nat, mask):
    return jnp.where(mask, jnp.concatenate([nat] * HPG, axis=1), 0.0).astype(BF16)


def _diag_blocks(m):
    return jnp.concatenate([m[h * HEAD:(h + 1) * HEAD, h * HEAD:(h + 1) * HEAD] for h in range(HPG)],
                           axis=0)


def _wkv_kernel(*refs, lb, carried):
    R = WKV_ROWS
    if carried:
        (zr_ref, zk_ref, zv_ref, zl_ref,
         mur_ref, muk_ref, muv_ref, mul_ref,
         w0_ref, a0_ref, kkp_ref, kap_ref, rkp_ref, gng_ref, gnb_ref,
         wd_ref, wa_ref, wg_ref,
         oa_ref, so_ref,
         s_ref, cr_ref, ck_ref, cv_ref, cl_ref) = refs
    else:
        (zr_ref, zk_ref, zv_ref, zl_ref,
         pr_ref, pk_ref, pv_ref, pl_ref, si_ref,
         mur_ref, muk_ref, muv_ref, mul_ref,
         w0_ref, a0_ref, kkp_ref, kap_ref, rkp_ref, gng_ref, gnb_ref,
         wd_ref, wa_ref, wg_ref,
         oa_ref, so_ref) = refs
    nblk = R // lb

    if carried:
        @pl.when(pl.program_id(1) == 0)
        def _():
            s_ref[...] = jnp.zeros_like(s_ref)
            cr_ref[...] = jnp.zeros_like(cr_ref)
            ck_ref[...] = jnp.zeros_like(ck_ref)
            cv_ref[...] = jnp.zeros_like(cv_ref)
            cl_ref[...] = jnp.zeros_like(cl_ref)

    row1 = lax.broadcasted_iota(jnp.int32, (R, 1), 0)
    first = (row1 % lb) == 0

    def shift(z_ref, prev0, mu_ref):
        z = z_ref[...]
        prev = jnp.where(first, prev0, pltpu.roll(z, 1, axis=0))
        return z + (prev - z) * mu_ref[...]

    if carried:
        r = shift(zr_ref, cr_ref[...], mur_ref)
        k = shift(zk_ref, ck_ref[...], muk_ref)
        v = shift(zv_ref, cv_ref[...], muv_ref)
        l = shift(zl_ref, cl_ref[...], mul_ref)
        cr_ref[...] = zr_ref[R - 1:R, :]
        ck_ref[...] = zk_ref[R - 1:R, :]
        cv_ref[...] = zv_ref[R - 1:R, :]
        cl_ref[...] = zl_ref[R - 1:R, :]
    else:
        r = shift(zr_ref, pr_ref[...], mur_ref)
        k = shift(zk_ref, pk_ref[...], muk_ref)
        v = shift(zv_ref, pv_ref[...], muv_ref)
        l = shift(zl_ref, pl_ref[...], mul_ref)

    w_raw = w0_ref[...] + _dot(jnp.tanh(l).astype(BF16), wd_ref[...])
    logw = jax.nn.sigmoid(w_raw) * (-math.exp(-0.5))
    a = jax.nn.sigmoid(a0_ref[...] + _dot(l.astype(BF16), wa_ref[...]))
    g = _dot(jax.nn.sigmoid(l).astype(BF16), wg_ref[...])
    kk = k * kkp_ref[...]
    k2 = k * (1.0 + (a - 1.0) * kap_ref[...])

    ri = lax.broadcasted_iota(jnp.int32, (GRP, GRP), 0)
    ci = lax.broadcasted_iota(jnp.int32, (GRP, GRP), 1)
    bdm = (ri // HEAD) == (ci // HEAD)
    ones_bd = jnp.where(bdm, 1.0, 0.0).astype(BF16)
    tr = lax.broadcasted_iota(jnp.int32, (R, R), 0)
    tc = lax.broadcasted_iota(jnp.int32, (R, R), 1)
    same = (tr // lb) == (tc // lb)
    incl01 = jnp.where(same & (tc <= tr), 1.0, 0.0).astype(BF16)
    wr = lax.broadcasted_iota(jnp.int32, (R, GRP), 0)
    wc = lax.broadcasted_iota(jnp.int32, (R, GRP), 1) % HEAD
    wsame = (wr // lb) == (wc // lb)
    strict_w = wsame & (wc < wr)
    incl_w = wsame & (wc <= wr)
    eye_w = jnp.where(wc == wr, 1.0, 0.0)

    def seg_sums(xs):
        parts = []
        for x in xs:
            parts.extend(_split2(x))
        stacked = jnp.concatenate(parts, axis=0)
        cols = [_dot(stacked[:, gi * GRP:(gi + 1) * GRP], ones_bd) for gi in range(N_GRP)]
        full = jnp.concatenate(cols, axis=1)
        return [full[2 * i * R:(2 * i + 1) * R] + full[(2 * i + 1) * R:(2 * i + 2) * R]
                for i in range(len(xs))]

    ss, rk = seg_sums([kk * kk, r * k2 * rkp_ref[...]])
    kk = kk / jnp.maximum(jnp.sqrt(ss), 1e-12)
    bonus = rk * v
    av = -kk
    bv = kk * a

    def cum(m01, x):
        hi, mid, lo = _split3(x)
        return _dot(m01, hi) + _dot(m01, mid) + _dot(m01, lo)

    cl = cum(incl01, logw)
    if nblk == 1:
        cl_end = cl[R - 1:R, :]
    else:
        cl_end = cum(jnp.where(same, 1.0, 0.0).astype(BF16), logw)
    e_neg = jnp.exp(-cl)
    e_end = jnp.exp(cl_end - cl)
    at = av * jnp.exp(cl - logw)
    rt = r * jnp.exp(cl)
    bt = bv * e_neg
    kt = k2 * e_neg
    bh = bv * e_end
    kh = k2 * e_end
    p_end = jnp.exp(cl_end)

    grps = range(N_GRP)
    sls = [slice(gi * GRP, (gi + 1) * GRP) for gi in grps]
    blks = [slice(b * lb, (b + 1) * lb) for b in range(nblk)]
    v_g = [v[:, sl] for sl in sls]
    lhs = [jnp.concatenate([at[:, sl], rt[:, sl]], axis=0).astype(BF16) for sl in sls]
    o1 = [_dot_nt(lhs[gi], jnp.concatenate([_bd(bt[:, sls[gi]], bdm), _bd(kt[:, sls[gi]], bdm)], axis=0))
          for gi in grps]
    w_ab = [jnp.where(strict_w, o[:R, :GRP], 0.0) for o in o1]
    w_ak = [jnp.where(strict_w, o[:R, GRP:], 0.0) for o in o1]
    w_rb = [jnp.where(incl_w, o[R:, :GRP], 0.0) for o in o1]
    w_rk = [jnp.where(incl_w, o[R:, GRP:], 0.0) for o in o1]

    def state_part(gi):
        if carried:
            o2 = _dot_nt(lhs[gi], s_ref[gi].astype(BF16))
            return o2[:R], o2[R:]
        a_rows, r_rows = [], []
        for b, rows in enumerate(blks):
            lhs_b = jnp.concatenate([at[rows, sls[gi]], rt[rows, sls[gi]]], axis=0).astype(BF16)
            o2 = _dot_nt(lhs_b, _bd_heads(si_ref[b, sls[gi], :], bdm))
            a_rows.append(o2[:lb])
            r_rows.append(o2[lb:])
        return jnp.concatenate(a_rows, axis=0), jnp.concatenate(r_rows, axis=0)

    x_w = list(w_ab)
    t_w = [eye_w + x for x in x_w]
    n_lvl = max(1, int(math.log2(lb)))
    a_s = r_s = o3 = None
    for j in range(n_lvl):
        for gi in grps:
            y_bd = _bd(x_w[gi], bdm)
            if j == 0:
                x_w[gi] = _dot(x_w[gi].astype(BF16), y_bd)
            elif j == n_lvl - 1:
                t_w[gi] = t_w[gi] + _dot(t_w[gi].astype(BF16), y_bd)
            else:
                res = _dot(jnp.concatenate([t_w[gi], x_w[gi]], axis=0).astype(BF16), y_bd)
                t_w[gi] = t_w[gi] + res[:R]
                x_w[gi] = res[R:]
        if j == 0:
            parts = [state_part(gi) for gi in grps]
            a_s, r_s = [p[0] for p in parts], [p[1] for p in parts]
        elif j == 1:
            o3 = [_dot(jnp.concatenate([w_ak[gi], w_rk[gi]], axis=0).astype(BF16), _bd(v_g[gi], bdm))
                  for gi in grps]

    u = [_dot(t_w[gi].astype(BF16), _bd(a_s[gi] + o3[gi][:R], bdm)) for gi in grps]
    o_parts = [r_s[gi] + o3[gi][R:] + _dot(w_rb[gi].astype(BF16), _bd(u[gi], bdm)) for gi in grps]

    for gi in grps:
        sl = sls[gi]
        if carried:
            upd = _dot_tn(jnp.concatenate([u[gi], v_g[gi]], axis=0).astype(BF16),
                          jnp.concatenate([bh[:, sl], kh[:, sl]], axis=0).astype(BF16))
            s_new = jnp.where(bdm, s_ref[gi] * p_end[:, sl] + upd, 0.0)
            s_ref[gi] = s_new
            so_ref[sl, :] = _diag_blocks(s_new)
        else:
            for b, rows in enumerate(blks):
                upd = _dot_tn(jnp.concatenate([u[gi][rows], v_g[gi][rows]], axis=0).astype(BF16),
                              jnp.concatenate([bh[rows, sl], kh[rows, sl]], axis=0).astype(BF16))
                p_b = p_end[b * lb:b * lb + 1, sl]
                p_nat = jnp.concatenate(
                    [jnp.broadcast_to(p_b[:, h * HEAD:(h + 1) * HEAD], (HEAD, HEAD)) for h in range(HPG)],
                    axis=0)
                so_ref[b, sl, :] = si_ref[b, sl, :] * p_nat + _diag_blocks(upd)

    o = jnp.concatenate(o_parts, axis=1)
    (mu,) = seg_sums([o])
    dlt = o - mu * (1.0 / HEAD)
    (var,) = seg_sums([dlt * dlt])
    on = dlt * lax.rsqrt(var * (1.0 / HEAD) + GN_EPS) * gng_ref[...] + gnb_ref[...]
    oa_ref[...] = ((on + bonus) * g).astype(BF16)


def _wkv(z, vecs, mats, n_seq, seq_len, lb, prev_rows=None, state_in=None):
    R = WKV_ROWS
    carried = prev_rows is None
    rows = z.shape[0]
    if carried:
        steps = seq_len // R
        grid = (n_seq, steps)
        rmap = lambda c: (lambda b, t: (b * steps + t, c))
        cmap = lambda b, t: (0, 0)
        seq_per_step = 1
        smap = lambda b, t: (b, 0, 0)
        sem = ("arbitrary", "arbitrary")
    else:
        grid = (rows // R,)
        rmap = lambda c: (lambda i: (i, c))
        cmap = lambda i: (0, 0)
        seq_per_step = R // lb
        smap = lambda i: (i, 0, 0)
        sem = ("arbitrary",)

    zspecs = [pl.BlockSpec((R, RW), rmap(Z_R // RW)),
              pl.BlockSpec((R, RW), rmap(Z_K // RW)),
              pl.BlockSpec((R, RW), rmap(Z_VR // RW)),
              pl.BlockSpec((R, LORA), rmap(Z_L // LORA))]
    in_specs = list(zspecs)
    args = [z, z, z, z]
    if not carried:
        in_specs += [pl.BlockSpec((R, RW), rmap(0)), pl.BlockSpec((R, RW), rmap(1)),
                     pl.BlockSpec((R, RW), rmap(2)), pl.BlockSpec((R, LORA), rmap(3 * RW // LORA)),
                     pl.BlockSpec((seq_per_step, RW, HEAD), smap)]
        args += [prev_rows] * 4 + [state_in]
    for a in vecs + mats:
        in_specs.append(pl.BlockSpec(a.shape, cmap))
        args.append(a)

    scratch = []
    if carried:
        scratch = [pltpu.VMEM((N_GRP, GRP, GRP), F32), pltpu.VMEM((1, RW), F32),
                   pltpu.VMEM((1, RW), F32), pltpu.VMEM((1, RW), F32), pltpu.VMEM((1, LORA), F32)]
    n_state = n_seq
    return pl.pallas_call(
        functools.partial(_wkv_kernel, lb=lb, carried=carried),
        grid=grid,
        in_specs=in_specs,
        out_specs=[pl.BlockSpec((R, RW), rmap(0)),
                   pl.BlockSpec((None, RW, HEAD), smap) if carried
                   else pl.BlockSpec((seq_per_step, RW, HEAD), smap)],
        out_shape=[jax.ShapeDtypeStruct((rows, RW), BF16),
                   jax.ShapeDtypeStruct((n_state, RW, HEAD), F32)],
        scratch_shapes=scratch,
        compiler_params=_params(sem),
        name="wkv_carried" if carried else "wkv_blocks",
    )(*args)


def _gmlp_kernel(zu_ref, zv_ref, lng_ref, lnb_ref, ws_ref, bias_ref, ub_ref, vn_ref=None):
    u = jax.nn.gelu(zu_ref[...])
    vg = jax.nn.gelu(zv_ref[...])
    mu = jnp.mean(vg, -1, keepdims=True)
    var = jnp.mean(jnp.square(vg - mu), -1, keepdims=True)
    vn = (vg - mu) * lax.rsqrt(var + LN_EPS) * lng_ref[...] + lnb_ref[...]
    if vn_ref is not None:
        vn_ref[...] = vn
    tr = lax.broadcasted_iota(jnp.int32, (CHUNK, CHUNK), 0)
    tc = lax.broadcasted_iota(jnp.int32, (CHUNK, CHUNK), 1)
    causal = tc <= tr
    vb = vn.astype(BF16)
    cols = []
    for gi in range(MLP_GROUPS):
        wsg = jnp.where(causal, ws_ref[gi], 0.0).astype(BF16)
        cols.append(_dot(wsg, vb[:, gi * MLP_GD:(gi + 1) * MLP_GD]))
    mixed = jnp.concatenate(cols, axis=1) + bias_ref[...]
    ub_ref[...] = (u * mixed).astype(BF16)


def _gmlp(z, ln_g, ln_b, ws, bias, want_vn):
    rows = z.shape[0]
    n_out = 2 if want_vn else 1
    return pl.pallas_call(
        _gmlp_kernel,
        grid=(rows // CHUNK,),
        in_specs=[pl.BlockSpec((CHUNK, MLP_W), lambda i: (i, Z_U // MLP_W)),
                  pl.BlockSpec((CHUNK, MLP_W), lambda i: (i, Z_V // MLP_W)),
                  pl.BlockSpec((1, MLP_W), lambda i: (0, 0)),
                  pl.BlockSpec((1, MLP_W), lambda i: (0, 0)),
                  pl.BlockSpec((MLP_GROUPS, CHUNK, CHUNK), lambda i: (0, 0, 0)),
                  pl.BlockSpec((CHUNK, MLP_W), lambda i: (0, 0))],
        out_specs=[pl.BlockSpec((CHUNK, MLP_W), lambda i: (i, 0)),
                   pl.BlockSpec((CHUNK, MLP_W), lambda i: (i, 0))][:n_out],
        out_shape=[jax.ShapeDtypeStruct((rows, MLP_W), BF16),
                   jax.ShapeDtypeStruct((rows, MLP_W), F32)][:n_out],
        compiler_params=_params(("arbitrary",)),
        name="gmlp",
    )(z, z, ln_g, ln_b, ws, bias)


def _merge_kernel(oa_ref, ub_ref, zga_ref, zgb_ref, x_ref, gm_ref, sc_ref, sh_ref, g_ref,
                  wa_ref, wb_ref, wo_ref, x1_ref, h2_ref):
    ya = _dot(oa_ref[...], wa_ref[...])
    yb = _dot(ub_ref[...], wb_ref[...])
    merged = jax.nn.sigmoid(zga_ref[...]) * ya + jax.nn.sigmoid(zgb_ref[...]) * yb
    mix = _dot(merged.astype(BF16), wo_ref[...])
    x1 = x_ref[...] + gm_ref[...] * mix
    x1_ref[...] = x1
    h2_ref[...] = (_rms(x1, g_ref[...]) * (1.0 + sc_ref[...]) + sh_ref[...]).astype(BF16)


def _merge(oa, ub, z, x, mod, g_ffn, wa, wb, wo, per_row, rows_per_seq):
    m, tm = x.shape[0], (128 if per_row else 256)
    const = lambda shape: pl.BlockSpec(shape, lambda i: (0, 0), pipeline_mode=pl.Buffered(1))
    mspec = lambda k: _mod_spec(k, per_row, tm, rows_per_seq)
    rows = lambda w, c: pl.BlockSpec((tm, w), lambda i: (i, c))
    return pl.pallas_call(
        _merge_kernel,
        grid=(m // tm,),
        in_specs=[rows(RW, 0), rows(MLP_W, 0),
                  rows(D_MODEL, Z_GA // D_MODEL), rows(D_MODEL, Z_GB // D_MODEL), rows(D_MODEL, 0),
                  mspec(MOD_GATE_M), mspec(MOD_SCALE_F), mspec(MOD_SHIFT_F),
                  pl.BlockSpec((1, D_MODEL), lambda i: (0, 0)),
                  const((RW, D_MODEL)), const((MLP_W, D_MODEL)), const((D_MODEL, D_MODEL))],
        out_specs=[rows(D_MODEL, 0), rows(D_MODEL, 0)],
        out_shape=[jax.ShapeDtypeStruct((m, D_MODEL), F32), jax.ShapeDtypeStruct((m, D_MODEL), BF16)],
        compiler_params=_params(("arbitrary",)),
        name="merge",
    )(oa, ub, z, z, x, mod, mod, mod, g_ffn, wa, wb, wo)


def _ffn_kernel(h_ref, x_ref, gf_ref, gfin_ref, wg_ref, wu_ref, wo_ref, y_ref):
    j = pl.program_id(1)

    @pl.when(j == 0)
    def _():
        y_ref[...] = jnp.zeros_like(y_ref)

    hb = h_ref[...]
    gt = _dot(hb, wg_ref[...])
    up = _dot(hb, wu_ref[...])
    act = gt * jax.nn.sigmoid(gt) * up
    y_ref[...] += _dot(act.astype(BF16), wo_ref[...])

    @pl.when(j == pl.num_programs(1) - 1)
    def _():
        x2 = x_ref[...] + gf_ref[...] * y_ref[...]
        y_ref[...] = _rms(x2, gfin_ref[...])


def _ffn(h2, x1, mod, g_final, w_in, w_out, per_row, rows_per_seq):
    m, tm, tf = x1.shape[0], 512, 512
    nf = D_FF // tf
    return pl.pallas_call(
        _ffn_kernel,
        grid=(m // tm, nf),
        in_specs=[pl.BlockSpec((tm, D_MODEL), lambda i, j: (i, 0)),
                  pl.BlockSpec((tm, D_MODEL), lambda i, j: (i, 0)),
                  _mod_spec(MOD_GATE_F, per_row, tm, rows_per_seq),
                  pl.BlockSpec((1, D_MODEL), lambda i, j: (0, 0)),
                  pl.BlockSpec((D_MODEL, tf), lambda i, j: (0, j)),
                  pl.BlockSpec((D_MODEL, tf), lambda i, j: (0, nf + j)),
                  pl.BlockSpec((tf, D_MODEL), lambda i, j: (j, 0))],
        out_specs=pl.BlockSpec((tm, D_MODEL), lambda i, j: (i, 0)),
        out_shape=jax.ShapeDtypeStruct((m, D_MODEL), F32),
        compiler_params=_params(("arbitrary", "arbitrary")),
        name="ffn",
    )(h2, x1, mod, g_final, w_in, w_in, w_out)


def kernel(x_prompt, x_sample, state_wkv, state_shift, c_prompt, c_sample, w_ada, b_ada, norm_mix_g, w_in, mu_shift, w0, w_decay_up, a0, w_aaa_up, w_gate_up, k_k, k_a, r_k, gn_g, gn_b, ln_v_g, ln_v_b, w_spatial, b_spatial, w_branch_a, w_branch_b, w_out, norm_ffn_g, w_ffn_in, w_ffn_out, norm_final_g):
    assert w_ada.shape[0] == 1, "single layer"
    bp, tp, _ = x_prompt.shape
    bs, ts, _ = x_sample.shape
    assert tp % CHUNK == 0 and WKV_ROWS % ts == 0 and CHUNK % ts == 0

    c_all = jnp.concatenate([c_sample, c_prompt], axis=0)
    c_all = jnp.pad(c_all, ((0, -c_all.shape[0] % 8), (0, 0)))
    mod_s = _ada(c_all, w_ada[0], b_ada[0], bs, ts)
    mod_p = mod_s[bs * ts:bs * ts + bp].reshape(bp, 1, 6 * D_MODEL)

    wa_b, wb_b, wo_b = w_branch_a[0].astype(BF16), w_branch_b[0].astype(BF16), w_out[0].astype(BF16)
    wfi_b, wfo_b = w_ffn_in[0].astype(BF16), w_ffn_out[0].astype(BF16)

    row = lambda a: a.reshape(1, -1)
    mu = mu_shift[0]
    vecs = [row(mu[:RW]), row(mu[RW:2 * RW]), row(mu[2 * RW:3 * RW]), row(mu[3 * RW:]),
            row(w0[0]), row(a0[0]), row(k_k[0]), row(k_a[0]), row(r_k[0]), row(gn_g[0]), row(gn_b[0])]
    zpad = lambda w, lo: jnp.pad(w, ((lo, LORA - lo - w.shape[0]), (0, 0))).astype(BF16)
    mats = [zpad(w_decay_up[0], 0), zpad(w_aaa_up[0], DECAY_LORA),
            zpad(w_gate_up[0], DECAY_LORA + AAA_LORA)]

    bias_p = jnp.repeat(b_spatial[0].T, MLP_GD, axis=1)
    ws_p = w_spatial[0]
    reps = CHUNK // ts
    ws_s = jnp.einsum("ab,gts->gatbs", jnp.eye(reps, dtype=F32),
                      w_spatial[0][:, :ts, :ts]).reshape(MLP_GROUPS, CHUNK, CHUNK)
    bias_s = jnp.tile(bias_p[:ts], (reps, 1))

    xp = x_prompt.reshape(bp * tp, D_MODEL)
    xs = x_sample.reshape(bs * ts, D_MODEL)
    g_mix, g_ffn, g_fin = row(norm_mix_g[0]), row(norm_ffn_g[0]), row(norm_final_g)
    lng, lnb = row(ln_v_g[0]), row(ln_v_b[0])

    zp = _inproj(xp, mod_p, g_mix, w_in[0], False, tp)
    oa_p, st_p = _wkv(zp, vecs, mats, bp, tp, WKV_ROWS)
    (ub_p,) = _gmlp(zp, lng, lnb, ws_p, bias_p, False)
    x1_p, h2_p = _merge(oa_p, ub_p, zp, xp, mod_p, g_ffn, wa_b, wb_b, wo_b, False, tp)
    y_p = _ffn(h2_p, x1_p, mod_p, g_fin, wfi_b, wfo_b, False, tp)

    zs = _inproj(xs, mod_s, g_mix, w_in[0], True, ts)
    prev_rows = jnp.repeat(state_shift[0], ts, axis=0)
    oa_s, st_s = _wkv(zs, vecs, mats, bs, ts, ts, prev_rows=prev_rows,
                      state_in=state_wkv[0].reshape(bs, RW, HEAD))
    ub_s, vn_s = _gmlp(zs, lng, lnb, ws_s, bias_s, True)
    x1_s, h2_s = _merge(oa_s, ub_s, zs, xs, mod_s, g_ffn, wa_b, wb_b, wo_b, True, ts)
    y_s = _ffn(h2_s, x1_s, mod_s, g_fin, wfi_b, wfo_b, True, ts)

    shift_p = zp.reshape(bp, tp, Z_W)[:, -1, Z_R:Z_R + C_SHIFT]
    shift_s = zs.reshape(bs, ts, Z_W)[:, -1, Z_R:Z_R + C_SHIFT]
    return (y_p.reshape(bp, tp, D_MODEL),
            y_s.reshape(bs, ts, D_MODEL),
            st_p.reshape(1, bp, N_HEADS, HEAD, HEAD),
            shift_p[None],
            st_s.reshape(1, bs, N_HEADS, HEAD, HEAD),
            shift_s[None],
            vn_s.reshape(bs, ts, MLP_W)[None])
```

```python
import functools
import math

import jax
import jax.numpy as jnp
from jax import lax
from jax.experimental import pallas as pl
from jax.experimental.pallas import tpu as pltpu

F32 = jnp.float32
BF16 = jnp.bfloat16

D_MODEL = 2048
HEAD = 64
RW = D_MODEL // 2
N_HEADS = RW // HEAD
DECAY_LORA = D_MODEL // 32
AAA_LORA = D_MODEL // 32
GATE_LORA = D_MODEL // 16
LORA = DECAY_LORA + AAA_LORA + GATE_LORA
CHUNK = 128
MLP_W = D_MODEL // 2
MLP_GROUPS = 8
MLP_GD = MLP_W // MLP_GROUPS
D_FF = ((-(-8 * D_MODEL // 3)) + 255) // 256 * 256
C_SHIFT = 3 * RW + LORA
C_IN = C_SHIFT + 2 * MLP_W + 2 * D_MODEL
NORM_EPS = 1e-6
GN_EPS = 64e-5
LN_EPS = 1e-5

Z_GA, Z_GB, Z_U, Z_V, Z_R, Z_K, Z_VR, Z_L = 0, 2048, 4096, 5120, 6144, 7168, 8192, 9216
Z_TN = 512
Z_W = -(-C_IN // Z_TN) * Z_TN

GRP = 256
HPG = GRP // HEAD
N_GRP = RW // GRP
WKV_ROWS = 64
VMEM_LIMIT = 56 * 1024 * 1024


def _dot(a, b):
    return jnp.dot(a, b, preferred_element_type=F32)


def _dot_nt(a, b):
    return lax.dot_general(a, b, (((1,), (1,)), ((), ())), preferred_element_type=F32)


def _dot_tn(a, b):
    return lax.dot_general(a, b, (((0,), (0,)), ((), ())), preferred_element_type=F32)


def _rms(x, g):
    return x * lax.rsqrt(jnp.mean(x * x, -1, keepdims=True) + NORM_EPS) * g


def _split2(x):
    hi = x.astype(BF16)
    lo = (x - hi.astype(F32)).astype(BF16)
    return hi, lo


def _split3(x):
    hi = x.astype(BF16)
    r1 = x - hi.astype(F32)
    mid = r1.astype(BF16)
    lo = (r1 - mid.astype(F32)).astype(BF16)
    return hi, mid, lo


def _params(sem):
    return pltpu.CompilerParams(dimension_semantics=sem, vmem_limit_bytes=VMEM_LIMIT)


def _ada_kernel(c_ref, w_ref, b_ref, o_ref, s_ref, e_ref, *, n_rep, rep):
    m_out, m_in = e_ref.shape

    @pl.when(pl.program_id(0) == 0)
    def _():
        c = c_ref[...]
        s_ref[...] = (c * jax.nn.sigmoid(c)).astype(BF16)
        r = lax.broadcasted_iota(jnp.int32, (m_out, m_in), 0)
        s = lax.broadcasted_iota(jnp.int32, (m_out, m_in), 1)
        src = jnp.where(r < n_rep * rep, r // rep, r - n_rep * (rep - 1))
        e_ref[...] = jnp.where(s == src, 1.0, 0.0).astype(BF16)

    mod = _dot(s_ref[...], w_ref[...].astype(BF16)) + b_ref[...]
    hi, mid, lo = _split3(mod)
    e = e_ref[...]
    o_ref[...] = _dot(e, hi) + _dot(e, mid) + _dot(e, lo)


def _ada(c, w_ada, b_ada, n_rep, rep):
    m_in, n, tn = c.shape[0], w_ada.shape[1], 512
    m_out = m_in + n_rep * (rep - 1)
    return pl.pallas_call(
        functools.partial(_ada_kernel, n_rep=n_rep, rep=rep),
        grid=(n // tn,),
        in_specs=[pl.BlockSpec((m_in, D_MODEL), lambda j: (0, 0)),
                  pl.BlockSpec((D_MODEL, tn), lambda j: (0, j)),
                  pl.BlockSpec((1, tn), lambda j: (0, j))],
        out_specs=pl.BlockSpec((m_out, tn), lambda j: (0, j)),
        out_shape=jax.ShapeDtypeStruct((m_out, n), F32),
        scratch_shapes=[pltpu.VMEM((m_in, D_MODEL), BF16), pltpu.VMEM((m_out, m_in), BF16)],
        compiler_params=_params(("arbitrary",)),
        name="ada",
    )(c, w_ada, b_ada.reshape(1, n))


MOD_SHIFT_M, MOD_SCALE_M, MOD_GATE_M, MOD_SHIFT_F, MOD_SCALE_F, MOD_GATE_F = range(6)


def _mod_spec(k, per_row, tm, rows_per_seq):
    if per_row:
        return pl.BlockSpec((tm, D_MODEL), lambda i, *_: (i, k))
    return pl.BlockSpec((None, 1, D_MODEL), lambda i, *_: (i * tm // rows_per_seq, 0, k))


ZB = 256
_N_RW_B, _N_MLP_B, _N_GATE_B = C_SHIFT // ZB, 2 * MLP_W // ZB, 2 * D_MODEL // ZB


def _w_in_block(zb):
    return jnp.where(zb < _N_GATE_B, zb + _N_RW_B + _N_MLP_B,
                     jnp.where(zb < _N_GATE_B + _N_MLP_B, zb - _N_GATE_B + _N_RW_B,
                               jnp.minimum(zb - _N_GATE_B - _N_MLP_B, _N_RW_B - 1)))


def _inproj_kernel(x_ref, sc_ref, sh_ref, g_ref, wlo_ref, whi_ref, o_ref, h_ref):
    @pl.when(pl.program_id(1) == 0)
    def _():
        h = _rms(x_ref[...], g_ref[...]) * (1.0 + sc_ref[...]) + sh_ref[...]
        h_ref[...] = h.astype(BF16)

    h = h_ref[...]
    o_ref[:, :ZB] = _dot(h, wlo_ref[...].astype(BF16))
    o_ref[:, ZB:] = _dot(h, whi_ref[...].astype(BF16))


def _inproj(x, mod, g, w, per_row, rows_per_seq):
    m, tm = x.shape[0], 1024
    assert Z_TN == 2 * ZB
    mspec = lambda k: _mod_spec(k, per_row, tm, rows_per_seq)
    return pl.pallas_call(
        _inproj_kernel,
        grid=(m // tm, Z_W // Z_TN),
        in_specs=[pl.BlockSpec((tm, D_MODEL), lambda i, j: (i, 0)),
                  mspec(MOD_SCALE_M), mspec(MOD_SHIFT_M),
                  pl.BlockSpec((1, D_MODEL), lambda i, j: (0, 0)),
                  pl.BlockSpec((D_MODEL, ZB), lambda i, j: (0, _w_in_block(2 * j))),
                  pl.BlockSpec((D_MODEL, ZB), lambda i, j: (0, _w_in_block(2 * j + 1)))],
        out_specs=pl.BlockSpec((tm, Z_TN), lambda i, j: (i, j)),
        out_shape=jax.ShapeDtypeStruct((m, Z_W), F32),
        scratch_shapes=[pltpu.VMEM((tm, D_MODEL), BF16)],
        compiler_params=_params(("arbitrary", "arbitrary")),
        name="inproj",
    )(x, mod, mod, g, w, w)


def _bd(w, mask):
    return jnp.where(mask, jnp.concatenate([w] * HPG, axis=0), 0.0).astype(BF16)


def _bd_heads(nat, mask):
    return jnp.where(mask, jnp.concatenate([nat] * HPG, axis=1), 0.0).astype(BF16)


def _diag_blocks(m):
    return jnp.concatenate([m[h * HEAD:(h + 1) * HEAD, h * HEAD:(h + 1) * HEAD] for h in range(HPG)],
                           axis=0)


def _wkv_kernel(*refs, lb, carried):
    R = WKV_ROWS
    if carried:
        (zr_ref, zk_ref, zv_ref, zl_ref,
         mur_ref, muk_ref, muv_ref, mul_ref,
         w0_ref, a0_ref, kkp_ref, kap_ref, rkp_ref, gng_ref, gnb_ref,
         wd_ref, wa_ref, wg_ref,
         oa_ref, so_ref, sho_ref,
         s_ref, cr_ref, ck_ref, cv_ref, cl_ref) = refs
    else:
        (zr_ref, zk_ref, zv_ref, zl_ref,
         pr_ref, pk_ref, pv_ref, pl_ref, si_ref,
         mur_ref, muk_ref, muv_ref, mul_ref,
         w0_ref, a0_ref, kkp_ref, kap_ref, rkp_ref, gng_ref, gnb_ref,
         wd_ref, wa_ref, wg_ref,
         oa_ref, so_ref, sho_ref) = refs
    nblk = R // lb

    if carried:
        @pl.when(pl.program_id(1) == 0)
        def _():
            s_ref[...] = jnp.zeros_like(s_ref)
            cr_ref[...] = jnp.zeros_like(cr_ref)
            ck_ref[...] = jnp.zeros_like(ck_ref)
            cv_ref[...] = jnp.zeros_like(cv_ref)
            cl_ref[...] = jnp.zeros_like(cl_ref)

    row1 = lax.broadcasted_iota(jnp.int32, (R, 1), 0)
    first = (row1 % lb) == 0

    def shift(z_ref, prev0, mu_ref):
        z = z_ref[...]
        prev = jnp.where(first, prev0, pltpu.roll(z, 1, axis=0))
        return z + (prev - z) * mu_ref[...]

    if carried:
        r = shift(zr_ref, cr_ref[...], mur_ref)
        k = shift(zk_ref, ck_ref[...], muk_ref)
        v = shift(zv_ref, cv_ref[...], muv_ref)
        l = shift(zl_ref, cl_ref[...], mul_ref)
        cr_ref[...] = zr_ref[R - 1:R, :]
        ck_ref[...] = zk_ref[R - 1:R, :]
        cv_ref[...] = zv_ref[R - 1:R, :]
        cl_ref[...] = zl_ref[R - 1:R, :]
        last = lambda z_ref: z_ref[R - 1:R, :]
    else:
        r = shift(zr_ref, pr_ref[...], mur_ref)
        k = shift(zk_ref, pk_ref[...], muk_ref)
        v = shift(zv_ref, pv_ref[...], muv_ref)
        l = shift(zl_ref, pl_ref[...], mul_ref)
        last = lambda z_ref: jnp.concatenate(
            [z_ref[b * lb + lb - 1:b * lb + lb, :] for b in range(nblk)], axis=0)
    sho_ref[:, 0:RW] = last(zr_ref)
    sho_ref[:, RW:2 * RW] = last(zk_ref)
    sho_ref[:, 2 * RW:3 * RW] = last(zv_ref)
    sho_ref[:, 3 * RW:] = last(zl_ref)

    w_raw = w0_ref[...] + _dot(jnp.tanh(l).astype(BF16), wd_ref[...])
    logw = jax.nn.sigmoid(w_raw) * (-math.exp(-0.5))
    a = jax.nn.sigmoid(a0_ref[...] + _dot(l.astype(BF16), wa_ref[...]))
    g = _dot(jax.nn.sigmoid(l).astype(BF16), wg_ref[...])
    kk = k * kkp_ref[...]
    k2 = k * (1.0 + (a - 1.0) * kap_ref[...])

    ri = lax.broadcasted_iota(jnp.int32, (GRP, GRP), 0)
    ci = lax.broadcasted_iota(jnp.int32, (GRP, GRP), 1)
    bdm = (ri // HEAD) == (ci // HEAD)
    ones_bd = jnp.where(bdm, 1.0, 0.0).astype(BF16)
    tr = lax.broadcasted_iota(jnp.int32, (R, R), 0)
    tc = lax.broadcasted_iota(jnp.int32, (R, R), 1)
    same = (tr // lb) == (tc // lb)
    incl01 = jnp.where(same & (tc <= tr), 1.0, 0.0).astype(BF16)
    wr = lax.broadcasted_iota(jnp.int32, (R, GRP), 0)
    wc = lax.broadcasted_iota(jnp.int32, (R, GRP), 1) % HEAD
    wsame = (wr // lb) == (wc // lb)
    strict_w = wsame & (wc < wr)
    incl_w = wsame & (wc <= wr)
    eye_w = jnp.where(wc == wr, 1.0, 0.0)

    def seg_sums(xs):
        parts = []
        for x in xs:
            parts.extend(_split2(x))
        stacked = jnp.concatenate(parts, axis=0)
        cols = [_dot(stacked[:, gi * GRP:(gi + 1) * GRP], ones_bd) for gi in range(N_GRP)]
        full = jnp.concatenate(cols, axis=1)
        return [full[2 * i * R:(2 * i + 1) * R] + full[(2 * i + 1) * R:(2 * i + 2) * R]
                for i in range(len(xs))]

    ss, rk = seg_sums([kk * kk, r * k2 * rkp_ref[...]])
    kk = kk / jnp.maximum(jnp.sqrt(ss), 1e-12)
    bonus = rk * v
    av = -kk
    bv = kk * a

    def cum(m01, x):
        hi, mid, lo = _split3(x)
        return _dot(m01, hi) + _dot(m01, mid) + _dot(m01, lo)

    cl = cum(incl01, logw)
    if nblk == 1:
        cl_end = cl[R - 1:R, :]
    else:
        cl_end = cum(jnp.where(same, 1.0, 0.0).astype(BF16), logw)
    e_neg = jnp.exp(-cl)
    e_end = jnp.exp(cl_end - cl)
    at = av * jnp.exp(cl - logw)
    rt = r * jnp.exp(cl)
    bt = bv * e_neg
    kt = k2 * e_neg
    bh = bv * e_end
    kh = k2 * e_end
    p_end = jnp.exp(cl_end)

    grps = range(N_GRP)
    sls = [slice(gi * GRP, (gi + 1) * GRP) for gi in grps]
    blks = [slice(b * lb, (b + 1) * lb) for b in range(nblk)]
    v_g = [v[:, sl] for sl in sls]
    lhs = [jnp.concatenate([at[:, sl], rt[:, sl]], axis=0).astype(BF16) for sl in sls]
    o1 = [_dot_nt(lhs[gi], jnp.concatenate([_bd(bt[:, sls[gi]], bdm), _bd(kt[:, sls[gi]], bdm)], axis=0))
          for gi in grps]
    w_ab = [jnp.where(strict_w, o[:R, :GRP], 0.0) for o in o1]
    w_ak = [jnp.where(strict_w, o[:R, GRP:], 0.0) for o in o1]
    w_rb = [jnp.where(incl_w, o[R:, :GRP], 0.0) for o in o1]
    w_rk = [jnp.where(incl_w, o[R:, GRP:], 0.0) for o in o1]

    def state_part(gi):
        if carried:
            o2 = _dot_nt(lhs[gi], s_ref[gi].astype(BF16))
            return o2[:R], o2[R:]
        a_rows, r_rows = [], []
        for b, rows in enumerate(blks):
            lhs_b = jnp.concatenate([at[rows, sls[gi]], rt[rows, sls[gi]]], axis=0).astype(BF16)
            o2 = _dot_nt(lhs_b, _bd_heads(si_ref[b, sls[gi], :], bdm))
            a_rows.append(o2[:lb])
            r_rows.append(o2[lb:])
        return jnp.concatenate(a_rows, axis=0), jnp.concatenate(r_rows, axis=0)

    x_w = list(w_ab)
    t_w = [eye_w + x for x in x_w]
    n_lvl = max(1, int(math.log2(lb)))
    a_s = r_s = o3 = None
    for j in range(n_lvl):
        for gi in grps:
            y_bd = _bd(x_w[gi], bdm)
            if j == 0:
                x_w[gi] = _dot(x_w[gi].astype(BF16), y_bd)
            elif j == n_lvl - 1:
                t_w[gi] = t_w[gi] + _dot(t_w[gi].astype(BF16), y_bd)
            else:
                res = _dot(jnp.concatenate([t_w[gi], x_w[gi]], axis=0).astype(BF16), y_bd)
                t_w[gi] = t_w[gi] + res[:R]
                x_w[gi] = res[R:]
        if j == 0:
            parts = [state_part(gi) for gi in grps]
            a_s, r_s = [p[0] for p in parts], [p[1] for p in parts]
        elif j == 1:
            o3 = [_dot(jnp.concatenate([w_ak[gi], w_rk[gi]], axis=0).astype(BF16), _bd(v_g[gi], bdm))
                  for gi in grps]

    u = [_dot(t_w[gi].astype(BF16), _bd(a_s[gi] + o3[gi][:R], bdm)) for gi in grps]
    o_parts = [r_s[gi] + o3[gi][R:] + _dot(w_rb[gi].astype(BF16), _bd(u[gi], bdm)) for gi in grps]

    for gi in grps:
        sl = sls[gi]
        if carried:
            upd = _dot_tn(jnp.concatenate([u[gi], v_g[gi]], axis=0).astype(BF16),
                          jnp.concatenate([bh[:, sl], kh[:, sl]], axis=0).astype(BF16))
            s_new = jnp.where(bdm, s_ref[gi] * p_end[:, sl] + upd, 0.0)
            s_ref[gi] = s_new
            so_ref[sl, :] = _diag_blocks(s_new)
        else:
            for b, rows in enumerate(blks):
                upd = _dot_tn(jnp.concatenate([u[gi][rows], v_g[gi][rows]], axis=0).astype(BF16),
                              jnp.concatenate([bh[rows, sl], kh[rows, sl]], axis=0).astype(BF16))
                p_b = p_end[b * lb:b * lb + 1, sl]
                p_nat = jnp.concatenate(
                    [jnp.broadcast_to(p_b[:, h * HEAD:(h + 1) * HEAD], (HEAD, HEAD)) for h in range(HPG)],
                    axis=0)
                so_ref[b, sl, :] = si_ref[b, sl, :] * p_nat + _diag_blocks(upd)

    o = jnp.concatenate(o_parts, axis=1)
    (mu,) = seg_sums([o])
    dlt = o - mu * (1.0 / HEAD)
    (var,) = seg_sums([dlt * dlt])
    on = dlt * lax.rsqrt(var * (1.0 / HEAD) + GN_EPS) * gng_ref[...] + gnb_ref[...]
    oa_ref[...] = ((on + bonus) * g).astype(BF16)


def _wkv(z, vecs, mats, n_seq, seq_len, lb, prev_rows=None, state_in=None):
    R = WKV_ROWS
    carried = prev_rows is None
    rows = z.shape[0]
    if carried:
        steps = seq_len // R
        grid = (n_seq, steps)
        rmap = lambda c: (lambda b, t: (b * steps + t, c))
        cmap = lambda b, t: (0, 0)
        seq_per_step = 1
        smap = lambda b, t: (b, 0, 0)
        sem = ("arbitrary", "arbitrary")
    else:
        grid = (rows // R,)
        rmap = lambda c: (lambda i: (i, c))
        cmap = lambda i: (0, 0)
        seq_per_step = R // lb
        smap = lambda i: (i, 0, 0)
        sem = ("arbitrary",)

    zspecs = [pl.BlockSpec((R, RW), rmap(Z_R // RW)),
              pl.BlockSpec((R, RW), rmap(Z_K // RW)),
              pl.BlockSpec((R, RW), rmap(Z_VR // RW)),
              pl.BlockSpec((R, LORA), rmap(Z_L // LORA))]
    in_specs = list(zspecs)
    args = [z, z, z, z]
    if not carried:
        in_specs += [pl.BlockSpec((R, RW), rmap(0)), pl.BlockSpec((R, RW), rmap(1)),
                     pl.BlockSpec((R, RW), rmap(2)), pl.BlockSpec((R, LORA), rmap(3 * RW // LORA)),
                     pl.BlockSpec((seq_per_step, RW, HEAD), smap)]
        args += [prev_rows] * 4 + [state_in]
    for a in vecs + mats:
        in_specs.append(pl.BlockSpec(a.shape, cmap))
        args.append(a)

    scratch = []
    if carried:
        scratch = [pltpu.VMEM((N_GRP, GRP, GRP), F32), pltpu.VMEM((1, RW), F32),
                   pltpu.VMEM((1, RW), F32), pltpu.VMEM((1, RW), F32), pltpu.VMEM((1, LORA), F32)]
    if carried:
        state_spec = pl.BlockSpec((None, RW, HEAD), smap)
        shift_spec = pl.BlockSpec((None, 1, C_SHIFT), smap)
        shift_shape = (n_seq, 1, C_SHIFT)
    else:
        state_spec = pl.BlockSpec((seq_per_step, RW, HEAD), smap)
        shift_spec = pl.BlockSpec((seq_per_step, C_SHIFT), lambda i: (i, 0))
        shift_shape = (n_seq, C_SHIFT)
    return pl.pallas_call(
        functools.partial(_wkv_kernel, lb=lb, carried=carried),
        grid=grid,
        in_specs=in_specs,
        out_specs=[pl.BlockSpec((R, RW), rmap(0)), state_spec, shift_spec],
        out_shape=[jax.ShapeDtypeStruct((rows, RW), BF16),
                   jax.ShapeDtypeStruct((n_seq, RW, HEAD), F32),
                   jax.ShapeDtypeStruct(shift_shape, F32)],
        scratch_shapes=scratch,
        compiler_params=_params(sem),
        name="wkv_carried" if carried else "wkv_blocks",
    )(*args)


def _gmlp_kernel(zu_ref, zv_ref, lng_ref, lnb_ref, ws_ref, bias_ref, ub_ref, vn_ref=None):
    u = jax.nn.gelu(zu_ref[...])
    vg = jax.nn.gelu(zv_ref[...])
    mu = jnp.mean(vg, -1, keepdims=True)
    var = jnp.mean(jnp.square(vg - mu), -1, keepdims=True)
    vn = (vg - mu) * lax.rsqrt(var + LN_EPS) * lng_ref[...] + lnb_ref[...]
    if vn_ref is not None:
        vn_ref[...] = vn
    tr = lax.broadcasted_iota(jnp.int32, (CHUNK, CHUNK), 0)
    tc = lax.broadcasted_iota(jnp.int32, (CHUNK, CHUNK), 1)
    causal = tc <= tr
    vb = vn.astype(BF16)
    cols = []
    for gi in range(MLP_GROUPS):
        wsg = jnp.where(causal, ws_ref[gi], 0.0).astype(BF16)
        cols.append(_dot(wsg, vb[:, gi * MLP_GD:(gi + 1) * MLP_GD]))
    mixed = jnp.concatenate(cols, axis=1) + bias_ref[...]
    ub_ref[...] = (u * mixed).astype(BF16)


def _gmlp(z, ln_g, ln_b, ws, bias, want_vn):
    rows = z.shape[0]
    n_out = 2 if want_vn else 1
    return pl.pallas_call(
        _gmlp_kernel,
        grid=(rows // CHUNK,),
        in_specs=[pl.BlockSpec((CHUNK, MLP_W), lambda i: (i, Z_U // MLP_W)),
                  pl.BlockSpec((CHUNK, MLP_W), lambda i: (i, Z_V // MLP_W)),
                  pl.BlockSpec((1, MLP_W), lambda i: (0, 0)),
                  pl.BlockSpec((1, MLP_W), lambda i: (0, 0)),
                  pl.BlockSpec((MLP_GROUPS, CHUNK, CHUNK), lambda i: (0, 0, 0)),
                  pl.BlockSpec((CHUNK, MLP_W), lambda i: (0, 0))],
        out_specs=[pl.BlockSpec((CHUNK, MLP_W), lambda i: (i, 0)),
                   pl.BlockSpec((CHUNK, MLP_W), lambda i: (i, 0))][:n_out],
        out_shape=[jax.ShapeDtypeStruct((rows, MLP_W), BF16),
                   jax.ShapeDtypeStruct((rows, MLP_W), F32)][:n_out],
        compiler_params=_params(("arbitrary",)),
        name="gmlp",
    )(z, z, ln_g, ln_b, ws, bias)


def _merge_kernel(oa_ref, ub_ref, zga_ref, zgb_ref, x_ref, gm_ref, sc_ref, sh_ref, g_ref,
                  wa_ref, wb_ref, wo_ref, x1_ref, h2_ref):
    ya = _dot(oa_ref[...], wa_ref[...])
    yb = _dot(ub_ref[...], wb_ref[...])
    merged = jax.nn.sigmoid(zga_ref[...]) * ya + jax.nn.sigmoid(zgb_ref[...]) * yb
    mix = _dot(merged.astype(BF16), wo_ref[...])
    x1 = x_ref[...] + gm_ref[...] * mix
    x1_ref[...] = x1
    h2_ref[...] = (_rms(x1, g_ref[...]) * (1.0 + sc_ref[...]) + sh_ref[...]).astype(BF16)


def _merge(oa, ub, z, x, mod, g_ffn, wa, wb, wo, per_row, rows_per_seq):
    m, tm = x.shape[0], (128 if per_row else 256)
    const = lambda shape: pl.BlockSpec(shape, lambda i: (0, 0), pipeline_mode=pl.Buffered(1))
    mspec = lambda k: _mod_spec(k, per_row, tm, rows_per_seq)
    rows = lambda w, c: pl.BlockSpec((tm, w), lambda i: (i, c))
    return pl.pallas_call(
        _merge_kernel,
        grid=(m // tm,),
        in_specs=[rows(RW, 0), rows(MLP_W, 0),
                  rows(D_MODEL, Z_GA // D_MODEL), rows(D_MODEL, Z_GB // D_MODEL), rows(D_MODEL, 0),
                  mspec(MOD_GATE_M), mspec(MOD_SCALE_F), mspec(MOD_SHIFT_F),
                  pl.BlockSpec((1, D_MODEL), lambda i: (0, 0)),
                  const((RW, D_MODEL)), const((MLP_W, D_MODEL)), const((D_MODEL, D_MODEL))],
        out_specs=[rows(D_MODEL, 0), rows(D_MODEL, 0)],
        out_shape=[jax.ShapeDtypeStruct((m, D_MODEL), F32), jax.ShapeDtypeStruct((m, D_MODEL), BF16)],
        compiler_params=_params(("arbitrary",)),
        name="merge",
    )(oa, ub, z, z, x, mod, mod, mod, g_ffn, wa, wb, wo)


def _ffn_kernel(h_ref, x_ref, gf_ref, gfin_ref, wg_ref, wu_ref, wo_ref, y_ref):
    j = pl.program_id(1)

    @pl.when(j == 0)
    def _():
        y_ref[...] = jnp.zeros_like(y_ref)

    hb = h_ref[...]
    gt = _dot(hb, wg_ref[...].astype(BF16))
    up = _dot(hb, wu_ref[...].astype(BF16))
    act = gt * jax.nn.sigmoid(gt) * up
    y_ref[...] += _dot(act.astype(BF16), wo_ref[...].astype(BF16))

    @pl.when(j == pl.num_programs(1) - 1)
    def _():
        x2 = x_ref[...] + gf_ref[...] * y_ref[...]
        y_ref[...] = _rms(x2, gfin_ref[...])


def _ffn(h2, x1, mod, g_final, w_in, w_out, per_row, rows_per_seq):
    m, tm, tf = x1.shape[0], (512 if per_row else 1024), 256
    nf = D_FF // tf
    return pl.pallas_call(
        _ffn_kernel,
        grid=(m // tm, nf),
        in_specs=[pl.BlockSpec((tm, D_MODEL), lambda i, j: (i, 0)),
                  pl.BlockSpec((tm, D_MODEL), lambda i, j: (i, 0), pipeline_mode=pl.Buffered(1)),
                  _mod_spec(MOD_GATE_F, per_row, tm, rows_per_seq),
                  pl.BlockSpec((1, D_MODEL), lambda i, j: (0, 0)),
                  pl.BlockSpec((D_MODEL, tf), lambda i, j: (0, j)),
                  pl.BlockSpec((D_MODEL, tf), lambda i, j: (0, nf + j)),
                  pl.BlockSpec((tf, D_MODEL), lambda i, j: (j, 0))],
        out_specs=pl.BlockSpec((tm, D_MODEL), lambda i, j: (i, 0)),
        out_shape=jax.ShapeDtypeStruct((m, D_MODEL), F32),
        compiler_params=_params(("arbitrary", "arbitrary")),
        name="ffn",
    )(h2, x1, mod, g_final, w_in, w_in, w_out)


def kernel(x_prompt, x_sample, state_wkv, state_shift, c_prompt, c_sample, w_ada, b_ada, norm_mix_g, w_in, mu_shift, w0, w_decay_up, a0, w_aaa_up, w_gate_up, k_k, k_a, r_k, gn_g, gn_b, ln_v_g, ln_v_b, w_spatial, b_spatial, w_branch_a, w_branch_b, w_out, norm_ffn_g, w_ffn_in, w_ffn_out, norm_final_g):
    assert w_ada.shape[0] == 1, "single layer"
    bp, tp, _ = x_prompt.shape
    bs, ts, _ = x_sample.shape
    assert tp % CHUNK == 0 and WKV_ROWS % ts == 0 and CHUNK % ts == 0

    c_all = jnp.concatenate([c_sample, c_prompt], axis=0)
    c_all = jnp.pad(c_all, ((0, -c_all.shape[0] % 8), (0, 0)))
    mod_s = _ada(c_all, w_ada[0], b_ada[0], bs, ts)
    mod_p = mod_s[bs * ts:bs * ts + bp].reshape(bp, 1, 6 * D_MODEL)

    wa_b, wb_b, wo_b = w_branch_a[0].astype(BF16), w_branch_b[0].astype(BF16), w_out[0].astype(BF16)
    wfi, wfo = w_ffn_in[0], w_ffn_out[0]

    row = lambda a: a.reshape(1, -1)
    mu = mu_shift[0]
    vecs = [row(mu[:RW]), row(mu[RW:2 * RW]), row(mu[2 * RW:3 * RW]), row(mu[3 * RW:]),
            row(w0[0]), row(a0[0]), row(k_k[0]), row(k_a[0]), row(r_k[0]), row(gn_g[0]), row(gn_b[0])]
    zpad = lambda w, lo: jnp.pad(w, ((lo, LORA - lo - w.shape[0]), (0, 0))).astype(BF16)
    mats = [zpad(w_decay_up[0], 0), zpad(w_aaa_up[0], DECAY_LORA),
            zpad(w_gate_up[0], DECAY_LORA + AAA_LORA)]

    bias_p = jnp.repeat(b_spatial[0].T, MLP_GD, axis=1)
    ws_p = w_spatial[0]
    reps = CHUNK // ts
    blk_id = jnp.arange(CHUNK) // ts
    ws_s = jnp.where(blk_id[:, None] == blk_id[None, :],
                     jnp.tile(w_spatial[0][:, :ts, :ts], (1, reps, reps)), 0.0)
    bias_s = jnp.tile(bias_p[:ts], (reps, 1))

    xp = x_prompt.reshape(bp * tp, D_MODEL)
    xs = x_sample.reshape(bs * ts, D_MODEL)
    g_mix, g_ffn, g_fin = row(norm_mix_g[0]), row(norm_ffn_g[0]), row(norm_final_g)
    lng, lnb = row(ln_v_g[0]), row(ln_v_b[0])

    zp = _inproj(xp, mod_p, g_mix, w_in[0], False, tp)
    oa_p, st_p, shift_p = _wkv(zp, vecs, mats, bp, tp, WKV_ROWS)
    (ub_p,) = _gmlp(zp, lng, lnb, ws_p, bias_p, False)
    x1_p, h2_p = _merge(oa_p, ub_p, zp, xp, mod_p, g_ffn, wa_b, wb_b, wo_b, False, tp)
    y_p = _ffn(h2_p, x1_p, mod_p, g_fin, wfi, wfo, False, tp)

    zs = _inproj(xs, mod_s, g_mix, w_in[0], True, ts)
    prev_rows = jnp.repeat(state_shift[0], ts, axis=0)
    oa_s, st_s, shift_s = _wkv(zs, vecs, mats, bs, ts, ts, prev_rows=prev_rows,
                      state_in=state_wkv[0].reshape(bs, RW, HEAD))
    ub_s, vn_s = _gmlp(zs, lng, lnb, ws_s, bias_s, True)
    x1_s, h2_s = _merge(oa_s, ub_s, zs, xs, mod_s, g_ffn, wa_b, wb_b, wo_b, True, ts)
    y_s = _ffn(h2_s, x1_s, mod_s, g_fin, wfi, wfo, True, ts)

    return (y_p.reshape(bp, tp, D_MODEL),
            y_s.reshape(bs, ts, D_MODEL),
            st_p.reshape(1, bp, N_HEADS, HEAD, HEAD),
            shift_p.reshape(1, bp, C_SHIFT),
            st_s.reshape(1, bs, N_HEADS, HEAD, HEAD),
            shift_s.reshape(1, bs, C_SHIFT),
            vn_s.reshape(bs, ts, MLP_W)[None])
```

```python
import functools
import math

import jax
import jax.numpy as jnp
from jax import lax
from jax.experimental import pallas as pl
from jax.experimental.pallas import tpu as pltpu

F32 = jnp.float32
BF16 = jnp.bfloat16

D_MODEL = 2048
HEAD = 64
RW = D_MODEL // 2
N_HEADS = RW // HEAD
DECAY_LORA = D_MODEL // 32
AAA_LORA = D_MODEL // 32
GATE_LORA = D_MODEL // 16
LORA = DECAY_LORA + AAA_LORA + GATE_LORA
CHUNK = 128
MLP_W = D_MODEL // 2
MLP_GROUPS = 8
MLP_GD = MLP_W // MLP_GROUPS
D_FF = ((-(-8 * D_MODEL // 3)) + 255) // 256 * 256
C_SHIFT = 3 * RW + LORA
C_IN = C_SHIFT + 2 * MLP_W + 2 * D_MODEL
NORM_EPS = 1e-6
GN_EPS = 64e-5
LN_EPS = 1e-5

Z_GA, Z_GB, Z_U, Z_V, Z_R, Z_K, Z_VR, Z_L = 0, 2048, 4096, 5120, 6144, 7168, 8192, 9216
Z_TN = 512
Z_W = -(-C_IN // Z_TN) * Z_TN

GRP = 256
HPG = GRP // HEAD
N_GRP = RW // GRP
WKV_ROWS = 64
VMEM_LIMIT = 56 * 1024 * 1024


def _dot(a, b):
    return jnp.dot(a, b, preferred_element_type=F32)


def _dot_nt(a, b):
    return lax.dot_general(a, b, (((1,), (1,)), ((), ())), preferred_element_type=F32)


def _dot_tn(a, b):
    return lax.dot_general(a, b, (((0,), (0,)), ((), ())), preferred_element_type=F32)


def _rms(x, g):
    return x * lax.rsqrt(jnp.mean(x * x, -1, keepdims=True) + NORM_EPS) * g


def _split2(x):
    hi = x.astype(BF16)
    lo = (x - hi.astype(F32)).astype(BF16)
    return hi, lo


def _split3(x):
    hi = x.astype(BF16)
    r1 = x - hi.astype(F32)
    mid = r1.astype(BF16)
    lo = (r1 - mid.astype(F32)).astype(BF16)
    return hi, mid, lo


def _params(sem):
    return pltpu.CompilerParams(dimension_semantics=sem, vmem_limit_bytes=VMEM_LIMIT)


def _ada_kernel(c_ref, w_ref, b_ref, o_ref, s_ref, e_ref, *, n_rep, rep):
    m_out, m_in = e_ref.shape

    @pl.when(pl.program_id(0) == 0)
    def _():
        c = c_ref[...]
        s_ref[...] = (c * jax.nn.sigmoid(c)).astype(BF16)
        r = lax.broadcasted_iota(jnp.int32, (m_out, m_in), 0)
        s = lax.broadcasted_iota(jnp.int32, (m_out, m_in), 1)
        src = jnp.where(r < n_rep * rep, r // rep, r - n_rep * (rep - 1))
        e_ref[...] = jnp.where(s == src, 1.0, 0.0).astype(BF16)

    mod = _dot(s_ref[...], w_ref[...].astype(BF16)) + b_ref[...]
    hi, mid, lo = _split3(mod)
    e = e_ref[...]
    o_ref[...] = _dot(e, hi) + _dot(e, mid) + _dot(e, lo)


def _ada(c, w_ada, b_ada, n_rep, rep):
    m_in, n, tn = c.shape[0], w_ada.shape[1], 512
    m_out = m_in + n_rep * (rep - 1)
    return pl.pallas_call(
        functools.partial(_ada_kernel, n_rep=n_rep, rep=rep),
        grid=(n // tn,),
        in_specs=[pl.BlockSpec((m_in, D_MODEL), lambda j: (0, 0)),
                  pl.BlockSpec((D_MODEL, tn), lambda j: (0, j)),
                  pl.BlockSpec((1, tn), lambda j: (0, j))],
        out_specs=pl.BlockSpec((m_out, tn), lambda j: (0, j)),
        out_shape=jax.ShapeDtypeStruct((m_out, n), F32),
        scratch_shapes=[pltpu.VMEM((m_in, D_MODEL), BF16), pltpu.VMEM((m_out, m_in), BF16)],
        compiler_params=_params(("arbitrary",)),
        name="ada",
    )(c, w_ada, b_ada.reshape(1, n))


MOD_SHIFT_M, MOD_SCALE_M, MOD_GATE_M, MOD_SHIFT_F, MOD_SCALE_F, MOD_GATE_F = range(6)


def _mod_spec(k, per_row, tm, rows_per_seq):
    if per_row:
        return pl.BlockSpec((tm, D_MODEL), lambda i, *_: (i, k))
    return pl.BlockSpec((None, 1, D_MODEL), lambda i, *_: (i * tm // rows_per_seq, 0, k))


ZB = 256
_N_RW_B, _N_MLP_B, _N_GATE_B = C_SHIFT // ZB, 2 * MLP_W // ZB, 2 * D_MODEL // ZB


def _w_in_block(zb):
    return jnp.where(zb < _N_GATE_B, zb + _N_RW_B + _N_MLP_B,
                     jnp.where(zb < _N_GATE_B + _N_MLP_B, zb - _N_GATE_B + _N_RW_B,
                               jnp.minimum(zb - _N_GATE_B - _N_MLP_B, _N_RW_B - 1)))


def _hnorm_kernel(x_ref, sc_ref, sh_ref, g_ref, h_ref):
    h = _rms(x_ref[...], g_ref[...]) * (1.0 + sc_ref[...]) + sh_ref[...]
    h_ref[...] = h.astype(BF16)


def _hnorm(x, mod, g, per_row, rows_per_seq):
    m, tm = x.shape[0], 512
    mspec = lambda k: _mod_spec(k, per_row, tm, rows_per_seq)
    return pl.pallas_call(
        _hnorm_kernel,
        grid=(m // tm,),
        in_specs=[pl.BlockSpec((tm, D_MODEL), lambda i: (i, 0)),
                  mspec(MOD_SCALE_M), mspec(MOD_SHIFT_M),
                  pl.BlockSpec((1, D_MODEL), lambda i: (0, 0))],
        out_specs=pl.BlockSpec((tm, D_MODEL), lambda i: (i, 0)),
        out_shape=jax.ShapeDtypeStruct((m, D_MODEL), BF16),
        compiler_params=_params(("arbitrary",)),
        name="hnorm",
    )(x, mod, mod, g)


def _inproj_kernel(h_ref, wlo_ref, whi_ref, o_ref):
    h = h_ref[...]
    o_ref[:, :ZB] = _dot(h, wlo_ref[...].astype(BF16))
    o_ref[:, ZB:] = _dot(h, whi_ref[...].astype(BF16))


def _inproj(h, w):
    m = h.shape[0]
    tm = min(m, 2048)
    assert Z_TN == 2 * ZB and m % tm == 0
    return pl.pallas_call(
        _inproj_kernel,
        grid=(m // tm, Z_W // Z_TN),
        in_specs=[pl.BlockSpec((tm, D_MODEL), lambda i, j: (i, 0)),
                  pl.BlockSpec((D_MODEL, ZB), lambda i, j: (0, _w_in_block(2 * j))),
                  pl.BlockSpec((D_MODEL, ZB), lambda i, j: (0, _w_in_block(2 * j + 1)))],
        out_specs=pl.BlockSpec((tm, Z_TN), lambda i, j: (i, j)),
        out_shape=jax.ShapeDtypeStruct((m, Z_W), F32),
        compiler_params=_params(("arbitrary", "arbitrary")),
        name="inproj",
    )(h, w, w)


def _bd(w, mask):
    return jnp.where(mask, jnp.concatenate([w] * HPG, axis=0), 0.0).astype(BF16)


def _bd_heads(nat, mask):
    return jnp.where(mask, jnp.concatenate([nat] * HPG, axis=1), 0.0).astype(BF16)


def _diag_blocks(m):
    return jnp.concatenate([m[h * HEAD:(h + 1) * HEAD, h * HEAD:(h + 1) * HEAD] for h in range(HPG)],
                           axis=0)


def _wkv_kernel(*refs, lb, carried):
    R = WKV_ROWS
    if carried:
        (zr_ref, zk_ref, zv_ref, zl_ref,
         mur_ref, muk_ref, muv_ref, mul_ref,
         w0_ref, a0_ref, kkp_ref, kap_ref, rkp_ref, gng_ref, gnb_ref,
         wd_ref, wa_ref, wg_ref,
         oa_ref, so_ref, sho_ref,
         s_ref, cr_ref, ck_ref, cv_ref, cl_ref) = refs
    else:
        (zr_ref, zk_ref, zv_ref, zl_ref,
         pr_ref, pk_ref, pv_ref, pl_ref, si_ref,
         mur_ref, muk_ref, muv_ref, mul_ref,
         w0_ref, a0_ref, kkp_ref, kap_ref, rkp_ref, gng_ref, gnb_ref,
         wd_ref, wa_ref, wg_ref,
         oa_ref, so_ref, sho_ref) = refs
    nblk = R // lb

    if carried:
        @pl.when(pl.program_id(1) == 0)
        def _():
            s_ref[...] = jnp.zeros_like(s_ref)
            cr_ref[...] = jnp.zeros_like(cr_ref)
            ck_ref[...] = jnp.zeros_like(ck_ref)
            cv_ref[...] = jnp.zeros_like(cv_ref)
            cl_ref[...] = jnp.zeros_like(cl_ref)

    row1 = lax.broadcasted_iota(jnp.int32, (R, 1), 0)
    first = (row1 % lb) == 0

    def shift(z_ref, prev0, mu_ref):
        z = z_ref[...]
        prev = jnp.where(first, prev0, pltpu.roll(z, 1, axis=0))
        return z + (prev - z) * mu_ref[...]

    if carried:
        r = shift(zr_ref, cr_ref[...], mur_ref)
        k = shift(zk_ref, ck_ref[...], muk_ref)
        v = shift(zv_ref, cv_ref[...], muv_ref)
        l = shift(zl_ref, cl_ref[...], mul_ref)
        cr_ref[...] = zr_ref[R - 1:R, :]
        ck_ref[...] = zk_ref[R - 1:R, :]
        cv_ref[...] = zv_ref[R - 1:R, :]
        cl_ref[...] = zl_ref[R - 1:R, :]
        last = lambda z_ref: z_ref[R - 1:R, :]
    else:
        r = shift(zr_ref, pr_ref[...], mur_ref)
        k = shift(zk_ref, pk_ref[...], muk_ref)
        v = shift(zv_ref, pv_ref[...], muv_ref)
        l = shift(zl_ref, pl_ref[...], mul_ref)
        last = lambda z_ref: jnp.concatenate(
            [z_ref[b * lb + lb - 1:b * lb + lb, :] for b in range(nblk)], axis=0)
    sho_ref[:, 0:RW] = last(zr_ref)
    sho_ref[:, RW:2 * RW] = last(zk_ref)
    sho_ref[:, 2 * RW:3 * RW] = last(zv_ref)
    sho_ref[:, 3 * RW:] = last(zl_ref)

    w_raw = w0_ref[...] + _dot(jnp.tanh(l).astype(BF16), wd_ref[...])
    logw = jax.nn.sigmoid(w_raw) * (-math.exp(-0.5))
    a = jax.nn.sigmoid(a0_ref[...] + _dot(l.astype(BF16), wa_ref[...]))
    g = _dot(jax.nn.sigmoid(l).astype(BF16), wg_ref[...])
    kk = k * kkp_ref[...]
    k2 = k * (1.0 + (a - 1.0) * kap_ref[...])

    ri = lax.broadcasted_iota(jnp.int32, (GRP, GRP), 0)
    ci = lax.broadcasted_iota(jnp.int32, (GRP, GRP), 1)
    bdm = (ri // HEAD) == (ci // HEAD)
    ones_bd = jnp.where(bdm, 1.0, 0.0).astype(BF16)
    tr = lax.broadcasted_iota(jnp.int32, (R, R), 0)
    tc = lax.broadcasted_iota(jnp.int32, (R, R), 1)
    same = (tr // lb) == (tc // lb)
    incl01 = jnp.where(same & (tc <= tr), 1.0, 0.0).astype(BF16)
    wr = lax.broadcasted_iota(jnp.int32, (R, GRP), 0)
    wc = lax.broadcasted_iota(jnp.int32, (R, GRP), 1) % HEAD
    wsame = (wr // lb) == (wc // lb)
    strict_w = wsame & (wc < wr)
    incl_w = wsame & (wc <= wr)
    eye_w = jnp.where(wc == wr, 1.0, 0.0)

    def seg_sums(xs):
        parts = []
        for x in xs:
            parts.extend(_split2(x))
        stacked = jnp.concatenate(parts, axis=0)
        cols = [_dot(stacked[:, gi * GRP:(gi + 1) * GRP], ones_bd) for gi in range(N_GRP)]
        full = jnp.concatenate(cols, axis=1)
        return [full[2 * i * R:(2 * i + 1) * R] + full[(2 * i + 1) * R:(2 * i + 2) * R]
                for i in range(len(xs))]

    ss, rk = seg_sums([kk * kk, r * k2 * rkp_ref[...]])
    kk = kk / jnp.maximum(jnp.sqrt(ss), 1e-12)
    bonus = rk * v
    av = -kk
    bv = kk * a

    def cum(m01, x):
        hi, mid, lo = _split3(x)
        return _dot(m01, hi) + _dot(m01, mid) + _dot(m01, lo)

    cl = cum(incl01, logw)
    if nblk == 1:
        cl_end = cl[R - 1:R, :]
    else:
        cl_end = cum(jnp.where(same, 1.0, 0.0).astype(BF16), logw)
    e_neg = jnp.exp(-cl)
    e_end = jnp.exp(cl_end - cl)
    at = av * jnp.exp(cl - logw)
    rt = r * jnp.exp(cl)
    bt = bv * e_neg
    kt = k2 * e_neg
    bh = bv * e_end
    kh = k2 * e_end
    p_end = jnp.exp(cl_end)

    grps = range(N_GRP)
    sls = [slice(gi * GRP, (gi + 1) * GRP) for gi in grps]
    blks = [slice(b * lb, (b + 1) * lb) for b in range(nblk)]
    v_g = [v[:, sl] for sl in sls]
    lhs = [jnp.concatenate([at[:, sl], rt[:, sl]], axis=0).astype(BF16) for sl in sls]
    o1 = [_dot_nt(lhs[gi], jnp.concatenate([_bd(bt[:, sls[gi]], bdm), _bd(kt[:, sls[gi]], bdm)], axis=0))
          for gi in grps]
    w_ab = [jnp.where(strict_w, o[:R, :GRP], 0.0) for o in o1]
    w_ak = [jnp.where(strict_w, o[:R, GRP:], 0.0) for o in o1]
    w_rb = [jnp.where(incl_w, o[R:, :GRP], 0.0) for o in o1]
    w_rk = [jnp.where(incl_w, o[R:, GRP:], 0.0) for o in o1]

    def state_part(gi):
        if carried:
            o2 = _dot_nt(lhs[gi], s_ref[gi].astype(BF16))
            return o2[:R], o2[R:]
        a_rows, r_rows = [], []
        for b, rows in enumerate(blks):
            lhs_b = jnp.concatenate([at[rows, sls[gi]], rt[rows, sls[gi]]], axis=0).astype(BF16)
            o2 = _dot_nt(lhs_b, _bd_heads(si_ref[b, sls[gi], :], bdm))
            a_rows.append(o2[:lb])
            r_rows.append(o2[lb:])
        return jnp.concatenate(a_rows, axis=0), jnp.concatenate(r_rows, axis=0)

    x_w = list(w_ab)
    t_w = [eye_w + x for x in x_w]
    n_lvl = max(1, int(math.log2(lb)))
    a_s = r_s = o3 = None
    for j in range(n_lvl):
        for gi in grps:
            y_bd = _bd(x_w[gi], bdm)
            if j == 0:
                x_w[gi] = _dot(x_w[gi].astype(BF16), y_bd)
            elif j == n_lvl - 1:
                t_w[gi] = t_w[gi] + _dot(t_w[gi].astype(BF16), y_bd)
            else:
                res = _dot(jnp.concatenate([t_w[gi], x_w[gi]], axis=0).astype(BF16), y_bd)
                t_w[gi] = t_w[gi] + res[:R]
                x_w[gi] = res[R:]
        if j == 0:
            parts = [state_part(gi) for gi in grps]
            a_s, r_s = [p[0] for p in parts], [p[1] for p in parts]
        elif j == 1:
            o3 = [_dot(jnp.concatenate([w_ak[gi], w_rk[gi]], axis=0).astype(BF16), _bd(v_g[gi], bdm))
                  for gi in grps]

    u = [_dot(t_w[gi].astype(BF16), _bd(a_s[gi] + o3[gi][:R], bdm)) for gi in grps]
    o_parts = [r_s[gi] + o3[gi][R:] + _dot(w_rb[gi].astype(BF16), _bd(u[gi], bdm)) for gi in grps]

    for gi in grps:
        sl = sls[gi]
        if carried:
            upd = _dot_tn(jnp.concatenate([u[gi], v_g[gi]], axis=0).astype(BF16),
                          jnp.concatenate([bh[:, sl], kh[:, sl]], axis=0).astype(BF16))
            s_new = jnp.where(bdm, s_ref[gi] * p_end[:, sl] + upd, 0.0)
            s_ref[gi] = s_new
            so_ref[sl, :] = _diag_blocks(s_new)
        else:
            for b, rows in enumerate(blks):
                upd = _dot_tn(jnp.concatenate([u[gi][rows], v_g[gi][rows]], axis=0).astype(BF16),
                              jnp.concatenate([bh[rows, sl], kh[rows, sl]], axis=0).astype(BF16))
                p_b = p_end[b * lb:b * lb + 1, sl]
                p_nat = jnp.concatenate(
                    [jnp.broadcast_to(p_b[:, h * HEAD:(h + 1) * HEAD], (HEAD, HEAD)) for h in range(HPG)],
                    axis=0)
                so_ref[b, sl, :] = si_ref[b, sl, :] * p_nat + _diag_blocks(upd)

    o = jnp.concatenate(o_parts, axis=1)
    (mu,) = seg_sums([o])
    dlt = o - mu * (1.0 / HEAD)
    (var,) = seg_sums([dlt * dlt])
    on = dlt * lax.rsqrt(var * (1.0 / HEAD) + GN_EPS) * gng_ref[...] + gnb_ref[...]
    oa_ref[...] = ((on + bonus) * g).astype(BF16)


def _wkv(z, vecs, mats, n_seq, seq_len, lb, prev_rows=None, state_in=None):
    R = WKV_ROWS
    carried = prev_rows is None
    rows = z.shape[0]
    if carried:
        steps = seq_len // R
        grid = (n_seq, steps)
        rmap = lambda c: (lambda b, t: (b * steps + t, c))
        cmap = lambda b, t: (0, 0)
        seq_per_step = 1
        smap = lambda b, t: (b, 0, 0)
        sem = ("arbitrary", "arbitrary")
    else:
        grid = (rows // R,)
        rmap = lambda c: (lambda i: (i, c))
        cmap = lambda i: (0, 0)
        seq_per_step = R // lb
        smap = lambda i: (i, 0, 0)
        sem = ("arbitrary",)

    zspecs = [pl.BlockSpec((R, RW), rmap(Z_R // RW)),
              pl.BlockSpec((R, RW), rmap(Z_K // RW)),
              pl.BlockSpec((R, RW), rmap(Z_VR // RW)),
              pl.BlockSpec((R, LORA), rmap(Z_L // LORA))]
    in_specs = list(zspecs)
    args = [z, z, z, z]
    if not carried:
        in_specs += [pl.BlockSpec((R, RW), rmap(0)), pl.BlockSpec((R, RW), rmap(1)),
                     pl.BlockSpec((R, RW), rmap(2)), pl.BlockSpec((R, LORA), rmap(3 * RW // LORA)),
                     pl.BlockSpec((seq_per_step, RW, HEAD), smap)]
        args += [prev_rows] * 4 + [state_in]
    for a in vecs + mats:
        in_specs.append(pl.BlockSpec(a.shape, cmap))
        args.append(a)

    scratch = []
    if carried:
        scratch = [pltpu.VMEM((N_GRP, GRP, GRP), F32), pltpu.VMEM((1, RW), F32),
                   pltpu.VMEM((1, RW), F32), pltpu.VMEM((1, RW), F32), pltpu.VMEM((1, LORA), F32)]
    if carried:
        state_spec = pl.BlockSpec((None, RW, HEAD), smap)
        shift_spec = pl.BlockSpec((None, 1, C_SHIFT), smap)
        shift_shape = (n_seq, 1, C_SHIFT)
    else:
        state_spec = pl.BlockSpec((seq_per_step, RW, HEAD), smap)
        shift_spec = pl.BlockSpec((seq_per_step, C_SHIFT), lambda i: (i, 0))
        shift_shape = (n_seq, C_SHIFT)
    return pl.pallas_call(
        functools.partial(_wkv_kernel, lb=lb, carried=carried),
        grid=grid,
        in_specs=in_specs,
        out_specs=[pl.BlockSpec((R, RW), rmap(0)), state_spec, shift_spec],
        out_shape=[jax.ShapeDtypeStruct((rows, RW), BF16),
                   jax.ShapeDtypeStruct((n_seq, RW, HEAD), F32),
                   jax.ShapeDtypeStruct(shift_shape, F32)],
        scratch_shapes=scratch,
        compiler_params=_params(sem),
        name="wkv_carried" if carried else "wkv_blocks",
    )(*args)


def _gmlp_kernel(zu_ref, zv_ref, lng_ref, lnb_ref, ws_ref, bias_ref, ub_ref, vn_ref=None, *, seq):
    reps = CHUNK // seq
    u = jax.nn.gelu(zu_ref[...])
    vg = jax.nn.gelu(zv_ref[...])
    mu = jnp.mean(vg, -1, keepdims=True)
    var = jnp.mean(jnp.square(vg - mu), -1, keepdims=True)
    vn = (vg - mu) * lax.rsqrt(var + LN_EPS) * lng_ref[...] + lnb_ref[...]
    if vn_ref is not None:
        vn_ref[...] = vn
    tr = lax.broadcasted_iota(jnp.int32, (CHUNK, CHUNK), 0)
    tc = lax.broadcasted_iota(jnp.int32, (CHUNK, CHUNK), 1)
    causal = (tc <= tr) & ((tr // seq) == (tc // seq))
    vb = vn.astype(BF16)
    cols = []
    for gi in range(MLP_GROUPS):
        w_rows = ws_ref[gi]
        w_full = jnp.concatenate([w_rows if s == 0 else pltpu.roll(w_rows, s * seq, axis=1)
                                  for s in range(reps)], axis=0)
        wsg = jnp.where(causal, w_full, 0.0).astype(BF16)
        cols.append(_dot(wsg, vb[:, gi * MLP_GD:(gi + 1) * MLP_GD]))
    mixed = jnp.concatenate(cols, axis=1) + jnp.concatenate([bias_ref[...]] * reps, axis=0)
    ub_ref[...] = (u * mixed).astype(BF16)


def _gmlp(z, ln_g, ln_b, ws, bias, seq, want_vn):
    rows = z.shape[0]
    n_out = 2 if want_vn else 1
    return pl.pallas_call(
        functools.partial(_gmlp_kernel, seq=seq),
        grid=(rows // CHUNK,),
        in_specs=[pl.BlockSpec((CHUNK, MLP_W), lambda i: (i, Z_U // MLP_W)),
                  pl.BlockSpec((CHUNK, MLP_W), lambda i: (i, Z_V // MLP_W)),
                  pl.BlockSpec((1, MLP_W), lambda i: (0, 0)),
                  pl.BlockSpec((1, MLP_W), lambda i: (0, 0)),
                  pl.BlockSpec((MLP_GROUPS, seq, CHUNK), lambda i: (0, 0, 0)),
                  pl.BlockSpec((seq, MLP_W), lambda i: (0, 0))],
        out_specs=[pl.BlockSpec((CHUNK, MLP_W), lambda i: (i, 0)),
                   pl.BlockSpec((CHUNK, MLP_W), lambda i: (i, 0))][:n_out],
        out_shape=[jax.ShapeDtypeStruct((rows, MLP_W), BF16),
                   jax.ShapeDtypeStruct((rows, MLP_W), F32)][:n_out],
        compiler_params=_params(("arbitrary",)),
        name="gmlp",
    )(z, z, ln_g, ln_b, ws, bias)


def _merge_kernel(oa_ref, ub_ref, zga_ref, zgb_ref, x_ref, gm_ref, sc_ref, sh_ref, g_ref,
                  wa_ref, wb_ref, wo_ref, x1_ref, h2_ref):
    ya = _dot(oa_ref[...], wa_ref[...])
    yb = _dot(ub_ref[...], wb_ref[...])
    merged = jax.nn.sigmoid(zga_ref[...]) * ya + jax.nn.sigmoid(zgb_ref[...]) * yb
    mix = _dot(merged.astype(BF16), wo_ref[...])
    x1 = x_ref[...] + gm_ref[...] * mix
    x1_ref[...] = x1
    h2_ref[...] = (_rms(x1, g_ref[...]) * (1.0 + sc_ref[...]) + sh_ref[...]).astype(BF16)


def _merge(oa, ub, z, x, mod, g_ffn, wa, wb, wo, per_row, rows_per_seq):
    m, tm = x.shape[0], (128 if per_row else 256)
    const = lambda shape: pl.BlockSpec(shape, lambda i: (0, 0), pipeline_mode=pl.Buffered(1))
    mspec = lambda k: _mod_spec(k, per_row, tm, rows_per_seq)
    rows = lambda w, c: pl.BlockSpec((tm, w), lambda i: (i, c))
    return pl.pallas_call(
        _merge_kernel,
        grid=(m // tm,),
        in_specs=[rows(RW, 0), rows(MLP_W, 0),
                  rows(D_MODEL, Z_GA // D_MODEL), rows(D_MODEL, Z_GB // D_MODEL), rows(D_MODEL, 0),
                  mspec(MOD_GATE_M), mspec(MOD_SCALE_F), mspec(MOD_SHIFT_F),
                  pl.BlockSpec((1, D_MODEL), lambda i: (0, 0)),
                  const((RW, D_MODEL)), const((MLP_W, D_MODEL)), const((D_MODEL, D_MODEL))],
        out_specs=[rows(D_MODEL, 0), rows(D_MODEL, 0)],
        out_shape=[jax.ShapeDtypeStruct((m, D_MODEL), F32), jax.ShapeDtypeStruct((m, D_MODEL), BF16)],
        compiler_params=_params(("arbitrary",)),
        name="merge",
    )(oa, ub, z, z, x, mod, mod, mod, g_ffn, wa, wb, wo)


def _ffn_kernel(h_ref, x_ref, gf_ref, gfin_ref, wg_ref, wu_ref, wo_ref, y_ref):
    j = pl.program_id(1)

    @pl.when(j == 0)
    def _():
        y_ref[...] = jnp.zeros_like(y_ref)

    hb = h_ref[...]
    gt = _dot(hb, wg_ref[...].astype(BF16))
    up = _dot(hb, wu_ref[...].astype(BF16))
    act = gt * jax.nn.sigmoid(gt) * up
    y_ref[...] += _dot(act.astype(BF16), wo_ref[...].astype(BF16))

    @pl.when(j == pl.num_programs(1) - 1)
    def _():
        x2 = x_ref[...] + gf_ref[...] * y_ref[...]
        y_ref[...] = _rms(x2, gfin_ref[...])


def _ffn(h2, x1, mod, g_final, w_in, w_out, per_row, rows_per_seq):
    m, tm, tf = x1.shape[0], (512 if per_row else 1024), 256
    nf = D_FF // tf
    return pl.pallas_call(
        _ffn_kernel,
        grid=(m // tm, nf),
        in_specs=[pl.BlockSpec((tm, D_MODEL), lambda i, j: (i, 0)),
                  pl.BlockSpec((tm, D_MODEL), lambda i, j: (i, 0), pipeline_mode=pl.Buffered(1)),
                  _mod_spec(MOD_GATE_F, per_row, tm, rows_per_seq),
                  pl.BlockSpec((1, D_MODEL), lambda i, j: (0, 0)),
                  pl.BlockSpec((D_MODEL, tf), lambda i, j: (0, j)),
                  pl.BlockSpec((D_MODEL, tf), lambda i, j: (0, nf + j)),
                  pl.BlockSpec((tf, D_MODEL), lambda i, j: (j, 0))],
        out_specs=pl.BlockSpec((tm, D_MODEL), lambda i, j: (i, 0)),
        out_shape=jax.ShapeDtypeStruct((m, D_MODEL), F32),
        compiler_params=_params(("arbitrary", "arbitrary")),
        name="ffn",
    )(h2, x1, mod, g_final, w_in, w_in, w_out)


def kernel(x_prompt, x_sample, state_wkv, state_shift, c_prompt, c_sample, w_ada, b_ada, norm_mix_g, w_in, mu_shift, w0, w_decay_up, a0, w_aaa_up, w_gate_up, k_k, k_a, r_k, gn_g, gn_b, ln_v_g, ln_v_b, w_spatial, b_spatial, w_branch_a, w_branch_b, w_out, norm_ffn_g, w_ffn_in, w_ffn_out, norm_final_g):
    assert w_ada.shape[0] == 1, "single layer"
    bp, tp, _ = x_prompt.shape
    bs, ts, _ = x_sample.shape
    assert tp % CHUNK == 0 and WKV_ROWS % ts == 0 and CHUNK % ts == 0

    c_all = jnp.concatenate([c_sample, c_prompt], axis=0)
    c_all = jnp.pad(c_all, ((0, -c_all.shape[0] % 8), (0, 0)))
    mod_s = _ada(c_all, w_ada[0], b_ada[0], bs, ts)
    mod_p = mod_s[bs * ts:bs * ts + bp].reshape(bp, 1, 6 * D_MODEL)

    wa_b, wb_b, wo_b = w_branch_a[0].astype(BF16), w_branch_b[0].astype(BF16), w_out[0].astype(BF16)
    wfi, wfo = w_ffn_in[0], w_ffn_out[0]

    row = lambda a: a.reshape(1, -1)
    mu = mu_shift[0]
    vecs = [row(mu[:RW]), row(mu[RW:2 * RW]), row(mu[2 * RW:3 * RW]), row(mu[3 * RW:]),
            row(w0[0]), row(a0[0]), row(k_k[0]), row(k_a[0]), row(r_k[0]), row(gn_g[0]), row(gn_b[0])]
    zpad = lambda w, lo: jnp.pad(w, ((lo, LORA - lo - w.shape[0]), (0, 0))).astype(BF16)
    mats = [zpad(w_decay_up[0], 0), zpad(w_aaa_up[0], DECAY_LORA),
            zpad(w_gate_up[0], DECAY_LORA + AAA_LORA)]

    bias = jnp.repeat(b_spatial[0].T, MLP_GD, axis=1)
    ws = w_spatial[0]

    xp = x_prompt.reshape(bp * tp, D_MODEL)
    xs = x_sample.reshape(bs * ts, D_MODEL)
    g_mix, g_ffn, g_fin = row(norm_mix_g[0]), row(norm_ffn_g[0]), row(norm_final_g)
    lng, lnb = row(ln_v_g[0]), row(ln_v_b[0])

    zp = _inproj(_hnorm(xp, mod_p, g_mix, False, tp), w_in[0])
    oa_p, st_p, shift_p = _wkv(zp, vecs, mats, bp, tp, WKV_ROWS)
    (ub_p,) = _gmlp(zp, lng, lnb, ws, bias, CHUNK, False)
    x1_p, h2_p = _merge(oa_p, ub_p, zp, xp, mod_p, g_ffn, wa_b, wb_b, wo_b, False, tp)
    y_p = _ffn(h2_p, x1_p, mod_p, g_fin, wfi, wfo, False, tp)

    zs = _inproj(_hnorm(xs, mod_s, g_mix, True, ts), w_in[0])
    prev_rows = jnp.repeat(state_shift[0], ts, axis=0)
    oa_s, st_s, shift_s = _wkv(zs, vecs, mats, bs, ts, ts, prev_rows=prev_rows,
                      state_in=state_wkv[0].reshape(bs, RW, HEAD))
    ub_s, vn_s = _gmlp(zs, lng, lnb, ws, bias, ts, True)
    x1_s, h2_s = _merge(oa_s, ub_s, zs, xs, mod_s, g_ffn, wa_b, wb_b, wo_b, True, ts)
    y_s = _ffn(h2_s, x1_s, mod_s, g_fin, wfi, wfo, True, ts)

    return (y_p.reshape(bp, tp, D_MODEL),
            y_s.reshape(bs, ts, D_MODEL),
            st_p.reshape(1, bp, N_HEADS, HEAD, HEAD),
            shift_p.reshape(1, bp, C_SHIFT),
            st_s.reshape(1, bs, N_HEADS, HEAD, HEAD),
            shift_s.reshape(1, bs, C_SHIFT),
            vn_s.reshape(bs, ts, MLP_W)[None])
```

```python
import functools
import math

import jax
import jax.numpy as jnp
from jax import lax
from jax.experimental import pallas as pl
from jax.experimental.pallas import tpu as pltpu

F32 = jnp.float32
BF16 = jnp.bfloat16

D_MODEL = 2048
HEAD = 64
RW = D_MODEL // 2
N_HEADS = RW // HEAD
DECAY_LORA = D_MODEL // 32
AAA_LORA = D_MODEL // 32
GATE_LORA = D_MODEL // 16
LORA = DECAY_LORA + AAA_LORA + GATE_LORA
CHUNK = 128
MLP_W = D_MODEL // 2
MLP_GROUPS = 8
MLP_GD = MLP_W // MLP_GROUPS
D_FF = ((-(-8 * D_MODEL // 3)) + 255) // 256 * 256
C_SHIFT = 3 * RW + LORA
C_IN = C_SHIFT + 2 * MLP_W + 2 * D_MODEL
NORM_EPS = 1e-6
GN_EPS = 64e-5
LN_EPS = 1e-5

Z_GA, Z_GB, Z_U, Z_V, Z_R, Z_K, Z_VR, Z_L = 0, 2048, 4096, 5120, 6144, 7168, 8192, 9216
Z_TN = 512
Z_W = -(-C_IN // Z_TN) * Z_TN

GRP = 256
HPG = GRP // HEAD
N_GRP = RW // GRP
WKV_ROWS = 64
WKV_NCH = 2
WKV_NSEQ = 2
VMEM_LIMIT = 56 * 1024 * 1024


def _dot(a, b):
    return jnp.dot(a, b, preferred_element_type=F32)


def _dot_nt(a, b):
    return lax.dot_general(a, b, (((1,), (1,)), ((), ())), preferred_element_type=F32)


def _dot_tn(a, b):
    return lax.dot_general(a, b, (((0,), (0,)), ((), ())), preferred_element_type=F32)


def _rms(x, g):
    return x * lax.rsqrt(jnp.mean(x * x, -1, keepdims=True) + NORM_EPS) * g


def _split2(x):
    hi = x.astype(BF16)
    lo = (x - hi.astype(F32)).astype(BF16)
    return hi, lo


def _split3(x):
    hi = x.astype(BF16)
    r1 = x - hi.astype(F32)
    mid = r1.astype(BF16)
    lo = (r1 - mid.astype(F32)).astype(BF16)
    return hi, mid, lo


def _params(sem):
    return pltpu.CompilerParams(dimension_semantics=sem, vmem_limit_bytes=VMEM_LIMIT)


def _ada_kernel(c_ref, w_ref, b_ref, o_ref, s_ref, e_ref, *, n_rep, rep):
    m_out, m_in = e_ref.shape

    @pl.when(pl.program_id(0) == 0)
    def _():
        c = c_ref[...]
        s_ref[...] = (c * jax.nn.sigmoid(c)).astype(BF16)
        r = lax.broadcasted_iota(jnp.int32, (m_out, m_in), 0)
        s = lax.broadcasted_iota(jnp.int32, (m_out, m_in), 1)
        src = jnp.where(r < n_rep * rep, r // rep, r - n_rep * (rep - 1))
        e_ref[...] = jnp.where(s == src, 1.0, 0.0).astype(BF16)

    mod = _dot(s_ref[...], w_ref[...].astype(BF16)) + b_ref[...]
    hi, mid, lo = _split3(mod)
    e = e_ref[...]
    o_ref[...] = _dot(e, hi) + _dot(e, mid) + _dot(e, lo)


def _ada(c, w_ada, b_ada, n_rep, rep):
    m_in, n, tn = c.shape[0], w_ada.shape[1], 512
    m_out = m_in + n_rep * (rep - 1)
    return pl.pallas_call(
        functools.partial(_ada_kernel, n_rep=n_rep, rep=rep),
        grid=(n // tn,),
        in_specs=[pl.BlockSpec((m_in, D_MODEL), lambda j: (0, 0)),
                  pl.BlockSpec((D_MODEL, tn), lambda j: (0, j)),
                  pl.BlockSpec((1, tn), lambda j: (0, j))],
        out_specs=pl.BlockSpec((m_out, tn), lambda j: (0, j)),
        out_shape=jax.ShapeDtypeStruct((m_out, n), F32),
        scratch_shapes=[pltpu.VMEM((m_in, D_MODEL), BF16), pltpu.VMEM((m_out, m_in), BF16)],
        compiler_params=_params(("arbitrary",)),
        name="ada",
    )(c, w_ada, b_ada.reshape(1, n))


MOD_SHIFT_M, MOD_SCALE_M, MOD_GATE_M, MOD_SHIFT_F, MOD_SCALE_F, MOD_GATE_F = range(6)


def _mod_spec(k, per_row, tm, rows_per_seq):
    if per_row:
        return pl.BlockSpec((tm, D_MODEL), lambda i, *_: (i, k))
    return pl.BlockSpec((None, 1, D_MODEL), lambda i, *_: (i * tm // rows_per_seq, 0, k))


ZB = 256
_N_RW_B, _N_MLP_B, _N_GATE_B = C_SHIFT // ZB, 2 * MLP_W // ZB, 2 * D_MODEL // ZB


def _w_in_block(zb):
    return jnp.where(zb < _N_GATE_B, zb + _N_RW_B + _N_MLP_B,
                     jnp.where(zb < _N_GATE_B + _N_MLP_B, zb - _N_GATE_B + _N_RW_B,
                               jnp.minimum(zb - _N_GATE_B - _N_MLP_B, _N_RW_B - 1)))


def _hnorm_kernel(x_ref, sc_ref, sh_ref, g_ref, h_ref):
    h = _rms(x_ref[...], g_ref[...]) * (1.0 + sc_ref[...]) + sh_ref[...]
    h_ref[...] = h.astype(BF16)


def _hnorm(x, mod, g, per_row, rows_per_seq):
    m, tm = x.shape[0], 512
    mspec = lambda k: _mod_spec(k, per_row, tm, rows_per_seq)
    return pl.pallas_call(
        _hnorm_kernel,
        grid=(m // tm,),
        in_specs=[pl.BlockSpec((tm, D_MODEL), lambda i: (i, 0)),
                  mspec(MOD_SCALE_M), mspec(MOD_SHIFT_M),
                  pl.BlockSpec((1, D_MODEL), lambda i: (0, 0))],
        out_specs=pl.BlockSpec((tm, D_MODEL), lambda i: (i, 0)),
        out_shape=jax.ShapeDtypeStruct((m, D_MODEL), BF16),
        compiler_params=_params(("arbitrary",)),
        name="hnorm",
    )(x, mod, mod, g)


def _inproj_kernel(h_ref, wlo_ref, whi_ref, o_ref):
    h = h_ref[...]
    o_ref[:, :ZB] = _dot(h, wlo_ref[...].astype(BF16))
    o_ref[:, ZB:] = _dot(h, whi_ref[...].astype(BF16))


def _inproj(h, w):
    m = h.shape[0]
    tm = min(m, 2048)
    assert Z_TN == 2 * ZB and m % tm == 0
    return pl.pallas_call(
        _inproj_kernel,
        grid=(m // tm, Z_W // Z_TN),
        in_specs=[pl.BlockSpec((tm, D_MODEL), lambda i, j: (i, 0)),
                  pl.BlockSpec((D_MODEL, ZB), lambda i, j: (0, _w_in_block(2 * j))),
                  pl.BlockSpec((D_MODEL, ZB), lambda i, j: (0, _w_in_block(2 * j + 1)))],
        out_specs=pl.BlockSpec((tm, Z_TN), lambda i, j: (i, j)),
        out_shape=jax.ShapeDtypeStruct((m, Z_W), F32),
        compiler_params=_params(("arbitrary", "arbitrary")),
        name="inproj",
    )(h, w, w)


def _bd(w, mask):
    return jnp.where(mask, jnp.concatenate([w] * HPG, axis=0), 0.0).astype(BF16)


def _bd_heads(nat, mask):
    return jnp.where(mask, jnp.concatenate([nat] * HPG, axis=1), 0.0).astype(BF16)


def _diag_blocks(m):
    return jnp.concatenate([m[h * HEAD:(h + 1) * HEAD, h * HEAD:(h + 1) * HEAD] for h in range(HPG)],
                           axis=0)


def _wkv_kernel(*refs, lb, carried, nch, nseq):
    R = WKV_ROWS
    if carried:
        (zr_ref, zk_ref, zv_ref, zl_ref,
         mur_ref, muk_ref, muv_ref, mul_ref,
         w0_ref, a0_ref, kkp_ref, kap_ref, rkp_ref, gng_ref, gnb_ref,
         wd_ref, wa_ref, wg_ref,
         oa_ref, so_ref, sho_ref,
         s_ref, cr_ref, ck_ref, cv_ref, cl_ref) = refs
        carry = {id(zr_ref): cr_ref, id(zk_ref): ck_ref, id(zv_ref): cv_ref, id(zl_ref): cl_ref}
    else:
        (zr_ref, zk_ref, zv_ref, zl_ref,
         pr_ref, pk_ref, pv_ref, pl_ref, si_ref,
         mur_ref, muk_ref, muv_ref, mul_ref,
         w0_ref, a0_ref, kkp_ref, kap_ref, rkp_ref, gng_ref, gnb_ref,
         wd_ref, wa_ref, wg_ref,
         oa_ref, so_ref, sho_ref) = refs
        prev_of = {id(zr_ref): pr_ref, id(zk_ref): pk_ref, id(zv_ref): pv_ref, id(zl_ref): pl_ref}
    nblk = R // lb
    z_refs = (zr_ref, zk_ref, zv_ref, zl_ref)
    mu_refs = (mur_ref, muk_ref, muv_ref, mul_ref)

    if carried:
        @pl.when(pl.program_id(1) == 0)
        def _():
            s_ref[...] = jnp.zeros_like(s_ref)
            for ref in (cr_ref, ck_ref, cv_ref, cl_ref):
                ref[...] = jnp.zeros_like(ref)

    row1 = lax.broadcasted_iota(jnp.int32, (R, 1), 0)
    first = (row1 % lb) == 0
    ri = lax.broadcasted_iota(jnp.int32, (GRP, GRP), 0)
    ci = lax.broadcasted_iota(jnp.int32, (GRP, GRP), 1)
    bdm = (ri // HEAD) == (ci // HEAD)
    ones_bd = jnp.where(bdm, 1.0, 0.0).astype(BF16)
    tr = lax.broadcasted_iota(jnp.int32, (R, R), 0)
    tc = lax.broadcasted_iota(jnp.int32, (R, R), 1)
    same = (tr // lb) == (tc // lb)
    incl01 = jnp.where(same & (tc <= tr), 1.0, 0.0).astype(BF16)
    same01 = jnp.where(same, 1.0, 0.0).astype(BF16)
    wr = lax.broadcasted_iota(jnp.int32, (R, GRP), 0)
    wc = lax.broadcasted_iota(jnp.int32, (R, GRP), 1) % HEAD
    wsame = (wr // lb) == (wc // lb)
    strict_w = wsame & (wc < wr)
    incl_w = wsame & (wc <= wr)
    eye_w = jnp.where(wc == wr, 1.0, 0.0)
    sls = [slice(gi * GRP, (gi + 1) * GRP) for gi in range(N_GRP)]
    blks = [slice(b * lb, (b + 1) * lb) for b in range(nblk)]
    units = [(s, gi) for s in range(nseq) for gi in range(N_GRP)]
    uids = range(len(units))

    def seg_sums(xs):
        parts = []
        for x in xs:
            parts.extend(_split2(x))
        stacked = jnp.concatenate(parts, axis=0)
        cols = [_dot(stacked[:, sl], ones_bd) for sl in sls]
        full = jnp.concatenate(cols, axis=1)
        return [full[2 * i * R:(2 * i + 1) * R] + full[(2 * i + 1) * R:(2 * i + 2) * R]
                for i in range(len(xs))]

    def cum(m01, x):
        hi, mid, lo = _split3(x)
        return _dot(m01, hi) + _dot(m01, mid) + _dot(m01, lo)

    def prep(s, c, out):
        rows = slice(c * R, (c + 1) * R)

        def shift(z_ref, mu_ref):
            z = z_ref[s, rows, :]
            if not carried:
                prev0 = prev_of[id(z_ref)][s, rows, :]
            elif c == 0:
                prev0 = carry[id(z_ref)][s]
            else:
                prev0 = z_ref[s, c * R - 1:c * R, :]
            prev = jnp.where(first, prev0, pltpu.roll(z, 1, axis=0))
            return z + (prev - z) * mu_ref[...]

        r, k, v, l = [shift(z_ref, mu_ref) for z_ref, mu_ref in zip(z_refs, mu_refs)]
        w_raw = w0_ref[...] + _dot(jnp.tanh(l).astype(BF16), wd_ref[...])
        a_raw = a0_ref[...] + _dot(l.astype(BF16), wa_ref[...])
        g = _dot(jax.nn.sigmoid(l).astype(BF16), wg_ref[...])
        yield
        logw = jax.nn.sigmoid(w_raw) * (-math.exp(-0.5))
        a = jax.nn.sigmoid(a_raw)
        kk = k * kkp_ref[...]
        k2 = k * (1.0 + (a - 1.0) * kap_ref[...])
        ss, rk = seg_sums([kk * kk, r * k2 * rkp_ref[...]])
        cl = cum(incl01, logw)
        cl_end = cl[R - 1:R, :] if nblk == 1 else cum(same01, logw)
        yield
        kk = kk / jnp.maximum(jnp.sqrt(ss), 1e-12)
        av = -kk
        bv = kk * a
        e_neg = jnp.exp(-cl)
        e_end = jnp.exp(cl_end - cl)
        out.update(v=v, g=g, bonus=rk * v,
                   at=av * jnp.exp(cl - logw), rt=r * jnp.exp(cl),
                   bt=bv * e_neg, kt=k2 * e_neg, bh=bv * e_end, kh=k2 * e_end,
                   p_end=jnp.exp(cl_end))

    def main(c, ps):
        rows_c = slice(c * R, (c + 1) * R)
        col = lambda name, ui: ps[units[ui][0]][name][:, sls[units[ui][1]]]
        v_g = [col("v", ui) for ui in uids]
        lhs = [jnp.concatenate([col("at", ui), col("rt", ui)], axis=0).astype(BF16) for ui in uids]
        o1 = [_dot_nt(lhs[ui], jnp.concatenate([_bd(col("bt", ui), bdm), _bd(col("kt", ui), bdm)], axis=0))
              for ui in uids]
        yield
        w_ab = [jnp.where(strict_w, o[:R, :GRP], 0.0) for o in o1]
        w_ak = [jnp.where(strict_w, o[:R, GRP:], 0.0) for o in o1]
        w_rb = [jnp.where(incl_w, o[R:, :GRP], 0.0) for o in o1]
        w_rk = [jnp.where(incl_w, o[R:, GRP:], 0.0) for o in o1]

        def state_part(ui):
            s, gi = units[ui]
            if carried:
                o2 = _dot_nt(lhs[ui], s_ref[s, gi].astype(BF16))
                return o2[:R], o2[R:]
            at_u, rt_u = col("at", ui), col("rt", ui)
            a_rows, r_rows = [], []
            for b, rows in enumerate(blks):
                lhs_b = jnp.concatenate([at_u[rows], rt_u[rows]], axis=0).astype(BF16)
                o2 = _dot_nt(lhs_b, _bd_heads(si_ref[s, c * nblk + b, sls[gi], :], bdm))
                a_rows.append(o2[:lb])
                r_rows.append(o2[lb:])
            return jnp.concatenate(a_rows, axis=0), jnp.concatenate(r_rows, axis=0)

        x_w = list(w_ab)
        t_w = [eye_w + x for x in x_w]
        n_lvl = max(1, int(math.log2(lb)))
        a_s = r_s = o3 = None
        for j in range(n_lvl):
            for ui in uids:
                y_bd = _bd(x_w[ui], bdm)
                if j == 0:
                    x_w[ui] = _dot(x_w[ui].astype(BF16), y_bd)
                elif j == n_lvl - 1:
                    t_w[ui] = t_w[ui] + _dot(t_w[ui].astype(BF16), y_bd)
                else:
                    res = _dot(jnp.concatenate([t_w[ui], x_w[ui]], axis=0).astype(BF16), y_bd)
                    t_w[ui] = t_w[ui] + res[:R]
                    x_w[ui] = res[R:]
            yield
            if j == 0:
                parts = [state_part(ui) for ui in uids]
                a_s, r_s = [q[0] for q in parts], [q[1] for q in parts]
                yield
            elif j == 1:
                o3 = [_dot(jnp.concatenate([w_ak[ui], w_rk[ui]], axis=0).astype(BF16), _bd(v_g[ui], bdm))
                      for ui in uids]
                yield

        u = [_dot(t_w[ui].astype(BF16), _bd(a_s[ui] + o3[ui][:R], bdm)) for ui in uids]
        yield
        o_parts = [r_s[ui] + o3[ui][R:] + _dot(w_rb[ui].astype(BF16), _bd(u[ui], bdm)) for ui in uids]
        yield

        for ui in uids:
            s, gi = units[ui]
            sl = sls[gi]
            bh_u, kh_u, p_end = col("bh", ui), col("kh", ui), ps[s]["p_end"][:, sl]
            if carried:
                upd = _dot_tn(jnp.concatenate([u[ui], v_g[ui]], axis=0).astype(BF16),
                              jnp.concatenate([bh_u, kh_u], axis=0).astype(BF16))
                s_new = jnp.where(bdm, s_ref[s, gi] * p_end + upd, 0.0)
                s_ref[s, gi] = s_new
                if c == nch - 1:
                    so_ref[s, sl, :] = _diag_blocks(s_new)
            else:
                for b, rows in enumerate(blks):
                    upd = _dot_tn(jnp.concatenate([u[ui][rows], v_g[ui][rows]], axis=0).astype(BF16),
                                  jnp.concatenate([bh_u[rows], kh_u[rows]], axis=0).astype(BF16))
                    p_b = p_end[b * lb:b * lb + 1]
                    p_nat = jnp.concatenate(
                        [jnp.broadcast_to(p_b[:, h * HEAD:(h + 1) * HEAD], (HEAD, HEAD)) for h in range(HPG)],
                        axis=0)
                    so_ref[s, c * nblk + b, sl, :] = (si_ref[s, c * nblk + b, sl, :] * p_nat
                                                      + _diag_blocks(upd))
        yield

        o = [jnp.concatenate(o_parts[s * N_GRP:(s + 1) * N_GRP], axis=1) for s in range(nseq)]
        mu = seg_sums(o)
        yield
        dlt = [o[s] - mu[s] * (1.0 / HEAD) for s in range(nseq)]
        var = seg_sums([d * d for d in dlt])
        for s in range(nseq):
            on = dlt[s] * lax.rsqrt(var[s] * (1.0 / HEAD) + GN_EPS) * gng_ref[...] + gnb_ref[...]
            oa_ref[s, rows_c, :] = ((on + ps[s]["bonus"]) * ps[s]["g"]).astype(BF16)

    def run(gen):
        for _ in gen:
            pass

    cur = [{} for _ in range(nseq)]
    for s in range(nseq):
        run(prep(s, 0, cur[s]))
    for c in range(nch):
        nxt = [{} for _ in range(nseq)]
        sides = [prep(s, c + 1, nxt[s]) for s in range(nseq)] if c + 1 < nch else []
        for tick, _ in enumerate(main(c, cur)):
            if tick % 3 == 1:
                for side in sides:
                    next(side, None)
        for side in sides:
            run(side)
        cur = nxt

    n_rows = nch * R
    for s in range(nseq):
        if carried:
            last = lambda z_ref: z_ref[s, n_rows - 1:n_rows, :]
            for z_ref in z_refs:
                carry[id(z_ref)][s] = last(z_ref)
        else:
            last = lambda z_ref: jnp.concatenate(
                [z_ref[s, b * lb + lb - 1:b * lb + lb, :] for b in range(nch * nblk)], axis=0)
        sho_ref[s, :, 0:RW] = last(zr_ref)
        sho_ref[s, :, RW:2 * RW] = last(zk_ref)
        sho_ref[s, :, 2 * RW:3 * RW] = last(zv_ref)
        sho_ref[s, :, 3 * RW:] = last(zl_ref)


def _wkv(z, vecs, mats, n_seq, seq_len, lb, nch, nseq, prev_rows=None, state_in=None):
    R = nch * WKV_ROWS
    carried = prev_rows is None
    rows = z.shape[0]
    n_streams = n_seq if carried else nseq
    stream_rows = rows // n_streams
    assert n_streams % nseq == 0 and stream_rows % R == 0 and rows == n_streams * stream_rows
    z3 = z.reshape(n_streams, stream_rows, Z_W)
    if carried:
        grid = (n_streams // nseq, stream_rows // R)
        rmap = lambda c: (lambda b, t: (b, t, c))
        cmap = lambda b, t: (0, 0)
        smap = lambda b, t: (b, 0, 0)
        sem = ("arbitrary", "arbitrary")
    else:
        grid = (stream_rows // R,)
        rmap = lambda c: (lambda i: (0, i, c))
        cmap = lambda i: (0, 0)
        sem = ("arbitrary",)
        blk_per_step = R // lb
        seq_per_stream = n_seq // nseq

    in_specs = [pl.BlockSpec((nseq, R, RW), rmap(Z_R // RW)),
                pl.BlockSpec((nseq, R, RW), rmap(Z_K // RW)),
                pl.BlockSpec((nseq, R, RW), rmap(Z_VR // RW)),
                pl.BlockSpec((nseq, R, LORA), rmap(Z_L // LORA))]
    args = [z3, z3, z3, z3]
    if not carried:
        p3 = prev_rows.reshape(nseq, stream_rows, C_SHIFT)
        st4 = state_in.reshape(nseq, seq_per_stream, RW, HEAD)
        in_specs += [pl.BlockSpec((nseq, R, RW), rmap(0)), pl.BlockSpec((nseq, R, RW), rmap(1)),
                     pl.BlockSpec((nseq, R, RW), rmap(2)), pl.BlockSpec((nseq, R, LORA), rmap(3 * RW // LORA)),
                     pl.BlockSpec((nseq, blk_per_step, RW, HEAD), lambda i: (0, i, 0, 0))]
        args += [p3] * 4 + [st4]
    for a in vecs + mats:
        in_specs.append(pl.BlockSpec(a.shape, cmap))
        args.append(a)

    if carried:
        scratch = [pltpu.VMEM((nseq, N_GRP, GRP, GRP), F32), pltpu.VMEM((nseq, 1, RW), F32),
                   pltpu.VMEM((nseq, 1, RW), F32), pltpu.VMEM((nseq, 1, RW), F32),
                   pltpu.VMEM((nseq, 1, LORA), F32)]
        state_spec = pl.BlockSpec((nseq, RW, HEAD), smap)
        state_shape = (n_seq, RW, HEAD)
        shift_spec = pl.BlockSpec((nseq, 1, C_SHIFT), smap)
        shift_shape = (n_seq, 1, C_SHIFT)
    else:
        scratch = []
        state_spec = pl.BlockSpec((nseq, blk_per_step, RW, HEAD), lambda i: (0, i, 0, 0))
        state_shape = (nseq, seq_per_stream, RW, HEAD)
        shift_spec = pl.BlockSpec((nseq, blk_per_step, C_SHIFT), lambda i: (0, i, 0))
        shift_shape = (nseq, seq_per_stream, C_SHIFT)
    oa, st, sh = pl.pallas_call(
        functools.partial(_wkv_kernel, lb=lb, carried=carried, nch=nch, nseq=nseq),
        grid=grid,
        in_specs=in_specs,
        out_specs=[pl.BlockSpec((nseq, R, RW), rmap(0)), state_spec, shift_spec],
        out_shape=[jax.ShapeDtypeStruct((n_streams, stream_rows, RW), BF16),
                   jax.ShapeDtypeStruct(state_shape, F32),
                   jax.ShapeDtypeStruct(shift_shape, F32)],
        scratch_shapes=scratch,
        compiler_params=_params(sem),
        name="wkv_carried" if carried else "wkv_blocks",
    )(*args)
    return oa.reshape(rows, RW), st.reshape(n_seq, RW, HEAD), sh.reshape(n_seq, C_SHIFT)


def _gmlp_kernel(zu_ref, zv_ref, lng_ref, lnb_ref, ws_ref, bias_ref, ub_ref, vn_ref=None, *, seq):
    reps = CHUNK // seq
    u = jax.nn.gelu(zu_ref[...])
    vg = jax.nn.gelu(zv_ref[...])
    mu = jnp.mean(vg, -1, keepdims=True)
    var = jnp.mean(jnp.square(vg - mu), -1, keepdims=True)
    vn = (vg - mu) * lax.rsqrt(var + LN_EPS) * lng_ref[...] + lnb_ref[...]
    if vn_ref is not None:
        vn_ref[...] = vn
    tr = lax.broadcasted_iota(jnp.int32, (CHUNK, CHUNK), 0)
    tc = lax.broadcasted_iota(jnp.int32, (CHUNK, CHUNK), 1)
    causal = (tc <= tr) & ((tr // seq) == (tc // seq))
    vb = vn.astype(BF16)
    cols = []
    for gi in range(MLP_GROUPS):
        w_rows = ws_ref[gi]
        w_full = jnp.concatenate([w_rows if s == 0 else pltpu.roll(w_rows, s * seq, axis=1)
                                  for s in range(reps)], axis=0)
        wsg = jnp.where(causal, w_full, 0.0).astype(BF16)
        cols.append(_dot(wsg, vb[:, gi * MLP_GD:(gi + 1) * MLP_GD]))
    mixed = jnp.concatenate(cols, axis=1) + jnp.concatenate([bias_ref[...]] * reps, axis=0)
    ub_ref[...] = (u * mixed).astype(BF16)


def _gmlp(z, ln_g, ln_b, ws, bias, seq, want_vn):
    rows = z.shape[0]
    n_out = 2 if want_vn else 1
    return pl.pallas_call(
        functools.partial(_gmlp_kernel, seq=seq),
        grid=(rows // CHUNK,),
        in_specs=[pl.BlockSpec((CHUNK, MLP_W), lambda i: (i, Z_U // MLP_W)),
                  pl.BlockSpec((CHUNK, MLP_W), lambda i: (i, Z_V // MLP_W)),
                  pl.BlockSpec((1, MLP_W), lambda i: (0, 0)),
                  pl.BlockSpec((1, MLP_W), lambda i: (0, 0)),
                  pl.BlockSpec((MLP_GROUPS, seq, CHUNK), lambda i: (0, 0, 0)),
                  pl.BlockSpec((seq, MLP_W), lambda i: (0, 0))],
        out_specs=[pl.BlockSpec((CHUNK, MLP_W), lambda i: (i, 0)),
                   pl.BlockSpec((CHUNK, MLP_W), lambda i: (i, 0))][:n_out],
        out_shape=[jax.ShapeDtypeStruct((rows, MLP_W), BF16),
                   jax.ShapeDtypeStruct((rows, MLP_W), F32)][:n_out],
        compiler_params=_params(("arbitrary",)),
        name="gmlp",
    )(z, z, ln_g, ln_b, ws, bias)


def _merge_kernel(oa_ref, ub_ref, zga_ref, zgb_ref, x_ref, gm_ref, sc_ref, sh_ref, g_ref,
                  wa_ref, wb_ref, wo_ref, x1_ref, h2_ref):
    ya = _dot(oa_ref[...], wa_ref[...])
    yb = _dot(ub_ref[...], wb_ref[...])
    merged = jax.nn.sigmoid(zga_ref[...]) * ya + jax.nn.sigmoid(zgb_ref[...]) * yb
    mix = _dot(merged.astype(BF16), wo_ref[...])
    x1 = x_ref[...] + gm_ref[...] * mix
    x1_ref[...] = x1
    h2_ref[...] = (_rms(x1, g_ref[...]) * (1.0 + sc_ref[...]) + sh_ref[...]).astype(BF16)


def _merge(oa, ub, z, x, mod, g_ffn, wa, wb, wo, per_row, rows_per_seq):
    m, tm = x.shape[0], (128 if per_row else 256)
    const = lambda shape: pl.BlockSpec(shape, lambda i: (0, 0), pipeline_mode=pl.Buffered(1))
    mspec = lambda k: _mod_spec(k, per_row, tm, rows_per_seq)
    rows = lambda w, c: pl.BlockSpec((tm, w), lambda i: (i, c))
    return pl.pallas_call(
        _merge_kernel,
        grid=(m // tm,),
        in_specs=[rows(RW, 0), rows(MLP_W, 0),
                  rows(D_MODEL, Z_GA // D_MODEL), rows(D_MODEL, Z_GB // D_MODEL), rows(D_MODEL, 0),
                  mspec(MOD_GATE_M), mspec(MOD_SCALE_F), mspec(MOD_SHIFT_F),
                  pl.BlockSpec((1, D_MODEL), lambda i: (0, 0)),
                  const((RW, D_MODEL)), const((MLP_W, D_MODEL)), const((D_MODEL, D_MODEL))],
        out_specs=[rows(D_MODEL, 0), rows(D_MODEL, 0)],
        out_shape=[jax.ShapeDtypeStruct((m, D_MODEL), F32), jax.ShapeDtypeStruct((m, D_MODEL), BF16)],
        compiler_params=_params(("arbitrary",)),
        name="merge",
    )(oa, ub, z, z, x, mod, mod, mod, g_ffn, wa, wb, wo)


def _ffn_kernel(h_ref, x_ref, gf_ref, gfin_ref, wg_ref, wu_ref, wo_ref, y_ref):
    j = pl.program_id(1)

    @pl.when(j == 0)
    def _():
        y_ref[...] = jnp.zeros_like(y_ref)

    hb = h_ref[...]
    gt = _dot(hb, wg_ref[...].astype(BF16))
    up = _dot(hb, wu_ref[...].astype(BF16))
    act = gt * jax.nn.sigmoid(gt) * up
    y_ref[...] += _dot(act.astype(BF16), wo_ref[...].astype(BF16))

    @pl.when(j == pl.num_programs(1) - 1)
    def _():
        x2 = x_ref[...] + gf_ref[...] * y_ref[...]
        y_ref[...] = _rms(x2, gfin_ref[...])


def _ffn(h2, x1, mod, g_final, w_in, w_out, per_row, rows_per_seq):
    m, tm, tf = x1.shape[0], (512 if per_row else 1024), 256
    nf = D_FF // tf
    return pl.pallas_call(
        _ffn_kernel,
        grid=(m // tm, nf),
        in_specs=[pl.BlockSpec((tm, D_MODEL), lambda i, j: (i, 0)),
                  pl.BlockSpec((tm, D_MODEL), lambda i, j: (i, 0), pipeline_mode=pl.Buffered(1)),
                  _mod_spec(MOD_GATE_F, per_row, tm, rows_per_seq),
                  pl.BlockSpec((1, D_MODEL), lambda i, j: (0, 0)),
                  pl.BlockSpec((D_MODEL, tf), lambda i, j: (0, j)),
                  pl.BlockSpec((D_MODEL, tf), lambda i, j: (0, nf + j)),
                  pl.BlockSpec((tf, D_MODEL), lambda i, j: (j, 0))],
        out_specs=pl.BlockSpec((tm, D_MODEL), lambda i, j: (i, 0)),
        out_shape=jax.ShapeDtypeStruct((m, D_MODEL), F32),
        compiler_params=_params(("arbitrary", "arbitrary")),
        name="ffn",
    )(h2, x1, mod, g_final, w_in, w_in, w_out)


def kernel(x_prompt, x_sample, state_wkv, state_shift, c_prompt, c_sample, w_ada, b_ada, norm_mix_g, w_in, mu_shift, w0, w_decay_up, a0, w_aaa_up, w_gate_up, k_k, k_a, r_k, gn_g, gn_b, ln_v_g, ln_v_b, w_spatial, b_spatial, w_branch_a, w_branch_b, w_out, norm_ffn_g, w_ffn_in, w_ffn_out, norm_final_g):
    assert w_ada.shape[0] == 1, "single layer"
    bp, tp, _ = x_prompt.shape
    bs, ts, _ = x_sample.shape
    assert tp % CHUNK == 0 and WKV_ROWS % ts == 0 and CHUNK % ts == 0

    c_all = jnp.concatenate([c_sample, c_prompt], axis=0)
    c_all = jnp.pad(c_all, ((0, -c_all.shape[0] % 8), (0, 0)))
    mod_s = _ada(c_all, w_ada[0], b_ada[0], bs, ts)
    mod_p = mod_s[bs * ts:bs * ts + bp].reshape(bp, 1, 6 * D_MODEL)

    wa_b, wb_b, wo_b = w_branch_a[0].astype(BF16), w_branch_b[0].astype(BF16), w_out[0].astype(BF16)
    wfi, wfo = w_ffn_in[0], w_ffn_out[0]

    row = lambda a: a.reshape(1, -1)
    mu = mu_shift[0]
    vecs = [row(mu[:RW]), row(mu[RW:2 * RW]), row(mu[2 * RW:3 * RW]), row(mu[3 * RW:]),
            row(w0[0]), row(a0[0]), row(k_k[0]), row(k_a[0]), row(r_k[0]), row(gn_g[0]), row(gn_b[0])]
    zpad = lambda w, lo: jnp.pad(w, ((lo, LORA - lo - w.shape[0]), (0, 0))).astype(BF16)
    mats = [zpad(w_decay_up[0], 0), zpad(w_aaa_up[0], DECAY_LORA),
            zpad(w_gate_up[0], DECAY_LORA + AAA_LORA)]

    bias = jnp.repeat(b_spatial[0].T, MLP_GD, axis=1)
    ws = w_spatial[0]

    xp = x_prompt.reshape(bp * tp, D_MODEL)
    xs = x_sample.reshape(bs * ts, D_MODEL)
    g_mix, g_ffn, g_fin = row(norm_mix_g[0]), row(norm_ffn_g[0]), row(norm_final_g)
    lng, lnb = row(ln_v_g[0]), row(ln_v_b[0])

    zp = _inproj(_hnorm(xp, mod_p, g_mix, False, tp), w_in[0])
    oa_p, st_p, shift_p = _wkv(zp, vecs, mats, bp, tp, WKV_ROWS, WKV_NCH, WKV_NSEQ)
    (ub_p,) = _gmlp(zp, lng, lnb, ws, bias, CHUNK, False)
    x1_p, h2_p = _merge(oa_p, ub_p, zp, xp, mod_p, g_ffn, wa_b, wb_b, wo_b, False, tp)
    y_p = _ffn(h2_p, x1_p, mod_p, g_fin, wfi, wfo, False, tp)

    zs = _inproj(_hnorm(xs, mod_s, g_mix, True, ts), w_in[0])
    prev_rows = jnp.repeat(state_shift[0], ts, axis=0)
    oa_s, st_s, shift_s = _wkv(zs, vecs, mats, bs, ts, ts, 1, WKV_NSEQ, prev_rows=prev_rows,
                               state_in=state_wkv[0].reshape(bs, RW, HEAD))
    ub_s, vn_s = _gmlp(zs, lng, lnb, ws, bias, ts, True)
    x1_s, h2_s = _merge(oa_s, ub_s, zs, xs, mod_s, g_ffn, wa_b, wb_b, wo_b, True, ts)
    y_s = _ffn(h2_s, x1_s, mod_s, g_fin, wfi, wfo, True, ts)

    return (y_p.reshape(bp, tp, D_MODEL),
            y_s.reshape(bs, ts, D_MODEL),
            st_p.reshape(1, bp, N_HEADS, HEAD, HEAD),
            shift_p.reshape(1, bp, C_SHIFT),
            st_s.reshape(1, bs, N_HEADS, HEAD, HEAD),
            shift_s.reshape(1, bs, C_SHIFT),
            vn_s.reshape(bs, ts, MLP_W)[None])
```

```python
import functools
import math

import jax
import jax.numpy as jnp
from jax import lax
from jax.experimental import pallas as pl
from jax.experimental.pallas import tpu as pltpu

F32 = jnp.float32
BF16 = jnp.bfloat16

D_MODEL = 2048
HEAD = 64
RW = D_MODEL // 2
N_HEADS = RW // HEAD
DECAY_LORA = D_MODEL // 32
AAA_LORA = D_MODEL // 32
GATE_LORA = D_MODEL // 16
LORA = DECAY_LORA + AAA_LORA + GATE_LORA
CHUNK = 128
MLP_W = D_MODEL // 2
MLP_GROUPS = 8
MLP_GD = MLP_W // MLP_GROUPS
D_FF = ((-(-8 * D_MODEL // 3)) + 255) // 256 * 256
C_SHIFT = 3 * RW + LORA
C_IN = C_SHIFT + 2 * MLP_W + 2 * D_MODEL
NORM_EPS = 1e-6
GN_EPS = 64e-5
LN_EPS = 1e-5

Z_GA, Z_GB, Z_U, Z_V, Z_R, Z_K, Z_VR, Z_L = 0, 2048, 4096, 5120, 6144, 7168, 8192, 9216
Z_TN = 512
Z_W = -(-C_IN // Z_TN) * Z_TN

GRP = 256
HPG = GRP // HEAD
N_GRP = RW // GRP
WKV_ROWS = 64
WKV_NCH = 2
WKV_NSEQ = 2
VMEM_LIMIT = 56 * 1024 * 1024


def _dot(a, b):
    return jnp.dot(a, b, preferred_element_type=F32)


def _dot_nt(a, b):
    return lax.dot_general(a, b, (((1,), (1,)), ((), ())), preferred_element_type=F32)


def _dot_tn(a, b):
    return lax.dot_general(a, b, (((0,), (0,)), ((), ())), preferred_element_type=F32)


def _rms(x, g):
    return x * lax.rsqrt(jnp.mean(x * x, -1, keepdims=True) + NORM_EPS) * g


def _split2(x):
    hi = x.astype(BF16)
    lo = (x - hi.astype(F32)).astype(BF16)
    return hi, lo


def _split3(x):
    hi = x.astype(BF16)
    r1 = x - hi.astype(F32)
    mid = r1.astype(BF16)
    lo = (r1 - mid.astype(F32)).astype(BF16)
    return hi, mid, lo


def _params(sem):
    return pltpu.CompilerParams(dimension_semantics=sem, vmem_limit_bytes=VMEM_LIMIT)


def _ada_kernel(c_ref, w_ref, b_ref, o_ref, s_ref, e_ref, *, n_rep, rep):
    m_out, m_in = e_ref.shape

    @pl.when(pl.program_id(0) == 0)
    def _():
        c = c_ref[...]
        s_ref[...] = (c * jax.nn.sigmoid(c)).astype(BF16)
        r = lax.broadcasted_iota(jnp.int32, (m_out, m_in), 0)
        s = lax.broadcasted_iota(jnp.int32, (m_out, m_in), 1)
        src = jnp.where(r < n_rep * rep, r // rep, r - n_rep * (rep - 1))
        e_ref[...] = jnp.where(s == src, 1.0, 0.0).astype(BF16)

    mod = _dot(s_ref[...], w_ref[...].astype(BF16)) + b_ref[...]
    hi, mid, lo = _split3(mod)
    e = e_ref[...]
    o_ref[...] = _dot(e, hi) + _dot(e, mid) + _dot(e, lo)


def _ada(c, w_ada, b_ada, n_rep, rep):
    m_in, n, tn = c.shape[0], w_ada.shape[1], 512
    m_out = m_in + n_rep * (rep - 1)
    return pl.pallas_call(
        functools.partial(_ada_kernel, n_rep=n_rep, rep=rep),
        grid=(n // tn,),
        in_specs=[pl.BlockSpec((m_in, D_MODEL), lambda j: (0, 0)),
                  pl.BlockSpec((D_MODEL, tn), lambda j: (0, j)),
                  pl.BlockSpec((1, tn), lambda j: (0, j))],
        out_specs=pl.BlockSpec((m_out, tn), lambda j: (0, j)),
        out_shape=jax.ShapeDtypeStruct((m_out, n), F32),
        scratch_shapes=[pltpu.VMEM((m_in, D_MODEL), BF16), pltpu.VMEM((m_out, m_in), BF16)],
        compiler_params=_params(("arbitrary",)),
        name="ada",
    )(c, w_ada, b_ada.reshape(1, n))


MOD_SHIFT_M, MOD_SCALE_M, MOD_GATE_M, MOD_SHIFT_F, MOD_SCALE_F, MOD_GATE_F = range(6)


def _mod_spec(k, per_row, tm, rows_per_seq):
    if per_row:
        return pl.BlockSpec((tm, D_MODEL), lambda i, *_: (i, k))
    return pl.BlockSpec((None, 1, D_MODEL), lambda i, *_: (i * tm // rows_per_seq, 0, k))


ZB = 256
_N_RW_B, _N_MLP_B, _N_GATE_B = C_SHIFT // ZB, 2 * MLP_W // ZB, 2 * D_MODEL // ZB


def _w_in_block(zb):
    return jnp.where(zb < _N_GATE_B, zb + _N_RW_B + _N_MLP_B,
                     jnp.where(zb < _N_GATE_B + _N_MLP_B, zb - _N_GATE_B + _N_RW_B,
                               jnp.minimum(zb - _N_GATE_B - _N_MLP_B, _N_RW_B - 1)))


def _hnorm_kernel(x_ref, sc_ref, sh_ref, g_ref, h_ref):
    h = _rms(x_ref[...], g_ref[...]) * (1.0 + sc_ref[...]) + sh_ref[...]
    h_ref[...] = h.astype(BF16)


def _hnorm(x, mod, g, per_row, rows_per_seq):
    m, tm = x.shape[0], (512 if per_row else 1024)
    mspec = lambda k: _mod_spec(k, per_row, tm, rows_per_seq)
    return pl.pallas_call(
        _hnorm_kernel,
        grid=(m // tm,),
        in_specs=[pl.BlockSpec((tm, D_MODEL), lambda i: (i, 0)),
                  mspec(MOD_SCALE_M), mspec(MOD_SHIFT_M),
                  pl.BlockSpec((1, D_MODEL), lambda i: (0, 0))],
        out_specs=pl.BlockSpec((tm, D_MODEL), lambda i: (i, 0)),
        out_shape=jax.ShapeDtypeStruct((m, D_MODEL), BF16),
        compiler_params=_params(("arbitrary",)),
        name="hnorm",
    )(x, mod, mod, g)


def _inproj_kernel(h_ref, wlo_ref, whi_ref, o_ref):
    h = h_ref[...]
    o_ref[:, :ZB] = _dot(h, wlo_ref[...].astype(BF16))
    o_ref[:, ZB:] = _dot(h, whi_ref[...].astype(BF16))


def _inproj(h, w):
    m = h.shape[0]
    tm = min(m, 2048)
    assert Z_TN == 2 * ZB and m % tm == 0
    return pl.pallas_call(
        _inproj_kernel,
        grid=(m // tm, Z_W // Z_TN),
        in_specs=[pl.BlockSpec((tm, D_MODEL), lambda i, j: (i, 0)),
                  pl.BlockSpec((D_MODEL, ZB), lambda i, j: (0, _w_in_block(2 * j))),
                  pl.BlockSpec((D_MODEL, ZB), lambda i, j: (0, _w_in_block(2 * j + 1)))],
        out_specs=pl.BlockSpec((tm, Z_TN), lambda i, j: (i, j)),
        out_shape=jax.ShapeDtypeStruct((m, Z_W), F32),
        compiler_params=_params(("arbitrary", "arbitrary")),
        name="inproj",
    )(h, w, w)


def _bd(w, mask):
    return jnp.where(mask, jnp.concatenate([w] * HPG, axis=0), 0.0).astype(BF16)


def _bd_heads(nat, mask):
    return jnp.where(mask, jnp.concatenate([nat] * HPG, axis=1), 0.0).astype(BF16)


def _diag_blocks(m):
    return jnp.concatenate([m[h * HEAD:(h + 1) * HEAD, h * HEAD:(h + 1) * HEAD] for h in range(HPG)],
                           axis=0)


def _wkv_kernel(*refs, lb, carried, nch, nseq):
    R = WKV_ROWS
    if carried:
        (zr_ref, zk_ref, zv_ref, zl_ref,
         mur_ref, muk_ref, muv_ref, mul_ref,
         w0_ref, a0_ref, kkp_ref, kap_ref, rkp_ref, gng_ref, gnb_ref,
         wd_ref, wa_ref, wg_ref,
         oa_ref, so_ref, sho_ref,
         s_ref, cr_ref, ck_ref, cv_ref, cl_ref) = refs
        carry = {id(zr_ref): cr_ref, id(zk_ref): ck_ref, id(zv_ref): cv_ref, id(zl_ref): cl_ref}
    else:
        (zr_ref, zk_ref, zv_ref, zl_ref,
         pr_ref, pk_ref, pv_ref, pl_ref, si_ref,
         mur_ref, muk_ref, muv_ref, mul_ref,
         w0_ref, a0_ref, kkp_ref, kap_ref, rkp_ref, gng_ref, gnb_ref,
         wd_ref, wa_ref, wg_ref,
         oa_ref, so_ref, sho_ref) = refs
        prev_of = {id(zr_ref): pr_ref, id(zk_ref): pk_ref, id(zv_ref): pv_ref, id(zl_ref): pl_ref}
    nblk = R // lb
    z_refs = (zr_ref, zk_ref, zv_ref, zl_ref)
    mu_refs = (mur_ref, muk_ref, muv_ref, mul_ref)

    if carried:
        @pl.when(pl.program_id(1) == 0)
        def _():
            s_ref[...] = jnp.zeros_like(s_ref)
            for ref in (cr_ref, ck_ref, cv_ref, cl_ref):
                ref[...] = jnp.zeros_like(ref)

    row1 = lax.broadcasted_iota(jnp.int32, (R, 1), 0)
    first = (row1 % lb) == 0
    ri = lax.broadcasted_iota(jnp.int32, (GRP, GRP), 0)
    ci = lax.broadcasted_iota(jnp.int32, (GRP, GRP), 1)
    bdm = (ri // HEAD) == (ci // HEAD)
    ones_bd = jnp.where(bdm, 1.0, 0.0).astype(BF16)
    tr = lax.broadcasted_iota(jnp.int32, (R, R), 0)
    tc = lax.broadcasted_iota(jnp.int32, (R, R), 1)
    same = (tr // lb) == (tc // lb)
    incl01 = jnp.where(same & (tc <= tr), 1.0, 0.0).astype(BF16)
    same01 = jnp.where(same, 1.0, 0.0).astype(BF16)
    wr = lax.broadcasted_iota(jnp.int32, (R, GRP), 0)
    wc = lax.broadcasted_iota(jnp.int32, (R, GRP), 1) % HEAD
    wsame = (wr // lb) == (wc // lb)
    strict_w = wsame & (wc < wr)
    incl_w = wsame & (wc <= wr)
    eye_w = jnp.where(wc == wr, 1.0, 0.0)
    sls = [slice(gi * GRP, (gi + 1) * GRP) for gi in range(N_GRP)]
    blks = [slice(b * lb, (b + 1) * lb) for b in range(nblk)]
    units = [(s, gi) for s in range(nseq) for gi in range(N_GRP)]
    uids = range(len(units))

    def seg_sums(xs):
        n = xs[0].shape[0]
        parts = []
        for x in xs:
            parts.extend(_split2(x))
        stacked = jnp.concatenate(parts, axis=0)
        cols = [_dot(stacked[:, sl], ones_bd) for sl in sls]
        full = jnp.concatenate(cols, axis=1)
        return [full[2 * i * n:(2 * i + 1) * n] + full[(2 * i + 1) * n:(2 * i + 2) * n]
                for i in range(len(xs))]

    def cum(m01, x):
        hi, mid, lo = _split3(x)
        return _dot(m01, hi) + _dot(m01, mid) + _dot(m01, lo)

    def prep(c, outs):
        rows = slice(c * R, (c + 1) * R)

        def shift(z_ref, mu_ref):
            parts = []
            for s in range(nseq):
                z = z_ref[s, rows, :]
                if not carried:
                    prev0 = prev_of[id(z_ref)][s, rows, :]
                elif c == 0:
                    prev0 = carry[id(z_ref)][s]
                else:
                    prev0 = z_ref[s, c * R - 1:c * R, :]
                prev = jnp.where(first, prev0, pltpu.roll(z, 1, axis=0))
                parts.append(z + (prev - z) * mu_ref[...])
            return jnp.concatenate(parts, axis=0)

        per_stream = lambda x: [x[s * R:(s + 1) * R] for s in range(nseq)]
        r, k, v, l = [shift(z_ref, mu_ref) for z_ref, mu_ref in zip(z_refs, mu_refs)]
        w_raw = w0_ref[...] + _dot(jnp.tanh(l).astype(BF16), wd_ref[...])
        a_raw = a0_ref[...] + _dot(l.astype(BF16), wa_ref[...])
        g = _dot(jax.nn.sigmoid(l).astype(BF16), wg_ref[...])
        yield
        logw = jax.nn.sigmoid(w_raw) * (-math.exp(-0.5))
        a = jax.nn.sigmoid(a_raw)
        kk = k * kkp_ref[...]
        k2 = k * (1.0 + (a - 1.0) * kap_ref[...])
        ss, rk = seg_sums([kk * kk, r * k2 * rkp_ref[...]])
        cl_s = [cum(incl01, lw) for lw in per_stream(logw)]
        if nblk == 1:
            ce_s = [jnp.broadcast_to(x[R - 1:R, :], (R, RW)) for x in cl_s]
        else:
            ce_s = [cum(same01, lw) for lw in per_stream(logw)]
        yield
        cl = jnp.concatenate(cl_s, axis=0)
        cl_end = jnp.concatenate(ce_s, axis=0)
        kk = kk / jnp.maximum(jnp.sqrt(ss), 1e-12)
        av = -kk
        bv = kk * a
        e_neg = jnp.exp(-cl)
        e_end = jnp.exp(cl_end - cl)
        full = dict(v=v, g=g, bonus=rk * v,
                    at=av * jnp.exp(cl - logw), rt=r * jnp.exp(cl),
                    bt=bv * e_neg, kt=k2 * e_neg, bh=bv * e_end, kh=k2 * e_end)
        for name, x in full.items():
            for s, x_s in enumerate(per_stream(x)):
                outs[s][name] = x_s
        for s in range(nseq):
            outs[s]["p_end"] = jnp.exp(cl_s[s][R - 1:R, :] if nblk == 1 else ce_s[s])

    def main(c, ps):
        rows_c = slice(c * R, (c + 1) * R)
        col = lambda name, ui: ps[units[ui][0]][name][:, sls[units[ui][1]]]
        v_g = [col("v", ui) for ui in uids]
        lhs = [jnp.concatenate([col("at", ui), col("rt", ui)], axis=0).astype(BF16) for ui in uids]
        o1 = [_dot_nt(lhs[ui], jnp.concatenate([_bd(col("bt", ui), bdm), _bd(col("kt", ui), bdm)], axis=0))
              for ui in uids]
        yield
        w_ab = [jnp.where(strict_w, o[:R, :GRP], 0.0) for o in o1]
        w_ak = [jnp.where(strict_w, o[:R, GRP:], 0.0) for o in o1]
        w_rb = [jnp.where(incl_w, o[R:, :GRP], 0.0) for o in o1]
        w_rk = [jnp.where(incl_w, o[R:, GRP:], 0.0) for o in o1]

        def state_part(ui):
            s, gi = units[ui]
            if carried:
                o2 = _dot_nt(lhs[ui], s_ref[s, gi].astype(BF16))
                return o2[:R], o2[R:]
            at_u, rt_u = col("at", ui), col("rt", ui)
            a_rows, r_rows = [], []
            for b, rows in enumerate(blks):
                lhs_b = jnp.concatenate([at_u[rows], rt_u[rows]], axis=0).astype(BF16)
                o2 = _dot_nt(lhs_b, _bd_heads(si_ref[s, c * nblk + b, sls[gi], :], bdm))
                a_rows.append(o2[:lb])
                r_rows.append(o2[lb:])
            return jnp.concatenate(a_rows, axis=0), jnp.concatenate(r_rows, axis=0)

        x_w = list(w_ab)
        t_w = [eye_w + x for x in x_w]
        n_lvl = max(1, int(math.log2(lb)))
        a_s = r_s = o3 = None
        for j in range(n_lvl):
            for ui in uids:
                y_bd = _bd(x_w[ui], bdm)
                if j == 0:
                    x_w[ui] = _dot(x_w[ui].astype(BF16), y_bd)
                elif j == n_lvl - 1:
                    t_w[ui] = t_w[ui] + _dot(t_w[ui].astype(BF16), y_bd)
                else:
                    res = _dot(jnp.concatenate([t_w[ui], x_w[ui]], axis=0).astype(BF16), y_bd)
                    t_w[ui] = t_w[ui] + res[:R]
                    x_w[ui] = res[R:]
            yield
            if j == 0:
                parts = [state_part(ui) for ui in uids]
                a_s, r_s = [q[0] for q in parts], [q[1] for q in parts]
                yield
            elif j == 1:
                o3 = [_dot(jnp.concatenate([w_ak[ui], w_rk[ui]], axis=0).astype(BF16), _bd(v_g[ui], bdm))
                      for ui in uids]
                yield

        u = [_dot(t_w[ui].astype(BF16), _bd(a_s[ui] + o3[ui][:R], bdm)) for ui in uids]
        yield
        o_parts = [r_s[ui] + o3[ui][R:] + _dot(w_rb[ui].astype(BF16), _bd(u[ui], bdm)) for ui in uids]
        yield

        for ui in uids:
            s, gi = units[ui]
            sl = sls[gi]
            bh_u, kh_u, p_end = col("bh", ui), col("kh", ui), ps[s]["p_end"][:, sl]
            if carried:
                upd = _dot_tn(jnp.concatenate([u[ui], v_g[ui]], axis=0).astype(BF16),
                              jnp.concatenate([bh_u, kh_u], axis=0).astype(BF16))
                s_new = jnp.where(bdm, s_ref[s, gi] * p_end + upd, 0.0)
                s_ref[s, gi] = s_new
                if c == nch - 1:
                    so_ref[s, sl, :] = _diag_blocks(s_new)
            else:
                for b, rows in enumerate(blks):
                    upd = _dot_tn(jnp.concatenate([u[ui][rows], v_g[ui][rows]], axis=0).astype(BF16),
                                  jnp.concatenate([bh_u[rows], kh_u[rows]], axis=0).astype(BF16))
                    p_b = p_end[b * lb:b * lb + 1]
                    p_nat = jnp.concatenate(
                        [jnp.broadcast_to(p_b[:, h * HEAD:(h + 1) * HEAD], (HEAD, HEAD)) for h in range(HPG)],
                        axis=0)
                    so_ref[s, c * nblk + b, sl, :] = (si_ref[s, c * nblk + b, sl, :] * p_nat
                                                      + _diag_blocks(upd))
        yield

        o = [jnp.concatenate(o_parts[s * N_GRP:(s + 1) * N_GRP], axis=1) for s in range(nseq)]
        mu = seg_sums(o)
        yield
        dlt = [o[s] - mu[s] * (1.0 / HEAD) for s in range(nseq)]
        var = seg_sums([d * d for d in dlt])
        for s in range(nseq):
            on = dlt[s] * lax.rsqrt(var[s] * (1.0 / HEAD) + GN_EPS) * gng_ref[...] + gnb_ref[...]
            oa_ref[s, rows_c, :] = ((on + ps[s]["bonus"]) * ps[s]["g"]).astype(BF16)

    def run(gen):
        for _ in gen:
            pass

    cur = [{} for _ in range(nseq)]
    run(prep(0, cur))
    for c in range(nch):
        nxt = [{} for _ in range(nseq)]
        side = prep(c + 1, nxt) if c + 1 < nch else iter(())
        for tick, _ in enumerate(main(c, cur)):
            if tick % 3 == 1:
                next(side, None)
        run(side)
        cur = nxt

    n_rows = nch * R
    for s in range(nseq):
        if carried:
            last = lambda z_ref: z_ref[s, n_rows - 1:n_rows, :]
            for z_ref in z_refs:
                carry[id(z_ref)][s] = last(z_ref)
        else:
            last = lambda z_ref: jnp.concatenate(
                [z_ref[s, b * lb + lb - 1:b * lb + lb, :] for b in range(nch * nblk)], axis=0)
        sho_ref[s, :, 0:RW] = last(zr_ref)
        sho_ref[s, :, RW:2 * RW] = last(zk_ref)
        sho_ref[s, :, 2 * RW:3 * RW] = last(zv_ref)
        sho_ref[s, :, 3 * RW:] = last(zl_ref)


def _wkv(z, vecs, mats, n_seq, seq_len, lb, nch, nseq, prev_rows=None, state_in=None):
    R = nch * WKV_ROWS
    carried = prev_rows is None
    rows = z.shape[0]
    n_streams = n_seq if carried else nseq
    stream_rows = rows // n_streams
    assert n_streams % nseq == 0 and stream_rows % R == 0 and rows == n_streams * stream_rows
    z3 = z.reshape(n_streams, stream_rows, Z_W)
    if carried:
        grid = (n_streams // nseq, stream_rows // R)
        rmap = lambda c: (lambda b, t: (b, t, c))
        cmap = lambda b, t: (0, 0)
        smap = lambda b, t: (b, 0, 0)
        sem = ("arbitrary", "arbitrary")
    else:
        grid = (stream_rows // R,)
        rmap = lambda c: (lambda i: (0, i, c))
        cmap = lambda i: (0, 0)
        sem = ("arbitrary",)
        blk_per_step = R // lb
        seq_per_stream = n_seq // nseq

    in_specs = [pl.BlockSpec((nseq, R, RW), rmap(Z_R // RW)),
                pl.BlockSpec((nseq, R, RW), rmap(Z_K // RW)),
                pl.BlockSpec((nseq, R, RW), rmap(Z_VR // RW)),
                pl.BlockSpec((nseq, R, LORA), rmap(Z_L // LORA))]
    args = [z3, z3, z3, z3]
    if not carried:
        p3 = prev_rows.reshape(nseq, stream_rows, C_SHIFT)
        st4 = state_in.reshape(nseq, seq_per_stream, RW, HEAD)
        in_specs += [pl.BlockSpec((nseq, R, RW), rmap(0)), pl.BlockSpec((nseq, R, RW), rmap(1)),
                     pl.BlockSpec((nseq, R, RW), rmap(2)), pl.BlockSpec((nseq, R, LORA), rmap(3 * RW // LORA)),
                     pl.BlockSpec((nseq, blk_per_step, RW, HEAD), lambda i: (0, i, 0, 0))]
        args += [p3] * 4 + [st4]
    for a in vecs + mats:
        in_specs.append(pl.BlockSpec(a.shape, cmap))
        args.append(a)

    if carried:
        scratch = [pltpu.VMEM((nseq, N_GRP, GRP, GRP), F32), pltpu.VMEM((nseq, 1, RW), F32),
                   pltpu.VMEM((nseq, 1, RW), F32), pltpu.VMEM((nseq, 1, RW), F32),
                   pltpu.VMEM((nseq, 1, LORA), F32)]
        state_spec = pl.BlockSpec((nseq, RW, HEAD), smap)
        state_shape = (n_seq, RW, HEAD)
        shift_spec = pl.BlockSpec((nseq, 1, C_SHIFT), smap)
        shift_shape = (n_seq, 1, C_SHIFT)
    else:
        scratch = []
        state_spec = pl.BlockSpec((nseq, blk_per_step, RW, HEAD), lambda i: (0, i, 0, 0))
        state_shape = (nseq, seq_per_stream, RW, HEAD)
        shift_spec = pl.BlockSpec((nseq, blk_per_step, C_SHIFT), lambda i: (0, i, 0))
        shift_shape = (nseq, seq_per_stream, C_SHIFT)
    oa, st, sh = pl.pallas_call(
        functools.partial(_wkv_kernel, lb=lb, carried=carried, nch=nch, nseq=nseq),
        grid=grid,
        in_specs=in_specs,
        out_specs=[pl.BlockSpec((nseq, R, RW), rmap(0)), state_spec, shift_spec],
        out_shape=[jax.ShapeDtypeStruct((n_streams, stream_rows, RW), BF16),
                   jax.ShapeDtypeStruct(state_shape, F32),
                   jax.ShapeDtypeStruct(shift_shape, F32)],
        scratch_shapes=scratch,
        compiler_params=_params(sem),
        name="wkv_carried" if carried else "wkv_blocks",
    )(*args)
    return oa.reshape(rows, RW), st.reshape(n_seq, RW, HEAD), sh.reshape(n_seq, C_SHIFT)


def _gmlp_kernel(zu_ref, zv_ref, lng_ref, lnb_ref, ws_ref, bias_ref, ub_ref, vn_ref=None, *, seq):
    reps = CHUNK // seq
    tr = lax.broadcasted_iota(jnp.int32, (CHUNK, CHUNK), 0)
    tc = lax.broadcasted_iota(jnp.int32, (CHUNK, CHUNK), 1)
    causal = (tc <= tr) & ((tr // seq) == (tc // seq))
    ws_g = []
    for gi in range(MLP_GROUPS):
        w_rows = ws_ref[gi]
        w_full = jnp.concatenate([w_rows if s == 0 else pltpu.roll(w_rows, s * seq, axis=1)
                                  for s in range(reps)], axis=0)
        ws_g.append(jnp.where(causal, w_full, 0.0).astype(BF16))
    bias = jnp.concatenate([bias_ref[...]] * reps, axis=0)
    for c in range(zu_ref.shape[0] // CHUNK):
        rows = slice(c * CHUNK, (c + 1) * CHUNK)
        u = jax.nn.gelu(zu_ref[rows, :])
        vg = jax.nn.gelu(zv_ref[rows, :])
        mu = jnp.mean(vg, -1, keepdims=True)
        var = jnp.mean(jnp.square(vg - mu), -1, keepdims=True)
        vn = (vg - mu) * lax.rsqrt(var + LN_EPS) * lng_ref[...] + lnb_ref[...]
        if vn_ref is not None:
            vn_ref[rows, :] = vn
        vb = vn.astype(BF16)
        cols = [_dot(ws_g[gi], vb[:, gi * MLP_GD:(gi + 1) * MLP_GD]) for gi in range(MLP_GROUPS)]
        mixed = jnp.concatenate(cols, axis=1) + bias
        ub_ref[rows, :] = (u * mixed).astype(BF16)


def _gmlp(z, ln_g, ln_b, ws, bias, seq, want_vn):
    rows, tr = z.shape[0], 2 * CHUNK
    n_out = 2 if want_vn else 1
    return pl.pallas_call(
        functools.partial(_gmlp_kernel, seq=seq),
        grid=(rows // tr,),
        in_specs=[pl.BlockSpec((tr, MLP_W), lambda i: (i, Z_U // MLP_W)),
                  pl.BlockSpec((tr, MLP_W), lambda i: (i, Z_V // MLP_W)),
                  pl.BlockSpec((1, MLP_W), lambda i: (0, 0)),
                  pl.BlockSpec((1, MLP_W), lambda i: (0, 0)),
                  pl.BlockSpec((MLP_GROUPS, seq, CHUNK), lambda i: (0, 0, 0)),
                  pl.BlockSpec((seq, MLP_W), lambda i: (0, 0))],
        out_specs=[pl.BlockSpec((tr, MLP_W), lambda i: (i, 0)),
                   pl.BlockSpec((tr, MLP_W), lambda i: (i, 0))][:n_out],
        out_shape=[jax.ShapeDtypeStruct((rows, MLP_W), BF16),
                   jax.ShapeDtypeStruct((rows, MLP_W), F32)][:n_out],
        compiler_params=_params(("arbitrary",)),
        name="gmlp",
    )(z, z, ln_g, ln_b, ws, bias)


def _merge_kernel(oa_ref, ub_ref, zga_ref, zgb_ref, x_ref, gm_ref, sc_ref, sh_ref, g_ref,
                  wa_ref, wb_ref, wo_ref, x1_ref, h2_ref):
    ya = _dot(oa_ref[...], wa_ref[...])
    yb = _dot(ub_ref[...], wb_ref[...])
    merged = jax.nn.sigmoid(zga_ref[...]) * ya + jax.nn.sigmoid(zgb_ref[...]) * yb
    mix = _dot(merged.astype(BF16), wo_ref[...])
    x1 = x_ref[...] + gm_ref[...] * mix
    x1_ref[...] = x1
    h2_ref[...] = (_rms(x1, g_ref[...]) * (1.0 + sc_ref[...]) + sh_ref[...]).astype(BF16)


def _merge(oa, ub, z, x, mod, g_ffn, wa, wb, wo, per_row, rows_per_seq):
    m, tm = x.shape[0], (128 if per_row else 256)
    const = lambda shape: pl.BlockSpec(shape, lambda i: (0, 0), pipeline_mode=pl.Buffered(1))
    mspec = lambda k: _mod_spec(k, per_row, tm, rows_per_seq)
    rows = lambda w, c: pl.BlockSpec((tm, w), lambda i: (i, c))
    return pl.pallas_call(
        _merge_kernel,
        grid=(m // tm,),
        in_specs=[rows(RW, 0), rows(MLP_W, 0),
                  rows(D_MODEL, Z_GA // D_MODEL), rows(D_MODEL, Z_GB // D_MODEL), rows(D_MODEL, 0),
                  mspec(MOD_GATE_M), mspec(MOD_SCALE_F), mspec(MOD_SHIFT_F),
                  pl.BlockSpec((1, D_MODEL), lambda i: (0, 0)),
                  const((RW, D_MODEL)), const((MLP_W, D_MODEL)), const((D_MODEL, D_MODEL))],
        out_specs=[rows(D_MODEL, 0), rows(D_MODEL, 0)],
        out_shape=[jax.ShapeDtypeStruct((m, D_MODEL), F32), jax.ShapeDtypeStruct((m, D_MODEL), BF16)],
        compiler_params=_params(("arbitrary",)),
        name="merge",
    )(oa, ub, z, z, x, mod, mod, mod, g_ffn, wa, wb, wo)


def _ffn_kernel(*refs, fused):
    if fused:
        h_ref, x_ref, gf_ref, gfin_ref, wg_ref, wu_ref, wo_ref, y_ref = refs
    else:
        h_ref, wg_ref, wu_ref, wo_ref, y_ref = refs
    j = pl.program_id(1)

    @pl.when(j == 0)
    def _():
        y_ref[...] = jnp.zeros_like(y_ref)

    hb = h_ref[...]
    gt = _dot(hb, wg_ref[...].astype(BF16))
    up = _dot(hb, wu_ref[...].astype(BF16))
    act = gt * jax.nn.sigmoid(gt) * up
    y_ref[...] += _dot(act.astype(BF16), wo_ref[...].astype(BF16))

    if fused:
        @pl.when(j == pl.num_programs(1) - 1)
        def _():
            x2 = x_ref[...] + gf_ref[...] * y_ref[...]
            y_ref[...] = _rms(x2, gfin_ref[...])


def _final_kernel(f_ref, x_ref, gf_ref, gfin_ref, y_ref):
    y_ref[...] = _rms(x_ref[...] + gf_ref[...] * f_ref[...], gfin_ref[...])


def _ffn(h2, x1, mod, g_final, w_in, w_out, per_row, rows_per_seq):
    m, tm, tf = x1.shape[0], 1024, 256
    nf = D_FF // tf
    fused = not per_row
    rows = pl.BlockSpec((tm, D_MODEL), lambda i, j: (i, 0))
    vec = pl.BlockSpec((1, D_MODEL), lambda i, j: (0, 0))
    wspecs = [pl.BlockSpec((D_MODEL, tf), lambda i, j: (0, j)),
              pl.BlockSpec((D_MODEL, tf), lambda i, j: (0, nf + j)),
              pl.BlockSpec((tf, D_MODEL), lambda i, j: (j, 0))]
    if fused:
        in_specs = [rows, pl.BlockSpec((tm, D_MODEL), lambda i, j: (i, 0), pipeline_mode=pl.Buffered(1)),
                    _mod_spec(MOD_GATE_F, per_row, tm, rows_per_seq), vec] + wspecs
        args = (h2, x1, mod, g_final, w_in, w_in, w_out)
    else:
        in_specs = [rows] + wspecs
        args = (h2, w_in, w_in, w_out)
    y = pl.pallas_call(
        functools.partial(_ffn_kernel, fused=fused),
        grid=(m // tm, nf),
        in_specs=in_specs,
        out_specs=rows,
        out_shape=jax.ShapeDtypeStruct((m, D_MODEL), F32),
        compiler_params=_params(("arbitrary", "arbitrary")),
        name="ffn",
    )(*args)
    if fused:
        return y
    te = 256
    erows = pl.BlockSpec((te, D_MODEL), lambda i: (i, 0))
    return pl.pallas_call(
        _final_kernel,
        grid=(m // te,),
        in_specs=[erows, erows, _mod_spec(MOD_GATE_F, per_row, te, rows_per_seq),
                  pl.BlockSpec((1, D_MODEL), lambda i: (0, 0))],
        out_specs=erows,
        out_shape=jax.ShapeDtypeStruct((m, D_MODEL), F32),
        compiler_params=_params(("arbitrary",)),
        name="final",
    )(y, x1, mod, g_final)


def kernel(x_prompt, x_sample, state_wkv, state_shift, c_prompt, c_sample, w_ada, b_ada, norm_mix_g, w_in, mu_shift, w0, w_decay_up, a0, w_aaa_up, w_gate_up, k_k, k_a, r_k, gn_g, gn_b, ln_v_g, ln_v_b, w_spatial, b_spatial, w_branch_a, w_branch_b, w_out, norm_ffn_g, w_ffn_in, w_ffn_out, norm_final_g):
    assert w_ada.shape[0] == 1, "single layer"
    bp, tp, _ = x_prompt.shape
    bs, ts, _ = x_sample.shape
    assert tp % CHUNK == 0 and WKV_ROWS % ts == 0 and CHUNK % ts == 0

    c_all = jnp.concatenate([c_sample, c_prompt], axis=0)
    c_all = jnp.pad(c_all, ((0, -c_all.shape[0] % 8), (0, 0)))
    mod_s = _ada(c_all, w_ada[0], b_ada[0], bs, ts)
    mod_p = mod_s[bs * ts:bs * ts + bp].reshape(bp, 1, 6 * D_MODEL)

    wa_b, wb_b, wo_b = w_branch_a[0].astype(BF16), w_branch_b[0].astype(BF16), w_out[0].astype(BF16)
    wfi, wfo = w_ffn_in[0], w_ffn_out[0]

    row = lambda a: a.reshape(1, -1)
    mu = mu_shift[0]
    vecs = [row(mu[:RW]), row(mu[RW:2 * RW]), row(mu[2 * RW:3 * RW]), row(mu[3 * RW:]),
            row(w0[0]), row(a0[0]), row(k_k[0]), row(k_a[0]), row(r_k[0]), row(gn_g[0]), row(gn_b[0])]
    zpad = lambda w, lo: jnp.pad(w, ((lo, LORA - lo - w.shape[0]), (0, 0))).astype(BF16)
    mats = [zpad(w_decay_up[0], 0), zpad(w_aaa_up[0], DECAY_LORA),
            zpad(w_gate_up[0], DECAY_LORA + AAA_LORA)]

    bias = jnp.repeat(b_spatial[0].T, MLP_GD, axis=1)
    ws = w_spatial[0]

    xp = x_prompt.reshape(bp * tp, D_MODEL)
    xs = x_sample.reshape(bs * ts, D_MODEL)
    g_mix, g_ffn, g_fin = row(norm_mix_g[0]), row(norm_ffn_g[0]), row(norm_final_g)
    lng, lnb = row(ln_v_g[0]), row(ln_v_b[0])

    zp = _inproj(_hnorm(xp, mod_p, g_mix, False, tp), w_in[0])
    oa_p, st_p, shift_p = _wkv(zp, vecs, mats, bp, tp, WKV_ROWS, WKV_NCH, WKV_NSEQ)
    (ub_p,) = _gmlp(zp, lng, lnb, ws, bias, CHUNK, False)
    x1_p, h2_p = _merge(oa_p, ub_p, zp, xp, mod_p, g_ffn, wa_b, wb_b, wo_b, False, tp)
    y_p = _ffn(h2_p, x1_p, mod_p, g_fin, wfi, wfo, False, tp)

    zs = _inproj(_hnorm(xs, mod_s, g_mix, True, ts), w_in[0])
    prev_rows = jnp.repeat(state_shift[0], ts, axis=0)
    oa_s, st_s, shift_s = _wkv(zs, vecs, mats, bs, ts, ts, 1, WKV_NSEQ, prev_rows=prev_rows,
                               state_in=state_wkv[0].reshape(bs, RW, HEAD))
    ub_s, vn_s = _gmlp(zs, lng, lnb, ws, bias, ts, True)
    x1_s, h2_s = _merge(oa_s, ub_s, zs, xs, mod_s, g_ffn, wa_b, wb_b, wo_b, True, ts)
    y_s = _ffn(h2_s, x1_s, mod_s, g_fin, wfi, wfo, True, ts)

    return (y_p.reshape(bp, tp, D_MODEL),
            y_s.reshape(bs, ts, D_MODEL),
            st_p.reshape(1, bp, N_HEADS, HEAD, HEAD),
            shift_p.reshape(1, bp, C_SHIFT),
            st_s.reshape(1, bs, N_HEADS, HEAD, HEAD),
            shift_s.reshape(1, bs, C_SHIFT),
            vn_s.reshape(bs, ts, MLP_W)[None])
```

```python
import functools
import math

import jax
import jax.numpy as jnp
from jax import lax
from jax.experimental import pallas as pl
from jax.experimental.pallas import tpu as pltpu

F32 = jnp.float32
BF16 = jnp.bfloat16

D_MODEL = 2048
HEAD = 64
RW = D_MODEL // 2
N_HEADS = RW // HEAD
DECAY_LORA = D_MODEL // 32
AAA_LORA = D_MODEL // 32
GATE_LORA = D_MODEL // 16
LORA = DECAY_LORA + AAA_LORA + GATE_LORA
CHUNK = 128
MLP_W = D_MODEL // 2
MLP_GROUPS = 8
MLP_GD = MLP_W // MLP_GROUPS
D_FF = ((-(-8 * D_MODEL // 3)) + 255) // 256 * 256
C_SHIFT = 3 * RW + LORA
C_IN = C_SHIFT + 2 * MLP_W + 2 * D_MODEL
NORM_EPS = 1e-6
GN_EPS = 64e-5
LN_EPS = 1e-5

Z_GA, Z_GB, Z_U, Z_V, Z_R, Z_K, Z_VR, Z_L = 0, 2048, 4096, 5120, 6144, 7168, 8192, 9216
Z_TN = 512
Z_W = -(-C_IN // Z_TN) * Z_TN

GRP = 256
HPG = GRP // HEAD
N_GRP = RW // GRP
WKV_ROWS = 64
WKV_NCH = 2
WKV_NSEQ = 2
VMEM_LIMIT = 56 * 1024 * 1024


def _dot(a, b):
    return jnp.dot(a, b, preferred_element_type=F32)


def _dot_nt(a, b):
    return lax.dot_general(a, b, (((1,), (1,)), ((), ())), preferred_element_type=F32)


def _dot_tn(a, b):
    return lax.dot_general(a, b, (((0,), (0,)), ((), ())), preferred_element_type=F32)


def _rms(x, g):
    return x * lax.rsqrt(jnp.mean(x * x, -1, keepdims=True) + NORM_EPS) * g


def _split2(x):
    hi = x.astype(BF16)
    lo = (x - hi.astype(F32)).astype(BF16)
    return hi, lo


def _split3(x):
    hi = x.astype(BF16)
    r1 = x - hi.astype(F32)
    mid = r1.astype(BF16)
    lo = (r1 - mid.astype(F32)).astype(BF16)
    return hi, mid, lo


def _params(sem):
    return pltpu.CompilerParams(dimension_semantics=sem, vmem_limit_bytes=VMEM_LIMIT)


def _ada_kernel(c_ref, w_ref, b_ref, o_ref, s_ref):
    @pl.when(pl.program_id(0) == 0)
    def _():
        c = c_ref[...]
        s_ref[...] = (c * jax.nn.sigmoid(c)).astype(BF16)

    o_ref[...] = _dot(s_ref[...], w_ref[...].astype(BF16)) + b_ref[...]


def _ada(c, w_ada, b_ada):
    m, n, tn = c.shape[0], w_ada.shape[1], 1024
    return pl.pallas_call(
        _ada_kernel,
        grid=(n // tn,),
        in_specs=[pl.BlockSpec((m, D_MODEL), lambda j: (0, 0)),
                  pl.BlockSpec((D_MODEL, tn), lambda j: (0, j)),
                  pl.BlockSpec((1, tn), lambda j: (0, j))],
        out_specs=pl.BlockSpec((m, tn), lambda j: (0, j)),
        out_shape=jax.ShapeDtypeStruct((m, n), F32),
        scratch_shapes=[pltpu.VMEM((m, D_MODEL), BF16)],
        compiler_params=_params(("arbitrary",)),
        name="ada",
    )(c, w_ada, b_ada.reshape(1, n))


MOD_SHIFT_M, MOD_SCALE_M, MOD_GATE_M, MOD_SHIFT_F, MOD_SCALE_F, MOD_GATE_F = range(6)


def _mod_spec(k, tm, rows_per_seq):
    if rows_per_seq >= tm:
        return pl.BlockSpec((None, 1, D_MODEL), lambda i, *_: (i * tm // rows_per_seq, 0, k))
    return pl.BlockSpec((tm // rows_per_seq, 1, D_MODEL), lambda i, *_: (i, 0, k))


def _mod_rows(ref, tm):
    m = ref[...]
    if m.ndim == 2:
        return m
    n_seq = m.shape[0]
    return jnp.broadcast_to(m, (n_seq, tm // n_seq, D_MODEL)).reshape(tm, D_MODEL)


ROW_CHUNK = 16


def _for_row_chunks(n_rows, body):
    def step(i, carry):
        body(pl.ds(pl.multiple_of(i * ROW_CHUNK, ROW_CHUNK), ROW_CHUNK), i)
        return carry
    lax.fori_loop(0, n_rows // ROW_CHUNK, step, 0, unroll=8)


def _mod_chunk(ref, i, tm):
    if len(ref.shape) == 2:
        return ref[...]
    per_seq = tm // ref.shape[0]
    n = ROW_CHUNK // per_seq
    m = ref[pl.ds(i * n, n)]
    return jnp.broadcast_to(m, (n, per_seq, D_MODEL)).reshape(ROW_CHUNK, D_MODEL)


ZB = 256
_N_RW_B, _N_MLP_B, _N_GATE_B = C_SHIFT // ZB, 2 * MLP_W // ZB, 2 * D_MODEL // ZB


def _w_in_block(zb):
    return jnp.where(zb < _N_GATE_B, zb + _N_RW_B + _N_MLP_B,
                     jnp.where(zb < _N_GATE_B + _N_MLP_B, zb - _N_GATE_B + _N_RW_B,
                               jnp.minimum(zb - _N_GATE_B - _N_MLP_B, _N_RW_B - 1)))


def _hnorm_kernel(x_ref, sc_ref, sh_ref, g_ref, h_ref):
    tm = x_ref.shape[0]

    def body(rows, i):
        h = _rms(x_ref[rows, :], g_ref[...]) * (1.0 + _mod_chunk(sc_ref, i, tm)) + _mod_chunk(sh_ref, i, tm)
        h_ref[rows, :] = h.astype(BF16)

    _for_row_chunks(tm, body)


def _hnorm(x, mod, g, rows_per_seq):
    m, tm = x.shape[0], (1024 if rows_per_seq >= 1024 else 512)
    mspec = lambda k: _mod_spec(k, tm, rows_per_seq)
    return pl.pallas_call(
        _hnorm_kernel,
        grid=(m // tm,),
        in_specs=[pl.BlockSpec((tm, D_MODEL), lambda i: (i, 0)),
                  mspec(MOD_SCALE_M), mspec(MOD_SHIFT_M),
                  pl.BlockSpec((1, D_MODEL), lambda i: (0, 0))],
        out_specs=pl.BlockSpec((tm, D_MODEL), lambda i: (i, 0)),
        out_shape=jax.ShapeDtypeStruct((m, D_MODEL), BF16),
        compiler_params=_params(("arbitrary",)),
        name="hnorm",
    )(x, mod, mod, g)


def _inproj_kernel(h_ref, wlo_ref, whi_ref, o_ref):
    h = h_ref[...]
    o_ref[:, :ZB] = _dot(h, wlo_ref[...].astype(BF16))
    o_ref[:, ZB:] = _dot(h, whi_ref[...].astype(BF16))


def _inproj(h, w):
    m = h.shape[0]
    tm = min(m, 2048)
    assert Z_TN == 2 * ZB and m % tm == 0
    return pl.pallas_call(
        _inproj_kernel,
        grid=(m // tm, Z_W // Z_TN),
        in_specs=[pl.BlockSpec((tm, D_MODEL), lambda i, j: (i, 0)),
                  pl.BlockSpec((D_MODEL, ZB), lambda i, j: (0, _w_in_block(2 * j))),
                  pl.BlockSpec((D_MODEL, ZB), lambda i, j: (0, _w_in_block(2 * j + 1)))],
        out_specs=pl.BlockSpec((tm, Z_TN), lambda i, j: (i, j)),
        out_shape=jax.ShapeDtypeStruct((m, Z_W), F32),
        compiler_params=_params(("arbitrary", "arbitrary")),
        name="inproj",
    )(h, w, w)


def _bd(w, mask):
    return jnp.where(mask, jnp.concatenate([w] * HPG, axis=0), 0.0).astype(BF16)


def _bd_heads(nat, mask):
    return jnp.where(mask, jnp.concatenate([nat] * HPG, axis=1), 0.0).astype(BF16)


def _diag_blocks(m):
    return jnp.concatenate([m[h * HEAD:(h + 1) * HEAD, h * HEAD:(h + 1) * HEAD] for h in range(HPG)],
                           axis=0)


def _wkv_kernel(*refs, lb, carried, nch, nseq):
    R = WKV_ROWS
    if carried:
        (zr_ref, zk_ref, zv_ref, zl_ref,
         mur_ref, muk_ref, muv_ref, mul_ref,
         w0_ref, a0_ref, kkp_ref, kap_ref, rkp_ref, gng_ref, gnb_ref,
         wd_ref, wa_ref, wg_ref,
         oa_ref, so_ref, sho_ref,
         s_ref, cr_ref, ck_ref, cv_ref, cl_ref) = refs
        carry = {id(zr_ref): cr_ref, id(zk_ref): ck_ref, id(zv_ref): cv_ref, id(zl_ref): cl_ref}
    else:
        (zr_ref, zk_ref, zv_ref, zl_ref,
         pr_ref, pk_ref, pv_ref, pl_ref, si_ref,
         mur_ref, muk_ref, muv_ref, mul_ref,
         w0_ref, a0_ref, kkp_ref, kap_ref, rkp_ref, gng_ref, gnb_ref,
         wd_ref, wa_ref, wg_ref,
         oa_ref, so_ref, sho_ref) = refs
        prev_of = {id(zr_ref): pr_ref, id(zk_ref): pk_ref, id(zv_ref): pv_ref, id(zl_ref): pl_ref}
    nblk = R // lb
    z_refs = (zr_ref, zk_ref, zv_ref, zl_ref)
    mu_refs = (mur_ref, muk_ref, muv_ref, mul_ref)

    if carried:
        @pl.when(pl.program_id(1) == 0)
        def _():
            s_ref[...] = jnp.zeros_like(s_ref)
            for ref in (cr_ref, ck_ref, cv_ref, cl_ref):
                ref[...] = jnp.zeros_like(ref)

    row1 = lax.broadcasted_iota(jnp.int32, (R, 1), 0)
    first = (row1 % lb) == 0
    ri = lax.broadcasted_iota(jnp.int32, (GRP, GRP), 0)
    ci = lax.broadcasted_iota(jnp.int32, (GRP, GRP), 1)
    bdm = (ri // HEAD) == (ci // HEAD)
    ones_bd = jnp.where(bdm, 1.0, 0.0).astype(BF16)
    tr = lax.broadcasted_iota(jnp.int32, (R, R), 0)
    tc = lax.broadcasted_iota(jnp.int32, (R, R), 1)
    same = (tr // lb) == (tc // lb)
    incl01 = jnp.where(same & (tc <= tr), 1.0, 0.0).astype(BF16)
    same01 = jnp.where(same, 1.0, 0.0).astype(BF16)
    wr = lax.broadcasted_iota(jnp.int32, (R, GRP), 0)
    wc = lax.broadcasted_iota(jnp.int32, (R, GRP), 1) % HEAD
    wsame = (wr // lb) == (wc // lb)
    strict_w = wsame & (wc < wr)
    incl_w = wsame & (wc <= wr)
    eye_w = jnp.where(wc == wr, 1.0, 0.0)
    sls = [slice(gi * GRP, (gi + 1) * GRP) for gi in range(N_GRP)]
    blks = [slice(b * lb, (b + 1) * lb) for b in range(nblk)]
    units = [(s, gi) for s in range(nseq) for gi in range(N_GRP)]
    uids = range(len(units))

    def seg_sums(xs):
        n = xs[0].shape[0]
        parts = []
        for x in xs:
            parts.extend(_split2(x))
        stacked = jnp.concatenate(parts, axis=0)
        cols = [_dot(stacked[:, sl], ones_bd) for sl in sls]
        full = jnp.concatenate(cols, axis=1)
        return [full[2 * i * n:(2 * i + 1) * n] + full[(2 * i + 1) * n:(2 * i + 2) * n]
                for i in range(len(xs))]

    def cum(m01, x):
        hi, mid, lo = _split3(x)
        return _dot(m01, hi) + _dot(m01, mid) + _dot(m01, lo)

    def prep(c, outs):
        rows = slice(c * R, (c + 1) * R)

        def shift(z_ref, mu_ref):
            parts = []
            for s in range(nseq):
                z = z_ref[s, rows, :]
                if not carried:
                    prev0 = prev_of[id(z_ref)][s, rows, :]
                elif c == 0:
                    prev0 = carry[id(z_ref)][s]
                else:
                    prev0 = z_ref[s, c * R - 1:c * R, :]
                prev = jnp.where(first, prev0, pltpu.roll(z, 1, axis=0))
                parts.append(z + (prev - z) * mu_ref[...])
            return jnp.concatenate(parts, axis=0)

        per_stream = lambda x: [x[s * R:(s + 1) * R] for s in range(nseq)]
        r, k, v, l = [shift(z_ref, mu_ref) for z_ref, mu_ref in zip(z_refs, mu_refs)]
        w_raw = w0_ref[...] + _dot(jnp.tanh(l).astype(BF16), wd_ref[...])
        a_raw = a0_ref[...] + _dot(l.astype(BF16), wa_ref[...])
        g = _dot(jax.nn.sigmoid(l).astype(BF16), wg_ref[...])
        yield
        logw = jax.nn.sigmoid(w_raw) * (-math.exp(-0.5))
        a = jax.nn.sigmoid(a_raw)
        kk = k * kkp_ref[...]
        k2 = k * (1.0 + (a - 1.0) * kap_ref[...])
        ss, rk = seg_sums([kk * kk, r * k2 * rkp_ref[...]])
        cl_s = [cum(incl01, lw) for lw in per_stream(logw)]
        if nblk == 1:
            ce_s = [jnp.broadcast_to(x[R - 1:R, :], (R, RW)) for x in cl_s]
        else:
            ce_s = [cum(same01, lw) for lw in per_stream(logw)]
        yield
        cl = jnp.concatenate(cl_s, axis=0)
        cl_end = jnp.concatenate(ce_s, axis=0)
        kk = kk / jnp.maximum(jnp.sqrt(ss), 1e-12)
        av = -kk
        bv = kk * a
        e_neg = jnp.exp(-cl)
        e_end = jnp.exp(cl_end - cl)
        full = dict(v=v, g=g, bonus=rk * v,
                    at=av * jnp.exp(cl - logw), rt=r * jnp.exp(cl),
                    bt=bv * e_neg, kt=k2 * e_neg, bh=bv * e_end, kh=k2 * e_end)
        for name, x in full.items():
            for s, x_s in enumerate(per_stream(x)):
                outs[s][name] = x_s
        for s in range(nseq):
            outs[s]["p_end"] = jnp.exp(cl_s[s][R - 1:R, :] if nblk == 1 else ce_s[s])

    def main(c, ps):
        rows_c = slice(c * R, (c + 1) * R)
        col = lambda name, ui: ps[units[ui][0]][name][:, sls[units[ui][1]]]
        v_g = [col("v", ui) for ui in uids]
        lhs = [jnp.concatenate([col("at", ui), col("rt", ui)], axis=0).astype(BF16) for ui in uids]
        o1 = [_dot_nt(lhs[ui], jnp.concatenate([_bd(col("bt", ui), bdm), _bd(col("kt", ui), bdm)], axis=0))
              for ui in uids]
        yield
        w_ab = [jnp.where(strict_w, o[:R, :GRP], 0.0) for o in o1]
        w_ak = [jnp.where(strict_w, o[:R, GRP:], 0.0) for o in o1]
        w_rb = [jnp.where(incl_w, o[R:, :GRP], 0.0) for o in o1]
        w_rk = [jnp.where(incl_w, o[R:, GRP:], 0.0) for o in o1]

        def state_part(ui):
            s, gi = units[ui]
            if carried:
                o2 = _dot_nt(lhs[ui], s_ref[s, gi].astype(BF16))
                return o2[:R], o2[R:]
            at_u, rt_u = col("at", ui), col("rt", ui)
            a_rows, r_rows = [], []
            for b, rows in enumerate(blks):
                lhs_b = jnp.concatenate([at_u[rows], rt_u[rows]], axis=0).astype(BF16)
                o2 = _dot_nt(lhs_b, _bd_heads(si_ref[s, c * nblk + b, sls[gi], :], bdm))
                a_rows.append(o2[:lb])
                r_rows.append(o2[lb:])
            return jnp.concatenate(a_rows, axis=0), jnp.concatenate(r_rows, axis=0)

        x_w = list(w_ab)
        t_w = [eye_w + x for x in x_w]
        n_lvl = max(1, int(math.log2(lb)))
        a_s = r_s = o3 = None
        for j in range(n_lvl):
            for ui in uids:
                y_bd = _bd(x_w[ui], bdm)
                if j == 0:
                    x_w[ui] = _dot(x_w[ui].astype(BF16), y_bd)
                elif j == n_lvl - 1:
                    t_w[ui] = t_w[ui] + _dot(t_w[ui].astype(BF16), y_bd)
                else:
                    res = _dot(jnp.concatenate([t_w[ui], x_w[ui]], axis=0).astype(BF16), y_bd)
                    t_w[ui] = t_w[ui] + res[:R]
                    x_w[ui] = res[R:]
            yield
            if j == 0:
                parts = [state_part(ui) for ui in uids]
                a_s, r_s = [q[0] for q in parts], [q[1] for q in parts]
                yield
            elif j == 1:
                o3 = [_dot(jnp.concatenate([w_ak[ui], w_rk[ui]], axis=0).astype(BF16), _bd(v_g[ui], bdm))
                      for ui in uids]
                yield

        u = [_dot(t_w[ui].astype(BF16), _bd(a_s[ui] + o3[ui][:R], bdm)) for ui in uids]
        yield
        o_parts = [r_s[ui] + o3[ui][R:] + _dot(w_rb[ui].astype(BF16), _bd(u[ui], bdm)) for ui in uids]
        yield

        for ui in uids:
            s, gi = units[ui]
            sl = sls[gi]
            bh_u, kh_u, p_end = col("bh", ui), col("kh", ui), ps[s]["p_end"][:, sl]
            if carried:
                upd = _dot_tn(jnp.concatenate([u[ui], v_g[ui]], axis=0).astype(BF16),
                              jnp.concatenate([bh_u, kh_u], axis=0).astype(BF16))
                s_new = jnp.where(bdm, s_ref[s, gi] * p_end + upd, 0.0)
                s_ref[s, gi] = s_new
                if c == nch - 1:
                    so_ref[s, sl, :] = _diag_blocks(s_new)
            else:
                for b, rows in enumerate(blks):
                    upd = _dot_tn(jnp.concatenate([u[ui][rows], v_g[ui][rows]], axis=0).astype(BF16),
                                  jnp.concatenate([bh_u[rows], kh_u[rows]], axis=0).astype(BF16))
                    p_b = p_end[b * lb:b * lb + 1]
                    p_nat = jnp.concatenate(
                        [jnp.broadcast_to(p_b[:, h * HEAD:(h + 1) * HEAD], (HEAD, HEAD)) for h in range(HPG)],
                        axis=0)
                    so_ref[s, c * nblk + b, sl, :] = (si_ref[s, c * nblk + b, sl, :] * p_nat
                                                      + _diag_blocks(upd))
        yield

        o = [jnp.concatenate(o_parts[s * N_GRP:(s + 1) * N_GRP], axis=1) for s in range(nseq)]
        mu = seg_sums(o)
        yield
        dlt = [o[s] - mu[s] * (1.0 / HEAD) for s in range(nseq)]
        var = seg_sums([d * d for d in dlt])
        for s in range(nseq):
            on = dlt[s] * lax.rsqrt(var[s] * (1.0 / HEAD) + GN_EPS) * gng_ref[...] + gnb_ref[...]
            oa_ref[s, rows_c, :] = ((on + ps[s]["bonus"]) * ps[s]["g"]).astype(BF16)

    def run(gen):
        for _ in gen:
            pass

    cur = [{} for _ in range(nseq)]
    run(prep(0, cur))
    for c in range(nch):
        nxt = [{} for _ in range(nseq)]
        side = prep(c + 1, nxt) if c + 1 < nch else iter(())
        for tick, _ in enumerate(main(c, cur)):
            if tick % 3 == 1:
                next(side, None)
        run(side)
        cur = nxt

    n_rows = nch * R
    for s in range(nseq):
        if carried:
            last = lambda z_ref: z_ref[s, n_rows - 1:n_rows, :]
            for z_ref in z_refs:
                carry[id(z_ref)][s] = last(z_ref)
        else:
            last = lambda z_ref: jnp.concatenate(
                [z_ref[s, b * lb + lb - 1:b * lb + lb, :] for b in range(nch * nblk)], axis=0)
        sho_ref[s, :, 0:RW] = last(zr_ref)
        sho_ref[s, :, RW:2 * RW] = last(zk_ref)
        sho_ref[s, :, 2 * RW:3 * RW] = last(zv_ref)
        sho_ref[s, :, 3 * RW:] = last(zl_ref)


def _wkv(z, vecs, mats, n_seq, seq_len, lb, nch, nseq, prev_rows=None, state_in=None):
    R = nch * WKV_ROWS
    carried = prev_rows is None
    rows = z.shape[0]
    n_streams = n_seq if carried else nseq
    stream_rows = rows // n_streams
    assert n_streams % nseq == 0 and stream_rows % R == 0 and rows == n_streams * stream_rows
    z3 = z.reshape(n_streams, stream_rows, Z_W)
    if carried:
        grid = (n_streams // nseq, stream_rows // R)
        rmap = lambda c: (lambda b, t: (b, t, c))
        cmap = lambda b, t: (0, 0)
        smap = lambda b, t: (b, 0, 0)
        sem = ("arbitrary", "arbitrary")
    else:
        grid = (stream_rows // R,)
        rmap = lambda c: (lambda i: (0, i, c))
        cmap = lambda i: (0, 0)
        sem = ("arbitrary",)
        blk_per_step = R // lb
        seq_per_stream = n_seq // nseq

    in_specs = [pl.BlockSpec((nseq, R, RW), rmap(Z_R // RW)),
                pl.BlockSpec((nseq, R, RW), rmap(Z_K // RW)),
                pl.BlockSpec((nseq, R, RW), rmap(Z_VR // RW)),
                pl.BlockSpec((nseq, R, LORA), rmap(Z_L // LORA))]
    args = [z3, z3, z3, z3]
    if not carried:
        p3 = prev_rows.reshape(nseq, stream_rows, C_SHIFT)
        st4 = state_in.reshape(nseq, seq_per_stream, RW, HEAD)
        in_specs += [pl.BlockSpec((nseq, R, RW), rmap(0)), pl.BlockSpec((nseq, R, RW), rmap(1)),
                     pl.BlockSpec((nseq, R, RW), rmap(2)), pl.BlockSpec((nseq, R, LORA), rmap(3 * RW // LORA)),
                     pl.BlockSpec((nseq, blk_per_step, RW, HEAD), lambda i: (0, i, 0, 0))]
        args += [p3] * 4 + [st4]
    for a in vecs + mats:
        in_specs.append(pl.BlockSpec(a.shape, cmap))
        args.append(a)

    if carried:
        scratch = [pltpu.VMEM((nseq, N_GRP, GRP, GRP), F32), pltpu.VMEM((nseq, 1, RW), F32),
                   pltpu.VMEM((nseq, 1, RW), F32), pltpu.VMEM((nseq, 1, RW), F32),
                   pltpu.VMEM((nseq, 1, LORA), F32)]
        state_spec = pl.BlockSpec((nseq, RW, HEAD), smap)
        state_shape = (n_seq, RW, HEAD)
        shift_spec = pl.BlockSpec((nseq, 1, C_SHIFT), smap)
        shift_shape = (n_seq, 1, C_SHIFT)
    else:
        scratch = []
        state_spec = pl.BlockSpec((nseq, blk_per_step, RW, HEAD), lambda i: (0, i, 0, 0))
        state_shape = (nseq, seq_per_stream, RW, HEAD)
        shift_spec = pl.BlockSpec((nseq, blk_per_step, C_SHIFT), lambda i: (0, i, 0))
        shift_shape = (nseq, seq_per_stream, C_SHIFT)
    oa, st, sh = pl.pallas_call(
        functools.partial(_wkv_kernel, lb=lb, carried=carried, nch=nch, nseq=nseq),
        grid=grid,
        in_specs=in_specs,
        out_specs=[pl.BlockSpec((nseq, R, RW), rmap(0)), state_spec, shift_spec],
        out_shape=[jax.ShapeDtypeStruct((n_streams, stream_rows, RW), BF16),
                   jax.ShapeDtypeStruct(state_shape, F32),
                   jax.ShapeDtypeStruct(shift_shape, F32)],
        scratch_shapes=scratch,
        compiler_params=_params(sem),
        name="wkv_carried" if carried else "wkv_blocks",
    )(*args)
    return oa.reshape(rows, RW), st.reshape(n_seq, RW, HEAD), sh.reshape(n_seq, C_SHIFT)


def _gmlp_kernel(zu_ref, zv_ref, lng_ref, lnb_ref, ws_ref, bias_ref, ub_ref, vn_ref=None, *, seq):
    reps = CHUNK // seq
    tr = lax.broadcasted_iota(jnp.int32, (CHUNK, CHUNK), 0)
    tc = lax.broadcasted_iota(jnp.int32, (CHUNK, CHUNK), 1)
    causal = (tc <= tr) & ((tr // seq) == (tc // seq))
    ws_g = []
    for gi in range(MLP_GROUPS):
        w_rows = ws_ref[gi]
        w_full = jnp.concatenate([w_rows if s == 0 else pltpu.roll(w_rows, s * seq, axis=1)
                                  for s in range(reps)], axis=0)
        ws_g.append(jnp.where(causal, w_full, 0.0).astype(BF16))
    bias = jnp.concatenate([bias_ref[...]] * reps, axis=0)
    for c in range(zu_ref.shape[0] // CHUNK):
        rows = slice(c * CHUNK, (c + 1) * CHUNK)
        u = jax.nn.gelu(zu_ref[rows, :])
        vg = jax.nn.gelu(zv_ref[rows, :])
        mu = jnp.mean(vg, -1, keepdims=True)
        var = jnp.mean(jnp.square(vg - mu), -1, keepdims=True)
        vn = (vg - mu) * lax.rsqrt(var + LN_EPS) * lng_ref[...] + lnb_ref[...]
        if vn_ref is not None:
            vn_ref[rows, :] = vn
        vb = vn.astype(BF16)
        cols = [_dot(ws_g[gi], vb[:, gi * MLP_GD:(gi + 1) * MLP_GD]) for gi in range(MLP_GROUPS)]
        mixed = jnp.concatenate(cols, axis=1) + bias
        ub_ref[rows, :] = (u * mixed).astype(BF16)


def _gmlp(z, ln_g, ln_b, ws, bias, seq, want_vn):
    rows, tr = z.shape[0], 2 * CHUNK
    n_out = 2 if want_vn else 1
    return pl.pallas_call(
        functools.partial(_gmlp_kernel, seq=seq),
        grid=(rows // tr,),
        in_specs=[pl.BlockSpec((tr, MLP_W), lambda i: (i, Z_U // MLP_W)),
                  pl.BlockSpec((tr, MLP_W), lambda i: (i, Z_V // MLP_W)),
                  pl.BlockSpec((1, MLP_W), lambda i: (0, 0)),
                  pl.BlockSpec((1, MLP_W), lambda i: (0, 0)),
                  pl.BlockSpec((MLP_GROUPS, seq, CHUNK), lambda i: (0, 0, 0)),
                  pl.BlockSpec((seq, MLP_W), lambda i: (0, 0))],
        out_specs=[pl.BlockSpec((tr, MLP_W), lambda i: (i, 0)),
                   pl.BlockSpec((tr, MLP_W), lambda i: (i, 0))][:n_out],
        out_shape=[jax.ShapeDtypeStruct((rows, MLP_W), BF16),
                   jax.ShapeDtypeStruct((rows, MLP_W), F32)][:n_out],
        compiler_params=_params(("arbitrary",)),
        name="gmlp",
    )(z, z, ln_g, ln_b, ws, bias)


def _merge_kernel(oa_ref, ub_ref, zga_ref, zgb_ref, x_ref, gm_ref, sc_ref, sh_ref, g_ref,
                  wa_ref, wb_ref, wo_ref, x1_ref, h2_ref):
    ya = _dot(oa_ref[...], wa_ref[...])
    yb = _dot(ub_ref[...], wb_ref[...])
    merged = jax.nn.sigmoid(zga_ref[...]) * ya + jax.nn.sigmoid(zgb_ref[...]) * yb
    mix = _dot(merged.astype(BF16), wo_ref[...])
    tm = x_ref.shape[0]
    x1 = x_ref[...] + _mod_rows(gm_ref, tm) * mix
    x1_ref[...] = x1
    h2_ref[...] = (_rms(x1, g_ref[...]) * (1.0 + _mod_rows(sc_ref, tm)) + _mod_rows(sh_ref, tm)).astype(BF16)


def _merge(oa, ub, z, x, mod, g_ffn, wa, wb, wo, rows_per_seq):
    m, tm = x.shape[0], (256 if rows_per_seq >= 256 else 128)
    const = lambda shape: pl.BlockSpec(shape, lambda i: (0, 0), pipeline_mode=pl.Buffered(1))
    mspec = lambda k: _mod_spec(k, tm, rows_per_seq)
    rows = lambda w, c: pl.BlockSpec((tm, w), lambda i: (i, c))
    return pl.pallas_call(
        _merge_kernel,
        grid=(m // tm,),
        in_specs=[rows(RW, 0), rows(MLP_W, 0),
                  rows(D_MODEL, Z_GA // D_MODEL), rows(D_MODEL, Z_GB // D_MODEL), rows(D_MODEL, 0),
                  mspec(MOD_GATE_M), mspec(MOD_SCALE_F), mspec(MOD_SHIFT_F),
                  pl.BlockSpec((1, D_MODEL), lambda i: (0, 0)),
                  const((RW, D_MODEL)), const((MLP_W, D_MODEL)), const((D_MODEL, D_MODEL))],
        out_specs=[rows(D_MODEL, 0), rows(D_MODEL, 0)],
        out_shape=[jax.ShapeDtypeStruct((m, D_MODEL), F32), jax.ShapeDtypeStruct((m, D_MODEL), BF16)],
        compiler_params=_params(("arbitrary",)),
        name="merge",
    )(oa, ub, z, z, x, mod, mod, mod, g_ffn, wa, wb, wo)


def _ffn_kernel(*refs, fused):
    if fused:
        h_ref, x_ref, gf_ref, gfin_ref, wg_ref, wu_ref, wo_ref, y_ref = refs
    else:
        h_ref, wg_ref, wu_ref, wo_ref, y_ref = refs
    j = pl.program_id(1)

    @pl.when(j == 0)
    def _():
        y_ref[...] = jnp.zeros_like(y_ref)

    hb = h_ref[...]
    gt = _dot(hb, wg_ref[...].astype(BF16))
    up = _dot(hb, wu_ref[...].astype(BF16))
    act = gt * jax.nn.sigmoid(gt) * up
    y_ref[...] += _dot(act.astype(BF16), wo_ref[...].astype(BF16))

    if fused:
        @pl.when(j == pl.num_programs(1) - 1)
        def _():
            x2 = x_ref[...] + gf_ref[...] * y_ref[...]
            y_ref[...] = _rms(x2, gfin_ref[...])


def _final_kernel(f_ref, x_ref, gf_ref, gfin_ref, y_ref):
    tm = x_ref.shape[0]

    def body(rows, i):
        x2 = x_ref[rows, :] + _mod_chunk(gf_ref, i, tm) * f_ref[rows, :]
        y_ref[rows, :] = _rms(x2, gfin_ref[...])

    _for_row_chunks(tm, body)


def _ffn(h2, x1, mod, g_final, w_in, w_out, rows_per_seq):
    m, tm, tf = x1.shape[0], 1024, 256
    nf = D_FF // tf
    fused = rows_per_seq >= tm
    rows = pl.BlockSpec((tm, D_MODEL), lambda i, j: (i, 0))
    vec = pl.BlockSpec((1, D_MODEL), lambda i, j: (0, 0))
    wspecs = [pl.BlockSpec((D_MODEL, tf), lambda i, j: (0, j)),
              pl.BlockSpec((D_MODEL, tf), lambda i, j: (0, nf + j)),
              pl.BlockSpec((tf, D_MODEL), lambda i, j: (j, 0))]
    if fused:
        in_specs = [rows, pl.BlockSpec((tm, D_MODEL), lambda i, j: (i, 0), pipeline_mode=pl.Buffered(1)),
                    _mod_spec(MOD_GATE_F, tm, rows_per_seq), vec] + wspecs
        args = (h2, x1, mod, g_final, w_in, w_in, w_out)
    else:
        in_specs = [rows] + wspecs
        args = (h2, w_in, w_in, w_out)
    y = pl.pallas_call(
        functools.partial(_ffn_kernel, fused=fused),
        grid=(m // tm, nf),
        in_specs=in_specs,
        out_specs=rows,
        out_shape=jax.ShapeDtypeStruct((m, D_MODEL), F32),
        compiler_params=_params(("arbitrary", "arbitrary")),
        name="ffn",
    )(*args)
    if fused:
        return y
    te = 256
    erows = pl.BlockSpec((te, D_MODEL), lambda i: (i, 0))
    return pl.pallas_call(
        _final_kernel,
        grid=(m // te,),
        in_specs=[erows, erows, _mod_spec(MOD_GATE_F, te, rows_per_seq),
                  pl.BlockSpec((1, D_MODEL), lambda i: (0, 0))],
        out_specs=erows,
        out_shape=jax.ShapeDtypeStruct((m, D_MODEL), F32),
        compiler_params=_params(("arbitrary",)),
        name="final",
    )(y, x1, mod, g_final)


def kernel(x_prompt, x_sample, state_wkv, state_shift, c_prompt, c_sample, w_ada, b_ada, norm_mix_g, w_in, mu_shift, w0, w_decay_up, a0, w_aaa_up, w_gate_up, k_k, k_a, r_k, gn_g, gn_b, ln_v_g, ln_v_b, w_spatial, b_spatial, w_branch_a, w_branch_b, w_out, norm_ffn_g, w_ffn_in, w_ffn_out, norm_final_g):
    assert w_ada.shape[0] == 1, "single layer"
    bp, tp, _ = x_prompt.shape
    bs, ts, _ = x_sample.shape
    assert tp % CHUNK == 0 and WKV_ROWS % ts == 0 and CHUNK % ts == 0

    c_all = jnp.concatenate([c_sample, c_prompt], axis=0)
    c_all = jnp.pad(c_all, ((0, -c_all.shape[0] % 8), (0, 0)))
    mod = _ada(c_all, w_ada[0], b_ada[0])
    mod_s = mod[:bs].reshape(bs, 1, 6 * D_MODEL)
    mod_p = mod[bs:bs + bp].reshape(bp, 1, 6 * D_MODEL)

    wa_b, wb_b, wo_b = w_branch_a[0].astype(BF16), w_branch_b[0].astype(BF16), w_out[0].astype(BF16)
    wfi, wfo = w_ffn_in[0], w_ffn_out[0]

    row = lambda a: a.reshape(1, -1)
    mu = mu_shift[0]
    vecs = [row(mu[:RW]), row(mu[RW:2 * RW]), row(mu[2 * RW:3 * RW]), row(mu[3 * RW:]),
            row(w0[0]), row(a0[0]), row(k_k[0]), row(k_a[0]), row(r_k[0]), row(gn_g[0]), row(gn_b[0])]
    zpad = lambda w, lo: jnp.pad(w, ((lo, LORA - lo - w.shape[0]), (0, 0))).astype(BF16)
    mats = [zpad(w_decay_up[0], 0), zpad(w_aaa_up[0], DECAY_LORA),
            zpad(w_gate_up[0], DECAY_LORA + AAA_LORA)]

    bias = jnp.repeat(b_spatial[0].T, MLP_GD, axis=1)
    ws = w_spatial[0]

    xp = x_prompt.reshape(bp * tp, D_MODEL)
    xs = x_sample.reshape(bs * ts, D_MODEL)
    g_mix, g_ffn, g_fin = row(norm_mix_g[0]), row(norm_ffn_g[0]), row(norm_final_g)
    lng, lnb = row(ln_v_g[0]), row(ln_v_b[0])

    zp = _inproj(_hnorm(xp, mod_p, g_mix, tp), w_in[0])
    oa_p, st_p, shift_p = _wkv(zp, vecs, mats, bp, tp, WKV_ROWS, WKV_NCH, WKV_NSEQ)
    (ub_p,) = _gmlp(zp, lng, lnb, ws, bias, CHUNK, False)
    x1_p, h2_p = _merge(oa_p, ub_p, zp, xp, mod_p, g_ffn, wa_b, wb_b, wo_b, tp)
    y_p = _ffn(h2_p, x1_p, mod_p, g_fin, wfi, wfo, tp)

    zs = _inproj(_hnorm(xs, mod_s, g_mix, ts), w_in[0])
    prev_rows = jnp.repeat(state_shift[0], ts, axis=0)
    oa_s, st_s, shift_s = _wkv(zs, vecs, mats, bs, ts, ts, 1, WKV_NSEQ, prev_rows=prev_rows,
                               state_in=state_wkv[0].reshape(bs, RW, HEAD))
    ub_s, vn_s = _gmlp(zs, lng, lnb, ws, bias, ts, True)
    x1_s, h2_s = _merge(oa_s, ub_s, zs, xs, mod_s, g_ffn, wa_b, wb_b, wo_b, ts)
    y_s = _ffn(h2_s, x1_s, mod_s, g_fin, wfi, wfo, ts)

    return (y_p.reshape(bp, tp, D_MODEL),
            y_s.reshape(bs, ts, D_MODEL),
            st_p.reshape(1, bp, N_HEADS, HEAD, HEAD),
            shift_p.reshape(1, bp, C_SHIFT),
            st_s.reshape(1, bs, N_HEADS, HEAD, HEAD),
            shift_s.reshape(1, bs, C_SHIFT),
            vn_s.reshape(bs, ts, MLP_W)[None])
```

```python
import functools
import math

import jax
import jax.numpy as jnp
from jax import lax
from jax.experimental import pallas as pl
from jax.experimental.pallas import tpu as pltpu

F32 = jnp.float32
BF16 = jnp.bfloat16

D_MODEL = 2048
HEAD = 64
RW = D_MODEL // 2
N_HEADS = RW // HEAD
DECAY_LORA = D_MODEL // 32
AAA_LORA = D_MODEL // 32
GATE_LORA = D_MODEL // 16
LORA = DECAY_LORA + AAA_LORA + GATE_LORA
CHUNK = 128
MLP_W = D_MODEL // 2
MLP_GROUPS = 8
MLP_GD = MLP_W // MLP_GROUPS
D_FF = ((-(-8 * D_MODEL // 3)) + 255) // 256 * 256
C_SHIFT = 3 * RW + LORA
C_IN = C_SHIFT + 2 * MLP_W + 2 * D_MODEL
NORM_EPS = 1e-6
GN_EPS = 64e-5
LN_EPS = 1e-5

Z_GA, Z_GB, Z_U, Z_V, Z_R, Z_K, Z_VR, Z_L = 0, 2048, 4096, 5120, 6144, 7168, 8192, 9216
Z_TN = 512
Z_W = -(-C_IN // Z_TN) * Z_TN

GRP = 256
HPG = GRP // HEAD
N_GRP = RW // GRP
WKV_ROWS = 64
WKV_NCH = 2
WKV_NSEQ_PROMPT = 4
WKV_NSEQ = 2
VMEM_LIMIT = 56 * 1024 * 1024


def _dot(a, b):
    return jnp.dot(a, b, preferred_element_type=F32)


def _dot_nt(a, b):
    return lax.dot_general(a, b, (((1,), (1,)), ((), ())), preferred_element_type=F32)


def _dot_tn(a, b):
    return lax.dot_general(a, b, (((0,), (0,)), ((), ())), preferred_element_type=F32)


def _rms(x, g):
    return x * lax.rsqrt(jnp.mean(x * x, -1, keepdims=True) + NORM_EPS) * g


def _split2(x):
    hi = x.astype(BF16)
    lo = (x - hi.astype(F32)).astype(BF16)
    return hi, lo


def _split3(x):
    hi = x.astype(BF16)
    r1 = x - hi.astype(F32)
    mid = r1.astype(BF16)
    lo = (r1 - mid.astype(F32)).astype(BF16)
    return hi, mid, lo


def _params(sem):
    return pltpu.CompilerParams(dimension_semantics=sem, vmem_limit_bytes=VMEM_LIMIT)


def _ada_kernel(c_ref, w_ref, b_ref, o_ref, s_ref):
    @pl.when(pl.program_id(0) == 0)
    def _():
        c = c_ref[...]
        s_ref[...] = (c * jax.nn.sigmoid(c)).astype(BF16)

    o_ref[...] = _dot(s_ref[...], w_ref[...].astype(BF16)) + b_ref[...]


def _ada(c, w_ada, b_ada):
    m, n, tn = c.shape[0], w_ada.shape[1], 1024
    return pl.pallas_call(
        _ada_kernel,
        grid=(n // tn,),
        in_specs=[pl.BlockSpec((m, D_MODEL), lambda j: (0, 0)),
                  pl.BlockSpec((D_MODEL, tn), lambda j: (0, j)),
                  pl.BlockSpec((1, tn), lambda j: (0, j))],
        out_specs=pl.BlockSpec((m, tn), lambda j: (0, j)),
        out_shape=jax.ShapeDtypeStruct((m, n), F32),
        scratch_shapes=[pltpu.VMEM((m, D_MODEL), BF16)],
        compiler_params=_params(("arbitrary",)),
        name="ada",
    )(c, w_ada, b_ada.reshape(1, n))


MOD_SHIFT_M, MOD_SCALE_M, MOD_GATE_M, MOD_SHIFT_F, MOD_SCALE_F, MOD_GATE_F = range(6)


def _mod_spec(k, tm, rows_per_seq):
    if rows_per_seq >= tm:
        return pl.BlockSpec((None, 1, D_MODEL), lambda i, *_: (i * tm // rows_per_seq, 0, k))
    return pl.BlockSpec((tm // rows_per_seq, 1, D_MODEL), lambda i, *_: (i, 0, k))


def _mod_rows(ref, tm):
    m = ref[...]
    if m.ndim == 2:
        return m
    n_seq = m.shape[0]
    return jnp.broadcast_to(m, (n_seq, tm // n_seq, D_MODEL)).reshape(tm, D_MODEL)


ROW_CHUNK = 16


def _for_row_chunks(n_rows, body):
    def step(i, carry):
        body(pl.ds(pl.multiple_of(i * ROW_CHUNK, ROW_CHUNK), ROW_CHUNK), i)
        return carry
    lax.fori_loop(0, n_rows // ROW_CHUNK, step, 0, unroll=8)


def _mod_chunk(ref, i, tm):
    if len(ref.shape) == 2:
        return ref[...]
    per_seq = tm // ref.shape[0]
    n = ROW_CHUNK // per_seq
    m = ref[pl.ds(i * n, n)]
    return jnp.broadcast_to(m, (n, per_seq, D_MODEL)).reshape(ROW_CHUNK, D_MODEL)


ZB = 256
_N_RW_B, _N_MLP_B, _N_GATE_B = C_SHIFT // ZB, 2 * MLP_W // ZB, 2 * D_MODEL // ZB


def _w_in_block(zb):
    return jnp.where(zb < _N_GATE_B, zb + _N_RW_B + _N_MLP_B,
                     jnp.where(zb < _N_GATE_B + _N_MLP_B, zb - _N_GATE_B + _N_RW_B,
                               jnp.minimum(zb - _N_GATE_B - _N_MLP_B, _N_RW_B - 1)))


def _hnorm_kernel(x_ref, sc_ref, sh_ref, g_ref, h_ref):
    tm = x_ref.shape[0]

    def body(rows, i):
        h = _rms(x_ref[rows, :], g_ref[...]) * (1.0 + _mod_chunk(sc_ref, i, tm)) + _mod_chunk(sh_ref, i, tm)
        h_ref[rows, :] = h.astype(BF16)

    _for_row_chunks(tm, body)


def _hnorm(x, mod, g, rows_per_seq):
    m, tm = x.shape[0], (1024 if rows_per_seq >= 1024 else 512)
    mspec = lambda k: _mod_spec(k, tm, rows_per_seq)
    return pl.pallas_call(
        _hnorm_kernel,
        grid=(m // tm,),
        in_specs=[pl.BlockSpec((tm, D_MODEL), lambda i: (i, 0)),
                  mspec(MOD_SCALE_M), mspec(MOD_SHIFT_M),
                  pl.BlockSpec((1, D_MODEL), lambda i: (0, 0))],
        out_specs=pl.BlockSpec((tm, D_MODEL), lambda i: (i, 0)),
        out_shape=jax.ShapeDtypeStruct((m, D_MODEL), BF16),
        compiler_params=_params(("arbitrary",)),
        name="hnorm",
    )(x, mod, mod, g)


def _inproj_kernel(h_ref, wlo_ref, whi_ref, o_ref):
    h = h_ref[...]
    o_ref[:, :ZB] = _dot(h, wlo_ref[...].astype(BF16))
    o_ref[:, ZB:] = _dot(h, whi_ref[...].astype(BF16))


def _inproj(h, w):
    m = h.shape[0]
    tm = min(m, 2048)
    assert Z_TN == 2 * ZB and m % tm == 0
    return pl.pallas_call(
        _inproj_kernel,
        grid=(m // tm, Z_W // Z_TN),
        in_specs=[pl.BlockSpec((tm, D_MODEL), lambda i, j: (i, 0)),
                  pl.BlockSpec((D_MODEL, ZB), lambda i, j: (0, _w_in_block(2 * j))),
                  pl.BlockSpec((D_MODEL, ZB), lambda i, j: (0, _w_in_block(2 * j + 1)))],
        out_specs=pl.BlockSpec((tm, Z_TN), lambda i, j: (i, j)),
        out_shape=jax.ShapeDtypeStruct((m, Z_W), F32),
        compiler_params=_params(("arbitrary", "arbitrary")),
        name="inproj",
    )(h, w, w)


def _bd(w, mask):
    return jnp.where(mask, jnp.concatenate([w] * HPG, axis=0), 0.0).astype(BF16)


def _bd_heads(nat, mask):
    return jnp.where(mask, jnp.concatenate([nat] * HPG, axis=1), 0.0).astype(BF16)


def _diag_blocks(m):
    return jnp.concatenate([m[h * HEAD:(h + 1) * HEAD, h * HEAD:(h + 1) * HEAD] for h in range(HPG)],
                           axis=0)


def _wkv_kernel(*refs, lb, carried, nch, nseq):
    R = WKV_ROWS
    if carried:
        (zr_ref, zk_ref, zv_ref, zl_ref,
         mur_ref, muk_ref, muv_ref, mul_ref,
         w0_ref, a0_ref, kkp_ref, kap_ref, rkp_ref, gng_ref, gnb_ref,
         wd_ref, wa_ref, wg_ref,
         oa_ref, so_ref, sho_ref,
         s_ref, cr_ref, ck_ref, cv_ref, cl_ref) = refs
        carry = {id(zr_ref): cr_ref, id(zk_ref): ck_ref, id(zv_ref): cv_ref, id(zl_ref): cl_ref}
    else:
        (zr_ref, zk_ref, zv_ref, zl_ref,
         pr_ref, pk_ref, pv_ref, pl_ref, si_ref,
         mur_ref, muk_ref, muv_ref, mul_ref,
         w0_ref, a0_ref, kkp_ref, kap_ref, rkp_ref, gng_ref, gnb_ref,
         wd_ref, wa_ref, wg_ref,
         oa_ref, so_ref, sho_ref) = refs
        prev_of = {id(zr_ref): pr_ref, id(zk_ref): pk_ref, id(zv_ref): pv_ref, id(zl_ref): pl_ref}
    nblk = R // lb
    z_refs = (zr_ref, zk_ref, zv_ref, zl_ref)
    mu_refs = (mur_ref, muk_ref, muv_ref, mul_ref)

    if carried:
        @pl.when(pl.program_id(1) == 0)
        def _():
            s_ref[...] = jnp.zeros_like(s_ref)
            for ref in (cr_ref, ck_ref, cv_ref, cl_ref):
                ref[...] = jnp.zeros_like(ref)

    row1 = lax.broadcasted_iota(jnp.int32, (R, 1), 0)
    first = (row1 % lb) == 0
    ri = lax.broadcasted_iota(jnp.int32, (GRP, GRP), 0)
    ci = lax.broadcasted_iota(jnp.int32, (GRP, GRP), 1)
    bdm = (ri // HEAD) == (ci // HEAD)
    ones_bd = jnp.where(bdm, 1.0, 0.0).astype(BF16)
    tr = lax.broadcasted_iota(jnp.int32, (R, R), 0)
    tc = lax.broadcasted_iota(jnp.int32, (R, R), 1)
    same = (tr // lb) == (tc // lb)
    incl01 = jnp.where(same & (tc <= tr), 1.0, 0.0).astype(BF16)
    same01 = jnp.where(same, 1.0, 0.0).astype(BF16)
    wr = lax.broadcasted_iota(jnp.int32, (R, GRP), 0)
    wc = lax.broadcasted_iota(jnp.int32, (R, GRP), 1) % HEAD
    wsame = (wr // lb) == (wc // lb)
    strict_w = wsame & (wc < wr)
    incl_w = wsame & (wc <= wr)
    eye_w = jnp.where(wc == wr, 1.0, 0.0)
    sls = [slice(gi * GRP, (gi + 1) * GRP) for gi in range(N_GRP)]
    blks = [slice(b * lb, (b + 1) * lb) for b in range(nblk)]
    units = [(s, gi) for s in range(nseq) for gi in range(N_GRP)]
    uids = range(len(units))

    def seg_sums(xs):
        n = xs[0].shape[0]
        parts = []
        for x in xs:
            parts.extend(_split2(x))
        stacked = jnp.concatenate(parts, axis=0)
        cols = [_dot(stacked[:, sl], ones_bd) for sl in sls]
        full = jnp.concatenate(cols, axis=1)
        return [full[2 * i * n:(2 * i + 1) * n] + full[(2 * i + 1) * n:(2 * i + 2) * n]
                for i in range(len(xs))]

    def cum(m01, x):
        hi, mid, lo = _split3(x)
        return _dot(m01, hi) + _dot(m01, mid) + _dot(m01, lo)

    def prep(c, outs):
        rows = slice(c * R, (c + 1) * R)

        def shift(z_ref, mu_ref):
            parts = []
            for s in range(nseq):
                z = z_ref[s, rows, :]
                if not carried:
                    prev0 = prev_of[id(z_ref)][s, rows, :]
                elif c == 0:
                    prev0 = carry[id(z_ref)][s]
                else:
                    prev0 = z_ref[s, c * R - 1:c * R, :]
                prev = jnp.where(first, prev0, pltpu.roll(z, 1, axis=0))
                parts.append(z + (prev - z) * mu_ref[...])
            return jnp.concatenate(parts, axis=0)

        per_stream = lambda x: [x[s * R:(s + 1) * R] for s in range(nseq)]
        r, k, v, l = [shift(z_ref, mu_ref) for z_ref, mu_ref in zip(z_refs, mu_refs)]
        w_raw = w0_ref[...] + _dot(jnp.tanh(l).astype(BF16), wd_ref[...])
        a_raw = a0_ref[...] + _dot(l.astype(BF16), wa_ref[...])
        g = _dot(jax.nn.sigmoid(l).astype(BF16), wg_ref[...])
        yield
        logw = jax.nn.sigmoid(w_raw) * (-math.exp(-0.5))
        a = jax.nn.sigmoid(a_raw)
        kk = k * kkp_ref[...]
        k2 = k * (1.0 + (a - 1.0) * kap_ref[...])
        ss, rk = seg_sums([kk * kk, r * k2 * rkp_ref[...]])
        cl_s = [cum(incl01, lw) for lw in per_stream(logw)]
        if nblk == 1:
            ce_s = [jnp.broadcast_to(x[R - 1:R, :], (R, RW)) for x in cl_s]
        else:
            ce_s = [cum(same01, lw) for lw in per_stream(logw)]
        yield
        cl = jnp.concatenate(cl_s, axis=0)
        cl_end = jnp.concatenate(ce_s, axis=0)
        kk = kk / jnp.maximum(jnp.sqrt(ss), 1e-12)
        av = -kk
        bv = kk * a
        e_neg = jnp.exp(-cl)
        e_end = jnp.exp(cl_end - cl)
        full = dict(v=v, g=g, bonus=rk * v,
                    at=av * jnp.exp(cl - logw), rt=r * jnp.exp(cl),
                    bt=bv * e_neg, kt=k2 * e_neg, bh=bv * e_end, kh=k2 * e_end)
        for name, x in full.items():
            for s, x_s in enumerate(per_stream(x)):
                outs[s][name] = x_s
        for s in range(nseq):
            outs[s]["p_end"] = jnp.exp(cl_s[s][R - 1:R, :] if nblk == 1 else ce_s[s])

    def main(c, ps):
        rows_c = slice(c * R, (c + 1) * R)
        col = lambda name, ui: ps[units[ui][0]][name][:, sls[units[ui][1]]]
        v_g = [col("v", ui) for ui in uids]
        lhs = [jnp.concatenate([col("at", ui), col("rt", ui)], axis=0).astype(BF16) for ui in uids]
        o1 = [_dot_nt(lhs[ui], jnp.concatenate([_bd(col("bt", ui), bdm), _bd(col("kt", ui), bdm)], axis=0))
              for ui in uids]
        yield
        w_ab = [jnp.where(strict_w, o[:R, :GRP], 0.0) for o in o1]
        w_ak = [jnp.where(strict_w, o[:R, GRP:], 0.0) for o in o1]
        w_rb = [jnp.where(incl_w, o[R:, :GRP], 0.0) for o in o1]
        w_rk = [jnp.where(incl_w, o[R:, GRP:], 0.0) for o in o1]

        def state_part(ui):
            s, gi = units[ui]
            if carried:
                o2 = _dot_nt(lhs[ui], s_ref[s, gi].astype(BF16))
                return o2[:R], o2[R:]
            at_u, rt_u = col("at", ui), col("rt", ui)
            a_rows, r_rows = [], []
            for b, rows in enumerate(blks):
                lhs_b = jnp.concatenate([at_u[rows], rt_u[rows]], axis=0).astype(BF16)
                o2 = _dot_nt(lhs_b, _bd_heads(si_ref[s, c * nblk + b, sls[gi], :], bdm))
                a_rows.append(o2[:lb])
                r_rows.append(o2[lb:])
            return jnp.concatenate(a_rows, axis=0), jnp.concatenate(r_rows, axis=0)

        x_w = list(w_ab)
        t_w = [eye_w + x for x in x_w]
        n_lvl = max(1, int(math.log2(lb)))
        a_s = r_s = o3 = None
        for j in range(n_lvl):
            for ui in uids:
                y_bd = _bd(x_w[ui], bdm)
                if j == 0:
                    x_w[ui] = _dot(x_w[ui].astype(BF16), y_bd)
                elif j == n_lvl - 1:
                    t_w[ui] = t_w[ui] + _dot(t_w[ui].astype(BF16), y_bd)
                else:
                    res = _dot(jnp.concatenate([t_w[ui], x_w[ui]], axis=0).astype(BF16), y_bd)
                    t_w[ui] = t_w[ui] + res[:R]
                    x_w[ui] = res[R:]
            yield
            if j == 0:
                parts = [state_part(ui) for ui in uids]
                a_s, r_s = [q[0] for q in parts], [q[1] for q in parts]
                yield
            elif j == 1:
                o3 = [_dot(jnp.concatenate([w_ak[ui], w_rk[ui]], axis=0).astype(BF16), _bd(v_g[ui], bdm))
                      for ui in uids]
                yield

        u = [_dot(t_w[ui].astype(BF16), _bd(a_s[ui] + o3[ui][:R], bdm)) for ui in uids]
        yield
        o_parts = [r_s[ui] + o3[ui][R:] + _dot(w_rb[ui].astype(BF16), _bd(u[ui], bdm)) for ui in uids]
        yield

        for ui in uids:
            s, gi = units[ui]
            sl = sls[gi]
            bh_u, kh_u, p_end = col("bh", ui), col("kh", ui), ps[s]["p_end"][:, sl]
            if carried:
                upd = _dot_tn(jnp.concatenate([u[ui], v_g[ui]], axis=0).astype(BF16),
                              jnp.concatenate([bh_u, kh_u], axis=0).astype(BF16))
                s_new = jnp.where(bdm, s_ref[s, gi] * p_end + upd, 0.0)
                s_ref[s, gi] = s_new
                if c == nch - 1:
                    so_ref[s, sl, :] = _diag_blocks(s_new)
            else:
                for b, rows in enumerate(blks):
                    upd = _dot_tn(jnp.concatenate([u[ui][rows], v_g[ui][rows]], axis=0).astype(BF16),
                                  jnp.concatenate([bh_u[rows], kh_u[rows]], axis=0).astype(BF16))
                    p_b = p_end[b * lb:b * lb + 1]
                    p_nat = jnp.concatenate(
                        [jnp.broadcast_to(p_b[:, h * HEAD:(h + 1) * HEAD], (HEAD, HEAD)) for h in range(HPG)],
                        axis=0)
                    so_ref[s, c * nblk + b, sl, :] = (si_ref[s, c * nblk + b, sl, :] * p_nat
                                                      + _diag_blocks(upd))
        yield

        o = [jnp.concatenate(o_parts[s * N_GRP:(s + 1) * N_GRP], axis=1) for s in range(nseq)]
        mu = seg_sums(o)
        yield
        dlt = [o[s] - mu[s] * (1.0 / HEAD) for s in range(nseq)]
        var = seg_sums([d * d for d in dlt])
        for s in range(nseq):
            on = dlt[s] * lax.rsqrt(var[s] * (1.0 / HEAD) + GN_EPS) * gng_ref[...] + gnb_ref[...]
            oa_ref[s, rows_c, :] = ((on + ps[s]["bonus"]) * ps[s]["g"]).astype(BF16)

    def run(gen):
        for _ in gen:
            pass

    cur = [{} for _ in range(nseq)]
    run(prep(0, cur))
    for c in range(nch):
        nxt = [{} for _ in range(nseq)]
        side = prep(c + 1, nxt) if c + 1 < nch else iter(())
        for tick, _ in enumerate(main(c, cur)):
            if tick % 3 == 1:
                next(side, None)
        run(side)
        cur = nxt

    n_rows = nch * R
    for s in range(nseq):
        if carried:
            last = lambda z_ref: z_ref[s, n_rows - 1:n_rows, :]
            for z_ref in z_refs:
                carry[id(z_ref)][s] = last(z_ref)
        else:
            last = lambda z_ref: jnp.concatenate(
                [z_ref[s, b * lb + lb - 1:b * lb + lb, :] for b in range(nch * nblk)], axis=0)
        sho_ref[s, :, 0:RW] = last(zr_ref)
        sho_ref[s, :, RW:2 * RW] = last(zk_ref)
        sho_ref[s, :, 2 * RW:3 * RW] = last(zv_ref)
        sho_ref[s, :, 3 * RW:] = last(zl_ref)


def _wkv(z, vecs, mats, n_seq, seq_len, lb, nch, nseq, prev_rows=None, state_in=None):
    R = nch * WKV_ROWS
    carried = prev_rows is None
    rows = z.shape[0]
    n_streams = n_seq if carried else nseq
    stream_rows = rows // n_streams
    assert n_streams % nseq == 0 and stream_rows % R == 0 and rows == n_streams * stream_rows
    z3 = z.reshape(n_streams, stream_rows, Z_W)
    if carried:
        grid = (n_streams // nseq, stream_rows // R)
        rmap = lambda c: (lambda b, t: (b, t, c))
        cmap = lambda b, t: (0, 0)
        smap = lambda b, t: (b, 0, 0)
        sem = ("arbitrary", "arbitrary")
    else:
        grid = (stream_rows // R,)
        rmap = lambda c: (lambda i: (0, i, c))
        cmap = lambda i: (0, 0)
        sem = ("arbitrary",)
        blk_per_step = R // lb
        seq_per_stream = n_seq // nseq

    in_specs = [pl.BlockSpec((nseq, R, RW), rmap(Z_R // RW)),
                pl.BlockSpec((nseq, R, RW), rmap(Z_K // RW)),
                pl.BlockSpec((nseq, R, RW), rmap(Z_VR // RW)),
                pl.BlockSpec((nseq, R, LORA), rmap(Z_L // LORA))]
    args = [z3, z3, z3, z3]
    if not carried:
        p3 = prev_rows.reshape(nseq, stream_rows, C_SHIFT)
        st4 = state_in.reshape(nseq, seq_per_stream, RW, HEAD)
        in_specs += [pl.BlockSpec((nseq, R, RW), rmap(0)), pl.BlockSpec((nseq, R, RW), rmap(1)),
                     pl.BlockSpec((nseq, R, RW), rmap(2)), pl.BlockSpec((nseq, R, LORA), rmap(3 * RW // LORA)),
                     pl.BlockSpec((nseq, blk_per_step, RW, HEAD), lambda i: (0, i, 0, 0))]
        args += [p3] * 4 + [st4]
    for a in vecs + mats:
        in_specs.append(pl.BlockSpec(a.shape, cmap))
        args.append(a)

    if carried:
        scratch = [pltpu.VMEM((nseq, N_GRP, GRP, GRP), F32), pltpu.VMEM((nseq, 1, RW), F32),
                   pltpu.VMEM((nseq, 1, RW), F32), pltpu.VMEM((nseq, 1, RW), F32),
                   pltpu.VMEM((nseq, 1, LORA), F32)]
        state_spec = pl.BlockSpec((nseq, RW, HEAD), smap)
        state_shape = (n_seq, RW, HEAD)
        shift_spec = pl.BlockSpec((nseq, 1, C_SHIFT), smap)
        shift_shape = (n_seq, 1, C_SHIFT)
    else:
        scratch = []
        state_spec = pl.BlockSpec((nseq, blk_per_step, RW, HEAD), lambda i: (0, i, 0, 0))
        state_shape = (nseq, seq_per_stream, RW, HEAD)
        shift_spec = pl.BlockSpec((nseq, blk_per_step, C_SHIFT), lambda i: (0, i, 0))
        shift_shape = (nseq, seq_per_stream, C_SHIFT)
    oa, st, sh = pl.pallas_call(
        functools.partial(_wkv_kernel, lb=lb, carried=carried, nch=nch, nseq=nseq),
        grid=grid,
        in_specs=in_specs,
        out_specs=[pl.BlockSpec((nseq, R, RW), rmap(0)), state_spec, shift_spec],
        out_shape=[jax.ShapeDtypeStruct((n_streams, stream_rows, RW), BF16),
                   jax.ShapeDtypeStruct(state_shape, F32),
                   jax.ShapeDtypeStruct(shift_shape, F32)],
        scratch_shapes=scratch,
        compiler_params=_params(sem),
        name="wkv_carried" if carried else "wkv_blocks",
    )(*args)
    return oa.reshape(rows, RW), st.reshape(n_seq, RW, HEAD), sh.reshape(n_seq, C_SHIFT)


def _gmlp_kernel(zu_ref, zv_ref, lng_ref, lnb_ref, ws_ref, bias_ref, ub_ref, vn_ref=None, *, seq):
    reps = CHUNK // seq
    tr = lax.broadcasted_iota(jnp.int32, (CHUNK, CHUNK), 0)
    tc = lax.broadcasted_iota(jnp.int32, (CHUNK, CHUNK), 1)
    causal = (tc <= tr) & ((tr // seq) == (tc // seq))
    ws_g = []
    for gi in range(MLP_GROUPS):
        w_rows = ws_ref[gi]
        w_full = jnp.concatenate([w_rows if s == 0 else pltpu.roll(w_rows, s * seq, axis=1)
                                  for s in range(reps)], axis=0)
        ws_g.append(jnp.where(causal, w_full, 0.0).astype(BF16))
    bias = jnp.concatenate([bias_ref[...]] * reps, axis=0)
    for c in range(zu_ref.shape[0] // CHUNK):
        rows = slice(c * CHUNK, (c + 1) * CHUNK)
        u = jax.nn.gelu(zu_ref[rows, :])
        vg = jax.nn.gelu(zv_ref[rows, :])
        mu = jnp.mean(vg, -1, keepdims=True)
        var = jnp.mean(jnp.square(vg - mu), -1, keepdims=True)
        vn = (vg - mu) * lax.rsqrt(var + LN_EPS) * lng_ref[...] + lnb_ref[...]
        if vn_ref is not None:
            vn_ref[rows, :] = vn
        vb = vn.astype(BF16)
        cols = [_dot(ws_g[gi], vb[:, gi * MLP_GD:(gi + 1) * MLP_GD]) for gi in range(MLP_GROUPS)]
        mixed = jnp.concatenate(cols, axis=1) + bias
        ub_ref[rows, :] = (u * mixed).astype(BF16)


def _gmlp(z, ln_g, ln_b, ws, bias, seq, want_vn):
    rows, tr = z.shape[0], 4 * CHUNK
    n_out = 2 if want_vn else 1
    return pl.pallas_call(
        functools.partial(_gmlp_kernel, seq=seq),
        grid=(rows // tr,),
        in_specs=[pl.BlockSpec((tr, MLP_W), lambda i: (i, Z_U // MLP_W)),
                  pl.BlockSpec((tr, MLP_W), lambda i: (i, Z_V // MLP_W)),
                  pl.BlockSpec((1, MLP_W), lambda i: (0, 0)),
                  pl.BlockSpec((1, MLP_W), lambda i: (0, 0)),
                  pl.BlockSpec((MLP_GROUPS, seq, CHUNK), lambda i: (0, 0, 0)),
                  pl.BlockSpec((seq, MLP_W), lambda i: (0, 0))],
        out_specs=[pl.BlockSpec((tr, MLP_W), lambda i: (i, 0)),
                   pl.BlockSpec((tr, MLP_W), lambda i: (i, 0))][:n_out],
        out_shape=[jax.ShapeDtypeStruct((rows, MLP_W), BF16),
                   jax.ShapeDtypeStruct((rows, MLP_W), F32)][:n_out],
        compiler_params=_params(("arbitrary",)),
        name="gmlp",
    )(z, z, ln_g, ln_b, ws, bias)


def _merge_kernel(oa_ref, ub_ref, zga_ref, zgb_ref, x_ref, gm_ref, sc_ref, sh_ref, g_ref,
                  wa_ref, wb_ref, wo_ref, x1_ref, h2_ref):
    ya = _dot(oa_ref[...], wa_ref[...])
    yb = _dot(ub_ref[...], wb_ref[...])
    merged = jax.nn.sigmoid(zga_ref[...]) * ya + jax.nn.sigmoid(zgb_ref[...]) * yb
    mix = _dot(merged.astype(BF16), wo_ref[...])
    tm = x_ref.shape[0]
    x1 = x_ref[...] + _mod_rows(gm_ref, tm) * mix
    x1_ref[...] = x1
    h2_ref[...] = (_rms(x1, g_ref[...]) * (1.0 + _mod_rows(sc_ref, tm)) + _mod_rows(sh_ref, tm)).astype(BF16)


def _merge(oa, ub, z, x, mod, g_ffn, wa, wb, wo, rows_per_seq):
    m, tm = x.shape[0], (256 if rows_per_seq >= 256 else 128)
    const = lambda shape: pl.BlockSpec(shape, lambda i: (0, 0), pipeline_mode=pl.Buffered(1))
    mspec = lambda k: _mod_spec(k, tm, rows_per_seq)
    rows = lambda w, c: pl.BlockSpec((tm, w), lambda i: (i, c))
    return pl.pallas_call(
        _merge_kernel,
        grid=(m // tm,),
        in_specs=[rows(RW, 0), rows(MLP_W, 0),
                  rows(D_MODEL, Z_GA // D_MODEL), rows(D_MODEL, Z_GB // D_MODEL), rows(D_MODEL, 0),
                  mspec(MOD_GATE_M), mspec(MOD_SCALE_F), mspec(MOD_SHIFT_F),
                  pl.BlockSpec((1, D_MODEL), lambda i: (0, 0)),
                  const((RW, D_MODEL)), const((MLP_W, D_MODEL)), const((D_MODEL, D_MODEL))],
        out_specs=[rows(D_MODEL, 0), rows(D_MODEL, 0)],
        out_shape=[jax.ShapeDtypeStruct((m, D_MODEL), F32), jax.ShapeDtypeStruct((m, D_MODEL), BF16)],
        compiler_params=_params(("arbitrary",)),
        name="merge",
    )(oa, ub, z, z, x, mod, mod, mod, g_ffn, wa, wb, wo)


def _ffn_kernel(*refs, fused):
    if fused:
        h_ref, x_ref, gf_ref, gfin_ref, wg_ref, wu_ref, wo_ref, y_ref = refs
    else:
        h_ref, wg_ref, wu_ref, wo_ref, y_ref = refs
    j = pl.program_id(1)

    @pl.when(j == 0)
    def _():
        y_ref[...] = jnp.zeros_like(y_ref)

    hb = h_ref[...]
    gt = _dot(hb, wg_ref[...].astype(BF16))
    up = _dot(hb, wu_ref[...].astype(BF16))
    act = gt * jax.nn.sigmoid(gt) * up
    y_ref[...] += _dot(act.astype(BF16), wo_ref[...].astype(BF16))

    if fused:
        @pl.when(j == pl.num_programs(1) - 1)
        def _():
            x2 = x_ref[...] + gf_ref[...] * y_ref[...]
            y_ref[...] = _rms(x2, gfin_ref[...])


def _final_kernel(f_ref, x_ref, gf_ref, gfin_ref, y_ref):
    tm = x_ref.shape[0]

    def body(rows, i):
        x2 = x_ref[rows, :] + _mod_chunk(gf_ref, i, tm) * f_ref[rows, :]
        y_ref[rows, :] = _rms(x2, gfin_ref[...])

    _for_row_chunks(tm, body)


def _ffn(h2, x1, mod, g_final, w_in, w_out, rows_per_seq):
    m, tm, tf = x1.shape[0], 1024, 256
    nf = D_FF // tf
    fused = rows_per_seq >= tm
    rows = pl.BlockSpec((tm, D_MODEL), lambda i, j: (i, 0))
    vec = pl.BlockSpec((1, D_MODEL), lambda i, j: (0, 0))
    wspecs = [pl.BlockSpec((D_MODEL, tf), lambda i, j: (0, j)),
              pl.BlockSpec((D_MODEL, tf), lambda i, j: (0, nf + j)),
              pl.BlockSpec((tf, D_MODEL), lambda i, j: (j, 0))]
    if fused:
        in_specs = [rows, pl.BlockSpec((tm, D_MODEL), lambda i, j: (i, 0), pipeline_mode=pl.Buffered(1)),
                    _mod_spec(MOD_GATE_F, tm, rows_per_seq), vec] + wspecs
        args = (h2, x1, mod, g_final, w_in, w_in, w_out)
    else:
        in_specs = [rows] + wspecs
        args = (h2, w_in, w_in, w_out)
    y = pl.pallas_call(
        functools.partial(_ffn_kernel, fused=fused),
        grid=(m // tm, nf),
        in_specs=in_specs,
        out_specs=rows,
        out_shape=jax.ShapeDtypeStruct((m, D_MODEL), F32),
        compiler_params=_params(("arbitrary", "arbitrary")),
        name="ffn",
    )(*args)
    if fused:
        return y
    te = 256
    erows = pl.BlockSpec((te, D_MODEL), lambda i: (i, 0))
    return pl.pallas_call(
        _final_kernel,
        grid=(m // te,),
        in_specs=[erows, erows, _mod_spec(MOD_GATE_F, te, rows_per_seq),
                  pl.BlockSpec((1, D_MODEL), lambda i: (0, 0))],
        out_specs=erows,
        out_shape=jax.ShapeDtypeStruct((m, D_MODEL), F32),
        compiler_params=_params(("arbitrary",)),
        name="final",
    )(y, x1, mod, g_final)


def kernel(x_prompt, x_sample, state_wkv, state_shift, c_prompt, c_sample, w_ada, b_ada, norm_mix_g, w_in, mu_shift, w0, w_decay_up, a0, w_aaa_up, w_gate_up, k_k, k_a, r_k, gn_g, gn_b, ln_v_g, ln_v_b, w_spatial, b_spatial, w_branch_a, w_branch_b, w_out, norm_ffn_g, w_ffn_in, w_ffn_out, norm_final_g):
    assert w_ada.shape[0] == 1, "single layer"
    bp, tp, _ = x_prompt.shape
    bs, ts, _ = x_sample.shape
    assert tp % CHUNK == 0 and WKV_ROWS % ts == 0 and CHUNK % ts == 0

    c_all = jnp.concatenate([c_sample, c_prompt], axis=0)
    c_all = jnp.pad(c_all, ((0, -c_all.shape[0] % 8), (0, 0)))
    mod = _ada(c_all, w_ada[0], b_ada[0])
    mod_s = mod[:bs].reshape(bs, 1, 6 * D_MODEL)
    mod_p = mod[bs:bs + bp].reshape(bp, 1, 6 * D_MODEL)

    wa_b, wb_b, wo_b = w_branch_a[0].astype(BF16), w_branch_b[0].astype(BF16), w_out[0].astype(BF16)
    wfi, wfo = w_ffn_in[0], w_ffn_out[0]

    row = lambda a: a.reshape(1, -1)
    mu = mu_shift[0]
    vecs = [row(mu[:RW]), row(mu[RW:2 * RW]), row(mu[2 * RW:3 * RW]), row(mu[3 * RW:]),
            row(w0[0]), row(a0[0]), row(k_k[0]), row(k_a[0]), row(r_k[0]), row(gn_g[0]), row(gn_b[0])]
    zpad = lambda w, lo: jnp.pad(w, ((lo, LORA - lo - w.shape[0]), (0, 0))).astype(BF16)
    mats = [zpad(w_decay_up[0], 0), zpad(w_aaa_up[0], DECAY_LORA),
            zpad(w_gate_up[0], DECAY_LORA + AAA_LORA)]

    bias = jnp.repeat(b_spatial[0].T, MLP_GD, axis=1)
    ws = w_spatial[0]

    xp = x_prompt.reshape(bp * tp, D_MODEL)
    xs = x_sample.reshape(bs * ts, D_MODEL)
    g_mix, g_ffn, g_fin = row(norm_mix_g[0]), row(norm_ffn_g[0]), row(norm_final_g)
    lng, lnb = row(ln_v_g[0]), row(ln_v_b[0])

    zp = _inproj(_hnorm(xp, mod_p, g_mix, tp), w_in[0])
    oa_p, st_p, shift_p = _wkv(zp, vecs, mats, bp, tp, WKV_ROWS, WKV_NCH, WKV_NSEQ_PROMPT)
    (ub_p,) = _gmlp(zp, lng, lnb, ws, bias, CHUNK, False)
    x1_p, h2_p = _merge(oa_p, ub_p, zp, xp, mod_p, g_ffn, wa_b, wb_b, wo_b, tp)
    y_p = _ffn(h2_p, x1_p, mod_p, g_fin, wfi, wfo, tp)

    zs = _inproj(_hnorm(xs, mod_s, g_mix, ts), w_in[0])
    prev_rows = jnp.repeat(state_shift[0], ts, axis=0)
    oa_s, st_s, shift_s = _wkv(zs, vecs, mats, bs, ts, ts, 1, WKV_NSEQ, prev_rows=prev_rows,
                               state_in=state_wkv[0].reshape(bs, RW, HEAD))
    ub_s, vn_s = _gmlp(zs, lng, lnb, ws, bias, ts, True)
    x1_s, h2_s = _merge(oa_s, ub_s, zs, xs, mod_s, g_ffn, wa_b, wb_b, wo_b, ts)
    y_s = _ffn(h2_s, x1_s, mod_s, g_fin, wfi, wfo, ts)

    return (y_p.reshape(bp, tp, D_MODEL),
            y_s.reshape(bs, ts, D_MODEL),
            st_p.reshape(1, bp, N_HEADS, HEAD, HEAD),
            shift_p.reshape(1, bp, C_SHIFT),
            st_s.reshape(1, bs, N_HEADS, HEAD, HEAD),
            shift_s.reshape(1, bs, C_SHIFT),
            vn_s.reshape(bs, ts, MLP_W)[None])
```

```python
import functools
import math

import jax
import jax.numpy as jnp
from jax import lax
from jax.experimental import pallas as pl
from jax.experimental.pallas import tpu as pltpu

F32 = jnp.float32
BF16 = jnp.bfloat16

D_MODEL = 2048
HEAD = 64
RW = D_MODEL // 2
N_HEADS = RW // HEAD
DECAY_LORA = D_MODEL // 32
AAA_LORA = D_MODEL // 32
GATE_LORA = D_MODEL // 16
LORA = DECAY_LORA + AAA_LORA + GATE_LORA
CHUNK = 128
MLP_W = D_MODEL // 2
MLP_GROUPS = 8
MLP_GD = MLP_W // MLP_GROUPS
D_FF = ((-(-8 * D_MODEL // 3)) + 255) // 256 * 256
C_SHIFT = 3 * RW + LORA
C_IN = C_SHIFT + 2 * MLP_W + 2 * D_MODEL
NORM_EPS = 1e-6
GN_EPS = 64e-5
LN_EPS = 1e-5

Z_GA, Z_GB, Z_U, Z_V, Z_R, Z_K, Z_VR, Z_L = 0, 2048, 4096, 5120, 6144, 7168, 8192, 9216
Z_TN = 512
Z_W = -(-C_IN // Z_TN) * Z_TN

GRP = 256
HPG = GRP // HEAD
N_GRP = RW // GRP
WKV_ROWS = 64
WKV_NCH = 2
WKV_NSEQ_PROMPT = 2
WKV_NSEQ = 2
VMEM_LIMIT = 56 * 1024 * 1024


def _dot(a, b):
    return jnp.dot(a, b, preferred_element_type=F32)


def _dot_nt(a, b):
    return lax.dot_general(a, b, (((1,), (1,)), ((), ())), preferred_element_type=F32)


def _dot_tn(a, b):
    return lax.dot_general(a, b, (((0,), (0,)), ((), ())), preferred_element_type=F32)


def _rms(x, g):
    return x * lax.rsqrt(jnp.mean(x * x, -1, keepdims=True) + NORM_EPS) * g


def _split2(x):
    hi = x.astype(BF16)
    lo = (x - hi.astype(F32)).astype(BF16)
    return hi, lo


def _split3(x):
    hi = x.astype(BF16)
    r1 = x - hi.astype(F32)
    mid = r1.astype(BF16)
    lo = (r1 - mid.astype(F32)).astype(BF16)
    return hi, mid, lo


def _params(sem):
    return pltpu.CompilerParams(dimension_semantics=sem, vmem_limit_bytes=VMEM_LIMIT)


def _ada_kernel(c_ref, w_ref, b_ref, o_ref, s_ref):
    @pl.when(pl.program_id(0) == 0)
    def _():
        c = c_ref[...]
        s_ref[...] = (c * jax.nn.sigmoid(c)).astype(BF16)

    o_ref[...] = _dot(s_ref[...], w_ref[...].astype(BF16)) + b_ref[...]


def _ada(c, w_ada, b_ada):
    m, n, tn = c.shape[0], w_ada.shape[1], 1024
    return pl.pallas_call(
        _ada_kernel,
        grid=(n // tn,),
        in_specs=[pl.BlockSpec((m, D_MODEL), lambda j: (0, 0)),
                  pl.BlockSpec((D_MODEL, tn), lambda j: (0, j)),
                  pl.BlockSpec((1, tn), lambda j: (0, j))],
        out_specs=pl.BlockSpec((m, tn), lambda j: (0, j)),
        out_shape=jax.ShapeDtypeStruct((m, n), F32),
        scratch_shapes=[pltpu.VMEM((m, D_MODEL), BF16)],
        compiler_params=_params(("arbitrary",)),
        name="ada",
    )(c, w_ada, b_ada.reshape(1, n))


MOD_SHIFT_M, MOD_SCALE_M, MOD_GATE_M, MOD_SHIFT_F, MOD_SCALE_F, MOD_GATE_F = range(6)


def _mod_spec(k, tm, rows_per_seq):
    if rows_per_seq >= tm:
        return pl.BlockSpec((None, 1, D_MODEL), lambda i, *_: (i * tm // rows_per_seq, 0, k))
    return pl.BlockSpec((tm // rows_per_seq, 1, D_MODEL), lambda i, *_: (i, 0, k))


def _mod_rows(ref, tm):
    m = ref[...]
    if m.ndim == 2:
        return m
    n_seq = m.shape[0]
    return jnp.broadcast_to(m, (n_seq, tm // n_seq, D_MODEL)).reshape(tm, D_MODEL)


ROW_CHUNK = 16


def _for_row_chunks(n_rows, body):
    def step(i, carry):
        body(pl.ds(pl.multiple_of(i * ROW_CHUNK, ROW_CHUNK), ROW_CHUNK), i)
        return carry
    lax.fori_loop(0, n_rows // ROW_CHUNK, step, 0, unroll=8)


def _mod_chunk(ref, i, tm):
    if len(ref.shape) == 2:
        return ref[...]
    per_seq = tm // ref.shape[0]
    n = ROW_CHUNK // per_seq
    m = ref[pl.ds(i * n, n)]
    return jnp.broadcast_to(m, (n, per_seq, D_MODEL)).reshape(ROW_CHUNK, D_MODEL)


ZB = 256
_N_RW_B, _N_MLP_B, _N_GATE_B = C_SHIFT // ZB, 2 * MLP_W // ZB, 2 * D_MODEL // ZB


def _w_in_block(zb):
    return jnp.where(zb < _N_GATE_B, zb + _N_RW_B + _N_MLP_B,
                     jnp.where(zb < _N_GATE_B + _N_MLP_B, zb - _N_GATE_B + _N_RW_B,
                               jnp.minimum(zb - _N_GATE_B - _N_MLP_B, _N_RW_B - 1)))


def _hnorm_kernel(x_ref, sc_ref, sh_ref, g_ref, h_ref):
    tm = x_ref.shape[0]

    def body(rows, i):
        h = _rms(x_ref[rows, :], g_ref[...]) * (1.0 + _mod_chunk(sc_ref, i, tm)) + _mod_chunk(sh_ref, i, tm)
        h_ref[rows, :] = h.astype(BF16)

    _for_row_chunks(tm, body)


def _hnorm(x, mod, g, rows_per_seq):
    m, tm = x.shape[0], (1024 if rows_per_seq >= 1024 else 512)
    mspec = lambda k: _mod_spec(k, tm, rows_per_seq)
    return pl.pallas_call(
        _hnorm_kernel,
        grid=(m // tm,),
        in_specs=[pl.BlockSpec((tm, D_MODEL), lambda i: (i, 0)),
                  mspec(MOD_SCALE_M), mspec(MOD_SHIFT_M),
                  pl.BlockSpec((1, D_MODEL), lambda i: (0, 0))],
        out_specs=pl.BlockSpec((tm, D_MODEL), lambda i: (i, 0)),
        out_shape=jax.ShapeDtypeStruct((m, D_MODEL), BF16),
        compiler_params=_params(("arbitrary",)),
        name="hnorm",
    )(x, mod, mod, g)


def _inproj_kernel(h_ref, wlo_ref, whi_ref, o_ref):
    h = h_ref[...]
    o_ref[:, :ZB] = _dot(h, wlo_ref[...].astype(BF16))
    o_ref[:, ZB:] = _dot(h, whi_ref[...].astype(BF16))


def _inproj(h, w):
    m = h.shape[0]
    tm = min(m, 2048)
    assert Z_TN == 2 * ZB and m % tm == 0
    return pl.pallas_call(
        _inproj_kernel,
        grid=(m // tm, Z_W // Z_TN),
        in_specs=[pl.BlockSpec((tm, D_MODEL), lambda i, j: (i, 0)),
                  pl.BlockSpec((D_MODEL, ZB), lambda i, j: (0, _w_in_block(2 * j))),
                  pl.BlockSpec((D_MODEL, ZB), lambda i, j: (0, _w_in_block(2 * j + 1)))],
        out_specs=pl.BlockSpec((tm, Z_TN), lambda i, j: (i, j)),
        out_shape=jax.ShapeDtypeStruct((m, Z_W), F32),
        compiler_params=_params(("arbitrary", "arbitrary")),
        name="inproj",
    )(h, w, w)


def _bd(w, mask):
    return jnp.where(mask, jnp.concatenate([w] * HPG, axis=0), 0.0).astype(BF16)


def _bd_heads(nat, mask):
    return jnp.where(mask, jnp.concatenate([nat] * HPG, axis=1), 0.0).astype(BF16)


def _diag_blocks(m):
    return jnp.concatenate([m[h * HEAD:(h + 1) * HEAD, h * HEAD:(h + 1) * HEAD] for h in range(HPG)],
                           axis=0)


def _wkv_kernel(*refs, lb, carried, nch, nseq):
    R = WKV_ROWS
    if carried:
        (zr_ref, zk_ref, zv_ref, zl_ref,
         mur_ref, muk_ref, muv_ref, mul_ref,
         w0_ref, a0_ref, kkp_ref, kap_ref, rkp_ref, gng_ref, gnb_ref,
         wd_ref, wa_ref, wg_ref,
         oa_ref, so_ref, sho_ref,
         s_ref, cr_ref, ck_ref, cv_ref, cl_ref) = refs
        carry = {id(zr_ref): cr_ref, id(zk_ref): ck_ref, id(zv_ref): cv_ref, id(zl_ref): cl_ref}
    else:
        (zr_ref, zk_ref, zv_ref, zl_ref,
         pr_ref, pk_ref, pv_ref, pl_ref, si_ref,
         mur_ref, muk_ref, muv_ref, mul_ref,
         w0_ref, a0_ref, kkp_ref, kap_ref, rkp_ref, gng_ref, gnb_ref,
         wd_ref, wa_ref, wg_ref,
         oa_ref, so_ref, sho_ref) = refs
        prev_of = {id(zr_ref): pr_ref, id(zk_ref): pk_ref, id(zv_ref): pv_ref, id(zl_ref): pl_ref}
    nblk = R // lb
    z_refs = (zr_ref, zk_ref, zv_ref, zl_ref)
    mu_refs = (mur_ref, muk_ref, muv_ref, mul_ref)

    if carried:
        @pl.when(pl.program_id(1) == 0)
        def _():
            s_ref[...] = jnp.zeros_like(s_ref)
            for ref in (cr_ref, ck_ref, cv_ref, cl_ref):
                ref[...] = jnp.zeros_like(ref)

    row1 = lax.broadcasted_iota(jnp.int32, (R, 1), 0)
    first = (row1 % lb) == 0
    ri = lax.broadcasted_iota(jnp.int32, (GRP, GRP), 0)
    ci = lax.broadcasted_iota(jnp.int32, (GRP, GRP), 1)
    bdm = (ri // HEAD) == (ci // HEAD)
    ones_bd = jnp.where(bdm, 1.0, 0.0).astype(BF16)
    tr = lax.broadcasted_iota(jnp.int32, (R, R), 0)
    tc = lax.broadcasted_iota(jnp.int32, (R, R), 1)
    same = (tr // lb) == (tc // lb)
    incl01 = jnp.where(same & (tc <= tr), 1.0, 0.0).astype(BF16)
    same01 = jnp.where(same, 1.0, 0.0).astype(BF16)
    wr = lax.broadcasted_iota(jnp.int32, (R, GRP), 0)
    wc = lax.broadcasted_iota(jnp.int32, (R, GRP), 1) % HEAD
    wsame = (wr // lb) == (wc // lb)
    strict_w = wsame & (wc < wr)
    incl_w = wsame & (wc <= wr)
    eye_w = jnp.where(wc == wr, 1.0, 0.0)
    sls = [slice(gi * GRP, (gi + 1) * GRP) for gi in range(N_GRP)]
    blks = [slice(b * lb, (b + 1) * lb) for b in range(nblk)]
    units = [(s, gi) for s in range(nseq) for gi in range(N_GRP)]
    uids = range(len(units))

    def seg_sums(xs):
        n = xs[0].shape[0]
        parts = []
        for x in xs:
            parts.extend(_split2(x))
        stacked = jnp.concatenate(parts, axis=0)
        cols = [_dot(stacked[:, sl], ones_bd) for sl in sls]
        full = jnp.concatenate(cols, axis=1)
        return [full[2 * i * n:(2 * i + 1) * n] + full[(2 * i + 1) * n:(2 * i + 2) * n]
                for i in range(len(xs))]

    def cum(m01, x):
        hi, mid, lo = _split3(x)
        return _dot(m01, hi) + _dot(m01, mid) + _dot(m01, lo)

    def prep(c, outs):
        rows = slice(c * R, (c + 1) * R)

        def shift(z_ref, mu_ref):
            parts = []
            for s in range(nseq):
                z = z_ref[s, rows, :]
                if not carried:
                    prev0 = prev_of[id(z_ref)][s, rows, :]
                elif c == 0:
                    prev0 = carry[id(z_ref)][s]
                else:
                    prev0 = z_ref[s, c * R - 1:c * R, :]
                prev = jnp.where(first, prev0, pltpu.roll(z, 1, axis=0))
                parts.append(z + (prev - z) * mu_ref[...])
            return jnp.concatenate(parts, axis=0)

        per_stream = lambda x: [x[s * R:(s + 1) * R] for s in range(nseq)]
        r, k, v, l = [shift(z_ref, mu_ref) for z_ref, mu_ref in zip(z_refs, mu_refs)]
        w_raw = w0_ref[...] + _dot(jnp.tanh(l).astype(BF16), wd_ref[...])
        a_raw = a0_ref[...] + _dot(l.astype(BF16), wa_ref[...])
        g = _dot(jax.nn.sigmoid(l).astype(BF16), wg_ref[...])
        yield
        logw = jax.nn.sigmoid(w_raw) * (-math.exp(-0.5))
        a = jax.nn.sigmoid(a_raw)
        kk = k * kkp_ref[...]
        k2 = k * (1.0 + (a - 1.0) * kap_ref[...])
        ss, rk = seg_sums([kk * kk, r * k2 * rkp_ref[...]])
        cl_s = [cum(incl01, lw) for lw in per_stream(logw)]
        if nblk == 1:
            ce_s = [jnp.broadcast_to(x[R - 1:R, :], (R, RW)) for x in cl_s]
        else:
            ce_s = [cum(same01, lw) for lw in per_stream(logw)]
        yield
        cl = jnp.concatenate(cl_s, axis=0)
        cl_end = jnp.concatenate(ce_s, axis=0)
        kk = kk / jnp.maximum(jnp.sqrt(ss), 1e-12)
        av = -kk
        bv = kk * a
        e_neg = jnp.exp(-cl)
        e_end = jnp.exp(cl_end - cl)
        full = dict(v=v, g=g, bonus=rk * v,
                    at=av * jnp.exp(cl - logw), rt=r * jnp.exp(cl),
                    bt=bv * e_neg, kt=k2 * e_neg, bh=bv * e_end, kh=k2 * e_end)
        for name, x in full.items():
            for s, x_s in enumerate(per_stream(x)):
                outs[s][name] = x_s
        for s in range(nseq):
            outs[s]["p_end"] = jnp.exp(cl_s[s][R - 1:R, :] if nblk == 1 else ce_s[s])

    def main(c, ps):
        rows_c = slice(c * R, (c + 1) * R)
        col = lambda name, ui: ps[units[ui][0]][name][:, sls[units[ui][1]]]
        v_g = [col("v", ui) for ui in uids]
        lhs = [jnp.concatenate([col("at", ui), col("rt", ui)], axis=0).astype(BF16) for ui in uids]
        o1 = [_dot_nt(lhs[ui], jnp.concatenate([_bd(col("bt", ui), bdm), _bd(col("kt", ui), bdm)], axis=0))
              for ui in uids]
        yield
        w_ab = [jnp.where(strict_w, o[:R, :GRP], 0.0) for o in o1]
        w_ak = [jnp.where(strict_w, o[:R, GRP:], 0.0) for o in o1]
        w_rb = [jnp.where(incl_w, o[R:, :GRP], 0.0) for o in o1]
        w_rk = [jnp.where(incl_w, o[R:, GRP:], 0.0) for o in o1]

        def state_part(ui):
            s, gi = units[ui]
            if carried:
                o2 = _dot_nt(lhs[ui], s_ref[s, gi].astype(BF16))
                return o2[:R], o2[R:]
            at_u, rt_u = col("at", ui), col("rt", ui)
            a_rows, r_rows = [], []
            for b, rows in enumerate(blks):
                lhs_b = jnp.concatenate([at_u[rows], rt_u[rows]], axis=0).astype(BF16)
                o2 = _dot_nt(lhs_b, _bd_heads(si_ref[s, c * nblk + b, sls[gi], :], bdm))
                a_rows.append(o2[:lb])
                r_rows.append(o2[lb:])
            return jnp.concatenate(a_rows, axis=0), jnp.concatenate(r_rows, axis=0)

        x_w = list(w_ab)
        t_w = [eye_w + x for x in x_w]
        n_lvl = max(1, int(math.log2(lb)))
        a_s = r_s = o3 = None
        for j in range(n_lvl):
            for ui in uids:
                y_bd = _bd(x_w[ui], bdm)
                if j == 0:
                    x_w[ui] = _dot(x_w[ui].astype(BF16), y_bd)
                elif j == n_lvl - 1:
                    t_w[ui] = t_w[ui] + _dot(t_w[ui].astype(BF16), y_bd)
                else:
                    res = _dot(jnp.concatenate([t_w[ui], x_w[ui]], axis=0).astype(BF16), y_bd)
                    t_w[ui] = t_w[ui] + res[:R]
                    x_w[ui] = res[R:]
            yield
            if j == 0:
                parts = [state_part(ui) for ui in uids]
                a_s, r_s = [q[0] for q in parts], [q[1] for q in parts]
                yield
            elif j == 1:
                o3 = [_dot(jnp.concatenate([w_ak[ui], w_rk[ui]], axis=0).astype(BF16), _bd(v_g[ui], bdm))
                      for ui in uids]
                yield

        u = [_dot(t_w[ui].astype(BF16), _bd(a_s[ui] + o3[ui][:R], bdm)) for ui in uids]
        yield
        o_parts = [r_s[ui] + o3[ui][R:] + _dot(w_rb[ui].astype(BF16), _bd(u[ui], bdm)) for ui in uids]
        yield

        for ui in uids:
            s, gi = units[ui]
            sl = sls[gi]
            bh_u, kh_u, p_end = col("bh", ui), col("kh", ui), ps[s]["p_end"][:, sl]
            if carried:
                upd = _dot_tn(jnp.concatenate([u[ui], v_g[ui]], axis=0).astype(BF16),
                              jnp.concatenate([bh_u, kh_u], axis=0).astype(BF16))
                s_new = jnp.where(bdm, s_ref[s, gi] * p_end + upd, 0.0)
                s_ref[s, gi] = s_new
                if c == nch - 1:
                    so_ref[s, sl, :] = _diag_blocks(s_new)
            else:
                for b, rows in enumerate(blks):
                    upd = _dot_tn(jnp.concatenate([u[ui][rows], v_g[ui][rows]], axis=0).astype(BF16),
                                  jnp.concatenate([bh_u[rows], kh_u[rows]], axis=0).astype(BF16))
                    p_b = p_end[b * lb:b * lb + 1]
                    p_nat = jnp.concatenate(
                        [jnp.broadcast_to(p_b[:, h * HEAD:(h + 1) * HEAD], (HEAD, HEAD)) for h in range(HPG)],
                        axis=0)
                    so_ref[s, c * nblk + b, sl, :] = (si_ref[s, c * nblk + b, sl, :] * p_nat
                                                      + _diag_blocks(upd))
        yield

        o = [jnp.concatenate(o_parts[s * N_GRP:(s + 1) * N_GRP], axis=1) for s in range(nseq)]
        mu = seg_sums(o)
        yield
        dlt = [o[s] - mu[s] * (1.0 / HEAD) for s in range(nseq)]
        var = seg_sums([d * d for d in dlt])
        for s in range(nseq):
            on = dlt[s] * lax.rsqrt(var[s] * (1.0 / HEAD) + GN_EPS) * gng_ref[...] + gnb_ref[...]
            oa_ref[s, rows_c, :] = ((on + ps[s]["bonus"]) * ps[s]["g"]).astype(BF16)

    def run(gen):
        for _ in gen:
            pass

    cur = [{} for _ in range(nseq)]
    run(prep(0, cur))
    for c in range(nch):
        nxt = [{} for _ in range(nseq)]
        side = prep(c + 1, nxt) if c + 1 < nch else iter(())
        for tick, _ in enumerate(main(c, cur)):
            if tick % 3 == 1:
                next(side, None)
        run(side)
        cur = nxt

    n_rows = nch * R
    for s in range(nseq):
        if carried:
            last = lambda z_ref: z_ref[s, n_rows - 1:n_rows, :]
            for z_ref in z_refs:
                carry[id(z_ref)][s] = last(z_ref)
        else:
            last = lambda z_ref: jnp.concatenate(
                [z_ref[s, b * lb + lb - 1:b * lb + lb, :] for b in range(nch * nblk)], axis=0)
        sho_ref[s, :, 0:RW] = last(zr_ref)
        sho_ref[s, :, RW:2 * RW] = last(zk_ref)
        sho_ref[s, :, 2 * RW:3 * RW] = last(zv_ref)
        sho_ref[s, :, 3 * RW:] = last(zl_ref)


def _wkv(z, vecs, mats, n_seq, seq_len, lb, nch, nseq, prev_rows=None, state_in=None):
    R = nch * WKV_ROWS
    carried = prev_rows is None
    rows = z.shape[0]
    n_streams = n_seq if carried else nseq
    stream_rows = rows // n_streams
    assert n_streams % nseq == 0 and stream_rows % R == 0 and rows == n_streams * stream_rows
    z3 = z.reshape(n_streams, stream_rows, Z_W)
    if carried:
        grid = (n_streams // nseq, stream_rows // R)
        rmap = lambda c: (lambda b, t: (b, t, c))
        cmap = lambda b, t: (0, 0)
        smap = lambda b, t: (b, 0, 0)
        sem = ("arbitrary", "arbitrary")
    else:
        grid = (stream_rows // R,)
        rmap = lambda c: (lambda i: (0, i, c))
        cmap = lambda i: (0, 0)
        sem = ("arbitrary",)
        blk_per_step = R // lb
        seq_per_stream = n_seq // nseq

    in_specs = [pl.BlockSpec((nseq, R, RW), rmap(Z_R // RW)),
                pl.BlockSpec((nseq, R, RW), rmap(Z_K // RW)),
                pl.BlockSpec((nseq, R, RW), rmap(Z_VR // RW)),
                pl.BlockSpec((nseq, R, LORA), rmap(Z_L // LORA))]
    args = [z3, z3, z3, z3]
    if not carried:
        p3 = prev_rows.reshape(nseq, stream_rows, C_SHIFT)
        st4 = state_in.reshape(nseq, seq_per_stream, RW, HEAD)
        in_specs += [pl.BlockSpec((nseq, R, RW), rmap(0)), pl.BlockSpec((nseq, R, RW), rmap(1)),
                     pl.BlockSpec((nseq, R, RW), rmap(2)), pl.BlockSpec((nseq, R, LORA), rmap(3 * RW // LORA)),
                     pl.BlockSpec((nseq, blk_per_step, RW, HEAD), lambda i: (0, i, 0, 0))]
        args += [p3] * 4 + [st4]
    for a in vecs + mats:
        in_specs.append(pl.BlockSpec(a.shape, cmap))
        args.append(a)

    if carried:
        scratch = [pltpu.VMEM((nseq, N_GRP, GRP, GRP), F32), pltpu.VMEM((nseq, 1, RW), F32),
                   pltpu.VMEM((nseq, 1, RW), F32), pltpu.VMEM((nseq, 1, RW), F32),
                   pltpu.VMEM((nseq, 1, LORA), F32)]
        state_spec = pl.BlockSpec((nseq, RW, HEAD), smap)
        state_shape = (n_seq, RW, HEAD)
        shift_spec = pl.BlockSpec((nseq, 1, C_SHIFT), smap)
        shift_shape = (n_seq, 1, C_SHIFT)
    else:
        scratch = []
        state_spec = pl.BlockSpec((nseq, blk_per_step, RW, HEAD), lambda i: (0, i, 0, 0))
        state_shape = (nseq, seq_per_stream, RW, HEAD)
        shift_spec = pl.BlockSpec((nseq, blk_per_step, C_SHIFT), lambda i: (0, i, 0))
        shift_shape = (nseq, seq_per_stream, C_SHIFT)
    oa, st, sh = pl.pallas_call(
        functools.partial(_wkv_kernel, lb=lb, carried=carried, nch=nch, nseq=nseq),
        grid=grid,
        in_specs=in_specs,
        out_specs=[pl.BlockSpec((nseq, R, RW), rmap(0)), state_spec, shift_spec],
        out_shape=[jax.ShapeDtypeStruct((n_streams, stream_rows, RW), BF16),
                   jax.ShapeDtypeStruct(state_shape, F32),
                   jax.ShapeDtypeStruct(shift_shape, F32)],
        scratch_shapes=scratch,
        compiler_params=_params(sem),
        name="wkv_carried" if carried else "wkv_blocks",
    )(*args)
    return oa.reshape(rows, RW), st.reshape(n_seq, RW, HEAD), sh.reshape(n_seq, C_SHIFT)


def _gmlp_kernel(*refs, seq, want_vn, n_cast):
    zu_ref, zv_ref, lng_ref, lnb_ref, ws_ref, bias_ref = refs[:6]
    cast_in = refs[6:6 + n_cast]
    ub_ref = refs[6 + n_cast]
    vn_ref = refs[7 + n_cast] if want_vn else None
    cast_out = refs[len(refs) - n_cast:]
    for src, dst in zip(cast_in, cast_out):
        dst[...] = src[...].astype(BF16)
    reps = CHUNK // seq
    tr = lax.broadcasted_iota(jnp.int32, (CHUNK, CHUNK), 0)
    tc = lax.broadcasted_iota(jnp.int32, (CHUNK, CHUNK), 1)
    causal = (tc <= tr) & ((tr // seq) == (tc // seq))
    ws_g = []
    for gi in range(MLP_GROUPS):
        w_rows = ws_ref[gi]
        w_full = jnp.concatenate([w_rows if s == 0 else pltpu.roll(w_rows, s * seq, axis=1)
                                  for s in range(reps)], axis=0)
        ws_g.append(jnp.where(causal, w_full, 0.0).astype(BF16))
    bias = jnp.concatenate([bias_ref[...]] * reps, axis=0)
    for c in range(zu_ref.shape[0] // CHUNK):
        rows = slice(c * CHUNK, (c + 1) * CHUNK)
        u = jax.nn.gelu(zu_ref[rows, :])
        vg = jax.nn.gelu(zv_ref[rows, :])
        mu = jnp.mean(vg, -1, keepdims=True)
        var = jnp.mean(jnp.square(vg - mu), -1, keepdims=True)
        vn = (vg - mu) * lax.rsqrt(var + LN_EPS) * lng_ref[...] + lnb_ref[...]
        if vn_ref is not None:
            vn_ref[rows, :] = vn
        vb = vn.astype(BF16)
        cols = [_dot(ws_g[gi], vb[:, gi * MLP_GD:(gi + 1) * MLP_GD]) for gi in range(MLP_GROUPS)]
        mixed = jnp.concatenate(cols, axis=1) + bias
        ub_ref[rows, :] = (u * mixed).astype(BF16)


def _gmlp(z, ln_g, ln_b, ws, bias, seq, want_vn, to_bf16=()):
    rows, tr = z.shape[0], 4 * CHUNK
    steps = rows // tr
    n_out = 2 if want_vn else 1
    cast_specs = [pl.BlockSpec((w.shape[0] // steps, w.shape[1]), lambda i: (i, 0)) for w in to_bf16]
    assert all(w.shape[0] % (16 * steps) == 0 for w in to_bf16)
    return pl.pallas_call(
        functools.partial(_gmlp_kernel, seq=seq, want_vn=want_vn, n_cast=len(to_bf16)),
        grid=(steps,),
        in_specs=[pl.BlockSpec((tr, MLP_W), lambda i: (i, Z_U // MLP_W)),
                  pl.BlockSpec((tr, MLP_W), lambda i: (i, Z_V // MLP_W)),
                  pl.BlockSpec((1, MLP_W), lambda i: (0, 0)),
                  pl.BlockSpec((1, MLP_W), lambda i: (0, 0)),
                  pl.BlockSpec((MLP_GROUPS, seq, CHUNK), lambda i: (0, 0, 0)),
                  pl.BlockSpec((seq, MLP_W), lambda i: (0, 0))] + cast_specs,
        out_specs=[pl.BlockSpec((tr, MLP_W), lambda i: (i, 0)),
                   pl.BlockSpec((tr, MLP_W), lambda i: (i, 0))][:n_out] + cast_specs,
        out_shape=[jax.ShapeDtypeStruct((rows, MLP_W), BF16),
                   jax.ShapeDtypeStruct((rows, MLP_W), F32)][:n_out]
                  + [jax.ShapeDtypeStruct(w.shape, BF16) for w in to_bf16],
        compiler_params=_params(("arbitrary",)),
        name="gmlp",
    )(z, z, ln_g, ln_b, ws, bias, *to_bf16)


def _merge_kernel(oa_ref, ub_ref, zga_ref, zgb_ref, x_ref, gm_ref, sc_ref, sh_ref, g_ref,
                  wa_ref, wb_ref, wo_ref, x1_ref, h2_ref):
    ya = _dot(oa_ref[...], wa_ref[...])
    yb = _dot(ub_ref[...], wb_ref[...])
    merged = jax.nn.sigmoid(zga_ref[...]) * ya + jax.nn.sigmoid(zgb_ref[...]) * yb
    mix = _dot(merged.astype(BF16), wo_ref[...])
    tm = x_ref.shape[0]
    x1 = x_ref[...] + _mod_rows(gm_ref, tm) * mix
    x1_ref[...] = x1
    h2_ref[...] = (_rms(x1, g_ref[...]) * (1.0 + _mod_rows(sc_ref, tm)) + _mod_rows(sh_ref, tm)).astype(BF16)


def _merge(oa, ub, z, x, mod, g_ffn, wa, wb, wo, rows_per_seq):
    m, tm = x.shape[0], (256 if rows_per_seq >= 256 else 128)
    const = lambda shape: pl.BlockSpec(shape, lambda i: (0, 0), pipeline_mode=pl.Buffered(1))
    mspec = lambda k: _mod_spec(k, tm, rows_per_seq)
    rows = lambda w, c: pl.BlockSpec((tm, w), lambda i: (i, c))
    return pl.pallas_call(
        _merge_kernel,
        grid=(m // tm,),
        in_specs=[rows(RW, 0), rows(MLP_W, 0),
                  rows(D_MODEL, Z_GA // D_MODEL), rows(D_MODEL, Z_GB // D_MODEL), rows(D_MODEL, 0),
                  mspec(MOD_GATE_M), mspec(MOD_SCALE_F), mspec(MOD_SHIFT_F),
                  pl.BlockSpec((1, D_MODEL), lambda i: (0, 0)),
                  const((RW, D_MODEL)), const((MLP_W, D_MODEL)), const((D_MODEL, D_MODEL))],
        out_specs=[rows(D_MODEL, 0), rows(D_MODEL, 0)],
        out_shape=[jax.ShapeDtypeStruct((m, D_MODEL), F32), jax.ShapeDtypeStruct((m, D_MODEL), BF16)],
        compiler_params=_params(("arbitrary",)),
        name="merge",
    )(oa, ub, z, z, x, mod, mod, mod, g_ffn, wa, wb, wo)


def _ffn_kernel(*refs, fused):
    if fused:
        h_ref, x_ref, gf_ref, gfin_ref, wg_ref, wu_ref, wo_ref, y_ref = refs
    else:
        h_ref, wg_ref, wu_ref, wo_ref, y_ref = refs
    j = pl.program_id(1)

    @pl.when(j == 0)
    def _():
        y_ref[...] = jnp.zeros_like(y_ref)

    hb = h_ref[...]
    gt = _dot(hb, wg_ref[...].astype(BF16))
    up = _dot(hb, wu_ref[...].astype(BF16))
    act = gt * jax.nn.sigmoid(gt) * up
    y_ref[...] += _dot(act.astype(BF16), wo_ref[...].astype(BF16))

    if fused:
        @pl.when(j == pl.num_programs(1) - 1)
        def _():
            x2 = x_ref[...] + gf_ref[...] * y_ref[...]
            y_ref[...] = _rms(x2, gfin_ref[...])


def _final_kernel(f_ref, x_ref, gf_ref, gfin_ref, y_ref):
    tm = x_ref.shape[0]

    def body(rows, i):
        x2 = x_ref[rows, :] + _mod_chunk(gf_ref, i, tm) * f_ref[rows, :]
        y_ref[rows, :] = _rms(x2, gfin_ref[...])

    _for_row_chunks(tm, body)


def _ffn(h2, x1, mod, g_final, w_in, w_out, rows_per_seq):
    m, tm, tf = x1.shape[0], 1024, 256
    nf = D_FF // tf
    fused = rows_per_seq >= tm
    rows = pl.BlockSpec((tm, D_MODEL), lambda i, j: (i, 0))
    vec = pl.BlockSpec((1, D_MODEL), lambda i, j: (0, 0))
    wspecs = [pl.BlockSpec((D_MODEL, tf), lambda i, j: (0, j)),
              pl.BlockSpec((D_MODEL, tf), lambda i, j: (0, nf + j)),
              pl.BlockSpec((tf, D_MODEL), lambda i, j: (j, 0))]
    if fused:
        in_specs = [rows, pl.BlockSpec((tm, D_MODEL), lambda i, j: (i, 0), pipeline_mode=pl.Buffered(1)),
                    _mod_spec(MOD_GATE_F, tm, rows_per_seq), vec] + wspecs
        args = (h2, x1, mod, g_final, w_in, w_in, w_out)
    else:
        in_specs = [rows] + wspecs
        args = (h2, w_in, w_in, w_out)
    y = pl.pallas_call(
        functools.partial(_ffn_kernel, fused=fused),
        grid=(m // tm, nf),
        in_specs=in_specs,
        out_specs=rows,
        out_shape=jax.ShapeDtypeStruct((m, D_MODEL), F32),
        compiler_params=_params(("arbitrary", "arbitrary")),
        name="ffn",
    )(*args)
    if fused:
        return y
    te = 256
    erows = pl.BlockSpec((te, D_MODEL), lambda i: (i, 0))
    return pl.pallas_call(
        _final_kernel,
        grid=(m // te,),
        in_specs=[erows, erows, _mod_spec(MOD_GATE_F, te, rows_per_seq),
                  pl.BlockSpec((1, D_MODEL), lambda i: (0, 0))],
        out_specs=erows,
        out_shape=jax.ShapeDtypeStruct((m, D_MODEL), F32),
        compiler_params=_params(("arbitrary",)),
        name="final",
    )(y, x1, mod, g_final)


def kernel(x_prompt, x_sample, state_wkv, state_shift, c_prompt, c_sample, w_ada, b_ada, norm_mix_g, w_in, mu_shift, w0, w_decay_up, a0, w_aaa_up, w_gate_up, k_k, k_a, r_k, gn_g, gn_b, ln_v_g, ln_v_b, w_spatial, b_spatial, w_branch_a, w_branch_b, w_out, norm_ffn_g, w_ffn_in, w_ffn_out, norm_final_g):
    assert w_ada.shape[0] == 1, "single layer"
    bp, tp, _ = x_prompt.shape
    bs, ts, _ = x_sample.shape
    assert tp % CHUNK == 0 and WKV_ROWS % ts == 0 and CHUNK % ts == 0

    c_all = jnp.concatenate([c_sample, c_prompt], axis=0)
    c_all = jnp.pad(c_all, ((0, -c_all.shape[0] % 8), (0, 0)))
    mod = _ada(c_all, w_ada[0], b_ada[0])
    mod_s = mod[:bs].reshape(bs, 1, 6 * D_MODEL)
    mod_p = mod[bs:bs + bp].reshape(bp, 1, 6 * D_MODEL)

    wfi, wfo = w_ffn_in[0], w_ffn_out[0]

    row = lambda a: a.reshape(1, -1)
    mu = mu_shift[0]
    vecs = [row(mu[:RW]), row(mu[RW:2 * RW]), row(mu[2 * RW:3 * RW]), row(mu[3 * RW:]),
            row(w0[0]), row(a0[0]), row(k_k[0]), row(k_a[0]), row(r_k[0]), row(gn_g[0]), row(gn_b[0])]
    zpad = lambda w, lo: jnp.pad(w, ((lo, LORA - lo - w.shape[0]), (0, 0))).astype(BF16)
    mats = [zpad(w_decay_up[0], 0), zpad(w_aaa_up[0], DECAY_LORA),
            zpad(w_gate_up[0], DECAY_LORA + AAA_LORA)]

    bias = jnp.repeat(b_spatial[0].T, MLP_GD, axis=1)
    ws = w_spatial[0]

    xp = x_prompt.reshape(bp * tp, D_MODEL)
    xs = x_sample.reshape(bs * ts, D_MODEL)
    g_mix, g_ffn, g_fin = row(norm_mix_g[0]), row(norm_ffn_g[0]), row(norm_final_g)
    lng, lnb = row(ln_v_g[0]), row(ln_v_b[0])

    zp = _inproj(_hnorm(xp, mod_p, g_mix, tp), w_in[0])
    oa_p, st_p, shift_p = _wkv(zp, vecs, mats, bp, tp, WKV_ROWS, WKV_NCH, WKV_NSEQ_PROMPT)
    ub_p, wa_b, wb_b, wo_b = _gmlp(zp, lng, lnb, ws, bias, CHUNK, False,
                                   to_bf16=(w_branch_a[0], w_branch_b[0], w_out[0]))
    x1_p, h2_p = _merge(oa_p, ub_p, zp, xp, mod_p, g_ffn, wa_b, wb_b, wo_b, tp)
    y_p = _ffn(h2_p, x1_p, mod_p, g_fin, wfi, wfo, tp)

    zs = _inproj(_hnorm(xs, mod_s, g_mix, ts), w_in[0])
    prev_rows = jnp.repeat(state_shift[0], ts, axis=0)
    oa_s, st_s, shift_s = _wkv(zs, vecs, mats, bs, ts, ts, 1, WKV_NSEQ, prev_rows=prev_rows,
                               state_in=state_wkv[0].reshape(bs, RW, HEAD))
    ub_s, vn_s = _gmlp(zs, lng, lnb, ws, bias, ts, True)
    x1_s, h2_s = _merge(oa_s, ub_s, zs, xs, mod_s, g_ffn, wa_b, wb_b, wo_b, ts)
    y_s = _ffn(h2_s, x1_s, mod_s, g_fin, wfi, wfo, ts)

    return (y_p.reshape(bp, tp, D_MODEL),
            y_s.reshape(bs, ts, D_MODEL),
            st_p.reshape(1, bp, N_HEADS, HEAD, HEAD),
            shift_p.reshape(1, bp, C_SHIFT),
            st_s.reshape(1, bs, N_HEADS, HEAD, HEAD),
            shift_s.reshape(1, bs, C_SHIFT),
            vn_s.reshape(bs, ts, MLP_W)[None])
```

```python
import functools
import math

import jax
import jax.numpy as jnp
from jax import lax
from jax.experimental import pallas as pl
from jax.experimental.pallas import tpu as pltpu

F32 = jnp.float32
BF16 = jnp.bfloat16

D_MODEL = 2048
HEAD = 64
RW = D_MODEL // 2
N_HEADS = RW // HEAD
DECAY_LORA = D_MODEL // 32
AAA_LORA = D_MODEL // 32
GATE_LORA = D_MODEL // 16
LORA = DECAY_LORA + AAA_LORA + GATE_LORA
CHUNK = 128
MLP_W = D_MODEL // 2
MLP_GROUPS = 8
MLP_GD = MLP_W // MLP_GROUPS
D_FF = ((-(-8 * D_MODEL // 3)) + 255) // 256 * 256
C_SHIFT = 3 * RW + LORA
C_IN = C_SHIFT + 2 * MLP_W + 2 * D_MODEL
NORM_EPS = 1e-6
GN_EPS = 64e-5
LN_EPS = 1e-5

Z_GA, Z_GB, Z_U, Z_V, Z_R, Z_K, Z_VR, Z_L = 0, 2048, 4096, 5120, 6144, 7168, 8192, 9216
Z_TN = 512
Z_W = -(-C_IN // Z_TN) * Z_TN

GRP = 256
HPG = GRP // HEAD
N_GRP = RW // GRP
WKV_ROWS = 64
WKV_NCH = 2
WKV_NSEQ_PROMPT = 2
WKV_NSEQ = 2
VMEM_LIMIT = 56 * 1024 * 1024


def _dot(a, b):
    return jnp.dot(a, b, preferred_element_type=F32)


def _dot_nt(a, b):
    return lax.dot_general(a, b, (((1,), (1,)), ((), ())), preferred_element_type=F32)


def _dot_tn(a, b):
    return lax.dot_general(a, b, (((0,), (0,)), ((), ())), preferred_element_type=F32)


def _rms(x, g):
    return x * lax.rsqrt(jnp.mean(x * x, -1, keepdims=True) + NORM_EPS) * g


def _split2(x):
    hi = x.astype(BF16)
    lo = (x - hi.astype(F32)).astype(BF16)
    return hi, lo


def _split3(x):
    hi = x.astype(BF16)
    r1 = x - hi.astype(F32)
    mid = r1.astype(BF16)
    lo = (r1 - mid.astype(F32)).astype(BF16)
    return hi, mid, lo


def _params(sem):
    return pltpu.CompilerParams(dimension_semantics=sem, vmem_limit_bytes=VMEM_LIMIT)


def _ada_kernel(c_ref, w_ref, b_ref, o_ref, s_ref):
    @pl.when(pl.program_id(0) == 0)
    def _():
        c = c_ref[...]
        s_ref[...] = (c * jax.nn.sigmoid(c)).astype(BF16)

    o_ref[...] = _dot(s_ref[...], w_ref[...].astype(BF16)) + b_ref[...]


def _ada(c, w_ada, b_ada):
    m, n, tn = c.shape[0], w_ada.shape[1], 1024
    return pl.pallas_call(
        _ada_kernel,
        grid=(n // tn,),
        in_specs=[pl.BlockSpec((m, D_MODEL), lambda j: (0, 0)),
                  pl.BlockSpec((D_MODEL, tn), lambda j: (0, j)),
                  pl.BlockSpec((1, tn), lambda j: (0, j))],
        out_specs=pl.BlockSpec((m, tn), lambda j: (0, j)),
        out_shape=jax.ShapeDtypeStruct((m, n), F32),
        scratch_shapes=[pltpu.VMEM((m, D_MODEL), BF16)],
        compiler_params=_params(("arbitrary",)),
        name="ada",
    )(c, w_ada, b_ada.reshape(1, n))


MOD_SHIFT_M, MOD_SCALE_M, MOD_GATE_M, MOD_SHIFT_F, MOD_SCALE_F, MOD_GATE_F = range(6)


def _mod_spec(k, tm, rows_per_seq):
    if rows_per_seq >= tm:
        return pl.BlockSpec((None, 1, D_MODEL), lambda i, *_: (i * tm // rows_per_seq, 0, k))
    return pl.BlockSpec((tm // rows_per_seq, D_MODEL), lambda i, *_: (i, k))


def _seq_rows(ref, first, n_seq, per_seq):
    return jnp.concatenate([jnp.broadcast_to(ref[pl.ds(first + s, 1), :], (per_seq, ref.shape[1]))
                            for s in range(n_seq)], axis=0)


def _mod_rows(ref, tm):
    n_seq = ref.shape[0]
    return ref[...] if n_seq == 1 else _seq_rows(ref, 0, n_seq, tm // n_seq)


ROW_CHUNK = 16


def _for_row_chunks(n_rows, body):
    def step(i, carry):
        body(pl.ds(pl.multiple_of(i * ROW_CHUNK, ROW_CHUNK), ROW_CHUNK), i)
        return carry
    lax.fori_loop(0, n_rows // ROW_CHUNK, step, 0, unroll=8)


def _mod_chunk(ref, i, tm):
    if ref.shape[0] == 1:
        return ref[...]
    per_seq = tm // ref.shape[0]
    n = ROW_CHUNK // per_seq
    return _seq_rows(ref, i * n, n, per_seq)


ZB = 256
_N_RW_B, _N_MLP_B, _N_GATE_B = C_SHIFT // ZB, 2 * MLP_W // ZB, 2 * D_MODEL // ZB


def _w_in_block(zb):
    return jnp.where(zb < _N_GATE_B, zb + _N_RW_B + _N_MLP_B,
                     jnp.where(zb < _N_GATE_B + _N_MLP_B, zb - _N_GATE_B + _N_RW_B,
                               jnp.minimum(zb - _N_GATE_B - _N_MLP_B, _N_RW_B - 1)))


def _hnorm_kernel(x_ref, sc_ref, sh_ref, g_ref, h_ref):
    tm = x_ref.shape[0]

    def body(rows, i):
        h = _rms(x_ref[rows, :], g_ref[...]) * (1.0 + _mod_chunk(sc_ref, i, tm)) + _mod_chunk(sh_ref, i, tm)
        h_ref[rows, :] = h.astype(BF16)

    _for_row_chunks(tm, body)


def _hnorm(x, mod, g, rows_per_seq):
    m, tm = x.shape[0], 1024
    mspec = lambda k: _mod_spec(k, tm, rows_per_seq)
    return pl.pallas_call(
        _hnorm_kernel,
        grid=(m // tm,),
        in_specs=[pl.BlockSpec((tm, D_MODEL), lambda i: (i, 0)),
                  mspec(MOD_SCALE_M), mspec(MOD_SHIFT_M),
                  pl.BlockSpec((1, D_MODEL), lambda i: (0, 0))],
        out_specs=pl.BlockSpec((tm, D_MODEL), lambda i: (i, 0)),
        out_shape=jax.ShapeDtypeStruct((m, D_MODEL), BF16),
        compiler_params=_params(("arbitrary",)),
        name="hnorm",
    )(x, mod, mod, g)


def _inproj_kernel(h_ref, wlo_ref, whi_ref, o_ref):
    h = h_ref[...]
    o_ref[:, :ZB] = _dot(h, wlo_ref[...].astype(BF16))
    o_ref[:, ZB:] = _dot(h, whi_ref[...].astype(BF16))


def _inproj(h, w):
    m = h.shape[0]
    tm = min(m, 2048)
    assert Z_TN == 2 * ZB and m % tm == 0
    return pl.pallas_call(
        _inproj_kernel,
        grid=(m // tm, Z_W // Z_TN),
        in_specs=[pl.BlockSpec((tm, D_MODEL), lambda i, j: (i, 0)),
                  pl.BlockSpec((D_MODEL, ZB), lambda i, j: (0, _w_in_block(2 * j))),
                  pl.BlockSpec((D_MODEL, ZB), lambda i, j: (0, _w_in_block(2 * j + 1)))],
        out_specs=pl.BlockSpec((tm, Z_TN), lambda i, j: (i, j)),
        out_shape=jax.ShapeDtypeStruct((m, Z_W), F32),
        compiler_params=_params(("arbitrary", "arbitrary")),
        name="inproj",
    )(h, w, w)


def _bd(w, mask):
    return jnp.where(mask, jnp.concatenate([w] * HPG, axis=0), 0.0).astype(BF16)


def _bd_heads(nat, mask):
    return jnp.where(mask, jnp.concatenate([nat] * HPG, axis=1), 0.0).astype(BF16)


def _diag_blocks(m):
    return jnp.concatenate([m[h * HEAD:(h + 1) * HEAD, h * HEAD:(h + 1) * HEAD] for h in range(HPG)],
                           axis=0)


def _wkv_kernel(*refs, lb, carried, nch, nseq):
    R = WKV_ROWS
    if carried:
        (zr_ref, zk_ref, zv_ref, zl_ref,
         mur_ref, muk_ref, muv_ref, mul_ref,
         w0_ref, a0_ref, kkp_ref, kap_ref, rkp_ref, gng_ref, gnb_ref,
         wd_ref, wa_ref, wg_ref,
         oa_ref, so_ref, sho_ref,
         s_ref, cr_ref, ck_ref, cv_ref, cl_ref) = refs
        carry = {id(zr_ref): cr_ref, id(zk_ref): ck_ref, id(zv_ref): cv_ref, id(zl_ref): cl_ref}
    else:
        (zr_ref, zk_ref, zv_ref, zl_ref,
         pr_ref, pk_ref, pv_ref, pl_ref, si_ref,
         mur_ref, muk_ref, muv_ref, mul_ref,
         w0_ref, a0_ref, kkp_ref, kap_ref, rkp_ref, gng_ref, gnb_ref,
         wd_ref, wa_ref, wg_ref,
         oa_ref, so_ref, sho_ref) = refs
        prev_of = {id(zr_ref): pr_ref, id(zk_ref): pk_ref, id(zv_ref): pv_ref, id(zl_ref): pl_ref}
    nblk = R // lb
    z_refs = (zr_ref, zk_ref, zv_ref, zl_ref)
    mu_refs = (mur_ref, muk_ref, muv_ref, mul_ref)

    if carried:
        @pl.when(pl.program_id(1) == 0)
        def _():
            s_ref[...] = jnp.zeros_like(s_ref)
            for ref in (cr_ref, ck_ref, cv_ref, cl_ref):
                ref[...] = jnp.zeros_like(ref)

    row1 = lax.broadcasted_iota(jnp.int32, (R, 1), 0)
    first = (row1 % lb) == 0
    ri = lax.broadcasted_iota(jnp.int32, (GRP, GRP), 0)
    ci = lax.broadcasted_iota(jnp.int32, (GRP, GRP), 1)
    bdm = (ri // HEAD) == (ci // HEAD)
    ones_bd = jnp.where(bdm, 1.0, 0.0).astype(BF16)
    tr = lax.broadcasted_iota(jnp.int32, (R, R), 0)
    tc = lax.broadcasted_iota(jnp.int32, (R, R), 1)
    same = (tr // lb) == (tc // lb)
    incl01 = jnp.where(same & (tc <= tr), 1.0, 0.0).astype(BF16)
    same01 = jnp.where(same, 1.0, 0.0).astype(BF16)
    wr = lax.broadcasted_iota(jnp.int32, (R, GRP), 0)
    wc = lax.broadcasted_iota(jnp.int32, (R, GRP), 1) % HEAD
    wsame = (wr // lb) == (wc // lb)
    strict_w = wsame & (wc < wr)
    incl_w = wsame & (wc <= wr)
    eye_w = jnp.where(wc == wr, 1.0, 0.0)
    sls = [slice(gi * GRP, (gi + 1) * GRP) for gi in range(N_GRP)]
    blks = [slice(b * lb, (b + 1) * lb) for b in range(nblk)]
    units = [(s, gi) for s in range(nseq) for gi in range(N_GRP)]
    uids = range(len(units))

    def seg_sums(xs):
        n = xs[0].shape[0]
        parts = []
        for x in xs:
            parts.extend(_split2(x))
        stacked = jnp.concatenate(parts, axis=0)
        cols = [_dot(stacked[:, sl], ones_bd) for sl in sls]
        full = jnp.concatenate(cols, axis=1)
        return [full[2 * i * n:(2 * i + 1) * n] + full[(2 * i + 1) * n:(2 * i + 2) * n]
                for i in range(len(xs))]

    def cum(m01, x):
        hi, mid, lo = _split3(x)
        return _dot(m01, hi) + _dot(m01, mid) + _dot(m01, lo)

    def prep(c, outs):
        rows = slice(c * R, (c + 1) * R)

        def shift(z_ref, mu_ref):
            parts = []
            for s in range(nseq):
                z = z_ref[s, rows, :]
                if not carried:
                    prev0 = _seq_rows(prev_of[id(z_ref)].at[s], c * nblk, nblk, lb)
                elif c == 0:
                    prev0 = carry[id(z_ref)][s]
                else:
                    prev0 = z_ref[s, c * R - 1:c * R, :]
                prev = jnp.where(first, prev0, pltpu.roll(z, 1, axis=0))
                parts.append(z + (prev - z) * mu_ref[...])
            return jnp.concatenate(parts, axis=0)

        per_stream = lambda x: [x[s * R:(s + 1) * R] for s in range(nseq)]
        r, k, v, l = [shift(z_ref, mu_ref) for z_ref, mu_ref in zip(z_refs, mu_refs)]
        w_raw = w0_ref[...] + _dot(jnp.tanh(l).astype(BF16), wd_ref[...])
        a_raw = a0_ref[...] + _dot(l.astype(BF16), wa_ref[...])
        g = _dot(jax.nn.sigmoid(l).astype(BF16), wg_ref[...])
        yield
        logw = jax.nn.sigmoid(w_raw) * (-math.exp(-0.5))
        a = jax.nn.sigmoid(a_raw)
        kk = k * kkp_ref[...]
        k2 = k * (1.0 + (a - 1.0) * kap_ref[...])
        ss, rk = seg_sums([kk * kk, r * k2 * rkp_ref[...]])
        cl_s = [cum(incl01, lw) for lw in per_stream(logw)]
        if nblk == 1:
            ce_s = [jnp.broadcast_to(x[R - 1:R, :], (R, RW)) for x in cl_s]
        else:
            ce_s = [cum(same01, lw) for lw in per_stream(logw)]
        yield
        cl = jnp.concatenate(cl_s, axis=0)
        cl_end = jnp.concatenate(ce_s, axis=0)
        kk = kk / jnp.maximum(jnp.sqrt(ss), 1e-12)
        av = -kk
        bv = kk * a
        e_neg = jnp.exp(-cl)
        e_end = jnp.exp(cl_end - cl)
        full = dict(v=v, g=g, bonus=rk * v,
                    at=av * jnp.exp(cl - logw), rt=r * jnp.exp(cl),
                    bt=bv * e_neg, kt=k2 * e_neg, bh=bv * e_end, kh=k2 * e_end)
        for name, x in full.items():
            for s, x_s in enumerate(per_stream(x)):
                outs[s][name] = x_s
        for s in range(nseq):
            outs[s]["p_end"] = jnp.exp(cl_s[s][R - 1:R, :] if nblk == 1 else ce_s[s])

    def main(c, ps):
        rows_c = slice(c * R, (c + 1) * R)
        col = lambda name, ui: ps[units[ui][0]][name][:, sls[units[ui][1]]]
        v_g = [col("v", ui) for ui in uids]
        lhs = [jnp.concatenate([col("at", ui), col("rt", ui)], axis=0).astype(BF16) for ui in uids]
        o1 = [_dot_nt(lhs[ui], jnp.concatenate([_bd(col("bt", ui), bdm), _bd(col("kt", ui), bdm)], axis=0))
              for ui in uids]
        yield
        w_ab = [jnp.where(strict_w, o[:R, :GRP], 0.0) for o in o1]
        w_ak = [jnp.where(strict_w, o[:R, GRP:], 0.0) for o in o1]
        w_rb = [jnp.where(incl_w, o[R:, :GRP], 0.0) for o in o1]
        w_rk = [jnp.where(incl_w, o[R:, GRP:], 0.0) for o in o1]

        def state_part(ui):
            s, gi = units[ui]
            if carried:
                o2 = _dot_nt(lhs[ui], s_ref[s, gi].astype(BF16))
                return o2[:R], o2[R:]
            at_u, rt_u = col("at", ui), col("rt", ui)
            a_rows, r_rows = [], []
            for b, rows in enumerate(blks):
                lhs_b = jnp.concatenate([at_u[rows], rt_u[rows]], axis=0).astype(BF16)
                o2 = _dot_nt(lhs_b, _bd_heads(si_ref[s, c * nblk + b, sls[gi], :], bdm))
                a_rows.append(o2[:lb])
                r_rows.append(o2[lb:])
            return jnp.concatenate(a_rows, axis=0), jnp.concatenate(r_rows, axis=0)

        x_w = list(w_ab)
        t_w = [eye_w + x for x in x_w]
        n_lvl = max(1, int(math.log2(lb)))
        a_s = r_s = o3 = None
        for j in range(n_lvl):
            for ui in uids:
                y_bd = _bd(x_w[ui], bdm)
                if j == 0:
                    x_w[ui] = _dot(x_w[ui].astype(BF16), y_bd)
                elif j == n_lvl - 1:
                    t_w[ui] = t_w[ui] + _dot(t_w[ui].astype(BF16), y_bd)
                else:
                    res = _dot(jnp.concatenate([t_w[ui], x_w[ui]], axis=0).astype(BF16), y_bd)
                    t_w[ui] = t_w[ui] + res[:R]
                    x_w[ui] = res[R:]
            yield
            if j == 0:
                parts = [state_part(ui) for ui in uids]
                a_s, r_s = [q[0] for q in parts], [q[1] for q in parts]
                yield
            elif j == 1:
                o3 = [_dot(jnp.concatenate([w_ak[ui], w_rk[ui]], axis=0).astype(BF16), _bd(v_g[ui], bdm))
                      for ui in uids]
                yield

        u = [_dot(t_w[ui].astype(BF16), _bd(a_s[ui] + o3[ui][:R], bdm)) for ui in uids]
        yield
        o_parts = [r_s[ui] + o3[ui][R:] + _dot(w_rb[ui].astype(BF16), _bd(u[ui], bdm)) for ui in uids]
        yield

        for ui in uids:
            s, gi = units[ui]
            sl = sls[gi]
            bh_u, kh_u, p_end = col("bh", ui), col("kh", ui), ps[s]["p_end"][:, sl]
            if carried:
                upd = _dot_tn(jnp.concatenate([u[ui], v_g[ui]], axis=0).astype(BF16),
                              jnp.concatenate([bh_u, kh_u], axis=0).astype(BF16))
                s_new = jnp.where(bdm, s_ref[s, gi] * p_end + upd, 0.0)
                s_ref[s, gi] = s_new
                if c == nch - 1:
                    so_ref[s, sl, :] = _diag_blocks(s_new)
            else:
                for b, rows in enumerate(blks):
                    upd = _dot_tn(jnp.concatenate([u[ui][rows], v_g[ui][rows]], axis=0).astype(BF16),
                                  jnp.concatenate([bh_u[rows], kh_u[rows]], axis=0).astype(BF16))
                    p_b = p_end[b * lb:b * lb + 1]
                    p_nat = jnp.concatenate(
                        [jnp.broadcast_to(p_b[:, h * HEAD:(h + 1) * HEAD], (HEAD, HEAD)) for h in range(HPG)],
                        axis=0)
                    so_ref[s, c * nblk + b, sl, :] = (si_ref[s, c * nblk + b, sl, :] * p_nat
                                                      + _diag_blocks(upd))
        yield

        o = [jnp.concatenate(o_parts[s * N_GRP:(s + 1) * N_GRP], axis=1) for s in range(nseq)]
        mu = seg_sums(o)
        yield
        dlt = [o[s] - mu[s] * (1.0 / HEAD) for s in range(nseq)]
        var = seg_sums([d * d for d in dlt])
        for s in range(nseq):
            on = dlt[s] * lax.rsqrt(var[s] * (1.0 / HEAD) + GN_EPS) * gng_ref[...] + gnb_ref[...]
            oa_ref[s, rows_c, :] = ((on + ps[s]["bonus"]) * ps[s]["g"]).astype(BF16)

    def run(gen):
        for _ in gen:
            pass

    cur = [{} for _ in range(nseq)]
    run(prep(0, cur))
    for c in range(nch):
        nxt = [{} for _ in range(nseq)]
        side = prep(c + 1, nxt) if c + 1 < nch else iter(())
        for tick, _ in enumerate(main(c, cur)):
            if tick % 3 == 1:
                next(side, None)
        run(side)
        cur = nxt

    n_rows = nch * R
    for s in range(nseq):
        if carried:
            last = lambda z_ref: z_ref[s, n_rows - 1:n_rows, :]
            for z_ref in z_refs:
                carry[id(z_ref)][s] = last(z_ref)
        else:
            last = lambda z_ref: jnp.concatenate(
                [z_ref[s, b * lb + lb - 1:b * lb + lb, :] for b in range(nch * nblk)], axis=0)
        sho_ref[s, :, 0:RW] = last(zr_ref)
        sho_ref[s, :, RW:2 * RW] = last(zk_ref)
        sho_ref[s, :, 2 * RW:3 * RW] = last(zv_ref)
        sho_ref[s, :, 3 * RW:] = last(zl_ref)


def _wkv(z, vecs, mats, n_seq, seq_len, lb, nch, nseq, prev_rows=None, state_in=None):
    R = nch * WKV_ROWS
    carried = prev_rows is None
    rows = z.shape[0]
    n_streams = n_seq if carried else nseq
    stream_rows = rows // n_streams
    assert n_streams % nseq == 0 and stream_rows % R == 0 and rows == n_streams * stream_rows
    z3 = z.reshape(n_streams, stream_rows, Z_W)
    if carried:
        grid = (n_streams // nseq, stream_rows // R)
        rmap = lambda c: (lambda b, t: (b, t, c))
        cmap = lambda b, t: (0, 0)
        smap = lambda b, t: (b, 0, 0)
        sem = ("arbitrary", "arbitrary")
    else:
        grid = (stream_rows // R,)
        rmap = lambda c: (lambda i: (0, i, c))
        cmap = lambda i: (0, 0)
        sem = ("arbitrary",)
        blk_per_step = R // lb
        seq_per_stream = n_seq // nseq

    in_specs = [pl.BlockSpec((nseq, R, RW), rmap(Z_R // RW)),
                pl.BlockSpec((nseq, R, RW), rmap(Z_K // RW)),
                pl.BlockSpec((nseq, R, RW), rmap(Z_VR // RW)),
                pl.BlockSpec((nseq, R, LORA), rmap(Z_L // LORA))]
    args = [z3, z3, z3, z3]
    if not carried:
        p3 = prev_rows.reshape(nseq, seq_per_stream, C_SHIFT)
        st4 = state_in.reshape(nseq, seq_per_stream, RW, HEAD)
        in_specs += [pl.BlockSpec((nseq, blk_per_step, RW), rmap(0)),
                     pl.BlockSpec((nseq, blk_per_step, RW), rmap(1)),
                     pl.BlockSpec((nseq, blk_per_step, RW), rmap(2)),
                     pl.BlockSpec((nseq, blk_per_step, LORA), rmap(3 * RW // LORA)),
                     pl.BlockSpec((nseq, blk_per_step, RW, HEAD), lambda i: (0, i, 0, 0))]
        args += [p3] * 4 + [st4]
    for a in vecs + mats:
        in_specs.append(pl.BlockSpec(a.shape, cmap))
        args.append(a)

    if carried:
        scratch = [pltpu.VMEM((nseq, N_GRP, GRP, GRP), F32), pltpu.VMEM((nseq, 1, RW), F32),
                   pltpu.VMEM((nseq, 1, RW), F32), pltpu.VMEM((nseq, 1, RW), F32),
                   pltpu.VMEM((nseq, 1, LORA), F32)]
        state_spec = pl.BlockSpec((nseq, RW, HEAD), smap)
        state_shape = (n_seq, RW, HEAD)
        shift_spec = pl.BlockSpec((nseq, 1, C_SHIFT), smap)
        shift_shape = (n_seq, 1, C_SHIFT)
    else:
        scratch = []
        state_spec = pl.BlockSpec((nseq, blk_per_step, RW, HEAD), lambda i: (0, i, 0, 0))
        state_shape = (nseq, seq_per_stream, RW, HEAD)
        shift_spec = pl.BlockSpec((nseq, blk_per_step, C_SHIFT), lambda i: (0, i, 0))
        shift_shape = (nseq, seq_per_stream, C_SHIFT)
    oa, st, sh = pl.pallas_call(
        functools.partial(_wkv_kernel, lb=lb, carried=carried, nch=nch, nseq=nseq),
        grid=grid,
        in_specs=in_specs,
        out_specs=[pl.BlockSpec((nseq, R, RW), rmap(0)), state_spec, shift_spec],
        out_shape=[jax.ShapeDtypeStruct((n_streams, stream_rows, RW), BF16),
                   jax.ShapeDtypeStruct(state_shape, F32),
                   jax.ShapeDtypeStruct(shift_shape, F32)],
        scratch_shapes=scratch,
        compiler_params=_params(sem),
        name="wkv_carried" if carried else "wkv_blocks",
    )(*args)
    return oa.reshape(rows, RW), st.reshape(n_seq, RW, HEAD), sh.reshape(n_seq, C_SHIFT)


def _gmlp_kernel(*refs, seq, want_vn, n_cast):
    zu_ref, zv_ref, lng_ref, lnb_ref, ws_ref, bias_ref = refs[:6]
    cast_in = refs[6:6 + n_cast]
    ub_ref = refs[6 + n_cast]
    vn_ref = refs[7 + n_cast] if want_vn else None
    cast_out = refs[len(refs) - n_cast:]
    for src, dst in zip(cast_in, cast_out):
        dst[...] = src[...].astype(BF16)
    reps = CHUNK // seq
    tr = lax.broadcasted_iota(jnp.int32, (CHUNK, CHUNK), 0)
    tc = lax.broadcasted_iota(jnp.int32, (CHUNK, CHUNK), 1)
    causal = (tc <= tr) & ((tr // seq) == (tc // seq))
    ws_g = []
    for gi in range(MLP_GROUPS):
        w_rows = ws_ref[gi]
        w_full = jnp.concatenate([w_rows if s == 0 else pltpu.roll(w_rows, s * seq, axis=1)
                                  for s in range(reps)], axis=0)
        ws_g.append(jnp.where(causal, w_full, 0.0).astype(BF16))
    bias = jnp.concatenate([bias_ref[...]] * reps, axis=0)
    for c in range(zu_ref.shape[0] // CHUNK):
        rows = slice(c * CHUNK, (c + 1) * CHUNK)
        u = jax.nn.gelu(zu_ref[rows, :])
        vg = jax.nn.gelu(zv_ref[rows, :])
        mu = jnp.mean(vg, -1, keepdims=True)
        var = jnp.mean(jnp.square(vg - mu), -1, keepdims=True)
        vn = (vg - mu) * lax.rsqrt(var + LN_EPS) * lng_ref[...] + lnb_ref[...]
        if vn_ref is not None:
            vn_ref[rows, :] = vn
        vb = vn.astype(BF16)
        cols = [_dot(ws_g[gi], vb[:, gi * MLP_GD:(gi + 1) * MLP_GD]) for gi in range(MLP_GROUPS)]
        mixed = jnp.concatenate(cols, axis=1) + bias
        ub_ref[rows, :] = (u * mixed).astype(BF16)


def _gmlp(z, ln_g, ln_b, ws, bias, seq, want_vn, to_bf16=()):
    rows, tr = z.shape[0], 4 * CHUNK
    steps = rows // tr
    n_out = 2 if want_vn else 1
    cast_specs = [pl.BlockSpec((w.shape[0] // steps, w.shape[1]), lambda i: (i, 0)) for w in to_bf16]
    assert all(w.shape[0] % (16 * steps) == 0 for w in to_bf16)
    return pl.pallas_call(
        functools.partial(_gmlp_kernel, seq=seq, want_vn=want_vn, n_cast=len(to_bf16)),
        grid=(steps,),
        in_specs=[pl.BlockSpec((tr, MLP_W), lambda i: (i, Z_U // MLP_W)),
                  pl.BlockSpec((tr, MLP_W), lambda i: (i, Z_V // MLP_W)),
                  pl.BlockSpec((1, MLP_W), lambda i: (0, 0)),
                  pl.BlockSpec((1, MLP_W), lambda i: (0, 0)),
                  pl.BlockSpec((MLP_GROUPS, seq, CHUNK), lambda i: (0, 0, 0)),
                  pl.BlockSpec((seq, MLP_W), lambda i: (0, 0))] + cast_specs,
        out_specs=[pl.BlockSpec((tr, MLP_W), lambda i: (i, 0)),
                   pl.BlockSpec((tr, MLP_W), lambda i: (i, 0))][:n_out] + cast_specs,
        out_shape=[jax.ShapeDtypeStruct((rows, MLP_W), BF16),
                   jax.ShapeDtypeStruct((rows, MLP_W), F32)][:n_out]
                  + [jax.ShapeDtypeStruct(w.shape, BF16) for w in to_bf16],
        compiler_params=_params(("arbitrary",)),
        name="gmlp",
    )(z, z, ln_g, ln_b, ws, bias, *to_bf16)


def _merge_kernel(oa_ref, ub_ref, zga_ref, zgb_ref, x_ref, gm_ref, sc_ref, sh_ref, g_ref,
                  wa_ref, wb_ref, wo_ref, x1_ref, h2_ref):
    ya = _dot(oa_ref[...], wa_ref[...])
    yb = _dot(ub_ref[...], wb_ref[...])
    merged = jax.nn.sigmoid(zga_ref[...]) * ya + jax.nn.sigmoid(zgb_ref[...]) * yb
    mix = _dot(merged.astype(BF16), wo_ref[...])
    tm = x_ref.shape[0]
    x1 = x_ref[...] + _mod_rows(gm_ref, tm) * mix
    x1_ref[...] = x1
    h2_ref[...] = (_rms(x1, g_ref[...]) * (1.0 + _mod_rows(sc_ref, tm)) + _mod_rows(sh_ref, tm)).astype(BF16)


def _merge(oa, ub, z, x, mod, g_ffn, wa, wb, wo, rows_per_seq):
    m, tm = x.shape[0], 256
    const = lambda shape: pl.BlockSpec(shape, lambda i: (0, 0), pipeline_mode=pl.Buffered(1))
    mspec = lambda k: _mod_spec(k, tm, rows_per_seq)
    rows = lambda w, c: pl.BlockSpec((tm, w), lambda i: (i, c))
    return pl.pallas_call(
        _merge_kernel,
        grid=(m // tm,),
        in_specs=[rows(RW, 0), rows(MLP_W, 0),
                  rows(D_MODEL, Z_GA // D_MODEL), rows(D_MODEL, Z_GB // D_MODEL), rows(D_MODEL, 0),
                  mspec(MOD_GATE_M), mspec(MOD_SCALE_F), mspec(MOD_SHIFT_F),
                  pl.BlockSpec((1, D_MODEL), lambda i: (0, 0)),
                  const((RW, D_MODEL)), const((MLP_W, D_MODEL)), const((D_MODEL, D_MODEL))],
        out_specs=[rows(D_MODEL, 0), rows(D_MODEL, 0)],
        out_shape=[jax.ShapeDtypeStruct((m, D_MODEL), F32), jax.ShapeDtypeStruct((m, D_MODEL), BF16)],
        compiler_params=_params(("arbitrary",)),
        name="merge",
    )(oa, ub, z, z, x, mod, mod, mod, g_ffn, wa, wb, wo)


def _ffn_kernel(*refs, fused):
    if fused:
        h_ref, x_ref, gf_ref, gfin_ref, wg_ref, wu_ref, wo_ref, y_ref = refs
    else:
        h_ref, wg_ref, wu_ref, wo_ref, y_ref = refs
    j = pl.program_id(1)

    @pl.when(j == 0)
    def _():
        y_ref[...] = jnp.zeros_like(y_ref)

    hb = h_ref[...]
    gt = _dot(hb, wg_ref[...].astype(BF16))
    up = _dot(hb, wu_ref[...].astype(BF16))
    act = gt * jax.nn.sigmoid(gt) * up
    y_ref[...] += _dot(act.astype(BF16), wo_ref[...].astype(BF16))

    if fused:
        @pl.when(j == pl.num_programs(1) - 1)
        def _():
            x2 = x_ref[...] + gf_ref[...] * y_ref[...]
            y_ref[...] = _rms(x2, gfin_ref[...])


def _final_kernel(f_ref, x_ref, gf_ref, gfin_ref, y_ref):
    tm = x_ref.shape[0]

    def body(rows, i):
        x2 = x_ref[rows, :] + _mod_chunk(gf_ref, i, tm) * f_ref[rows, :]
        y_ref[rows, :] = _rms(x2, gfin_ref[...])

    _for_row_chunks(tm, body)


def _ffn(h2, x1, mod, g_final, w_in, w_out, rows_per_seq):
    m, tm, tf = x1.shape[0], 1024, 256
    nf = D_FF // tf
    fused = rows_per_seq >= tm
    rows = pl.BlockSpec((tm, D_MODEL), lambda i, j: (i, 0))
    vec = pl.BlockSpec((1, D_MODEL), lambda i, j: (0, 0))
    wspecs = [pl.BlockSpec((D_MODEL, tf), lambda i, j: (0, j)),
              pl.BlockSpec((D_MODEL, tf), lambda i, j: (0, nf + j)),
              pl.BlockSpec((tf, D_MODEL), lambda i, j: (j, 0))]
    if fused:
        in_specs = [rows, pl.BlockSpec((tm, D_MODEL), lambda i, j: (i, 0), pipeline_mode=pl.Buffered(1)),
                    _mod_spec(MOD_GATE_F, tm, rows_per_seq), vec] + wspecs
        args = (h2, x1, mod, g_final, w_in, w_in, w_out)
    else:
        in_specs = [rows] + wspecs
        args = (h2, w_in, w_in, w_out)
    y = pl.pallas_call(
        functools.partial(_ffn_kernel, fused=fused),
        grid=(m // tm, nf),
        in_specs=in_specs,
        out_specs=rows,
        out_shape=jax.ShapeDtypeStruct((m, D_MODEL), F32),
        compiler_params=_params(("arbitrary", "arbitrary")),
        name="ffn",
    )(*args)
    if fused:
        return y
    te = 512
    erows = pl.BlockSpec((te, D_MODEL), lambda i: (i, 0))
    return pl.pallas_call(
        _final_kernel,
        grid=(m // te,),
        in_specs=[erows, erows, _mod_spec(MOD_GATE_F, te, rows_per_seq),
                  pl.BlockSpec((1, D_MODEL), lambda i: (0, 0))],
        out_specs=erows,
        out_shape=jax.ShapeDtypeStruct((m, D_MODEL), F32),
        compiler_params=_params(("arbitrary",)),
        name="final",
    )(y, x1, mod, g_final)


def kernel(x_prompt, x_sample, state_wkv, state_shift, c_prompt, c_sample, w_ada, b_ada, norm_mix_g, w_in, mu_shift, w0, w_decay_up, a0, w_aaa_up, w_gate_up, k_k, k_a, r_k, gn_g, gn_b, ln_v_g, ln_v_b, w_spatial, b_spatial, w_branch_a, w_branch_b, w_out, norm_ffn_g, w_ffn_in, w_ffn_out, norm_final_g):
    assert w_ada.shape[0] == 1, "single layer"
    bp, tp, _ = x_prompt.shape
    bs, ts, _ = x_sample.shape
    assert tp % CHUNK == 0 and WKV_ROWS % ts == 0 and CHUNK % ts == 0

    c_all = jnp.concatenate([c_sample, c_prompt], axis=0)
    c_all = jnp.pad(c_all, ((0, -c_all.shape[0] % 8), (0, 0)))
    mod = _ada(c_all, w_ada[0], b_ada[0])
    mod_s = mod
    mod_p = mod[bs:bs + bp].reshape(bp, 1, 6 * D_MODEL)

    wfi, wfo = w_ffn_in[0], w_ffn_out[0]

    row = lambda a: a.reshape(1, -1)
    mu = mu_shift[0]
    vecs = [row(mu[:RW]), row(mu[RW:2 * RW]), row(mu[2 * RW:3 * RW]), row(mu[3 * RW:]),
            row(w0[0]), row(a0[0]), row(k_k[0]), row(k_a[0]), row(r_k[0]), row(gn_g[0]), row(gn_b[0])]
    zpad = lambda w, lo: jnp.pad(w, ((lo, LORA - lo - w.shape[0]), (0, 0))).astype(BF16)
    mats = [zpad(w_decay_up[0], 0), zpad(w_aaa_up[0], DECAY_LORA),
            zpad(w_gate_up[0], DECAY_LORA + AAA_LORA)]

    bias = jnp.repeat(b_spatial[0].T, MLP_GD, axis=1)
    ws = w_spatial[0]

    xp = x_prompt.reshape(bp * tp, D_MODEL)
    xs = x_sample.reshape(bs * ts, D_MODEL)
    g_mix, g_ffn, g_fin = row(norm_mix_g[0]), row(norm_ffn_g[0]), row(norm_final_g)
    lng, lnb = row(ln_v_g[0]), row(ln_v_b[0])

    zp = _inproj(_hnorm(xp, mod_p, g_mix, tp), w_in[0])
    oa_p, st_p, shift_p = _wkv(zp, vecs, mats, bp, tp, WKV_ROWS, WKV_NCH, WKV_NSEQ_PROMPT)
    ub_p, wa_b, wb_b, wo_b = _gmlp(zp, lng, lnb, ws, bias, CHUNK, False,
                                   to_bf16=(w_branch_a[0], w_branch_b[0], w_out[0]))
    x1_p, h2_p = _merge(oa_p, ub_p, zp, xp, mod_p, g_ffn, wa_b, wb_b, wo_b, tp)
    y_p = _ffn(h2_p, x1_p, mod_p, g_fin, wfi, wfo, tp)

    zs = _inproj(_hnorm(xs, mod_s, g_mix, ts), w_in[0])
    oa_s, st_s, shift_s = _wkv(zs, vecs, mats, bs, ts, ts, 1, WKV_NSEQ, prev_rows=state_shift[0],
                               state_in=state_wkv[0].reshape(bs, RW, HEAD))
    ub_s, vn_s = _gmlp(zs, lng, lnb, ws, bias, ts, True)
    x1_s, h2_s = _merge(oa_s, ub_s, zs, xs, mod_s, g_ffn, wa_b, wb_b, wo_b, ts)
    y_s = _ffn(h2_s, x1_s, mod_s, g_fin, wfi, wfo, ts)

    return (y_p.reshape(bp, tp, D_MODEL),
            y_s.reshape(bs, ts, D_MODEL),
            st_p.reshape(1, bp, N_HEADS, HEAD, HEAD),
            shift_p.reshape(1, bp, C_SHIFT),
            st_s.reshape(1, bs, N_HEADS, HEAD, HEAD),
            shift_s.reshape(1, bs, C_SHIFT),
            vn_s.reshape(bs, ts, MLP_W)[None])
```

```python
import functools
import math

import jax
import jax.numpy as jnp
from jax import lax
from jax.experimental import pallas as pl
from jax.experimental.pallas import tpu as pltpu

F32 = jnp.float32
BF16 = jnp.bfloat16

D_MODEL = 2048
HEAD = 64
RW = D_MODEL // 2
N_HEADS = RW // HEAD
DECAY_LORA = D_MODEL // 32
AAA_LORA = D_MODEL // 32
GATE_LORA = D_MODEL // 16
LORA = DECAY_LORA + AAA_LORA + GATE_LORA
CHUNK = 128
MLP_W = D_MODEL // 2
MLP_GROUPS = 8
MLP_GD = MLP_W // MLP_GROUPS
D_FF = ((-(-8 * D_MODEL // 3)) + 255) // 256 * 256
C_SHIFT = 3 * RW + LORA
C_IN = C_SHIFT + 2 * MLP_W + 2 * D_MODEL
NORM_EPS = 1e-6
GN_EPS = 64e-5
LN_EPS = 1e-5

Z_GA, Z_GB, Z_U, Z_V, Z_R, Z_K, Z_VR, Z_L = 0, 2048, 4096, 5120, 6144, 7168, 8192, 9216
Z_TN = 512
Z_W = -(-C_IN // Z_TN) * Z_TN

GRP = 256
HPG = GRP // HEAD
N_GRP = RW // GRP
WKV_ROWS = 64
WKV_NCH = 2
WKV_NSEQ_PROMPT = 2
WKV_NSEQ = 2
VMEM_LIMIT = 56 * 1024 * 1024


def _dot(a, b):
    return jnp.dot(a, b, preferred_element_type=F32)


def _dot_nt(a, b):
    return lax.dot_general(a, b, (((1,), (1,)), ((), ())), preferred_element_type=F32)


def _dot_tn(a, b):
    return lax.dot_general(a, b, (((0,), (0,)), ((), ())), preferred_element_type=F32)


def _rms(x, g):
    return x * lax.rsqrt(jnp.mean(x * x, -1, keepdims=True) + NORM_EPS) * g


def _split2(x):
    hi = x.astype(BF16)
    lo = (x - hi.astype(F32)).astype(BF16)
    return hi, lo


def _split3(x):
    hi = x.astype(BF16)
    r1 = x - hi.astype(F32)
    mid = r1.astype(BF16)
    lo = (r1 - mid.astype(F32)).astype(BF16)
    return hi, mid, lo


def _params(sem):
    return pltpu.CompilerParams(dimension_semantics=sem, vmem_limit_bytes=VMEM_LIMIT)


def _ada_kernel(c_ref, w_ref, b_ref, o_ref, s_ref):
    @pl.when(pl.program_id(0) == 0)
    def _():
        c = c_ref[...]
        s_ref[...] = (c * jax.nn.sigmoid(c)).astype(BF16)

    o_ref[...] = _dot(s_ref[...], w_ref[...].astype(BF16)) + b_ref[...]


def _ada(c, w_ada, b_ada):
    m, n, tn = c.shape[0], w_ada.shape[1], 1024
    return pl.pallas_call(
        _ada_kernel,
        grid=(n // tn,),
        in_specs=[pl.BlockSpec((m, D_MODEL), lambda j: (0, 0)),
                  pl.BlockSpec((D_MODEL, tn), lambda j: (0, j)),
                  pl.BlockSpec((1, tn), lambda j: (0, j))],
        out_specs=pl.BlockSpec((m, tn), lambda j: (0, j)),
        out_shape=jax.ShapeDtypeStruct((m, n), F32),
        scratch_shapes=[pltpu.VMEM((m, D_MODEL), BF16)],
        compiler_params=_params(("arbitrary",)),
        name="ada",
    )(c, w_ada, b_ada.reshape(1, n))


MOD_SHIFT_M, MOD_SCALE_M, MOD_GATE_M, MOD_SHIFT_F, MOD_SCALE_F, MOD_GATE_F = range(6)


def _mod_spec(k, tm, rows_per_seq):
    if rows_per_seq >= tm:
        return pl.BlockSpec((None, 1, D_MODEL), lambda i, *_: (i * tm // rows_per_seq, 0, k))
    return pl.BlockSpec((tm // rows_per_seq, D_MODEL), lambda i, *_: (i, k))


def _seq_rows(ref, first, n_seq, per_seq):
    return jnp.concatenate([jnp.broadcast_to(ref[pl.ds(first + s, 1), :], (per_seq, ref.shape[1]))
                            for s in range(n_seq)], axis=0)


def _mod_rows(ref, tm):
    n_seq = ref.shape[0]
    return ref[...] if n_seq == 1 else _seq_rows(ref, 0, n_seq, tm // n_seq)


ROW_CHUNK = 16


def _for_row_chunks(n_rows, body):
    def step(i, carry):
        body(pl.ds(pl.multiple_of(i * ROW_CHUNK, ROW_CHUNK), ROW_CHUNK), i)
        return carry
    lax.fori_loop(0, n_rows // ROW_CHUNK, step, 0, unroll=8)


def _mod_chunk(ref, i, tm):
    if ref.shape[0] == 1:
        return ref[...]
    per_seq = tm // ref.shape[0]
    n = ROW_CHUNK // per_seq
    return _seq_rows(ref, i * n, n, per_seq)


ZB = 256
_N_RW_B, _N_MLP_B, _N_GATE_B = C_SHIFT // ZB, 2 * MLP_W // ZB, 2 * D_MODEL // ZB


def _w_in_block(zb):
    return jnp.where(zb < _N_GATE_B, zb + _N_RW_B + _N_MLP_B,
                     jnp.where(zb < _N_GATE_B + _N_MLP_B, zb - _N_GATE_B + _N_RW_B,
                               jnp.minimum(zb - _N_GATE_B - _N_MLP_B, _N_RW_B - 1)))


def _hnorm_kernel(x_ref, sc_ref, sh_ref, g_ref, h_ref):
    tm = x_ref.shape[0]

    def body(rows, i):
        h = _rms(x_ref[rows, :], g_ref[...]) * (1.0 + _mod_chunk(sc_ref, i, tm)) + _mod_chunk(sh_ref, i, tm)
        h_ref[rows, :] = h.astype(BF16)

    _for_row_chunks(tm, body)


def _hnorm(x, mod, g, rows_per_seq, n_rows=None):
    m, tm = (n_rows or x.shape[0]), 1024
    mspec = lambda k: _mod_spec(k, tm, rows_per_seq)
    return pl.pallas_call(
        _hnorm_kernel,
        grid=(m // tm,),
        in_specs=[pl.BlockSpec((tm, D_MODEL), lambda i: (i, 0)),
                  mspec(MOD_SCALE_M), mspec(MOD_SHIFT_M),
                  pl.BlockSpec((1, D_MODEL), lambda i: (0, 0))],
        out_specs=pl.BlockSpec((tm, D_MODEL), lambda i: (i, 0)),
        out_shape=jax.ShapeDtypeStruct((m, D_MODEL), BF16),
        compiler_params=_params(("arbitrary",)),
        name="hnorm",
    )(x, mod, mod, g)


def _inproj_kernel(h_ref, wlo_ref, whi_ref, o_ref):
    h = h_ref[...]
    o_ref[:, :ZB] = _dot(h, wlo_ref[...].astype(BF16))
    o_ref[:, ZB:] = _dot(h, whi_ref[...].astype(BF16))


def _inproj(h, w):
    m = h.shape[0]
    tm = min(m, 2048)
    assert Z_TN == 2 * ZB and m % tm == 0
    return pl.pallas_call(
        _inproj_kernel,
        grid=(m // tm, Z_W // Z_TN),
        in_specs=[pl.BlockSpec((tm, D_MODEL), lambda i, j: (i, 0)),
                  pl.BlockSpec((D_MODEL, ZB), lambda i, j: (0, _w_in_block(2 * j))),
                  pl.BlockSpec((D_MODEL, ZB), lambda i, j: (0, _w_in_block(2 * j + 1)))],
        out_specs=pl.BlockSpec((tm, Z_TN), lambda i, j: (i, j)),
        out_shape=jax.ShapeDtypeStruct((m, Z_W), F32),
        compiler_params=_params(("arbitrary", "arbitrary")),
        name="inproj",
    )(h, w, w)


INPROJ_TM = 2048
INPROJ_CHUNK = 128


def _inproj_norm_kernel(h0_ref, x_ref, sc_ref, sh_ref, g_ref, wlo_ref, whi_ref, o_ref, ha_ref, hb_ref):
    i, j = pl.program_id(0), pl.program_id(1)
    ch = x_ref.shape[0]
    n_chunks = ha_ref.shape[0] // ch

    @pl.when((i == 0) & (j == 0))
    def _():
        ha_ref[...] = h0_ref[...]

    def step(cur_ref, nxt_ref):
        row0 = pl.multiple_of(jnp.minimum(j, n_chunks - 1) * ch, ch)
        h = _rms(x_ref[...], g_ref[...]) * (1.0 + sc_ref[...]) + sh_ref[...]
        nxt_ref[pl.ds(row0, ch), :] = h.astype(BF16)
        h_cur = cur_ref[...]
        o_ref[:, :ZB] = _dot(h_cur, wlo_ref[...].astype(BF16))
        o_ref[:, ZB:] = _dot(h_cur, whi_ref[...].astype(BF16))

    @pl.when(i % 2 == 0)
    def _():
        step(ha_ref, hb_ref)

    @pl.when(i % 2 == 1)
    def _():
        step(hb_ref, ha_ref)


def _inproj_norm(x, mod, g, w, rows_per_seq):
    m, tm, ch = x.shape[0], INPROJ_TM, INPROJ_CHUNK
    nt, n_chunks, n_col = m // tm, tm // ch, Z_W // Z_TN
    assert m % tm == 0 and rows_per_seq % tm == 0 and n_chunks <= n_col and Z_TN == 2 * ZB
    h0 = _hnorm(x, mod, g, rows_per_seq, n_rows=tm)
    nxt = lambda i: jnp.minimum(i + 1, nt - 1)
    mspec = lambda k: pl.BlockSpec((None, 1, D_MODEL), lambda i, j: (nxt(i) * tm // rows_per_seq, 0, k))
    return pl.pallas_call(
        _inproj_norm_kernel,
        grid=(nt, n_col),
        in_specs=[pl.BlockSpec((tm, D_MODEL), lambda i, j: (0, 0), pipeline_mode=pl.Buffered(1)),
                  pl.BlockSpec((ch, D_MODEL), lambda i, j: (nxt(i) * n_chunks + jnp.minimum(j, n_chunks - 1), 0)),
                  mspec(MOD_SCALE_M), mspec(MOD_SHIFT_M),
                  pl.BlockSpec((1, D_MODEL), lambda i, j: (0, 0)),
                  pl.BlockSpec((D_MODEL, ZB), lambda i, j: (0, _w_in_block(2 * j))),
                  pl.BlockSpec((D_MODEL, ZB), lambda i, j: (0, _w_in_block(2 * j + 1)))],
        out_specs=pl.BlockSpec((tm, Z_TN), lambda i, j: (i, j)),
        out_shape=jax.ShapeDtypeStruct((m, Z_W), F32),
        scratch_shapes=[pltpu.VMEM((tm, D_MODEL), BF16), pltpu.VMEM((tm, D_MODEL), BF16)],
        compiler_params=_params(("arbitrary", "arbitrary")),
        name="inproj_norm",
    )(h0, x, mod, mod, g, w, w)


def _bd(w, mask):
    return jnp.where(mask, jnp.concatenate([w] * HPG, axis=0), 0.0).astype(BF16)


def _bd_heads(nat, mask):
    return jnp.where(mask, jnp.concatenate([nat] * HPG, axis=1), 0.0).astype(BF16)


def _diag_blocks(m):
    return jnp.concatenate([m[h * HEAD:(h + 1) * HEAD, h * HEAD:(h + 1) * HEAD] for h in range(HPG)],
                           axis=0)


def _wkv_kernel(*refs, lb, carried, nch, nseq):
    R = WKV_ROWS
    if carried:
        (zr_ref, zk_ref, zv_ref, zl_ref,
         mur_ref, muk_ref, muv_ref, mul_ref,
         w0_ref, a0_ref, kkp_ref, kap_ref, rkp_ref, gng_ref, gnb_ref,
         wd_ref, wa_ref, wg_ref,
         oa_ref, so_ref, sho_ref,
         s_ref, cr_ref, ck_ref, cv_ref, cl_ref) = refs
        carry = {id(zr_ref): cr_ref, id(zk_ref): ck_ref, id(zv_ref): cv_ref, id(zl_ref): cl_ref}
    else:
        (zr_ref, zk_ref, zv_ref, zl_ref,
         pr_ref, pk_ref, pv_ref, pl_ref, si_ref,
         mur_ref, muk_ref, muv_ref, mul_ref,
         w0_ref, a0_ref, kkp_ref, kap_ref, rkp_ref, gng_ref, gnb_ref,
         wd_ref, wa_ref, wg_ref,
         oa_ref, so_ref, sho_ref) = refs
        prev_of = {id(zr_ref): pr_ref, id(zk_ref): pk_ref, id(zv_ref): pv_ref, id(zl_ref): pl_ref}
    nblk = R // lb
    z_refs = (zr_ref, zk_ref, zv_ref, zl_ref)
    mu_refs = (mur_ref, muk_ref, muv_ref, mul_ref)

    if carried:
        @pl.when(pl.program_id(1) == 0)
        def _():
            s_ref[...] = jnp.zeros_like(s_ref)
            for ref in (cr_ref, ck_ref, cv_ref, cl_ref):
                ref[...] = jnp.zeros_like(ref)

    row1 = lax.broadcasted_iota(jnp.int32, (R, 1), 0)
    first = (row1 % lb) == 0
    ri = lax.broadcasted_iota(jnp.int32, (GRP, GRP), 0)
    ci = lax.broadcasted_iota(jnp.int32, (GRP, GRP), 1)
    bdm = (ri // HEAD) == (ci // HEAD)
    ones_bd = jnp.where(bdm, 1.0, 0.0).astype(BF16)
    tr = lax.broadcasted_iota(jnp.int32, (R, R), 0)
    tc = lax.broadcasted_iota(jnp.int32, (R, R), 1)
    same = (tr // lb) == (tc // lb)
    incl01 = jnp.where(same & (tc <= tr), 1.0, 0.0).astype(BF16)
    same01 = jnp.where(same, 1.0, 0.0).astype(BF16)
    wr = lax.broadcasted_iota(jnp.int32, (R, GRP), 0)
    wc = lax.broadcasted_iota(jnp.int32, (R, GRP), 1) % HEAD
    wsame = (wr // lb) == (wc // lb)
    strict_w = wsame & (wc < wr)
    incl_w = wsame & (wc <= wr)
    eye_w = jnp.where(wc == wr, 1.0, 0.0)
    sls = [slice(gi * GRP, (gi + 1) * GRP) for gi in range(N_GRP)]
    blks = [slice(b * lb, (b + 1) * lb) for b in range(nblk)]
    units = [(s, gi) for s in range(nseq) for gi in range(N_GRP)]
    uids = range(len(units))

    def seg_sums(xs):
        n = xs[0].shape[0]
        parts = []
        for x in xs:
            parts.extend(_split2(x))
        stacked = jnp.concatenate(parts, axis=0)
        cols = [_dot(stacked[:, sl], ones_bd) for sl in sls]
        full = jnp.concatenate(cols, axis=1)
        return [full[2 * i * n:(2 * i + 1) * n] + full[(2 * i + 1) * n:(2 * i + 2) * n]
                for i in range(len(xs))]

    def cum(m01, x):
        hi, mid, lo = _split3(x)
        return _dot(m01, hi) + _dot(m01, mid) + _dot(m01, lo)

    def prep(c, outs):
        rows = slice(c * R, (c + 1) * R)

        def shift(z_ref, mu_ref):
            parts = []
            for s in range(nseq):
                z = z_ref[s, rows, :]
                if not carried:
                    prev0 = _seq_rows(prev_of[id(z_ref)].at[s], c * nblk, nblk, lb)
                elif c == 0:
                    prev0 = carry[id(z_ref)][s]
                else:
                    prev0 = z_ref[s, c * R - 1:c * R, :]
                prev = jnp.where(first, prev0, pltpu.roll(z, 1, axis=0))
                parts.append(z + (prev - z) * mu_ref[...])
            return jnp.concatenate(parts, axis=0)

        per_stream = lambda x: [x[s * R:(s + 1) * R] for s in range(nseq)]
        r, k, v, l = [shift(z_ref, mu_ref) for z_ref, mu_ref in zip(z_refs, mu_refs)]
        w_raw = w0_ref[...] + _dot(jnp.tanh(l).astype(BF16), wd_ref[...])
        a_raw = a0_ref[...] + _dot(l.astype(BF16), wa_ref[...])
        g = _dot(jax.nn.sigmoid(l).astype(BF16), wg_ref[...])
        yield
        logw = jax.nn.sigmoid(w_raw) * (-math.exp(-0.5))
        a = jax.nn.sigmoid(a_raw)
        kk = k * kkp_ref[...]
        k2 = k * (1.0 + (a - 1.0) * kap_ref[...])
        ss, rk = seg_sums([kk * kk, r * k2 * rkp_ref[...]])
        cl_s = [cum(incl01, lw) for lw in per_stream(logw)]
        if nblk == 1:
            ce_s = [jnp.broadcast_to(x[R - 1:R, :], (R, RW)) for x in cl_s]
        else:
            ce_s = [cum(same01, lw) for lw in per_stream(logw)]
        yield
        cl = jnp.concatenate(cl_s, axis=0)
        cl_end = jnp.concatenate(ce_s, axis=0)
        kk = kk / jnp.maximum(jnp.sqrt(ss), 1e-12)
        av = -kk
        bv = kk * a
        e_neg = jnp.exp(-cl)
        e_end = jnp.exp(cl_end - cl)
        full = dict(v=v, g=g, bonus=rk * v,
                    at=av * jnp.exp(cl - logw), rt=r * jnp.exp(cl),
                    bt=bv * e_neg, kt=k2 * e_neg, bh=bv * e_end, kh=k2 * e_end)
        for name, x in full.items():
            for s, x_s in enumerate(per_stream(x)):
                outs[s][name] = x_s
        for s in range(nseq):
            outs[s]["p_end"] = jnp.exp(cl_s[s][R - 1:R, :] if nblk == 1 else ce_s[s])

    def main(c, ps):
        rows_c = slice(c * R, (c + 1) * R)
        col = lambda name, ui: ps[units[ui][0]][name][:, sls[units[ui][1]]]
        v_g = [col("v", ui) for ui in uids]
        lhs = [jnp.concatenate([col("at", ui), col("rt", ui)], axis=0).astype(BF16) for ui in uids]
        o1 = [_dot_nt(lhs[ui], jnp.concatenate([_bd(col("bt", ui), bdm), _bd(col("kt", ui), bdm)], axis=0))
              for ui in uids]
        yield
        w_ab = [jnp.where(strict_w, o[:R, :GRP], 0.0) for o in o1]
        w_ak = [jnp.where(strict_w, o[:R, GRP:], 0.0) for o in o1]
        w_rb = [jnp.where(incl_w, o[R:, :GRP], 0.0) for o in o1]
        w_rk = [jnp.where(incl_w, o[R:, GRP:], 0.0) for o in o1]

        def state_part(ui):
            s, gi = units[ui]
            if carried:
                o2 = _dot_nt(lhs[ui], s_ref[s, gi].astype(BF16))
                return o2[:R], o2[R:]
            at_u, rt_u = col("at", ui), col("rt", ui)
            a_rows, r_rows = [], []
            for b, rows in enumerate(blks):
                lhs_b = jnp.concatenate([at_u[rows], rt_u[rows]], axis=0).astype(BF16)
                o2 = _dot_nt(lhs_b, _bd_heads(si_ref[s, c * nblk + b, sls[gi], :], bdm))
                a_rows.append(o2[:lb])
                r_rows.append(o2[lb:])
            return jnp.concatenate(a_rows, axis=0), jnp.concatenate(r_rows, axis=0)

        x_w = list(w_ab)
        t_w = [eye_w + x for x in x_w]
        n_lvl = max(1, int(math.log2(lb)))
        a_s = r_s = o3 = None
        for j in range(n_lvl):
            for ui in uids:
                y_bd = _bd(x_w[ui], bdm)
                if j == 0:
                    x_w[ui] = _dot(x_w[ui].astype(BF16), y_bd)
                elif j == n_lvl - 1:
                    t_w[ui] = t_w[ui] + _dot(t_w[ui].astype(BF16), y_bd)
                else:
                    res = _dot(jnp.concatenate([t_w[ui], x_w[ui]], axis=0).astype(BF16), y_bd)
                    t_w[ui] = t_w[ui] + res[:R]
                    x_w[ui] = res[R:]
            yield
            if j == 0:
                parts = [state_part(ui) for ui in uids]
                a_s, r_s = [q[0] for q in parts], [q[1] for q in parts]
                yield
            elif j == 1:
                o3 = [_dot(jnp.concatenate([w_ak[ui], w_rk[ui]], axis=0).astype(BF16), _bd(v_g[ui], bdm))
                      for ui in uids]
                yield

        u = [_dot(t_w[ui].astype(BF16), _bd(a_s[ui] + o3[ui][:R], bdm)) for ui in uids]
        yield
        o_parts = [r_s[ui] + o3[ui][R:] + _dot(w_rb[ui].astype(BF16), _bd(u[ui], bdm)) for ui in uids]
        yield

        for ui in uids:
            s, gi = units[ui]
            sl = sls[gi]
            bh_u, kh_u, p_end = col("bh", ui), col("kh", ui), ps[s]["p_end"][:, sl]
            if carried:
                upd = _dot_tn(jnp.concatenate([u[ui], v_g[ui]], axis=0).astype(BF16),
                              jnp.concatenate([bh_u, kh_u], axis=0).astype(BF16))
                s_new = jnp.where(bdm, s_ref[s, gi] * p_end + upd, 0.0)
                s_ref[s, gi] = s_new
                if c == nch - 1:
                    so_ref[s, sl, :] = _diag_blocks(s_new)
            else:
                for b, rows in enumerate(blks):
                    upd = _dot_tn(jnp.concatenate([u[ui][rows], v_g[ui][rows]], axis=0).astype(BF16),
                                  jnp.concatenate([bh_u[rows], kh_u[rows]], axis=0).astype(BF16))
                    p_b = p_end[b * lb:b * lb + 1]
                    p_nat = jnp.concatenate(
                        [jnp.broadcast_to(p_b[:, h * HEAD:(h + 1) * HEAD], (HEAD, HEAD)) for h in range(HPG)],
                        axis=0)
                    so_ref[s, c * nblk + b, sl, :] = (si_ref[s, c * nblk + b, sl, :] * p_nat
                                                      + _diag_blocks(upd))
        yield

        o = [jnp.concatenate(o_parts[s * N_GRP:(s + 1) * N_GRP], axis=1) for s in range(nseq)]
        mu = seg_sums(o)
        yield
        dlt = [o[s] - mu[s] * (1.0 / HEAD) for s in range(nseq)]
        var = seg_sums([d * d for d in dlt])
        for s in range(nseq):
            on = dlt[s] * lax.rsqrt(var[s] * (1.0 / HEAD) + GN_EPS) * gng_ref[...] + gnb_ref[...]
            oa_ref[s, rows_c, :] = ((on + ps[s]["bonus"]) * ps[s]["g"]).astype(BF16)

    def run(gen):
        for _ in gen:
            pass

    cur = [{} for _ in range(nseq)]
    run(prep(0, cur))
    for c in range(nch):
        nxt = [{} for _ in range(nseq)]
        side = prep(c + 1, nxt) if c + 1 < nch else iter(())
        for tick, _ in enumerate(main(c, cur)):
            if tick % 3 == 1:
                next(side, None)
        run(side)
        cur = nxt

    n_rows = nch * R
    for s in range(nseq):
        if carried:
            last = lambda z_ref: z_ref[s, n_rows - 1:n_rows, :]
            for z_ref in z_refs:
                carry[id(z_ref)][s] = last(z_ref)
        else:
            last = lambda z_ref: jnp.concatenate(
                [z_ref[s, b * lb + lb - 1:b * lb + lb, :] for b in range(nch * nblk)], axis=0)
        sho_ref[s, :, 0:RW] = last(zr_ref)
        sho_ref[s, :, RW:2 * RW] = last(zk_ref)
        sho_ref[s, :, 2 * RW:3 * RW] = last(zv_ref)
        sho_ref[s, :, 3 * RW:] = last(zl_ref)


def _wkv(z, vecs, mats, n_seq, seq_len, lb, nch, nseq, prev_rows=None, state_in=None):
    R = nch * WKV_ROWS
    carried = prev_rows is None
    rows = z.shape[0]
    n_streams = n_seq if carried else nseq
    stream_rows = rows // n_streams
    assert n_streams % nseq == 0 and stream_rows % R == 0 and rows == n_streams * stream_rows
    z3 = z.reshape(n_streams, stream_rows, Z_W)
    if carried:
        grid = (n_streams // nseq, stream_rows // R)
        rmap = lambda c: (lambda b, t: (b, t, c))
        cmap = lambda b, t: (0, 0)
        smap = lambda b, t: (b, 0, 0)
        sem = ("arbitrary", "arbitrary")
    else:
        grid = (stream_rows // R,)
        rmap = lambda c: (lambda i: (0, i, c))
        cmap = lambda i: (0, 0)
        sem = ("arbitrary",)
        blk_per_step = R // lb
        seq_per_stream = n_seq // nseq

    in_specs = [pl.BlockSpec((nseq, R, RW), rmap(Z_R // RW)),
                pl.BlockSpec((nseq, R, RW), rmap(Z_K // RW)),
                pl.BlockSpec((nseq, R, RW), rmap(Z_VR // RW)),
                pl.BlockSpec((nseq, R, LORA), rmap(Z_L // LORA))]
    args = [z3, z3, z3, z3]
    if not carried:
        p3 = prev_rows.reshape(nseq, seq_per_stream, C_SHIFT)
        st4 = state_in.reshape(nseq, seq_per_stream, RW, HEAD)
        in_specs += [pl.BlockSpec((nseq, blk_per_step, RW), rmap(0)),
                     pl.BlockSpec((nseq, blk_per_step, RW), rmap(1)),
                     pl.BlockSpec((nseq, blk_per_step, RW), rmap(2)),
                     pl.BlockSpec((nseq, blk_per_step, LORA), rmap(3 * RW // LORA)),
                     pl.BlockSpec((nseq, blk_per_step, RW, HEAD), lambda i: (0, i, 0, 0))]
        args += [p3] * 4 + [st4]
    for a in vecs + mats:
        in_specs.append(pl.BlockSpec(a.shape, cmap))
        args.append(a)

    if carried:
        scratch = [pltpu.VMEM((nseq, N_GRP, GRP, GRP), F32), pltpu.VMEM((nseq, 1, RW), F32),
                   pltpu.VMEM((nseq, 1, RW), F32), pltpu.VMEM((nseq, 1, RW), F32),
                   pltpu.VMEM((nseq, 1, LORA), F32)]
        state_spec = pl.BlockSpec((nseq, RW, HEAD), smap)
        state_shape = (n_seq, RW, HEAD)
        shift_spec = pl.BlockSpec((nseq, 1, C_SHIFT), smap)
        shift_shape = (n_seq, 1, C_SHIFT)
    else:
        scratch = []
        state_spec = pl.BlockSpec((nseq, blk_per_step, RW, HEAD), lambda i: (0, i, 0, 0))
        state_shape = (nseq, seq_per_stream, RW, HEAD)
        shift_spec = pl.BlockSpec((nseq, blk_per_step, C_SHIFT), lambda i: (0, i, 0))
        shift_shape = (nseq, seq_per_stream, C_SHIFT)
    oa, st, sh = pl.pallas_call(
        functools.partial(_wkv_kernel, lb=lb, carried=carried, nch=nch, nseq=nseq),
        grid=grid,
        in_specs=in_specs,
        out_specs=[pl.BlockSpec((nseq, R, RW), rmap(0)), state_spec, shift_spec],
        out_shape=[jax.ShapeDtypeStruct((n_streams, stream_rows, RW), BF16),
                   jax.ShapeDtypeStruct(state_shape, F32),
                   jax.ShapeDtypeStruct(shift_shape, F32)],
        scratch_shapes=scratch,
        compiler_params=_params(sem),
        name="wkv_carried" if carried else "wkv_blocks",
    )(*args)
    return oa.reshape(rows, RW), st.reshape(n_seq, RW, HEAD), sh.reshape(n_seq, C_SHIFT)


def _gmlp_kernel(*refs, seq, want_vn, n_cast):
    zu_ref, zv_ref, lng_ref, lnb_ref, ws_ref, bias_ref = refs[:6]
    cast_in = refs[6:6 + n_cast]
    ub_ref = refs[6 + n_cast]
    vn_ref = refs[7 + n_cast] if want_vn else None
    cast_out = refs[len(refs) - n_cast:]
    for src, dst in zip(cast_in, cast_out):
        dst[...] = src[...].astype(BF16)
    reps = CHUNK // seq
    tr = lax.broadcasted_iota(jnp.int32, (CHUNK, CHUNK), 0)
    tc = lax.broadcasted_iota(jnp.int32, (CHUNK, CHUNK), 1)
    causal = (tc <= tr) & ((tr // seq) == (tc // seq))
    ws_g = []
    for gi in range(MLP_GROUPS):
        w_rows = ws_ref[gi]
        w_full = jnp.concatenate([w_rows if s == 0 else pltpu.roll(w_rows, s * seq, axis=1)
                                  for s in range(reps)], axis=0)
        ws_g.append(jnp.where(causal, w_full, 0.0).astype(BF16))
    bias = jnp.concatenate([bias_ref[...]] * reps, axis=0)
    for c in range(zu_ref.shape[0] // CHUNK):
        rows = slice(c * CHUNK, (c + 1) * CHUNK)
        u = jax.nn.gelu(zu_ref[rows, :])
        vg = jax.nn.gelu(zv_ref[rows, :])
        mu = jnp.mean(vg, -1, keepdims=True)
        var = jnp.mean(jnp.square(vg - mu), -1, keepdims=True)
        vn = (vg - mu) * lax.rsqrt(var + LN_EPS) * lng_ref[...] + lnb_ref[...]
        if vn_ref is not None:
            vn_ref[rows, :] = vn
        vb = vn.astype(BF16)
        cols = [_dot(ws_g[gi], vb[:, gi * MLP_GD:(gi + 1) * MLP_GD]) for gi in range(MLP_GROUPS)]
        mixed = jnp.concatenate(cols, axis=1) + bias
        ub_ref[rows, :] = (u * mixed).astype(BF16)


def _gmlp(z, ln_g, ln_b, ws, bias, seq, want_vn, to_bf16=()):
    rows, tr = z.shape[0], 4 * CHUNK
    steps = rows // tr
    n_out = 2 if want_vn else 1
    cast_specs = [pl.BlockSpec((w.shape[0] // steps, w.shape[1]), lambda i: (i, 0)) for w in to_bf16]
    assert all(w.shape[0] % (16 * steps) == 0 for w in to_bf16)
    return pl.pallas_call(
        functools.partial(_gmlp_kernel, seq=seq, want_vn=want_vn, n_cast=len(to_bf16)),
        grid=(steps,),
        in_specs=[pl.BlockSpec((tr, MLP_W), lambda i: (i, Z_U // MLP_W)),
                  pl.BlockSpec((tr, MLP_W), lambda i: (i, Z_V // MLP_W)),
                  pl.BlockSpec((1, MLP_W), lambda i: (0, 0)),
                  pl.BlockSpec((1, MLP_W), lambda i: (0, 0)),
                  pl.BlockSpec((MLP_GROUPS, seq, CHUNK), lambda i: (0, 0, 0)),
                  pl.BlockSpec((seq, MLP_W), lambda i: (0, 0))] + cast_specs,
        out_specs=[pl.BlockSpec((tr, MLP_W), lambda i: (i, 0)),
                   pl.BlockSpec((tr, MLP_W), lambda i: (i, 0))][:n_out] + cast_specs,
        out_shape=[jax.ShapeDtypeStruct((rows, MLP_W), BF16),
                   jax.ShapeDtypeStruct((rows, MLP_W), F32)][:n_out]
                  + [jax.ShapeDtypeStruct(w.shape, BF16) for w in to_bf16],
        compiler_params=_params(("arbitrary",)),
        name="gmlp",
    )(z, z, ln_g, ln_b, ws, bias, *to_bf16)


def _merge_kernel(oa_ref, ub_ref, zga_ref, zgb_ref, x_ref, gm_ref, sc_ref, sh_ref, g_ref,
                  wa_ref, wb_ref, wo_ref, x1_ref, h2_ref):
    ya = _dot(oa_ref[...], wa_ref[...])
    yb = _dot(ub_ref[...], wb_ref[...])
    merged = jax.nn.sigmoid(zga_ref[...]) * ya + jax.nn.sigmoid(zgb_ref[...]) * yb
    mix = _dot(merged.astype(BF16), wo_ref[...])
    tm = x_ref.shape[0]
    x1 = x_ref[...] + _mod_rows(gm_ref, tm) * mix
    x1_ref[...] = x1
    h2_ref[...] = (_rms(x1, g_ref[...]) * (1.0 + _mod_rows(sc_ref, tm)) + _mod_rows(sh_ref, tm)).astype(BF16)


def _merge(oa, ub, z, x, mod, g_ffn, wa, wb, wo, rows_per_seq):
    m, tm = x.shape[0], 256
    const = lambda shape: pl.BlockSpec(shape, lambda i: (0, 0), pipeline_mode=pl.Buffered(1))
    mspec = lambda k: _mod_spec(k, tm, rows_per_seq)
    rows = lambda w, c: pl.BlockSpec((tm, w), lambda i: (i, c))
    return pl.pallas_call(
        _merge_kernel,
        grid=(m // tm,),
        in_specs=[rows(RW, 0), rows(MLP_W, 0),
                  rows(D_MODEL, Z_GA // D_MODEL), rows(D_MODEL, Z_GB // D_MODEL), rows(D_MODEL, 0),
                  mspec(MOD_GATE_M), mspec(MOD_SCALE_F), mspec(MOD_SHIFT_F),
                  pl.BlockSpec((1, D_MODEL), lambda i: (0, 0)),
                  const((RW, D_MODEL)), const((MLP_W, D_MODEL)), const((D_MODEL, D_MODEL))],
        out_specs=[rows(D_MODEL, 0), rows(D_MODEL, 0)],
        out_shape=[jax.ShapeDtypeStruct((m, D_MODEL), F32), jax.ShapeDtypeStruct((m, D_MODEL), BF16)],
        compiler_params=_params(("arbitrary",)),
        name="merge",
    )(oa, ub, z, z, x, mod, mod, mod, g_ffn, wa, wb, wo)


def _ffn_kernel(*refs, fused):
    if fused:
        h_ref, x_ref, gf_ref, gfin_ref, wg_ref, wu_ref, wo_ref, y_ref = refs
    else:
        h_ref, wg_ref, wu_ref, wo_ref, y_ref = refs
    j = pl.program_id(1)

    @pl.when(j == 0)
    def _():
        y_ref[...] = jnp.zeros_like(y_ref)

    hb = h_ref[...]
    gt = _dot(hb, wg_ref[...].astype(BF16))
    up = _dot(hb, wu_ref[...].astype(BF16))
    act = gt * jax.nn.sigmoid(gt) * up
    y_ref[...] += _dot(act.astype(BF16), wo_ref[...].astype(BF16))

    if fused:
        @pl.when(j == pl.num_programs(1) - 1)
        def _():
            x2 = x_ref[...] + gf_ref[...] * y_ref[...]
            y_ref[...] = _rms(x2, gfin_ref[...])


def _final_kernel(f_ref, x_ref, gf_ref, gfin_ref, y_ref):
    tm = x_ref.shape[0]

    def body(rows, i):
        x2 = x_ref[rows, :] + _mod_chunk(gf_ref, i, tm) * f_ref[rows, :]
        y_ref[rows, :] = _rms(x2, gfin_ref[...])

    _for_row_chunks(tm, body)


def _ffn(h2, x1, mod, g_final, w_in, w_out, rows_per_seq):
    m, tm, tf = x1.shape[0], 1024, 256
    nf = D_FF // tf
    fused = rows_per_seq >= tm
    rows = pl.BlockSpec((tm, D_MODEL), lambda i, j: (i, 0))
    vec = pl.BlockSpec((1, D_MODEL), lambda i, j: (0, 0))
    wspecs = [pl.BlockSpec((D_MODEL, tf), lambda i, j: (0, j)),
              pl.BlockSpec((D_MODEL, tf), lambda i, j: (0, nf + j)),
              pl.BlockSpec((tf, D_MODEL), lambda i, j: (j, 0))]
    if fused:
        in_specs = [rows, pl.BlockSpec((tm, D_MODEL), lambda i, j: (i, 0), pipeline_mode=pl.Buffered(1)),
                    _mod_spec(MOD_GATE_F, tm, rows_per_seq), vec] + wspecs
        args = (h2, x1, mod, g_final, w_in, w_in, w_out)
    else:
        in_specs = [rows] + wspecs
        args = (h2, w_in, w_in, w_out)
    y = pl.pallas_call(
        functools.partial(_ffn_kernel, fused=fused),
        grid=(m // tm, nf),
        in_specs=in_specs,
        out_specs=rows,
        out_shape=jax.ShapeDtypeStruct((m, D_MODEL), F32),
        compiler_params=_params(("arbitrary", "arbitrary")),
        name="ffn",
    )(*args)
    if fused:
        return y
    te = 512
    erows = pl.BlockSpec((te, D_MODEL), lambda i: (i, 0))
    return pl.pallas_call(
        _final_kernel,
        grid=(m // te,),
        in_specs=[erows, erows, _mod_spec(MOD_GATE_F, te, rows_per_seq),
                  pl.BlockSpec((1, D_MODEL), lambda i: (0, 0))],
        out_specs=erows,
        out_shape=jax.ShapeDtypeStruct((m, D_MODEL), F32),
        compiler_params=_params(("arbitrary",)),
        name="final",
    )(y, x1, mod, g_final)


def kernel(x_prompt, x_sample, state_wkv, state_shift, c_prompt, c_sample, w_ada, b_ada, norm_mix_g, w_in, mu_shift, w0, w_decay_up, a0, w_aaa_up, w_gate_up, k_k, k_a, r_k, gn_g, gn_b, ln_v_g, ln_v_b, w_spatial, b_spatial, w_branch_a, w_branch_b, w_out, norm_ffn_g, w_ffn_in, w_ffn_out, norm_final_g):
    assert w_ada.shape[0] == 1, "single layer"
    bp, tp, _ = x_prompt.shape
    bs, ts, _ = x_sample.shape
    assert tp % CHUNK == 0 and WKV_ROWS % ts == 0 and CHUNK % ts == 0

    c_all = jnp.concatenate([c_sample, c_prompt], axis=0)
    c_all = jnp.pad(c_all, ((0, -c_all.shape[0] % 8), (0, 0)))
    mod = _ada(c_all, w_ada[0], b_ada[0])
    mod_s = mod
    mod_p = mod[bs:bs + bp].reshape(bp, 1, 6 * D_MODEL)

    wfi, wfo = w_ffn_in[0], w_ffn_out[0]

    row = lambda a: a.reshape(1, -1)
    mu = mu_shift[0]
    vecs = [row(mu[:RW]), row(mu[RW:2 * RW]), row(mu[2 * RW:3 * RW]), row(mu[3 * RW:]),
            row(w0[0]), row(a0[0]), row(k_k[0]), row(k_a[0]), row(r_k[0]), row(gn_g[0]), row(gn_b[0])]
    zpad = lambda w, lo: jnp.pad(w, ((lo, LORA - lo - w.shape[0]), (0, 0))).astype(BF16)
    mats = [zpad(w_decay_up[0], 0), zpad(w_aaa_up[0], DECAY_LORA),
            zpad(w_gate_up[0], DECAY_LORA + AAA_LORA)]

    bias = jnp.repeat(b_spatial[0].T, MLP_GD, axis=1)
    ws = w_spatial[0]

    xp = x_prompt.reshape(bp * tp, D_MODEL)
    xs = x_sample.reshape(bs * ts, D_MODEL)
    g_mix, g_ffn, g_fin = row(norm_mix_g[0]), row(norm_ffn_g[0]), row(norm_final_g)
    lng, lnb = row(ln_v_g[0]), row(ln_v_b[0])

    zp = _inproj_norm(xp, mod_p, g_mix, w_in[0], tp)
    oa_p, st_p, shift_p = _wkv(zp, vecs, mats, bp, tp, WKV_ROWS, WKV_NCH, WKV_NSEQ_PROMPT)
    ub_p, wa_b, wb_b, wo_b = _gmlp(zp, lng, lnb, ws, bias, CHUNK, False,
                                   to_bf16=(w_branch_a[0], w_branch_b[0], w_out[0]))
    x1_p, h2_p = _merge(oa_p, ub_p, zp, xp, mod_p, g_ffn, wa_b, wb_b, wo_b, tp)
    y_p = _ffn(h2_p, x1_p, mod_p, g_fin, wfi, wfo, tp)

    zs = _inproj(_hnorm(xs, mod_s, g_mix, ts), w_in[0])
    oa_s, st_s, shift_s = _wkv(zs, vecs, mats, bs, ts, ts, 1, WKV_NSEQ, prev_rows=state_shift[0],
                               state_in=state_wkv[0].reshape(bs, RW, HEAD))
    ub_s, vn_s = _gmlp(zs, lng, lnb, ws, bias, ts, True)
    x1_s, h2_s = _merge(oa_s, ub_s, zs, xs, mod_s, g_ffn, wa_b, wb_b, wo_b, ts)
    y_s = _ffn(h2_s, x1_s, mod_s, g_fin, wfi, wfo, ts)

    return (y_p.reshape(bp, tp, D_MODEL),
            y_s.reshape(bs, ts, D_MODEL),
            st_p.reshape(1, bp, N_HEADS, HEAD, HEAD),
            shift_p.reshape(1, bp, C_SHIFT),
            st_s.reshape(1, bs, N_HEADS, HEAD, HEAD),
            shift_s.reshape(1, bs, C_SHIFT),
            vn_s.reshape(bs, ts, MLP_W)[None])
```

```python
import functools
import math

import jax
import jax.numpy as jnp
from jax import lax
from jax.experimental import pallas as pl
from jax.experimental.pallas import tpu as pltpu

F32 = jnp.float32
BF16 = jnp.bfloat16

D_MODEL = 2048
HEAD = 64
RW = D_MODEL // 2
N_HEADS = RW // HEAD
DECAY_LORA = D_MODEL // 32
AAA_LORA = D_MODEL // 32
GATE_LORA = D_MODEL // 16
LORA = DECAY_LORA + AAA_LORA + GATE_LORA
CHUNK = 128
MLP_W = D_MODEL // 2
MLP_GROUPS = 8
MLP_GD = MLP_W // MLP_GROUPS
D_FF = ((-(-8 * D_MODEL // 3)) + 255) // 256 * 256
C_SHIFT = 3 * RW + LORA
C_IN = C_SHIFT + 2 * MLP_W + 2 * D_MODEL
NORM_EPS = 1e-6
GN_EPS = 64e-5
LN_EPS = 1e-5

Z_GA, Z_GB, Z_U, Z_V, Z_R, Z_K, Z_VR, Z_L = 0, 2048, 4096, 5120, 6144, 7168, 8192, 9216
Z_TN = 512
Z_W = -(-C_IN // Z_TN) * Z_TN

GRP = 256
HPG = GRP // HEAD
N_GRP = RW // GRP
WKV_ROWS = 64
WKV_NCH = 2
WKV_NSEQ_PROMPT = 2
WKV_NSEQ = 2
VMEM_LIMIT = 56 * 1024 * 1024
FFN_VMEM_LIMIT = 60 * 1024 * 1024


def _dot(a, b):
    return jnp.dot(a, b, preferred_element_type=F32)


def _dot_nt(a, b):
    return lax.dot_general(a, b, (((1,), (1,)), ((), ())), preferred_element_type=F32)


def _dot_tn(a, b):
    return lax.dot_general(a, b, (((0,), (0,)), ((), ())), preferred_element_type=F32)


def _rms(x, g):
    return x * lax.rsqrt(jnp.mean(x * x, -1, keepdims=True) + NORM_EPS) * g


def _split2(x):
    hi = x.astype(BF16)
    lo = (x - hi.astype(F32)).astype(BF16)
    return hi, lo


def _split3(x):
    hi = x.astype(BF16)
    r1 = x - hi.astype(F32)
    mid = r1.astype(BF16)
    lo = (r1 - mid.astype(F32)).astype(BF16)
    return hi, mid, lo


def _params(sem, vmem_limit=VMEM_LIMIT):
    return pltpu.CompilerParams(dimension_semantics=sem, vmem_limit_bytes=vmem_limit)


def _ada_kernel(c_ref, w_ref, b_ref, o_ref, s_ref):
    @pl.when(pl.program_id(0) == 0)
    def _():
        c = c_ref[...]
        s_ref[...] = (c * jax.nn.sigmoid(c)).astype(BF16)

    o_ref[...] = _dot(s_ref[...], w_ref[...].astype(BF16)) + b_ref[...]


def _ada(c, w_ada, b_ada):
    m, n, tn = c.shape[0], w_ada.shape[1], 1024
    return pl.pallas_call(
        _ada_kernel,
        grid=(n // tn,),
        in_specs=[pl.BlockSpec((m, D_MODEL), lambda j: (0, 0)),
                  pl.BlockSpec((D_MODEL, tn), lambda j: (0, j)),
                  pl.BlockSpec((1, tn), lambda j: (0, j))],
        out_specs=pl.BlockSpec((m, tn), lambda j: (0, j)),
        out_shape=jax.ShapeDtypeStruct((m, n), F32),
        scratch_shapes=[pltpu.VMEM((m, D_MODEL), BF16)],
        compiler_params=_params(("arbitrary",)),
        name="ada",
    )(c, w_ada, b_ada.reshape(1, n))


MOD_SHIFT_M, MOD_SCALE_M, MOD_GATE_M, MOD_SHIFT_F, MOD_SCALE_F, MOD_GATE_F = range(6)


def _mod_spec(k, tm, rows_per_seq):
    if rows_per_seq >= tm:
        return pl.BlockSpec((None, 1, D_MODEL), lambda i, *_: (i * tm // rows_per_seq, 0, k))
    return pl.BlockSpec((tm // rows_per_seq, D_MODEL), lambda i, *_: (i, k))


def _seq_rows(ref, first, n_seq, per_seq):
    return jnp.concatenate([jnp.broadcast_to(ref[pl.ds(first + s, 1), :], (per_seq, ref.shape[1]))
                            for s in range(n_seq)], axis=0)


def _mod_rows(ref, tm):
    n_seq = ref.shape[0]
    return ref[...] if n_seq == 1 else _seq_rows(ref, 0, n_seq, tm // n_seq)


ROW_CHUNK = 16


def _for_row_chunks(n_rows, body):
    def step(i, carry):
        body(pl.ds(pl.multiple_of(i * ROW_CHUNK, ROW_CHUNK), ROW_CHUNK), i)
        return carry
    lax.fori_loop(0, n_rows // ROW_CHUNK, step, 0, unroll=8)


def _mod_chunk(ref, i, tm):
    if ref.shape[0] == 1:
        return ref[...]
    per_seq = tm // ref.shape[0]
    n = ROW_CHUNK // per_seq
    return _seq_rows(ref, i * n, n, per_seq)


ZB = 256
_N_RW_B, _N_MLP_B, _N_GATE_B = C_SHIFT // ZB, 2 * MLP_W // ZB, 2 * D_MODEL // ZB


def _w_in_block(zb):
    return jnp.where(zb < _N_GATE_B, zb + _N_RW_B + _N_MLP_B,
                     jnp.where(zb < _N_GATE_B + _N_MLP_B, zb - _N_GATE_B + _N_RW_B,
                               jnp.minimum(zb - _N_GATE_B - _N_MLP_B, _N_RW_B - 1)))


def _hnorm_kernel(x_ref, sc_ref, sh_ref, g_ref, h_ref):
    tm = x_ref.shape[0]

    def body(rows, i):
        h = _rms(x_ref[rows, :], g_ref[...]) * (1.0 + _mod_chunk(sc_ref, i, tm)) + _mod_chunk(sh_ref, i, tm)
        h_ref[rows, :] = h.astype(BF16)

    _for_row_chunks(tm, body)


def _hnorm(x, mod, g, rows_per_seq, n_rows=None):
    m, tm = (n_rows or x.shape[0]), 1024
    mspec = lambda k: _mod_spec(k, tm, rows_per_seq)
    return pl.pallas_call(
        _hnorm_kernel,
        grid=(m // tm,),
        in_specs=[pl.BlockSpec((tm, D_MODEL), lambda i: (i, 0)),
                  mspec(MOD_SCALE_M), mspec(MOD_SHIFT_M),
                  pl.BlockSpec((1, D_MODEL), lambda i: (0, 0))],
        out_specs=pl.BlockSpec((tm, D_MODEL), lambda i: (i, 0)),
        out_shape=jax.ShapeDtypeStruct((m, D_MODEL), BF16),
        compiler_params=_params(("arbitrary",)),
        name="hnorm",
    )(x, mod, mod, g)


def _inproj_kernel(h_ref, wlo_ref, whi_ref, o_ref):
    h = h_ref[...]
    o_ref[:, :ZB] = _dot(h, wlo_ref[...].astype(BF16))
    o_ref[:, ZB:] = _dot(h, whi_ref[...].astype(BF16))


def _inproj(h, w):
    m = h.shape[0]
    tm = min(m, 2048)
    assert Z_TN == 2 * ZB and m % tm == 0
    return pl.pallas_call(
        _inproj_kernel,
        grid=(m // tm, Z_W // Z_TN),
        in_specs=[pl.BlockSpec((tm, D_MODEL), lambda i, j: (i, 0)),
                  pl.BlockSpec((D_MODEL, ZB), lambda i, j: (0, _w_in_block(2 * j))),
                  pl.BlockSpec((D_MODEL, ZB), lambda i, j: (0, _w_in_block(2 * j + 1)))],
        out_specs=pl.BlockSpec((tm, Z_TN), lambda i, j: (i, j)),
        out_shape=jax.ShapeDtypeStruct((m, Z_W), F32),
        compiler_params=_params(("arbitrary", "arbitrary")),
        name="inproj",
    )(h, w, w)


INPROJ_TM = 2048
INPROJ_CHUNK = 128


def _inproj_norm_kernel(h0_ref, x_ref, sc_ref, sh_ref, g_ref, wlo_ref, whi_ref, o_ref, ha_ref, hb_ref):
    i, j = pl.program_id(0), pl.program_id(1)
    ch = x_ref.shape[0]
    n_chunks = ha_ref.shape[0] // ch

    @pl.when((i == 0) & (j == 0))
    def _():
        ha_ref[...] = h0_ref[...]

    def step(cur_ref, nxt_ref):
        row0 = pl.multiple_of(jnp.minimum(j, n_chunks - 1) * ch, ch)
        h = _rms(x_ref[...], g_ref[...]) * (1.0 + sc_ref[...]) + sh_ref[...]
        nxt_ref[pl.ds(row0, ch), :] = h.astype(BF16)
        h_cur = cur_ref[...]
        o_ref[:, :ZB] = _dot(h_cur, wlo_ref[...].astype(BF16))
        o_ref[:, ZB:] = _dot(h_cur, whi_ref[...].astype(BF16))

    @pl.when(i % 2 == 0)
    def _():
        step(ha_ref, hb_ref)

    @pl.when(i % 2 == 1)
    def _():
        step(hb_ref, ha_ref)


def _inproj_norm(x, mod, g, w, rows_per_seq):
    m, tm, ch = x.shape[0], INPROJ_TM, INPROJ_CHUNK
    nt, n_chunks, n_col = m // tm, tm // ch, Z_W // Z_TN
    assert m % tm == 0 and rows_per_seq % tm == 0 and n_chunks <= n_col and Z_TN == 2 * ZB
    h0 = _hnorm(x, mod, g, rows_per_seq, n_rows=tm)
    nxt = lambda i: jnp.minimum(i + 1, nt - 1)
    mspec = lambda k: pl.BlockSpec((None, 1, D_MODEL), lambda i, j: (nxt(i) * tm // rows_per_seq, 0, k))
    return pl.pallas_call(
        _inproj_norm_kernel,
        grid=(nt, n_col),
        in_specs=[pl.BlockSpec((tm, D_MODEL), lambda i, j: (0, 0), pipeline_mode=pl.Buffered(1)),
                  pl.BlockSpec((ch, D_MODEL), lambda i, j: (nxt(i) * n_chunks + jnp.minimum(j, n_chunks - 1), 0)),
                  mspec(MOD_SCALE_M), mspec(MOD_SHIFT_M),
                  pl.BlockSpec((1, D_MODEL), lambda i, j: (0, 0)),
                  pl.BlockSpec((D_MODEL, ZB), lambda i, j: (0, _w_in_block(2 * j))),
                  pl.BlockSpec((D_MODEL, ZB), lambda i, j: (0, _w_in_block(2 * j + 1)))],
        out_specs=pl.BlockSpec((tm, Z_TN), lambda i, j: (i, j)),
        out_shape=jax.ShapeDtypeStruct((m, Z_W), F32),
        scratch_shapes=[pltpu.VMEM((tm, D_MODEL), BF16), pltpu.VMEM((tm, D_MODEL), BF16)],
        compiler_params=_params(("arbitrary", "arbitrary")),
        name="inproj_norm",
    )(h0, x, mod, mod, g, w, w)


def _bd(w, mask):
    return jnp.where(mask, jnp.concatenate([w] * HPG, axis=0), 0.0).astype(BF16)


def _bd_heads(nat, mask):
    return jnp.where(mask, jnp.concatenate([nat] * HPG, axis=1), 0.0).astype(BF16)


def _diag_blocks(m):
    return jnp.concatenate([m[h * HEAD:(h + 1) * HEAD, h * HEAD:(h + 1) * HEAD] for h in range(HPG)],
                           axis=0)


def _wkv_kernel(*refs, lb, carried, nch, nseq, n_cast):
    R = WKV_ROWS
    n_in = 18 if carried else 23
    refs = list(refs)
    cast_in = [refs.pop(n_in) for _ in range(n_cast)]
    cast_out = [refs.pop(n_in + 3) for _ in range(n_cast)]
    for src, dst in zip(cast_in, cast_out):
        dst[...] = src[...].astype(BF16)
    if carried:
        (zr_ref, zk_ref, zv_ref, zl_ref,
         mur_ref, muk_ref, muv_ref, mul_ref,
         w0_ref, a0_ref, kkp_ref, kap_ref, rkp_ref, gng_ref, gnb_ref,
         wd_ref, wa_ref, wg_ref,
         oa_ref, so_ref, sho_ref,
         s_ref, cr_ref, ck_ref, cv_ref, cl_ref) = refs
        carry = {id(zr_ref): cr_ref, id(zk_ref): ck_ref, id(zv_ref): cv_ref, id(zl_ref): cl_ref}
    else:
        (zr_ref, zk_ref, zv_ref, zl_ref,
         pr_ref, pk_ref, pv_ref, pl_ref, si_ref,
         mur_ref, muk_ref, muv_ref, mul_ref,
         w0_ref, a0_ref, kkp_ref, kap_ref, rkp_ref, gng_ref, gnb_ref,
         wd_ref, wa_ref, wg_ref,
         oa_ref, so_ref, sho_ref) = refs
        prev_of = {id(zr_ref): pr_ref, id(zk_ref): pk_ref, id(zv_ref): pv_ref, id(zl_ref): pl_ref}
    nblk = R // lb
    z_refs = (zr_ref, zk_ref, zv_ref, zl_ref)
    mu_refs = (mur_ref, muk_ref, muv_ref, mul_ref)

    if carried:
        @pl.when(pl.program_id(1) == 0)
        def _():
            s_ref[...] = jnp.zeros_like(s_ref)
            for ref in (cr_ref, ck_ref, cv_ref, cl_ref):
                ref[...] = jnp.zeros_like(ref)

    row1 = lax.broadcasted_iota(jnp.int32, (R, 1), 0)
    first = (row1 % lb) == 0
    ri = lax.broadcasted_iota(jnp.int32, (GRP, GRP), 0)
    ci = lax.broadcasted_iota(jnp.int32, (GRP, GRP), 1)
    bdm = (ri // HEAD) == (ci // HEAD)
    ones_bd = jnp.where(bdm, 1.0, 0.0).astype(BF16)
    tr = lax.broadcasted_iota(jnp.int32, (R, R), 0)
    tc = lax.broadcasted_iota(jnp.int32, (R, R), 1)
    same = (tr // lb) == (tc // lb)
    incl01 = jnp.where(same & (tc <= tr), 1.0, 0.0).astype(BF16)
    same01 = jnp.where(same, 1.0, 0.0).astype(BF16)
    wr = lax.broadcasted_iota(jnp.int32, (R, GRP), 0)
    wc = lax.broadcasted_iota(jnp.int32, (R, GRP), 1) % HEAD
    wsame = (wr // lb) == (wc // lb)
    strict_w = wsame & (wc < wr)
    incl_w = wsame & (wc <= wr)
    eye_w = jnp.where(wc == wr, 1.0, 0.0)
    sls = [slice(gi * GRP, (gi + 1) * GRP) for gi in range(N_GRP)]
    blks = [slice(b * lb, (b + 1) * lb) for b in range(nblk)]
    units = [(s, gi) for s in range(nseq) for gi in range(N_GRP)]
    uids = range(len(units))

    def seg_sums(xs):
        n = xs[0].shape[0]
        parts = []
        for x in xs:
            parts.extend(_split2(x))
        stacked = jnp.concatenate(parts, axis=0)
        cols = [_dot(stacked[:, sl], ones_bd) for sl in sls]
        full = jnp.concatenate(cols, axis=1)
        return [full[2 * i * n:(2 * i + 1) * n] + full[(2 * i + 1) * n:(2 * i + 2) * n]
                for i in range(len(xs))]

    def cum(m01, x):
        hi, mid, lo = _split3(x)
        return _dot(m01, hi) + _dot(m01, mid) + _dot(m01, lo)

    def prep(c, outs):
        rows = slice(c * R, (c + 1) * R)

        def shift(z_ref, mu_ref):
            parts = []
            for s in range(nseq):
                z = z_ref[s, rows, :]
                if not carried:
                    prev0 = _seq_rows(prev_of[id(z_ref)].at[s], c * nblk, nblk, lb)
                elif c == 0:
                    prev0 = carry[id(z_ref)][s]
                else:
                    prev0 = z_ref[s, c * R - 1:c * R, :]
                prev = jnp.where(first, prev0, pltpu.roll(z, 1, axis=0))
                parts.append(z + (prev - z) * mu_ref[...])
            return jnp.concatenate(parts, axis=0)

        per_stream = lambda x: [x[s * R:(s + 1) * R] for s in range(nseq)]
        r, k, v, l = [shift(z_ref, mu_ref) for z_ref, mu_ref in zip(z_refs, mu_refs)]
        w_raw = w0_ref[...] + _dot(jnp.tanh(l).astype(BF16), wd_ref[...])
        a_raw = a0_ref[...] + _dot(l.astype(BF16), wa_ref[...])
        g = _dot(jax.nn.sigmoid(l).astype(BF16), wg_ref[...])
        yield
        logw = jax.nn.sigmoid(w_raw) * (-math.exp(-0.5))
        a = jax.nn.sigmoid(a_raw)
        kk = k * kkp_ref[...]
        k2 = k * (1.0 + (a - 1.0) * kap_ref[...])
        ss, rk = seg_sums([kk * kk, r * k2 * rkp_ref[...]])
        cl_s = [cum(incl01, lw) for lw in per_stream(logw)]
        if nblk == 1:
            ce_s = [jnp.broadcast_to(x[R - 1:R, :], (R, RW)) for x in cl_s]
        else:
            ce_s = [cum(same01, lw) for lw in per_stream(logw)]
        yield
        cl = jnp.concatenate(cl_s, axis=0)
        cl_end = jnp.concatenate(ce_s, axis=0)
        kk = kk / jnp.maximum(jnp.sqrt(ss), 1e-12)
        av = -kk
        bv = kk * a
        e_neg = jnp.exp(-cl)
        e_end = jnp.exp(cl_end - cl)
        full = dict(v=v, g=g, bonus=rk * v,
                    at=av * jnp.exp(cl - logw), rt=r * jnp.exp(cl),
                    bt=bv * e_neg, kt=k2 * e_neg, bh=bv * e_end, kh=k2 * e_end)
        for name, x in full.items():
            for s, x_s in enumerate(per_stream(x)):
                outs[s][name] = x_s
        for s in range(nseq):
            outs[s]["p_end"] = jnp.exp(cl_s[s][R - 1:R, :] if nblk == 1 else ce_s[s])

    def main(c, ps):
        rows_c = slice(c * R, (c + 1) * R)
        col = lambda name, ui: ps[units[ui][0]][name][:, sls[units[ui][1]]]
        v_g = [col("v", ui) for ui in uids]
        lhs = [jnp.concatenate([col("at", ui), col("rt", ui)], axis=0).astype(BF16) for ui in uids]
        o1 = [_dot_nt(lhs[ui], jnp.concatenate([_bd(col("bt", ui), bdm), _bd(col("kt", ui), bdm)], axis=0))
              for ui in uids]
        yield
        w_ab = [jnp.where(strict_w, o[:R, :GRP], 0.0) for o in o1]
        w_ak = [jnp.where(strict_w, o[:R, GRP:], 0.0) for o in o1]
        w_rb = [jnp.where(incl_w, o[R:, :GRP], 0.0) for o in o1]
        w_rk = [jnp.where(incl_w, o[R:, GRP:], 0.0) for o in o1]

        def state_part(ui):
            s, gi = units[ui]
            if carried:
                o2 = _dot_nt(lhs[ui], s_ref[s, gi].astype(BF16))
                return o2[:R], o2[R:]
            at_u, rt_u = col("at", ui), col("rt", ui)
            a_rows, r_rows = [], []
            for b, rows in enumerate(blks):
                lhs_b = jnp.concatenate([at_u[rows], rt_u[rows]], axis=0).astype(BF16)
                o2 = _dot_nt(lhs_b, _bd_heads(si_ref[s, c * nblk + b, sls[gi], :], bdm))
                a_rows.append(o2[:lb])
                r_rows.append(o2[lb:])
            return jnp.concatenate(a_rows, axis=0), jnp.concatenate(r_rows, axis=0)

        x_w = list(w_ab)
        t_w = [eye_w + x for x in x_w]
        n_lvl = max(1, int(math.log2(lb)))
        a_s = r_s = o3 = None
        for j in range(n_lvl):
            for ui in uids:
                y_bd = _bd(x_w[ui], bdm)
                if j == 0:
                    x_w[ui] = _dot(x_w[ui].astype(BF16), y_bd)
                elif j == n_lvl - 1:
                    t_w[ui] = t_w[ui] + _dot(t_w[ui].astype(BF16), y_bd)
                else:
                    res = _dot(jnp.concatenate([t_w[ui], x_w[ui]], axis=0).astype(BF16), y_bd)
                    t_w[ui] = t_w[ui] + res[:R]
                    x_w[ui] = res[R:]
            yield
            if j == 0:
                parts = [state_part(ui) for ui in uids]
                a_s, r_s = [q[0] for q in parts], [q[1] for q in parts]
                yield
            elif j == 1:
                o3 = [_dot(jnp.concatenate([w_ak[ui], w_rk[ui]], axis=0).astype(BF16), _bd(v_g[ui], bdm))
                      for ui in uids]
                yield

        u = [_dot(t_w[ui].astype(BF16), _bd(a_s[ui] + o3[ui][:R], bdm)) for ui in uids]
        yield
        o_parts = [r_s[ui] + o3[ui][R:] + _dot(w_rb[ui].astype(BF16), _bd(u[ui], bdm)) for ui in uids]
        yield

        for ui in uids:
            s, gi = units[ui]
            sl = sls[gi]
            bh_u, kh_u, p_end = col("bh", ui), col("kh", ui), ps[s]["p_end"][:, sl]
            if carried:
                upd = _dot_tn(jnp.concatenate([u[ui], v_g[ui]], axis=0).astype(BF16),
                              jnp.concatenate([bh_u, kh_u], axis=0).astype(BF16))
                s_new = jnp.where(bdm, s_ref[s, gi] * p_end + upd, 0.0)
                s_ref[s, gi] = s_new
                if c == nch - 1:
                    so_ref[s, sl, :] = _diag_blocks(s_new)
            else:
                for b, rows in enumerate(blks):
                    upd = _dot_tn(jnp.concatenate([u[ui][rows], v_g[ui][rows]], axis=0).astype(BF16),
                                  jnp.concatenate([bh_u[rows], kh_u[rows]], axis=0).astype(BF16))
                    p_b = p_end[b * lb:b * lb + 1]
                    p_nat = jnp.concatenate(
                        [jnp.broadcast_to(p_b[:, h * HEAD:(h + 1) * HEAD], (HEAD, HEAD)) for h in range(HPG)],
                        axis=0)
                    so_ref[s, c * nblk + b, sl, :] = (si_ref[s, c * nblk + b, sl, :] * p_nat
                                                      + _diag_blocks(upd))
        yield

        o = [jnp.concatenate(o_parts[s * N_GRP:(s + 1) * N_GRP], axis=1) for s in range(nseq)]
        mu = seg_sums(o)
        yield
        dlt = [o[s] - mu[s] * (1.0 / HEAD) for s in range(nseq)]
        var = seg_sums([d * d for d in dlt])
        for s in range(nseq):
            on = dlt[s] * lax.rsqrt(var[s] * (1.0 / HEAD) + GN_EPS) * gng_ref[...] + gnb_ref[...]
            oa_ref[s, rows_c, :] = ((on + ps[s]["bonus"]) * ps[s]["g"]).astype(BF16)

    def run(gen):
        for _ in gen:
            pass

    cur = [{} for _ in range(nseq)]
    run(prep(0, cur))
    for c in range(nch):
        nxt = [{} for _ in range(nseq)]
        side = prep(c + 1, nxt) if c + 1 < nch else iter(())
        for tick, _ in enumerate(main(c, cur)):
            if tick % 3 == 1:
                next(side, None)
        run(side)
        cur = nxt

    n_rows = nch * R
    for s in range(nseq):
        if carried:
            last = lambda z_ref: z_ref[s, n_rows - 1:n_rows, :]
            for z_ref in z_refs:
                carry[id(z_ref)][s] = last(z_ref)
        else:
            last = lambda z_ref: jnp.concatenate(
                [z_ref[s, b * lb + lb - 1:b * lb + lb, :] for b in range(nch * nblk)], axis=0)
        sho_ref[s, :, 0:RW] = last(zr_ref)
        sho_ref[s, :, RW:2 * RW] = last(zk_ref)
        sho_ref[s, :, 2 * RW:3 * RW] = last(zv_ref)
        sho_ref[s, :, 3 * RW:] = last(zl_ref)


def _wkv(z, vecs, mats, n_seq, seq_len, lb, nch, nseq, prev_rows=None, state_in=None, to_bf16=()):
    R = nch * WKV_ROWS
    carried = prev_rows is None
    rows = z.shape[0]
    n_streams = n_seq if carried else nseq
    stream_rows = rows // n_streams
    assert n_streams % nseq == 0 and stream_rows % R == 0 and rows == n_streams * stream_rows
    z3 = z.reshape(n_streams, stream_rows, Z_W)
    if carried:
        grid = (n_streams // nseq, stream_rows // R)
        rmap = lambda c: (lambda b, t: (b, t, c))
        cmap = lambda b, t: (0, 0)
        smap = lambda b, t: (b, 0, 0)
        sem = ("arbitrary", "arbitrary")
    else:
        grid = (stream_rows // R,)
        rmap = lambda c: (lambda i: (0, i, c))
        cmap = lambda i: (0, 0)
        sem = ("arbitrary",)
        blk_per_step = R // lb
        seq_per_stream = n_seq // nseq

    in_specs = [pl.BlockSpec((nseq, R, RW), rmap(Z_R // RW)),
                pl.BlockSpec((nseq, R, RW), rmap(Z_K // RW)),
                pl.BlockSpec((nseq, R, RW), rmap(Z_VR // RW)),
                pl.BlockSpec((nseq, R, LORA), rmap(Z_L // LORA))]
    args = [z3, z3, z3, z3]
    if not carried:
        p3 = prev_rows.reshape(nseq, seq_per_stream, C_SHIFT)
        st4 = state_in.reshape(nseq, seq_per_stream, RW, HEAD)
        in_specs += [pl.BlockSpec((nseq, blk_per_step, RW), rmap(0)),
                     pl.BlockSpec((nseq, blk_per_step, RW), rmap(1)),
                     pl.BlockSpec((nseq, blk_per_step, RW), rmap(2)),
                     pl.BlockSpec((nseq, blk_per_step, LORA), rmap(3 * RW // LORA)),
                     pl.BlockSpec((nseq, blk_per_step, RW, HEAD), lambda i: (0, i, 0, 0))]
        args += [p3] * 4 + [st4]
    for a in vecs + mats:
        in_specs.append(pl.BlockSpec(a.shape, cmap))
        args.append(a)

    if carried:
        scratch = [pltpu.VMEM((nseq, N_GRP, GRP, GRP), F32), pltpu.VMEM((nseq, 1, RW), F32),
                   pltpu.VMEM((nseq, 1, RW), F32), pltpu.VMEM((nseq, 1, RW), F32),
                   pltpu.VMEM((nseq, 1, LORA), F32)]
        state_spec = pl.BlockSpec((nseq, RW, HEAD), smap)
        state_shape = (n_seq, RW, HEAD)
        shift_spec = pl.BlockSpec((nseq, 1, C_SHIFT), smap)
        shift_shape = (n_seq, 1, C_SHIFT)
    else:
        scratch = []
        state_spec = pl.BlockSpec((nseq, blk_per_step, RW, HEAD), lambda i: (0, i, 0, 0))
        state_shape = (nseq, seq_per_stream, RW, HEAD)
        shift_spec = pl.BlockSpec((nseq, blk_per_step, C_SHIFT), lambda i: (0, i, 0))
        shift_shape = (nseq, seq_per_stream, C_SHIFT)
    n_steps = math.prod(grid)
    step = (lambda b, t: b * grid[1] + t) if carried else (lambda i: i)
    cast_specs = [pl.BlockSpec((w.shape[0] // n_steps, w.shape[1]), lambda *g: (step(*g), 0)) for w in to_bf16]
    assert all(w.shape[0] % (16 * n_steps) == 0 for w in to_bf16)
    oa, st, sh, *casted = pl.pallas_call(
        functools.partial(_wkv_kernel, lb=lb, carried=carried, nch=nch, nseq=nseq, n_cast=len(to_bf16)),
        grid=grid,
        in_specs=in_specs + cast_specs,
        out_specs=[pl.BlockSpec((nseq, R, RW), rmap(0)), state_spec, shift_spec] + cast_specs,
        out_shape=[jax.ShapeDtypeStruct((n_streams, stream_rows, RW), BF16),
                   jax.ShapeDtypeStruct(state_shape, F32),
                   jax.ShapeDtypeStruct(shift_shape, F32)]
                  + [jax.ShapeDtypeStruct(w.shape, BF16) for w in to_bf16],
        scratch_shapes=scratch,
        compiler_params=_params(sem),
        name="wkv_carried" if carried else "wkv_blocks",
    )(*args, *to_bf16)
    return (oa.reshape(rows, RW), st.reshape(n_seq, RW, HEAD), sh.reshape(n_seq, C_SHIFT), *casted)


def _gmlp_kernel(*refs, seq, want_vn, n_cast):
    zu_ref, zv_ref, lng_ref, lnb_ref, ws_ref, bias_ref = refs[:6]
    cast_in = refs[6:6 + n_cast]
    ub_ref = refs[6 + n_cast]
    vn_ref = refs[7 + n_cast] if want_vn else None
    cast_out = refs[len(refs) - n_cast:]
    for src, dst in zip(cast_in, cast_out):
        dst[...] = src[...].astype(BF16)
    reps = CHUNK // seq
    tr = lax.broadcasted_iota(jnp.int32, (CHUNK, CHUNK), 0)
    tc = lax.broadcasted_iota(jnp.int32, (CHUNK, CHUNK), 1)
    causal = (tc <= tr) & ((tr // seq) == (tc // seq))
    ws_g = []
    for gi in range(MLP_GROUPS):
        w_rows = ws_ref[gi]
        w_full = jnp.concatenate([w_rows if s == 0 else pltpu.roll(w_rows, s * seq, axis=1)
                                  for s in range(reps)], axis=0)
        ws_g.append(jnp.where(causal, w_full, 0.0).astype(BF16))
    bias = jnp.concatenate([bias_ref[...]] * reps, axis=0)
    for c in range(zu_ref.shape[0] // CHUNK):
        rows = slice(c * CHUNK, (c + 1) * CHUNK)
        u = jax.nn.gelu(zu_ref[rows, :])
        vg = jax.nn.gelu(zv_ref[rows, :])
        mu = jnp.mean(vg, -1, keepdims=True)
        var = jnp.mean(jnp.square(vg - mu), -1, keepdims=True)
        vn = (vg - mu) * lax.rsqrt(var + LN_EPS) * lng_ref[...] + lnb_ref[...]
        if vn_ref is not None:
            vn_ref[rows, :] = vn
        vb = vn.astype(BF16)
        cols = [_dot(ws_g[gi], vb[:, gi * MLP_GD:(gi + 1) * MLP_GD]) for gi in range(MLP_GROUPS)]
        mixed = jnp.concatenate(cols, axis=1) + bias
        ub_ref[rows, :] = (u * mixed).astype(BF16)


def _gmlp(z, ln_g, ln_b, ws, bias, seq, want_vn, to_bf16=()):
    rows, tr = z.shape[0], 4 * CHUNK
    steps = rows // tr
    n_out = 2 if want_vn else 1
    cast_specs = [pl.BlockSpec((w.shape[0] // steps, w.shape[1]), lambda i: (i, 0)) for w in to_bf16]
    assert all(w.shape[0] % (16 * steps) == 0 for w in to_bf16)
    return pl.pallas_call(
        functools.partial(_gmlp_kernel, seq=seq, want_vn=want_vn, n_cast=len(to_bf16)),
        grid=(steps,),
        in_specs=[pl.BlockSpec((tr, MLP_W), lambda i: (i, Z_U // MLP_W)),
                  pl.BlockSpec((tr, MLP_W), lambda i: (i, Z_V // MLP_W)),
                  pl.BlockSpec((1, MLP_W), lambda i: (0, 0)),
                  pl.BlockSpec((1, MLP_W), lambda i: (0, 0)),
                  pl.BlockSpec((MLP_GROUPS, seq, CHUNK), lambda i: (0, 0, 0)),
                  pl.BlockSpec((seq, MLP_W), lambda i: (0, 0))] + cast_specs,
        out_specs=[pl.BlockSpec((tr, MLP_W), lambda i: (i, 0)),
                   pl.BlockSpec((tr, MLP_W), lambda i: (i, 0))][:n_out] + cast_specs,
        out_shape=[jax.ShapeDtypeStruct((rows, MLP_W), BF16),
                   jax.ShapeDtypeStruct((rows, MLP_W), F32)][:n_out]
                  + [jax.ShapeDtypeStruct(w.shape, BF16) for w in to_bf16],
        compiler_params=_params(("arbitrary",)),
        name="gmlp",
    )(z, z, ln_g, ln_b, ws, bias, *to_bf16)


def _merge_kernel(oa_ref, ub_ref, zga_ref, zgb_ref, x_ref, gm_ref, sc_ref, sh_ref, g_ref,
                  wa_ref, wb_ref, wo_ref, x1_ref, h2_ref):
    ya = _dot(oa_ref[...], wa_ref[...])
    yb = _dot(ub_ref[...], wb_ref[...])
    merged = jax.nn.sigmoid(zga_ref[...]) * ya + jax.nn.sigmoid(zgb_ref[...]) * yb
    mix = _dot(merged.astype(BF16), wo_ref[...])
    tm = x_ref.shape[0]
    x1 = x_ref[...] + _mod_rows(gm_ref, tm) * mix
    x1_ref[...] = x1
    h2_ref[...] = (_rms(x1, g_ref[...]) * (1.0 + _mod_rows(sc_ref, tm)) + _mod_rows(sh_ref, tm)).astype(BF16)


def _merge(oa, ub, z, x, mod, g_ffn, wa, wb, wo, rows_per_seq):
    m, tm = x.shape[0], 256
    const = lambda shape: pl.BlockSpec(shape, lambda i: (0, 0), pipeline_mode=pl.Buffered(1))
    mspec = lambda k: _mod_spec(k, tm, rows_per_seq)
    rows = lambda w, c: pl.BlockSpec((tm, w), lambda i: (i, c))
    return pl.pallas_call(
        _merge_kernel,
        grid=(m // tm,),
        in_specs=[rows(RW, 0), rows(MLP_W, 0),
                  rows(D_MODEL, Z_GA // D_MODEL), rows(D_MODEL, Z_GB // D_MODEL), rows(D_MODEL, 0),
                  mspec(MOD_GATE_M), mspec(MOD_SCALE_F), mspec(MOD_SHIFT_F),
                  pl.BlockSpec((1, D_MODEL), lambda i: (0, 0)),
                  const((RW, D_MODEL)), const((MLP_W, D_MODEL)), const((D_MODEL, D_MODEL))],
        out_specs=[rows(D_MODEL, 0), rows(D_MODEL, 0)],
        out_shape=[jax.ShapeDtypeStruct((m, D_MODEL), F32), jax.ShapeDtypeStruct((m, D_MODEL), BF16)],
        compiler_params=_params(("arbitrary",)),
        name="merge",
    )(oa, ub, z, z, x, mod, mod, mod, g_ffn, wa, wb, wo)


def _ffn_kernel(*refs, fused):
    if fused:
        h_ref, x_ref, gf_ref, gfin_ref, wg_ref, wu_ref, wo_ref, y_ref = refs
    else:
        h_ref, wg_ref, wu_ref, wo_ref, y_ref = refs
    j = pl.program_id(1)

    @pl.when(j == 0)
    def _():
        y_ref[...] = jnp.zeros_like(y_ref)

    hb = h_ref[...]
    gt = _dot(hb, wg_ref[...].astype(BF16))
    up = _dot(hb, wu_ref[...].astype(BF16))
    act = gt * jax.nn.sigmoid(gt) * up
    y_ref[...] += _dot(act.astype(BF16), wo_ref[...].astype(BF16))

    if fused:
        @pl.when(j == pl.num_programs(1) - 1)
        def _():
            x2 = x_ref[...] + gf_ref[...] * y_ref[...]
            y_ref[...] = _rms(x2, gfin_ref[...])


def _final_kernel(f_ref, x_ref, gf_ref, gfin_ref, y_ref):
    tm = x_ref.shape[0]

    def body(rows, i):
        x2 = x_ref[rows, :] + _mod_chunk(gf_ref, i, tm) * f_ref[rows, :]
        y_ref[rows, :] = _rms(x2, gfin_ref[...])

    _for_row_chunks(tm, body)


def _ffn(h2, x1, mod, g_final, w_in, w_out, rows_per_seq):
    m, tm, tf = x1.shape[0], 1024, (512 if w_in.dtype == BF16 else 256)
    nf = D_FF // tf
    fused = rows_per_seq >= tm
    rows = pl.BlockSpec((tm, D_MODEL), lambda i, j: (i, 0))
    vec = pl.BlockSpec((1, D_MODEL), lambda i, j: (0, 0))
    wspecs = [pl.BlockSpec((D_MODEL, tf), lambda i, j: (0, j)),
              pl.BlockSpec((D_MODEL, tf), lambda i, j: (0, nf + j)),
              pl.BlockSpec((tf, D_MODEL), lambda i, j: (j, 0))]
    if fused:
        in_specs = [rows, pl.BlockSpec((tm, D_MODEL), lambda i, j: (i, 0), pipeline_mode=pl.Buffered(1)),
                    _mod_spec(MOD_GATE_F, tm, rows_per_seq), vec] + wspecs
        args = (h2, x1, mod, g_final, w_in, w_in, w_out)
    else:
        in_specs = [rows] + wspecs
        args = (h2, w_in, w_in, w_out)
    y = pl.pallas_call(
        functools.partial(_ffn_kernel, fused=fused),
        grid=(m // tm, nf),
        in_specs=in_specs,
        out_specs=rows,
        out_shape=jax.ShapeDtypeStruct((m, D_MODEL), F32),
        compiler_params=_params(("arbitrary", "arbitrary"), FFN_VMEM_LIMIT),
        name="ffn",
    )(*args)
    if fused:
        return y
    te = 512
    erows = pl.BlockSpec((te, D_MODEL), lambda i: (i, 0))
    return pl.pallas_call(
        _final_kernel,
        grid=(m // te,),
        in_specs=[erows, erows, _mod_spec(MOD_GATE_F, te, rows_per_seq),
                  pl.BlockSpec((1, D_MODEL), lambda i: (0, 0))],
        out_specs=erows,
        out_shape=jax.ShapeDtypeStruct((m, D_MODEL), F32),
        compiler_params=_params(("arbitrary",)),
        name="final",
    )(y, x1, mod, g_final)


def kernel(x_prompt, x_sample, state_wkv, state_shift, c_prompt, c_sample, w_ada, b_ada, norm_mix_g, w_in, mu_shift, w0, w_decay_up, a0, w_aaa_up, w_gate_up, k_k, k_a, r_k, gn_g, gn_b, ln_v_g, ln_v_b, w_spatial, b_spatial, w_branch_a, w_branch_b, w_out, norm_ffn_g, w_ffn_in, w_ffn_out, norm_final_g):
    assert w_ada.shape[0] == 1, "single layer"
    bp, tp, _ = x_prompt.shape
    bs, ts, _ = x_sample.shape
    assert tp % CHUNK == 0 and WKV_ROWS % ts == 0 and CHUNK % ts == 0

    c_all = jnp.concatenate([c_sample, c_prompt], axis=0)
    c_all = jnp.pad(c_all, ((0, -c_all.shape[0] % 8), (0, 0)))
    mod = _ada(c_all, w_ada[0], b_ada[0])
    mod_s = mod
    mod_p = mod[bs:bs + bp].reshape(bp, 1, 6 * D_MODEL)


    row = lambda a: a.reshape(1, -1)
    mu = mu_shift[0]
    vecs = [row(mu[:RW]), row(mu[RW:2 * RW]), row(mu[2 * RW:3 * RW]), row(mu[3 * RW:]),
            row(w0[0]), row(a0[0]), row(k_k[0]), row(k_a[0]), row(r_k[0]), row(gn_g[0]), row(gn_b[0])]
    zpad = lambda w, lo: jnp.pad(w, ((lo, LORA - lo - w.shape[0]), (0, 0))).astype(BF16)
    mats = [zpad(w_decay_up[0], 0), zpad(w_aaa_up[0], DECAY_LORA),
            zpad(w_gate_up[0], DECAY_LORA + AAA_LORA)]

    bias = jnp.repeat(b_spatial[0].T, MLP_GD, axis=1)
    ws = w_spatial[0]

    xp = x_prompt.reshape(bp * tp, D_MODEL)
    xs = x_sample.reshape(bs * ts, D_MODEL)
    g_mix, g_ffn, g_fin = row(norm_mix_g[0]), row(norm_ffn_g[0]), row(norm_final_g)
    lng, lnb = row(ln_v_g[0]), row(ln_v_b[0])

    zp = _inproj_norm(xp, mod_p, g_mix, w_in[0], tp)
    oa_p, st_p, shift_p, wfi, wfo = _wkv(zp, vecs, mats, bp, tp, WKV_ROWS, WKV_NCH, WKV_NSEQ_PROMPT,
                                         to_bf16=(w_ffn_in[0], w_ffn_out[0]))
    ub_p, wa_b, wb_b, wo_b = _gmlp(zp, lng, lnb, ws, bias, CHUNK, False,
                                   to_bf16=(w_branch_a[0], w_branch_b[0], w_out[0]))
    x1_p, h2_p = _merge(oa_p, ub_p, zp, xp, mod_p, g_ffn, wa_b, wb_b, wo_b, tp)
    y_p = _ffn(h2_p, x1_p, mod_p, g_fin, wfi, wfo, tp)

    zs = _inproj(_hnorm(xs, mod_s, g_mix, ts), w_in[0])
    oa_s, st_s, shift_s = _wkv(zs, vecs, mats, bs, ts, ts, 1, WKV_NSEQ, prev_rows=state_shift[0],
                               state_in=state_wkv[0].reshape(bs, RW, HEAD))
    ub_s, vn_s = _gmlp(zs, lng, lnb, ws, bias, ts, True)
    x1_s, h2_s = _merge(oa_s, ub_s, zs, xs, mod_s, g_ffn, wa_b, wb_b, wo_b, ts)
    y_s = _ffn(h2_s, x1_s, mod_s, g_fin, wfi, wfo, ts)

    return (y_p.reshape(bp, tp, D_MODEL),
            y_s.reshape(bs, ts, D_MODEL),
            st_p.reshape(1, bp, N_HEADS, HEAD, HEAD),
            shift_p.reshape(1, bp, C_SHIFT),
            st_s.reshape(1, bs, N_HEADS, HEAD, HEAD),
            shift_s.reshape(1, bs, C_SHIFT),
            vn_s.reshape(bs, ts, MLP_W)[None])
```

```python
import functools
import math

import jax
import jax.numpy as jnp
from jax import lax
from jax.experimental import pallas as pl
from jax.experimental.pallas import tpu as pltpu

F32 = jnp.float32
BF16 = jnp.bfloat16

D_MODEL = 2048
HEAD = 64
RW = D_MODEL // 2
N_HEADS = RW // HEAD
DECAY_LORA = D_MODEL // 32
AAA_LORA = D_MODEL // 32
GATE_LORA = D_MODEL // 16
LORA = DECAY_LORA + AAA_LORA + GATE_LORA
CHUNK = 128
MLP_W = D_MODEL // 2
MLP_GROUPS = 8
MLP_GD = MLP_W // MLP_GROUPS
D_FF = ((-(-8 * D_MODEL // 3)) + 255) // 256 * 256
C_SHIFT = 3 * RW + LORA
C_IN = C_SHIFT + 2 * MLP_W + 2 * D_MODEL
NORM_EPS = 1e-6
GN_EPS = 64e-5
LN_EPS = 1e-5

Z_GA, Z_GB, Z_U, Z_V, Z_R, Z_K, Z_VR, Z_L = 0, 2048, 4096, 5120, 6144, 7168, 8192, 9216
Z_TN = 512
Z_W = -(-C_IN // Z_TN) * Z_TN

GRP = 256
HPG = GRP // HEAD
N_GRP = RW // GRP
WKV_ROWS = 64
WKV_NCH = 2
WKV_NSEQ_PROMPT = 2
WKV_NSEQ = 2
VMEM_LIMIT = 56 * 1024 * 1024
FFN_VMEM_LIMIT = 60 * 1024 * 1024


def _dot(a, b):
    return jnp.dot(a, b, preferred_element_type=F32)


def _dot_nt(a, b):
    return lax.dot_general(a, b, (((1,), (1,)), ((), ())), preferred_element_type=F32)


def _dot_tn(a, b):
    return lax.dot_general(a, b, (((0,), (0,)), ((), ())), preferred_element_type=F32)


def _rms(x, g):
    return x * lax.rsqrt(jnp.mean(x * x, -1, keepdims=True) + NORM_EPS) * g


def _split2(x):
    hi = x.astype(BF16)
    lo = (x - hi.astype(F32)).astype(BF16)
    return hi, lo


def _split3(x):
    hi = x.astype(BF16)
    r1 = x - hi.astype(F32)
    mid = r1.astype(BF16)
    lo = (r1 - mid.astype(F32)).astype(BF16)
    return hi, mid, lo


def _params(sem, vmem_limit=VMEM_LIMIT):
    return pltpu.CompilerParams(dimension_semantics=sem, vmem_limit_bytes=vmem_limit)


def _ada_kernel(c_ref, w_ref, b_ref, o_ref, s_ref):
    @pl.when(pl.program_id(0) == 0)
    def _():
        c = c_ref[...]
        s_ref[...] = (c * jax.nn.sigmoid(c)).astype(BF16)

    o_ref[...] = _dot(s_ref[...], w_ref[...].astype(BF16)) + b_ref[...]


def _ada(c, w_ada, b_ada):
    m, n, tn = c.shape[0], w_ada.shape[1], 1024
    return pl.pallas_call(
        _ada_kernel,
        grid=(n // tn,),
        in_specs=[pl.BlockSpec((m, D_MODEL), lambda j: (0, 0)),
                  pl.BlockSpec((D_MODEL, tn), lambda j: (0, j)),
                  pl.BlockSpec((1, tn), lambda j: (0, j))],
        out_specs=pl.BlockSpec((m, tn), lambda j: (0, j)),
        out_shape=jax.ShapeDtypeStruct((m, n), F32),
        scratch_shapes=[pltpu.VMEM((m, D_MODEL), BF16)],
        compiler_params=_params(("arbitrary",)),
        name="ada",
    )(c, w_ada, b_ada.reshape(1, n))


MOD_SHIFT_M, MOD_SCALE_M, MOD_GATE_M, MOD_SHIFT_F, MOD_SCALE_F, MOD_GATE_F = range(6)


def _mod_spec(k, tm, rows_per_seq):
    if rows_per_seq >= tm:
        return pl.BlockSpec((None, 1, D_MODEL), lambda i, *_: (i * tm // rows_per_seq, 0, k))
    return pl.BlockSpec((tm // rows_per_seq, D_MODEL), lambda i, *_: (i, k))


def _seq_rows(ref, first, n_seq, per_seq):
    return jnp.concatenate([jnp.broadcast_to(ref[pl.ds(first + s, 1), :], (per_seq, ref.shape[1]))
                            for s in range(n_seq)], axis=0)


def _mod_rows(ref, tm):
    n_seq = ref.shape[0]
    return ref[...] if n_seq == 1 else _seq_rows(ref, 0, n_seq, tm // n_seq)


ROW_CHUNK = 16


def _for_row_chunks(n_rows, body):
    def step(i, carry):
        body(pl.ds(pl.multiple_of(i * ROW_CHUNK, ROW_CHUNK), ROW_CHUNK), i)
        return carry
    lax.fori_loop(0, n_rows // ROW_CHUNK, step, 0, unroll=8)


def _mod_chunk(ref, i, tm):
    if ref.shape[0] == 1:
        return ref[...]
    per_seq = tm // ref.shape[0]
    n = ROW_CHUNK // per_seq
    return _seq_rows(ref, i * n, n, per_seq)


ZB = 256
_N_RW_B, _N_MLP_B, _N_GATE_B = C_SHIFT // ZB, 2 * MLP_W // ZB, 2 * D_MODEL // ZB


def _w_in_block(zb):
    return jnp.where(zb < _N_GATE_B, zb + _N_RW_B + _N_MLP_B,
                     jnp.where(zb < _N_GATE_B + _N_MLP_B, zb - _N_GATE_B + _N_RW_B,
                               jnp.minimum(zb - _N_GATE_B - _N_MLP_B, _N_RW_B - 1)))


def _hnorm_kernel(x_ref, sc_ref, sh_ref, g_ref, h_ref):
    tm = x_ref.shape[0]

    def body(rows, i):
        h = _rms(x_ref[rows, :], g_ref[...]) * (1.0 + _mod_chunk(sc_ref, i, tm)) + _mod_chunk(sh_ref, i, tm)
        h_ref[rows, :] = h.astype(BF16)

    _for_row_chunks(tm, body)


def _hnorm(x, mod, g, rows_per_seq, n_rows=None):
    m, tm = (n_rows or x.shape[0]), 1024
    mspec = lambda k: _mod_spec(k, tm, rows_per_seq)
    return pl.pallas_call(
        _hnorm_kernel,
        grid=(m // tm,),
        in_specs=[pl.BlockSpec((tm, D_MODEL), lambda i: (i, 0)),
                  mspec(MOD_SCALE_M), mspec(MOD_SHIFT_M),
                  pl.BlockSpec((1, D_MODEL), lambda i: (0, 0))],
        out_specs=pl.BlockSpec((tm, D_MODEL), lambda i: (i, 0)),
        out_shape=jax.ShapeDtypeStruct((m, D_MODEL), BF16),
        compiler_params=_params(("arbitrary",)),
        name="hnorm",
    )(x, mod, mod, g)


def _inproj_kernel(h_ref, wlo_ref, whi_ref, o_ref):
    h = h_ref[...]
    o_ref[:, :ZB] = _dot(h, wlo_ref[...].astype(BF16))
    o_ref[:, ZB:] = _dot(h, whi_ref[...].astype(BF16))


def _inproj(h, w):
    m = h.shape[0]
    tm = min(m, 2048)
    assert Z_TN == 2 * ZB and m % tm == 0
    return pl.pallas_call(
        _inproj_kernel,
        grid=(m // tm, Z_W // Z_TN),
        in_specs=[pl.BlockSpec((tm, D_MODEL), lambda i, j: (i, 0)),
                  pl.BlockSpec((D_MODEL, ZB), lambda i, j: (0, _w_in_block(2 * j))),
                  pl.BlockSpec((D_MODEL, ZB), lambda i, j: (0, _w_in_block(2 * j + 1)))],
        out_specs=pl.BlockSpec((tm, Z_TN), lambda i, j: (i, j)),
        out_shape=jax.ShapeDtypeStruct((m, Z_W), F32),
        compiler_params=_params(("arbitrary", "arbitrary")),
        name="inproj",
    )(h, w, w)


INPROJ_TM = 2048
INPROJ_CHUNK = 128


def _inproj_norm_kernel(h0_ref, x_ref, sc_ref, sh_ref, g_ref, wlo_ref, whi_ref, o_ref, ha_ref, hb_ref):
    i, j = pl.program_id(0), pl.program_id(1)
    ch = x_ref.shape[0]
    n_chunks = ha_ref.shape[0] // ch

    @pl.when((i == 0) & (j == 0))
    def _():
        ha_ref[...] = h0_ref[...]

    def step(cur_ref, nxt_ref):
        row0 = pl.multiple_of(jnp.minimum(j, n_chunks - 1) * ch, ch)
        h = _rms(x_ref[...], g_ref[...]) * (1.0 + sc_ref[...]) + sh_ref[...]
        nxt_ref[pl.ds(row0, ch), :] = h.astype(BF16)
        h_cur = cur_ref[...]
        o_ref[:, :ZB] = _dot(h_cur, wlo_ref[...].astype(BF16))
        o_ref[:, ZB:] = _dot(h_cur, whi_ref[...].astype(BF16))

    @pl.when(i % 2 == 0)
    def _():
        step(ha_ref, hb_ref)

    @pl.when(i % 2 == 1)
    def _():
        step(hb_ref, ha_ref)


def _inproj_norm(x, mod, g, w, rows_per_seq):
    m, tm, ch = x.shape[0], INPROJ_TM, INPROJ_CHUNK
    nt, n_chunks, n_col = m // tm, tm // ch, Z_W // Z_TN
    assert m % tm == 0 and rows_per_seq % tm == 0 and n_chunks <= n_col and Z_TN == 2 * ZB
    h0 = _hnorm(x, mod, g, rows_per_seq, n_rows=tm)
    nxt = lambda i: jnp.minimum(i + 1, nt - 1)
    mspec = lambda k: pl.BlockSpec((None, 1, D_MODEL), lambda i, j: (nxt(i) * tm // rows_per_seq, 0, k))
    return pl.pallas_call(
        _inproj_norm_kernel,
        grid=(nt, n_col),
        in_specs=[pl.BlockSpec((tm, D_MODEL), lambda i, j: (0, 0), pipeline_mode=pl.Buffered(1)),
                  pl.BlockSpec((ch, D_MODEL), lambda i, j: (nxt(i) * n_chunks + jnp.minimum(j, n_chunks - 1), 0)),
                  mspec(MOD_SCALE_M), mspec(MOD_SHIFT_M),
                  pl.BlockSpec((1, D_MODEL), lambda i, j: (0, 0)),
                  pl.BlockSpec((D_MODEL, ZB), lambda i, j: (0, _w_in_block(2 * j))),
                  pl.BlockSpec((D_MODEL, ZB), lambda i, j: (0, _w_in_block(2 * j + 1)))],
        out_specs=pl.BlockSpec((tm, Z_TN), lambda i, j: (i, j)),
        out_shape=jax.ShapeDtypeStruct((m, Z_W), F32),
        scratch_shapes=[pltpu.VMEM((tm, D_MODEL), BF16), pltpu.VMEM((tm, D_MODEL), BF16)],
        compiler_params=_params(("arbitrary", "arbitrary")),
        name="inproj_norm",
    )(h0, x, mod, mod, g, w, w)


def _bd(w, mask):
    return jnp.where(mask, jnp.concatenate([w] * HPG, axis=0), 0.0).astype(BF16)


def _bd_heads(nat, mask):
    return jnp.where(mask, jnp.concatenate([nat] * HPG, axis=1), 0.0).astype(BF16)


def _diag_blocks(m):
    return jnp.concatenate([m[h * HEAD:(h + 1) * HEAD, h * HEAD:(h + 1) * HEAD] for h in range(HPG)],
                           axis=0)


def _wkv_kernel(*refs, lb, carried, nch, nseq, n_cast):
    R = WKV_ROWS
    n_in = 18 if carried else 23
    refs = list(refs)
    cast_in = [refs.pop(n_in) for _ in range(n_cast)]
    cast_out = [refs.pop(n_in + 3) for _ in range(n_cast)]
    for src, dst in zip(cast_in, cast_out):
        dst[...] = src[...].astype(BF16)
    if carried:
        (zr_ref, zk_ref, zv_ref, zl_ref,
         mur_ref, muk_ref, muv_ref, mul_ref,
         w0_ref, a0_ref, kkp_ref, kap_ref, rkp_ref, gng_ref, gnb_ref,
         wd_ref, wa_ref, wg_ref,
         oa_ref, so_ref, sho_ref,
         s_ref, cr_ref, ck_ref, cv_ref, cl_ref) = refs
        carry = {id(zr_ref): cr_ref, id(zk_ref): ck_ref, id(zv_ref): cv_ref, id(zl_ref): cl_ref}
    else:
        (zr_ref, zk_ref, zv_ref, zl_ref,
         pr_ref, pk_ref, pv_ref, pl_ref, si_ref,
         mur_ref, muk_ref, muv_ref, mul_ref,
         w0_ref, a0_ref, kkp_ref, kap_ref, rkp_ref, gng_ref, gnb_ref,
         wd_ref, wa_ref, wg_ref,
         oa_ref, so_ref, sho_ref) = refs
        prev_of = {id(zr_ref): pr_ref, id(zk_ref): pk_ref, id(zv_ref): pv_ref, id(zl_ref): pl_ref}
    nblk = R // lb
    z_refs = (zr_ref, zk_ref, zv_ref, zl_ref)
    mu_refs = (mur_ref, muk_ref, muv_ref, mul_ref)

    if carried:
        @pl.when(pl.program_id(1) == 0)
        def _():
            s_ref[...] = jnp.zeros_like(s_ref)
            for ref in (cr_ref, ck_ref, cv_ref, cl_ref):
                ref[...] = jnp.zeros_like(ref)

    row1 = lax.broadcasted_iota(jnp.int32, (R, 1), 0)
    first = (row1 % lb) == 0
    ri = lax.broadcasted_iota(jnp.int32, (GRP, GRP), 0)
    ci = lax.broadcasted_iota(jnp.int32, (GRP, GRP), 1)
    bdm = (ri // HEAD) == (ci // HEAD)
    ones_bd = jnp.where(bdm, 1.0, 0.0).astype(BF16)
    tr = lax.broadcasted_iota(jnp.int32, (R, R), 0)
    tc = lax.broadcasted_iota(jnp.int32, (R, R), 1)
    same = (tr // lb) == (tc // lb)
    incl01 = jnp.where(same & (tc <= tr), 1.0, 0.0).astype(BF16)
    same01 = jnp.where(same, 1.0, 0.0).astype(BF16)
    wr = lax.broadcasted_iota(jnp.int32, (R, GRP), 0)
    wc = lax.broadcasted_iota(jnp.int32, (R, GRP), 1) % HEAD
    wsame = (wr // lb) == (wc // lb)
    strict_w = wsame & (wc < wr)
    incl_w = wsame & (wc <= wr)
    eye_w = jnp.where(wc == wr, 1.0, 0.0)
    sls = [slice(gi * GRP, (gi + 1) * GRP) for gi in range(N_GRP)]
    blks = [slice(b * lb, (b + 1) * lb) for b in range(nblk)]
    units = [(s, gi) for s in range(nseq) for gi in range(N_GRP)]
    uids = range(len(units))

    def seg_sums(xs):
        n = xs[0].shape[0]
        parts = []
        for x in xs:
            parts.extend(_split2(x))
        stacked = jnp.concatenate(parts, axis=0)
        cols = [_dot(stacked[:, sl], ones_bd) for sl in sls]
        full = jnp.concatenate(cols, axis=1)
        return [full[2 * i * n:(2 * i + 1) * n] + full[(2 * i + 1) * n:(2 * i + 2) * n]
                for i in range(len(xs))]

    def cum(m01, x):
        hi, mid, lo = _split3(x)
        return _dot(m01, hi) + _dot(m01, mid) + _dot(m01, lo)

    def prep(c, outs):
        rows = slice(c * R, (c + 1) * R)

        def shift(z_ref, mu_ref):
            parts = []
            for s in range(nseq):
                z = z_ref[s, rows, :]
                if not carried:
                    prev0 = _seq_rows(prev_of[id(z_ref)].at[s], c * nblk, nblk, lb)
                elif c == 0:
                    prev0 = carry[id(z_ref)][s]
                else:
                    prev0 = z_ref[s, c * R - 1:c * R, :]
                prev = jnp.where(first, prev0, pltpu.roll(z, 1, axis=0))
                parts.append(z + (prev - z) * mu_ref[...])
            return jnp.concatenate(parts, axis=0)

        per_stream = lambda x: [x[s * R:(s + 1) * R] for s in range(nseq)]
        r, k, v, l = [shift(z_ref, mu_ref) for z_ref, mu_ref in zip(z_refs, mu_refs)]
        w_raw = w0_ref[...] + _dot(jnp.tanh(l).astype(BF16), wd_ref[...])
        a_raw = a0_ref[...] + _dot(l.astype(BF16), wa_ref[...])
        g = _dot(jax.nn.sigmoid(l).astype(BF16), wg_ref[...])
        yield
        logw = jax.nn.sigmoid(w_raw) * (-math.exp(-0.5))
        a = jax.nn.sigmoid(a_raw)
        kk = k * kkp_ref[...]
        k2 = k * (1.0 + (a - 1.0) * kap_ref[...])
        ss, rk = seg_sums([kk * kk, r * k2 * rkp_ref[...]])
        cl_s = [cum(incl01, lw) for lw in per_stream(logw)]
        if nblk == 1:
            ce_s = [jnp.broadcast_to(x[R - 1:R, :], (R, RW)) for x in cl_s]
        else:
            ce_s = [cum(same01, lw) for lw in per_stream(logw)]
        yield
        cl = jnp.concatenate(cl_s, axis=0)
        cl_end = jnp.concatenate(ce_s, axis=0)
        kk = kk / jnp.maximum(jnp.sqrt(ss), 1e-12)
        av = -kk
        bv = kk * a
        e_neg = jnp.exp(-cl)
        e_end = jnp.exp(cl_end - cl)
        full = dict(v=v, g=g, bonus=rk * v,
                    at=av * jnp.exp(cl - logw), rt=r * jnp.exp(cl),
                    bt=bv * e_neg, kt=k2 * e_neg, bh=bv * e_end, kh=k2 * e_end)
        for name, x in full.items():
            for s, x_s in enumerate(per_stream(x)):
                outs[s][name] = x_s
        for s in range(nseq):
            outs[s]["p_end"] = jnp.exp(cl_s[s][R - 1:R, :] if nblk == 1 else ce_s[s])

    def main(c, ps):
        rows_c = slice(c * R, (c + 1) * R)
        col = lambda name, ui: ps[units[ui][0]][name][:, sls[units[ui][1]]]
        v_g = [col("v", ui) for ui in uids]
        lhs = [jnp.concatenate([col("at", ui), col("rt", ui)], axis=0).astype(BF16) for ui in uids]
        o1 = [_dot_nt(lhs[ui], jnp.concatenate([_bd(col("bt", ui), bdm), _bd(col("kt", ui), bdm)], axis=0))
              for ui in uids]
        yield
        w_ab = [jnp.where(strict_w, o[:R, :GRP], 0.0) for o in o1]
        w_ak = [jnp.where(strict_w, o[:R, GRP:], 0.0) for o in o1]
        w_rb = [jnp.where(incl_w, o[R:, :GRP], 0.0) for o in o1]
        w_rk = [jnp.where(incl_w, o[R:, GRP:], 0.0) for o in o1]

        def state_part(ui):
            s, gi = units[ui]
            if carried:
                o2 = _dot_nt(lhs[ui], s_ref[s, gi].astype(BF16))
                return o2[:R], o2[R:]
            at_u, rt_u = col("at", ui), col("rt", ui)
            a_rows, r_rows = [], []
            for b, rows in enumerate(blks):
                lhs_b = jnp.concatenate([at_u[rows], rt_u[rows]], axis=0).astype(BF16)
                o2 = _dot_nt(lhs_b, _bd_heads(si_ref[s, c * nblk + b, sls[gi], :], bdm))
                a_rows.append(o2[:lb])
                r_rows.append(o2[lb:])
            return jnp.concatenate(a_rows, axis=0), jnp.concatenate(r_rows, axis=0)

        x_w = list(w_ab)
        t_w = [eye_w + x for x in x_w]
        n_lvl = max(1, int(math.log2(lb)))
        a_s = r_s = o3 = None
        for j in range(n_lvl):
            for ui in uids:
                y_bd = _bd(x_w[ui], bdm)
                if j == 0:
                    x_w[ui] = _dot(x_w[ui].astype(BF16), y_bd)
                elif j == n_lvl - 1:
                    t_w[ui] = t_w[ui] + _dot(t_w[ui].astype(BF16), y_bd)
                else:
                    res = _dot(jnp.concatenate([t_w[ui], x_w[ui]], axis=0).astype(BF16), y_bd)
                    t_w[ui] = t_w[ui] + res[:R]
                    x_w[ui] = res[R:]
            yield
            if j == 0:
                parts = [state_part(ui) for ui in uids]
                a_s, r_s = [q[0] for q in parts], [q[1] for q in parts]
                yield
            elif j == 1:
                o3 = [_dot(jnp.concatenate([w_ak[ui], w_rk[ui]], axis=0).astype(BF16), _bd(v_g[ui], bdm))
                      for ui in uids]
                yield

        u = [_dot(t_w[ui].astype(BF16), _bd(a_s[ui] + o3[ui][:R], bdm)) for ui in uids]
        yield
        o_parts = [r_s[ui] + o3[ui][R:] + _dot(w_rb[ui].astype(BF16), _bd(u[ui], bdm)) for ui in uids]
        yield

        for ui in uids:
            s, gi = units[ui]
            sl = sls[gi]
            bh_u, kh_u, p_end = col("bh", ui), col("kh", ui), ps[s]["p_end"][:, sl]
            if carried:
                upd = _dot_tn(jnp.concatenate([u[ui], v_g[ui]], axis=0).astype(BF16),
                              jnp.concatenate([bh_u, kh_u], axis=0).astype(BF16))
                s_new = jnp.where(bdm, s_ref[s, gi] * p_end + upd, 0.0)
                s_ref[s, gi] = s_new
                if c == nch - 1:
                    so_ref[s, sl, :] = _diag_blocks(s_new)
            else:
                for b, rows in enumerate(blks):
                    upd = _dot_tn(jnp.concatenate([u[ui][rows], v_g[ui][rows]], axis=0).astype(BF16),
                                  jnp.concatenate([bh_u[rows], kh_u[rows]], axis=0).astype(BF16))
                    p_b = p_end[b * lb:b * lb + 1]
                    p_nat = jnp.concatenate(
                        [jnp.broadcast_to(p_b[:, h * HEAD:(h + 1) * HEAD], (HEAD, HEAD)) for h in range(HPG)],
                        axis=0)
                    so_ref[s, c * nblk + b, sl, :] = (si_ref[s, c * nblk + b, sl, :] * p_nat
                                                      + _diag_blocks(upd))
        yield

        o = [jnp.concatenate(o_parts[s * N_GRP:(s + 1) * N_GRP], axis=1) for s in range(nseq)]
        mu = seg_sums(o)
        yield
        dlt = [o[s] - mu[s] * (1.0 / HEAD) for s in range(nseq)]
        var = seg_sums([d * d for d in dlt])
        for s in range(nseq):
            on = dlt[s] * lax.rsqrt(var[s] * (1.0 / HEAD) + GN_EPS) * gng_ref[...] + gnb_ref[...]
            oa_ref[s, rows_c, :] = ((on + ps[s]["bonus"]) * ps[s]["g"]).astype(BF16)

    def run(gen):
        for _ in gen:
            pass

    cur = [{} for _ in range(nseq)]
    run(prep(0, cur))
    for c in range(nch):
        nxt = [{} for _ in range(nseq)]
        side = prep(c + 1, nxt) if c + 1 < nch else iter(())
        for tick, _ in enumerate(main(c, cur)):
            if tick % 3 == 1:
                next(side, None)
        run(side)
        cur = nxt

    n_rows = nch * R
    for s in range(nseq):
        if carried:
            last = lambda z_ref: z_ref[s, n_rows - 1:n_rows, :]
            for z_ref in z_refs:
                carry[id(z_ref)][s] = last(z_ref)
        else:
            last = lambda z_ref: jnp.concatenate(
                [z_ref[s, b * lb + lb - 1:b * lb + lb, :] for b in range(nch * nblk)], axis=0)
        sho_ref[s, :, 0:RW] = last(zr_ref)
        sho_ref[s, :, RW:2 * RW] = last(zk_ref)
        sho_ref[s, :, 2 * RW:3 * RW] = last(zv_ref)
        sho_ref[s, :, 3 * RW:] = last(zl_ref)


def _wkv(z, vecs, mats, n_seq, seq_len, lb, nch, nseq, prev_rows=None, state_in=None, to_bf16=()):
    R = nch * WKV_ROWS
    carried = prev_rows is None
    rows = z.shape[0]
    n_streams = n_seq if carried else nseq
    stream_rows = rows // n_streams
    assert n_streams % nseq == 0 and stream_rows % R == 0 and rows == n_streams * stream_rows
    z3 = z.reshape(n_streams, stream_rows, Z_W)
    if carried:
        grid = (n_streams // nseq, stream_rows // R)
        rmap = lambda c: (lambda b, t: (b, t, c))
        cmap = lambda b, t: (0, 0)
        smap = lambda b, t: (b, 0, 0)
        sem = ("arbitrary", "arbitrary")
    else:
        grid = (stream_rows // R,)
        rmap = lambda c: (lambda i: (0, i, c))
        cmap = lambda i: (0, 0)
        sem = ("arbitrary",)
        blk_per_step = R // lb
        seq_per_stream = n_seq // nseq

    in_specs = [pl.BlockSpec((nseq, R, RW), rmap(Z_R // RW)),
                pl.BlockSpec((nseq, R, RW), rmap(Z_K // RW)),
                pl.BlockSpec((nseq, R, RW), rmap(Z_VR // RW)),
                pl.BlockSpec((nseq, R, LORA), rmap(Z_L // LORA))]
    args = [z3, z3, z3, z3]
    if not carried:
        p3 = prev_rows.reshape(nseq, seq_per_stream, C_SHIFT)
        st4 = state_in.reshape(nseq, seq_per_stream, RW, HEAD)
        in_specs += [pl.BlockSpec((nseq, blk_per_step, RW), rmap(0)),
                     pl.BlockSpec((nseq, blk_per_step, RW), rmap(1)),
                     pl.BlockSpec((nseq, blk_per_step, RW), rmap(2)),
                     pl.BlockSpec((nseq, blk_per_step, LORA), rmap(3 * RW // LORA)),
                     pl.BlockSpec((nseq, blk_per_step, RW, HEAD), lambda i: (0, i, 0, 0))]
        args += [p3] * 4 + [st4]
    for a in vecs + mats:
        in_specs.append(pl.BlockSpec(a.shape, cmap))
        args.append(a)

    if carried:
        scratch = [pltpu.VMEM((nseq, N_GRP, GRP, GRP), F32), pltpu.VMEM((nseq, 1, RW), F32),
                   pltpu.VMEM((nseq, 1, RW), F32), pltpu.VMEM((nseq, 1, RW), F32),
                   pltpu.VMEM((nseq, 1, LORA), F32)]
        state_spec = pl.BlockSpec((nseq, RW, HEAD), smap)
        state_shape = (n_seq, RW, HEAD)
        shift_spec = pl.BlockSpec((nseq, 1, C_SHIFT), smap)
        shift_shape = (n_seq, 1, C_SHIFT)
    else:
        scratch = []
        state_spec = pl.BlockSpec((nseq, blk_per_step, RW, HEAD), lambda i: (0, i, 0, 0))
        state_shape = (nseq, seq_per_stream, RW, HEAD)
        shift_spec = pl.BlockSpec((nseq, blk_per_step, C_SHIFT), lambda i: (0, i, 0))
        shift_shape = (nseq, seq_per_stream, C_SHIFT)
    n_steps = math.prod(grid)
    step = (lambda b, t: b * grid[1] + t) if carried else (lambda i: i)
    cast_specs = [pl.BlockSpec((w.shape[0] // n_steps, w.shape[1]), lambda *g: (step(*g), 0)) for w in to_bf16]
    assert all(w.shape[0] % (16 * n_steps) == 0 for w in to_bf16)
    oa, st, sh, *casted = pl.pallas_call(
        functools.partial(_wkv_kernel, lb=lb, carried=carried, nch=nch, nseq=nseq, n_cast=len(to_bf16)),
        grid=grid,
        in_specs=in_specs + cast_specs,
        out_specs=[pl.BlockSpec((nseq, R, RW), rmap(0)), state_spec, shift_spec] + cast_specs,
        out_shape=[jax.ShapeDtypeStruct((n_streams, stream_rows, RW), BF16),
                   jax.ShapeDtypeStruct(state_shape, F32),
                   jax.ShapeDtypeStruct(shift_shape, F32)]
                  + [jax.ShapeDtypeStruct(w.shape, BF16) for w in to_bf16],
        scratch_shapes=scratch,
        compiler_params=_params(sem),
        name="wkv_carried" if carried else "wkv_blocks",
    )(*args, *to_bf16)
    return (oa.reshape(rows, RW), st.reshape(n_seq, RW, HEAD), sh.reshape(n_seq, C_SHIFT), *casted)


def _gmlp_kernel(*refs, seq, want_vn, n_cast):
    zu_ref, zv_ref, lng_ref, lnb_ref, ws_ref, bias_ref = refs[:6]
    cast_in = refs[6:6 + n_cast]
    ub_ref = refs[6 + n_cast]
    vn_ref = refs[7 + n_cast] if want_vn else None
    cast_out = refs[len(refs) - n_cast:]
    for src, dst in zip(cast_in, cast_out):
        dst[...] = src[...].astype(BF16)
    reps = CHUNK // seq
    tr = lax.broadcasted_iota(jnp.int32, (CHUNK, CHUNK), 0)
    tc = lax.broadcasted_iota(jnp.int32, (CHUNK, CHUNK), 1)
    causal = (tc <= tr) & ((tr // seq) == (tc // seq))
    ws_g = []
    for gi in range(MLP_GROUPS):
        w_rows = ws_ref[gi]
        w_full = jnp.concatenate([w_rows if s == 0 else pltpu.roll(w_rows, s * seq, axis=1)
                                  for s in range(reps)], axis=0)
        ws_g.append(jnp.where(causal, w_full, 0.0).astype(BF16))
    bias = jnp.concatenate([bias_ref[...]] * reps, axis=0)
    for c in range(zu_ref.shape[0] // CHUNK):
        rows = slice(c * CHUNK, (c + 1) * CHUNK)
        u = jax.nn.gelu(zu_ref[rows, :])
        vg = jax.nn.gelu(zv_ref[rows, :])
        mu = jnp.mean(vg, -1, keepdims=True)
        var = jnp.mean(jnp.square(vg - mu), -1, keepdims=True)
        vn = (vg - mu) * lax.rsqrt(var + LN_EPS) * lng_ref[...] + lnb_ref[...]
        if vn_ref is not None:
            vn_ref[rows, :] = vn
        vb = vn.astype(BF16)
        cols = [_dot(ws_g[gi], vb[:, gi * MLP_GD:(gi + 1) * MLP_GD]) for gi in range(MLP_GROUPS)]
        mixed = jnp.concatenate(cols, axis=1) + bias
        ub_ref[rows, :] = (u * mixed).astype(BF16)


def _gmlp(z, ln_g, ln_b, ws, bias, seq, want_vn, to_bf16=()):
    rows, tr = z.shape[0], 4 * CHUNK
    steps = rows // tr
    n_out = 2 if want_vn else 1
    cast_specs = [pl.BlockSpec((w.shape[0] // steps, w.shape[1]), lambda i: (i, 0)) for w in to_bf16]
    assert all(w.shape[0] % (16 * steps) == 0 for w in to_bf16)
    return pl.pallas_call(
        functools.partial(_gmlp_kernel, seq=seq, want_vn=want_vn, n_cast=len(to_bf16)),
        grid=(steps,),
        in_specs=[pl.BlockSpec((tr, MLP_W), lambda i: (i, Z_U // MLP_W)),
                  pl.BlockSpec((tr, MLP_W), lambda i: (i, Z_V // MLP_W)),
                  pl.BlockSpec((1, MLP_W), lambda i: (0, 0)),
                  pl.BlockSpec((1, MLP_W), lambda i: (0, 0)),
                  pl.BlockSpec((MLP_GROUPS, seq, CHUNK), lambda i: (0, 0, 0)),
                  pl.BlockSpec((seq, MLP_W), lambda i: (0, 0))] + cast_specs,
        out_specs=[pl.BlockSpec((tr, MLP_W), lambda i: (i, 0)),
                   pl.BlockSpec((tr, MLP_W), lambda i: (i, 0))][:n_out] + cast_specs,
        out_shape=[jax.ShapeDtypeStruct((rows, MLP_W), BF16),
                   jax.ShapeDtypeStruct((rows, MLP_W), F32)][:n_out]
                  + [jax.ShapeDtypeStruct(w.shape, BF16) for w in to_bf16],
        compiler_params=_params(("arbitrary",)),
        name="gmlp",
    )(z, z, ln_g, ln_b, ws, bias, *to_bf16)


def _merge_kernel(oa_ref, ub_ref, zga_ref, zgb_ref, x_ref, gm_ref, sc_ref, sh_ref, g_ref,
                  wa_ref, wb_ref, wo_ref, x1_ref, h2_ref):
    ya = _dot(oa_ref[...], wa_ref[...])
    yb = _dot(ub_ref[...], wb_ref[...])
    merged = jax.nn.sigmoid(zga_ref[...]) * ya + jax.nn.sigmoid(zgb_ref[...]) * yb
    mix = _dot(merged.astype(BF16), wo_ref[...])
    tm = x_ref.shape[0]
    x1 = x_ref[...] + _mod_rows(gm_ref, tm) * mix
    x1_ref[...] = x1
    h2_ref[...] = (_rms(x1, g_ref[...]) * (1.0 + _mod_rows(sc_ref, tm)) + _mod_rows(sh_ref, tm)).astype(BF16)


def _merge(oa, ub, z, x, mod, g_ffn, wa, wb, wo, rows_per_seq):
    m, tm = x.shape[0], 256
    const = lambda shape: pl.BlockSpec(shape, lambda i: (0, 0), pipeline_mode=pl.Buffered(1))
    mspec = lambda k: _mod_spec(k, tm, rows_per_seq)
    rows = lambda w, c: pl.BlockSpec((tm, w), lambda i: (i, c))
    return pl.pallas_call(
        _merge_kernel,
        grid=(m // tm,),
        in_specs=[rows(RW, 0), rows(MLP_W, 0),
                  rows(D_MODEL, Z_GA // D_MODEL), rows(D_MODEL, Z_GB // D_MODEL), rows(D_MODEL, 0),
                  mspec(MOD_GATE_M), mspec(MOD_SCALE_F), mspec(MOD_SHIFT_F),
                  pl.BlockSpec((1, D_MODEL), lambda i: (0, 0)),
                  const((RW, D_MODEL)), const((MLP_W, D_MODEL)), const((D_MODEL, D_MODEL))],
        out_specs=[rows(D_MODEL, 0), rows(D_MODEL, 0)],
        out_shape=[jax.ShapeDtypeStruct((m, D_MODEL), F32), jax.ShapeDtypeStruct((m, D_MODEL), BF16)],
        compiler_params=_params(("arbitrary",)),
        name="merge",
    )(oa, ub, z, z, x, mod, mod, mod, g_ffn, wa, wb, wo)


def _ffn_kernel(*refs, fused):
    if fused:
        h_ref, x_ref, gf_ref, gfin_ref, wg_ref, wu_ref, wo_ref, y_ref = refs
    else:
        h_ref, wg_ref, wu_ref, wo_ref, y_ref = refs
    j = pl.program_id(1)

    @pl.when(j == 0)
    def _():
        y_ref[...] = jnp.zeros_like(y_ref)

    hb = h_ref[...]
    gt = _dot(hb, wg_ref[...].astype(BF16))
    up = _dot(hb, wu_ref[...].astype(BF16))
    act = gt * jax.nn.sigmoid(gt) * up
    y_ref[...] += _dot(act.astype(BF16), wo_ref[...].astype(BF16))

    if fused:
        @pl.when(j == pl.num_programs(1) - 1)
        def _():
            x2 = x_ref[...] + gf_ref[...] * y_ref[...]
            y_ref[...] = _rms(x2, gfin_ref[...])


def _final_kernel(f_ref, x_ref, gf_ref, gfin_ref, y_ref):
    tm = x_ref.shape[0]

    def body(rows, i):
        x2 = x_ref[rows, :] + _mod_chunk(gf_ref, i, tm) * f_ref[rows, :]
        y_ref[rows, :] = _rms(x2, gfin_ref[...])

    _for_row_chunks(tm, body)


def _ffn(h2, x1, mod, g_final, w_in, w_out, rows_per_seq):
    m, tm, tf = x1.shape[0], 1024, (512 if w_in.dtype == BF16 else 256)
    nf = D_FF // tf
    fused = rows_per_seq >= tm
    rows = pl.BlockSpec((tm, D_MODEL), lambda i, j: (i, 0))
    vec = pl.BlockSpec((1, D_MODEL), lambda i, j: (0, 0))
    wspecs = [pl.BlockSpec((D_MODEL, tf), lambda i, j: (0, j)),
              pl.BlockSpec((D_MODEL, tf), lambda i, j: (0, nf + j)),
              pl.BlockSpec((tf, D_MODEL), lambda i, j: (j, 0))]
    if fused:
        in_specs = [rows, pl.BlockSpec((tm, D_MODEL), lambda i, j: (i, 0), pipeline_mode=pl.Buffered(1)),
                    _mod_spec(MOD_GATE_F, tm, rows_per_seq), vec] + wspecs
        args = (h2, x1, mod, g_final, w_in, w_in, w_out)
    else:
        in_specs = [rows] + wspecs
        args = (h2, w_in, w_in, w_out)
    y = pl.pallas_call(
        functools.partial(_ffn_kernel, fused=fused),
        grid=(m // tm, nf),
        in_specs=in_specs,
        out_specs=rows,
        out_shape=jax.ShapeDtypeStruct((m, D_MODEL), F32),
        compiler_params=_params(("arbitrary", "arbitrary"), FFN_VMEM_LIMIT),
        name="ffn",
    )(*args)
    if fused:
        return y
    te = 512
    erows = pl.BlockSpec((te, D_MODEL), lambda i: (i, 0))
    return pl.pallas_call(
        _final_kernel,
        grid=(m // te,),
        in_specs=[erows, erows, _mod_spec(MOD_GATE_F, te, rows_per_seq),
                  pl.BlockSpec((1, D_MODEL), lambda i: (0, 0))],
        out_specs=erows,
        out_shape=jax.ShapeDtypeStruct((m, D_MODEL), F32),
        compiler_params=_params(("arbitrary",)),
        name="final",
    )(y, x1, mod, g_final)


def kernel(x_prompt, x_sample, state_wkv, state_shift, c_prompt, c_sample, w_ada, b_ada, norm_mix_g, w_in, mu_shift, w0, w_decay_up, a0, w_aaa_up, w_gate_up, k_k, k_a, r_k, gn_g, gn_b, ln_v_g, ln_v_b, w_spatial, b_spatial, w_branch_a, w_branch_b, w_out, norm_ffn_g, w_ffn_in, w_ffn_out, norm_final_g):
    assert w_ada.shape[0] == 1, "single layer"
    bp, tp, _ = x_prompt.shape
    bs, ts, _ = x_sample.shape
    assert tp % CHUNK == 0 and WKV_ROWS % ts == 0 and CHUNK % ts == 0

    c_all = jnp.concatenate([c_sample, c_prompt], axis=0)
    c_all = jnp.pad(c_all, ((0, -c_all.shape[0] % 8), (0, 0)))
    mod = _ada(c_all, w_ada[0], b_ada[0])
    mod_s = mod
    mod_p = mod[bs:bs + bp].reshape(bp, 1, 6 * D_MODEL)


    row = lambda a: a.reshape(1, -1)
    mu = mu_shift[0]
    vecs = [row(mu[:RW]), row(mu[RW:2 * RW]), row(mu[2 * RW:3 * RW]), row(mu[3 * RW:]),
            row(w0[0]), row(a0[0]), row(k_k[0]), row(k_a[0]), row(r_k[0]), row(gn_g[0]), row(gn_b[0])]
    zpad = lambda w, lo: jnp.pad(w, ((lo, LORA - lo - w.shape[0]), (0, 0))).astype(BF16)
    mats = [zpad(w_decay_up[0], 0), zpad(w_aaa_up[0], DECAY_LORA),
            zpad(w_gate_up[0], DECAY_LORA + AAA_LORA)]

    bias = jnp.repeat(b_spatial[0].T, MLP_GD, axis=1)
    ws = w_spatial[0]

    xp = x_prompt.reshape(bp * tp, D_MODEL)
    xs = x_sample.reshape(bs * ts, D_MODEL)
    g_mix, g_ffn, g_fin = row(norm_mix_g[0]), row(norm_ffn_g[0]), row(norm_final_g)
    lng, lnb = row(ln_v_g[0]), row(ln_v_b[0])

    zp = _inproj_norm(xp, mod_p, g_mix, w_in[0], tp)
    f32_weights = (w_ffn_in[0], w_ffn_out[0], w_branch_a[0], w_branch_b[0], w_out[0], w_in[0])
    oa_p, st_p, shift_p, wfi, wfo, wa_b, wb_b, wo_b, w_in_b = _wkv(
        zp, vecs, mats, bp, tp, WKV_ROWS, WKV_NCH, WKV_NSEQ_PROMPT, to_bf16=f32_weights)
    (ub_p,) = _gmlp(zp, lng, lnb, ws, bias, CHUNK, False)
    x1_p, h2_p = _merge(oa_p, ub_p, zp, xp, mod_p, g_ffn, wa_b, wb_b, wo_b, tp)
    y_p = _ffn(h2_p, x1_p, mod_p, g_fin, wfi, wfo, tp)

    zs = _inproj(_hnorm(xs, mod_s, g_mix, ts), w_in_b)
    oa_s, st_s, shift_s = _wkv(zs, vecs, mats, bs, ts, ts, 1, WKV_NSEQ, prev_rows=state_shift[0],
                               state_in=state_wkv[0].reshape(bs, RW, HEAD))
    ub_s, vn_s = _gmlp(zs, lng, lnb, ws, bias, ts, True)
    x1_s, h2_s = _merge(oa_s, ub_s, zs, xs, mod_s, g_ffn, wa_b, wb_b, wo_b, ts)
    y_s = _ffn(h2_s, x1_s, mod_s, g_fin, wfi, wfo, ts)

    return (y_p.reshape(bp, tp, D_MODEL),
            y_s.reshape(bs, ts, D_MODEL),
            st_p.reshape(1, bp, N_HEADS, HEAD, HEAD),
            shift_p.reshape(1, bp, C_SHIFT),
            st_s.reshape(1, bs, N_HEADS, HEAD, HEAD),
            shift_s.reshape(1, bs, C_SHIFT),
            vn_s.reshape(bs, ts, MLP_W)[None])
```

```python
import functools
import math

import jax
import jax.numpy as jnp
from jax import lax
from jax.experimental import pallas as pl
from jax.experimental.pallas import tpu as pltpu

F32 = jnp.float32
BF16 = jnp.bfloat16

D_MODEL = 2048
HEAD = 64
RW = D_MODEL // 2
N_HEADS = RW // HEAD
DECAY_LORA = D_MODEL // 32
AAA_LORA = D_MODEL // 32
GATE_LORA = D_MODEL // 16
LORA = DECAY_LORA + AAA_LORA + GATE_LORA
CHUNK = 128
MLP_W = D_MODEL // 2
MLP_GROUPS = 8
MLP_GD = MLP_W // MLP_GROUPS
D_FF = ((-(-8 * D_MODEL // 3)) + 255) // 256 * 256
C_SHIFT = 3 * RW + LORA
C_IN = C_SHIFT + 2 * MLP_W + 2 * D_MODEL
NORM_EPS = 1e-6
GN_EPS = 64e-5
LN_EPS = 1e-5

Z_GA, Z_GB, Z_U, Z_V, Z_R, Z_K, Z_VR, Z_L = 0, 2048, 4096, 5120, 6144, 7168, 8192, 9216
Z_TN = 512
Z_W = -(-C_IN // Z_TN) * Z_TN

GRP = 256
HPG = GRP // HEAD
N_GRP = RW // GRP
WKV_ROWS = 64
WKV_NCH = 1
WKV_NSEQ_PROMPT = 4
WKV_NSEQ = 2
VMEM_LIMIT = 56 * 1024 * 1024
FFN_VMEM_LIMIT = 60 * 1024 * 1024


def _dot(a, b):
    return jnp.dot(a, b, preferred_element_type=F32)


def _dot_nt(a, b):
    return lax.dot_general(a, b, (((1,), (1,)), ((), ())), preferred_element_type=F32)


def _dot_tn(a, b):
    return lax.dot_general(a, b, (((0,), (0,)), ((), ())), preferred_element_type=F32)


def _rms(x, g):
    return x * lax.rsqrt(jnp.mean(x * x, -1, keepdims=True) + NORM_EPS) * g


def _split2(x):
    hi = x.astype(BF16)
    lo = (x - hi.astype(F32)).astype(BF16)
    return hi, lo


def _split3(x):
    hi = x.astype(BF16)
    r1 = x - hi.astype(F32)
    mid = r1.astype(BF16)
    lo = (r1 - mid.astype(F32)).astype(BF16)
    return hi, mid, lo


def _params(sem, vmem_limit=VMEM_LIMIT):
    return pltpu.CompilerParams(dimension_semantics=sem, vmem_limit_bytes=vmem_limit)


def _ada_kernel(c_ref, w_ref, b_ref, o_ref, s_ref):
    @pl.when(pl.program_id(0) == 0)
    def _():
        c = c_ref[...]
        s_ref[...] = (c * jax.nn.sigmoid(c)).astype(BF16)

    o_ref[...] = _dot(s_ref[...], w_ref[...].astype(BF16)) + b_ref[...]


def _ada(c, w_ada, b_ada):
    m, n, tn = c.shape[0], w_ada.shape[1], 1024
    return pl.pallas_call(
        _ada_kernel,
        grid=(n // tn,),
        in_specs=[pl.BlockSpec((m, D_MODEL), lambda j: (0, 0)),
                  pl.BlockSpec((D_MODEL, tn), lambda j: (0, j)),
                  pl.BlockSpec((1, tn), lambda j: (0, j))],
        out_specs=pl.BlockSpec((m, tn), lambda j: (0, j)),
        out_shape=jax.ShapeDtypeStruct((m, n), F32),
        scratch_shapes=[pltpu.VMEM((m, D_MODEL), BF16)],
        compiler_params=_params(("arbitrary",)),
        name="ada",
    )(c, w_ada, b_ada.reshape(1, n))


MOD_SHIFT_M, MOD_SCALE_M, MOD_GATE_M, MOD_SHIFT_F, MOD_SCALE_F, MOD_GATE_F = range(6)


def _mod_spec(k, tm, rows_per_seq):
    if rows_per_seq >= tm:
        return pl.BlockSpec((None, 1, D_MODEL), lambda i, *_: (i * tm // rows_per_seq, 0, k))
    return pl.BlockSpec((tm // rows_per_seq, D_MODEL), lambda i, *_: (i, k))


def _seq_rows(ref, first, n_seq, per_seq):
    return jnp.concatenate([jnp.broadcast_to(ref[pl.ds(first + s, 1), :], (per_seq, ref.shape[1]))
                            for s in range(n_seq)], axis=0)


def _mod_rows(ref, tm):
    n_seq = ref.shape[0]
    return ref[...] if n_seq == 1 else _seq_rows(ref, 0, n_seq, tm // n_seq)


ROW_CHUNK = 16


def _for_row_chunks(n_rows, body):
    def step(i, carry):
        body(pl.ds(pl.multiple_of(i * ROW_CHUNK, ROW_CHUNK), ROW_CHUNK), i)
        return carry
    lax.fori_loop(0, n_rows // ROW_CHUNK, step, 0, unroll=8)


def _mod_chunk(ref, i, tm):
    if ref.shape[0] == 1:
        return ref[...]
    per_seq = tm // ref.shape[0]
    n = ROW_CHUNK // per_seq
    return _seq_rows(ref, i * n, n, per_seq)


ZB = 256
_N_RW_B, _N_MLP_B, _N_GATE_B = C_SHIFT // ZB, 2 * MLP_W // ZB, 2 * D_MODEL // ZB


def _w_in_block(zb):
    return jnp.where(zb < _N_GATE_B, zb + _N_RW_B + _N_MLP_B,
                     jnp.where(zb < _N_GATE_B + _N_MLP_B, zb - _N_GATE_B + _N_RW_B,
                               jnp.minimum(zb - _N_GATE_B - _N_MLP_B, _N_RW_B - 1)))


def _hnorm_kernel(x_ref, sc_ref, sh_ref, g_ref, h_ref):
    tm = x_ref.shape[0]

    def body(rows, i):
        h = _rms(x_ref[rows, :], g_ref[...]) * (1.0 + _mod_chunk(sc_ref, i, tm)) + _mod_chunk(sh_ref, i, tm)
        h_ref[rows, :] = h.astype(BF16)

    _for_row_chunks(tm, body)


def _hnorm(x, mod, g, rows_per_seq, n_rows=None):
    m, tm = (n_rows or x.shape[0]), 1024
    mspec = lambda k: _mod_spec(k, tm, rows_per_seq)
    return pl.pallas_call(
        _hnorm_kernel,
        grid=(m // tm,),
        in_specs=[pl.BlockSpec((tm, D_MODEL), lambda i: (i, 0)),
                  mspec(MOD_SCALE_M), mspec(MOD_SHIFT_M),
                  pl.BlockSpec((1, D_MODEL), lambda i: (0, 0))],
        out_specs=pl.BlockSpec((tm, D_MODEL), lambda i: (i, 0)),
        out_shape=jax.ShapeDtypeStruct((m, D_MODEL), BF16),
        compiler_params=_params(("arbitrary",)),
        name="hnorm",
    )(x, mod, mod, g)


def _inproj_kernel(h_ref, wlo_ref, whi_ref, o_ref):
    h = h_ref[...]
    o_ref[:, :ZB] = _dot(h, wlo_ref[...].astype(BF16))
    o_ref[:, ZB:] = _dot(h, whi_ref[...].astype(BF16))


def _inproj(h, w):
    m = h.shape[0]
    tm = min(m, 2048)
    assert Z_TN == 2 * ZB and m % tm == 0
    return pl.pallas_call(
        _inproj_kernel,
        grid=(m // tm, Z_W // Z_TN),
        in_specs=[pl.BlockSpec((tm, D_MODEL), lambda i, j: (i, 0)),
                  pl.BlockSpec((D_MODEL, ZB), lambda i, j: (0, _w_in_block(2 * j))),
                  pl.BlockSpec((D_MODEL, ZB), lambda i, j: (0, _w_in_block(2 * j + 1)))],
        out_specs=pl.BlockSpec((tm, Z_TN), lambda i, j: (i, j)),
        out_shape=jax.ShapeDtypeStruct((m, Z_W), F32),
        compiler_params=_params(("arbitrary", "arbitrary")),
        name="inproj",
    )(h, w, w)


INPROJ_TM = 2048
INPROJ_CHUNK = 128


def _inproj_norm_kernel(h0_ref, x_ref, sc_ref, sh_ref, g_ref, wlo_ref, whi_ref, o_ref, ha_ref, hb_ref):
    i, j = pl.program_id(0), pl.program_id(1)
    ch = x_ref.shape[0]
    n_chunks = ha_ref.shape[0] // ch

    @pl.when((i == 0) & (j == 0))
    def _():
        ha_ref[...] = h0_ref[...]

    def step(cur_ref, nxt_ref):
        row0 = pl.multiple_of(jnp.minimum(j, n_chunks - 1) * ch, ch)
        h = _rms(x_ref[...], g_ref[...]) * (1.0 + sc_ref[...]) + sh_ref[...]
        nxt_ref[pl.ds(row0, ch), :] = h.astype(BF16)
        h_cur = cur_ref[...]
        o_ref[:, :ZB] = _dot(h_cur, wlo_ref[...].astype(BF16))
        o_ref[:, ZB:] = _dot(h_cur, whi_ref[...].astype(BF16))

    @pl.when(i % 2 == 0)
    def _():
        step(ha_ref, hb_ref)

    @pl.when(i % 2 == 1)
    def _():
        step(hb_ref, ha_ref)


def _inproj_norm(x, mod, g, w, rows_per_seq):
    m, tm, ch = x.shape[0], INPROJ_TM, INPROJ_CHUNK
    nt, n_chunks, n_col = m // tm, tm // ch, Z_W // Z_TN
    assert m % tm == 0 and rows_per_seq % tm == 0 and n_chunks <= n_col and Z_TN == 2 * ZB
    h0 = _hnorm(x, mod, g, rows_per_seq, n_rows=tm)
    nxt = lambda i: jnp.minimum(i + 1, nt - 1)
    mspec = lambda k: pl.BlockSpec((None, 1, D_MODEL), lambda i, j: (nxt(i) * tm // rows_per_seq, 0, k))
    return pl.pallas_call(
        _inproj_norm_kernel,
        grid=(nt, n_col),
        in_specs=[pl.BlockSpec((tm, D_MODEL), lambda i, j: (0, 0), pipeline_mode=pl.Buffered(1)),
                  pl.BlockSpec((ch, D_MODEL), lambda i, j: (nxt(i) * n_chunks + jnp.minimum(j, n_chunks - 1), 0)),
                  mspec(MOD_SCALE_M), mspec(MOD_SHIFT_M),
                  pl.BlockSpec((1, D_MODEL), lambda i, j: (0, 0)),
                  pl.BlockSpec((D_MODEL, ZB), lambda i, j: (0, _w_in_block(2 * j))),
                  pl.BlockSpec((D_MODEL, ZB), lambda i, j: (0, _w_in_block(2 * j + 1)))],
        out_specs=pl.BlockSpec((tm, Z_TN), lambda i, j: (i, j)),
        out_shape=jax.ShapeDtypeStruct((m, Z_W), F32),
        scratch_shapes=[pltpu.VMEM((tm, D_MODEL), BF16), pltpu.VMEM((tm, D_MODEL), BF16)],
        compiler_params=_params(("arbitrary", "arbitrary")),
        name="inproj_norm",
    )(h0, x, mod, mod, g, w, w)


def _bd(w, mask):
    return jnp.where(mask, jnp.concatenate([w] * HPG, axis=0), 0.0).astype(BF16)


def _bd_heads(nat, mask):
    return jnp.where(mask, jnp.concatenate([nat] * HPG, axis=1), 0.0).astype(BF16)


def _diag_blocks(m):
    return jnp.concatenate([m[h * HEAD:(h + 1) * HEAD, h * HEAD:(h + 1) * HEAD] for h in range(HPG)],
                           axis=0)


def _wkv_kernel(*refs, lb, carried, nch, nseq, n_cast):
    R = WKV_ROWS
    n_in = 18 if carried else 23
    refs = list(refs)
    cast_in = [refs.pop(n_in) for _ in range(n_cast)]
    cast_out = [refs.pop(n_in + 3) for _ in range(n_cast)]
    for src, dst in zip(cast_in, cast_out):
        dst[...] = src[...].astype(BF16)
    if carried:
        (zr_ref, zk_ref, zv_ref, zl_ref,
         mur_ref, muk_ref, muv_ref, mul_ref,
         w0_ref, a0_ref, kkp_ref, kap_ref, rkp_ref, gng_ref, gnb_ref,
         wd_ref, wa_ref, wg_ref,
         oa_ref, so_ref, sho_ref,
         s_ref, cr_ref, ck_ref, cv_ref, cl_ref) = refs
        carry = {id(zr_ref): cr_ref, id(zk_ref): ck_ref, id(zv_ref): cv_ref, id(zl_ref): cl_ref}
    else:
        (zr_ref, zk_ref, zv_ref, zl_ref,
         pr_ref, pk_ref, pv_ref, pl_ref, si_ref,
         mur_ref, muk_ref, muv_ref, mul_ref,
         w0_ref, a0_ref, kkp_ref, kap_ref, rkp_ref, gng_ref, gnb_ref,
         wd_ref, wa_ref, wg_ref,
         oa_ref, so_ref, sho_ref) = refs
        prev_of = {id(zr_ref): pr_ref, id(zk_ref): pk_ref, id(zv_ref): pv_ref, id(zl_ref): pl_ref}
    nblk = R // lb
    z_refs = (zr_ref, zk_ref, zv_ref, zl_ref)
    mu_refs = (mur_ref, muk_ref, muv_ref, mul_ref)

    if carried:
        @pl.when(pl.program_id(1) == 0)
        def _():
            s_ref[...] = jnp.zeros_like(s_ref)
            for ref in (cr_ref, ck_ref, cv_ref, cl_ref):
                ref[...] = jnp.zeros_like(ref)

    row1 = lax.broadcasted_iota(jnp.int32, (R, 1), 0)
    first = (row1 % lb) == 0
    ri = lax.broadcasted_iota(jnp.int32, (GRP, GRP), 0)
    ci = lax.broadcasted_iota(jnp.int32, (GRP, GRP), 1)
    bdm = (ri // HEAD) == (ci // HEAD)
    ones_bd = jnp.where(bdm, 1.0, 0.0).astype(BF16)
    tr = lax.broadcasted_iota(jnp.int32, (R, R), 0)
    tc = lax.broadcasted_iota(jnp.int32, (R, R), 1)
    same = (tr // lb) == (tc // lb)
    incl01 = jnp.where(same & (tc <= tr), 1.0, 0.0).astype(BF16)
    same01 = jnp.where(same, 1.0, 0.0).astype(BF16)
    wr = lax.broadcasted_iota(jnp.int32, (R, GRP), 0)
    wc = lax.broadcasted_iota(jnp.int32, (R, GRP), 1) % HEAD
    wsame = (wr // lb) == (wc // lb)
    strict_w = wsame & (wc < wr)
    incl_w = wsame & (wc <= wr)
    eye_w = jnp.where(wc == wr, 1.0, 0.0)
    sls = [slice(gi * GRP, (gi + 1) * GRP) for gi in range(N_GRP)]
    blks = [slice(b * lb, (b + 1) * lb) for b in range(nblk)]
    units = [(s, gi) for s in range(nseq) for gi in range(N_GRP)]
    uids = range(len(units))

    def seg_sums(xs):
        n = xs[0].shape[0]
        parts = []
        for x in xs:
            parts.extend(_split2(x))
        stacked = jnp.concatenate(parts, axis=0)
        cols = [_dot(stacked[:, sl], ones_bd) for sl in sls]
        full = jnp.concatenate(cols, axis=1)
        return [full[2 * i * n:(2 * i + 1) * n] + full[(2 * i + 1) * n:(2 * i + 2) * n]
                for i in range(len(xs))]

    def cum(m01, x):
        hi, mid, lo = _split3(x)
        return _dot(m01, hi) + _dot(m01, mid) + _dot(m01, lo)

    def prep(c, outs):
        rows = slice(c * R, (c + 1) * R)

        def shift(z_ref, mu_ref):
            parts = []
            for s in range(nseq):
                z = z_ref[s, rows, :]
                if not carried:
                    prev0 = _seq_rows(prev_of[id(z_ref)].at[s], c * nblk, nblk, lb)
                elif c == 0:
                    prev0 = carry[id(z_ref)][s]
                else:
                    prev0 = z_ref[s, c * R - 1:c * R, :]
                prev = jnp.where(first, prev0, pltpu.roll(z, 1, axis=0))
                parts.append(z + (prev - z) * mu_ref[...])
            return jnp.concatenate(parts, axis=0)

        per_stream = lambda x: [x[s * R:(s + 1) * R] for s in range(nseq)]
        r, k, v, l = [shift(z_ref, mu_ref) for z_ref, mu_ref in zip(z_refs, mu_refs)]
        w_raw = w0_ref[...] + _dot(jnp.tanh(l).astype(BF16), wd_ref[...])
        a_raw = a0_ref[...] + _dot(l.astype(BF16), wa_ref[...])
        g = _dot(jax.nn.sigmoid(l).astype(BF16), wg_ref[...])
        yield
        logw = jax.nn.sigmoid(w_raw) * (-math.exp(-0.5))
        a = jax.nn.sigmoid(a_raw)
        kk = k * kkp_ref[...]
        k2 = k * (1.0 + (a - 1.0) * kap_ref[...])
        ss, rk = seg_sums([kk * kk, r * k2 * rkp_ref[...]])
        cl_s = [cum(incl01, lw) for lw in per_stream(logw)]
        if nblk == 1:
            ce_s = [jnp.broadcast_to(x[R - 1:R, :], (R, RW)) for x in cl_s]
        else:
            ce_s = [cum(same01, lw) for lw in per_stream(logw)]
        yield
        cl = jnp.concatenate(cl_s, axis=0)
        cl_end = jnp.concatenate(ce_s, axis=0)
        kk = kk / jnp.maximum(jnp.sqrt(ss), 1e-12)
        av = -kk
        bv = kk * a
        e_neg = jnp.exp(-cl)
        e_end = jnp.exp(cl_end - cl)
        full = dict(v=v, g=g, bonus=rk * v,
                    at=av * jnp.exp(cl - logw), rt=r * jnp.exp(cl),
                    bt=bv * e_neg, kt=k2 * e_neg, bh=bv * e_end, kh=k2 * e_end)
        for name, x in full.items():
            for s, x_s in enumerate(per_stream(x)):
                outs[s][name] = x_s
        for s in range(nseq):
            outs[s]["p_end"] = jnp.exp(cl_s[s][R - 1:R, :] if nblk == 1 else ce_s[s])

    def main(c, ps):
        rows_c = slice(c * R, (c + 1) * R)
        col = lambda name, ui: ps[units[ui][0]][name][:, sls[units[ui][1]]]
        v_g = [col("v", ui) for ui in uids]
        lhs = [jnp.concatenate([col("at", ui), col("rt", ui)], axis=0).astype(BF16) for ui in uids]
        o1 = [_dot_nt(lhs[ui], jnp.concatenate([_bd(col("bt", ui), bdm), _bd(col("kt", ui), bdm)], axis=0))
              for ui in uids]
        yield
        w_ab = [jnp.where(strict_w, o[:R, :GRP], 0.0) for o in o1]
        w_ak = [jnp.where(strict_w, o[:R, GRP:], 0.0) for o in o1]
        w_rb = [jnp.where(incl_w, o[R:, :GRP], 0.0) for o in o1]
        w_rk = [jnp.where(incl_w, o[R:, GRP:], 0.0) for o in o1]

        def state_part(ui):
            s, gi = units[ui]
            if carried:
                o2 = _dot_nt(lhs[ui], s_ref[s, gi].astype(BF16))
                return o2[:R], o2[R:]
            at_u, rt_u = col("at", ui), col("rt", ui)
            a_rows, r_rows = [], []
            for b, rows in enumerate(blks):
                lhs_b = jnp.concatenate([at_u[rows], rt_u[rows]], axis=0).astype(BF16)
                o2 = _dot_nt(lhs_b, _bd_heads(si_ref[s, c * nblk + b, sls[gi], :], bdm))
                a_rows.append(o2[:lb])
                r_rows.append(o2[lb:])
            return jnp.concatenate(a_rows, axis=0), jnp.concatenate(r_rows, axis=0)

        x_w = list(w_ab)
        t_w = [eye_w + x for x in x_w]
        n_lvl = max(1, int(math.log2(lb)))
        a_s = r_s = o3 = None
        for j in range(n_lvl):
            for ui in uids:
                y_bd = _bd(x_w[ui], bdm)
                if j == 0:
                    x_w[ui] = _dot(x_w[ui].astype(BF16), y_bd)
                elif j == n_lvl - 1:
                    t_w[ui] = t_w[ui] + _dot(t_w[ui].astype(BF16), y_bd)
                else:
                    res = _dot(jnp.concatenate([t_w[ui], x_w[ui]], axis=0).astype(BF16), y_bd)
                    t_w[ui] = t_w[ui] + res[:R]
                    x_w[ui] = res[R:]
            yield
            if j == 0:
                parts = [state_part(ui) for ui in uids]
                a_s, r_s = [q[0] for q in parts], [q[1] for q in parts]
                yield
            elif j == 1:
                o3 = [_dot(jnp.concatenate([w_ak[ui], w_rk[ui]], axis=0).astype(BF16), _bd(v_g[ui], bdm))
                      for ui in uids]
                yield

        u = [_dot(t_w[ui].astype(BF16), _bd(a_s[ui] + o3[ui][:R], bdm)) for ui in uids]
        yield
        o_parts = [r_s[ui] + o3[ui][R:] + _dot(w_rb[ui].astype(BF16), _bd(u[ui], bdm)) for ui in uids]
        yield

        for ui in uids:
            s, gi = units[ui]
            sl = sls[gi]
            bh_u, kh_u, p_end = col("bh", ui), col("kh", ui), ps[s]["p_end"][:, sl]
            if carried:
                upd = _dot_tn(jnp.concatenate([u[ui], v_g[ui]], axis=0).astype(BF16),
                              jnp.concatenate([bh_u, kh_u], axis=0).astype(BF16))
                s_new = jnp.where(bdm, s_ref[s, gi] * p_end + upd, 0.0)
                s_ref[s, gi] = s_new
                if c == nch - 1:
                    so_ref[s, sl, :] = _diag_blocks(s_new)
            else:
                for b, rows in enumerate(blks):
                    upd = _dot_tn(jnp.concatenate([u[ui][rows], v_g[ui][rows]], axis=0).astype(BF16),
                                  jnp.concatenate([bh_u[rows], kh_u[rows]], axis=0).astype(BF16))
                    p_b = p_end[b * lb:b * lb + 1]
                    p_nat = jnp.concatenate(
                        [jnp.broadcast_to(p_b[:, h * HEAD:(h + 1) * HEAD], (HEAD, HEAD)) for h in range(HPG)],
                        axis=0)
                    so_ref[s, c * nblk + b, sl, :] = (si_ref[s, c * nblk + b, sl, :] * p_nat
                                                      + _diag_blocks(upd))
        yield

        o = [jnp.concatenate(o_parts[s * N_GRP:(s + 1) * N_GRP], axis=1) for s in range(nseq)]
        mu = seg_sums(o)
        yield
        dlt = [o[s] - mu[s] * (1.0 / HEAD) for s in range(nseq)]
        var = seg_sums([d * d for d in dlt])
        for s in range(nseq):
            on = dlt[s] * lax.rsqrt(var[s] * (1.0 / HEAD) + GN_EPS) * gng_ref[...] + gnb_ref[...]
            oa_ref[s, rows_c, :] = ((on + ps[s]["bonus"]) * ps[s]["g"]).astype(BF16)

    def run(gen):
        for _ in gen:
            pass

    cur = [{} for _ in range(nseq)]
    run(prep(0, cur))
    for c in range(nch):
        nxt = [{} for _ in range(nseq)]
        side = prep(c + 1, nxt) if c + 1 < nch else iter(())
        for tick, _ in enumerate(main(c, cur)):
            if tick % 3 == 1:
                next(side, None)
        run(side)
        cur = nxt

    n_rows = nch * R
    for s in range(nseq):
        if carried:
            last = lambda z_ref: z_ref[s, n_rows - 1:n_rows, :]
            for z_ref in z_refs:
                carry[id(z_ref)][s] = last(z_ref)
        else:
            last = lambda z_ref: jnp.concatenate(
                [z_ref[s, b * lb + lb - 1:b * lb + lb, :] for b in range(nch * nblk)], axis=0)
        sho_ref[s, :, 0:RW] = last(zr_ref)
        sho_ref[s, :, RW:2 * RW] = last(zk_ref)
        sho_ref[s, :, 2 * RW:3 * RW] = last(zv_ref)
        sho_ref[s, :, 3 * RW:] = last(zl_ref)


def _wkv(z, vecs, mats, n_seq, seq_len, lb, nch, nseq, prev_rows=None, state_in=None, to_bf16=()):
    R = nch * WKV_ROWS
    carried = prev_rows is None
    rows = z.shape[0]
    n_streams = n_seq if carried else nseq
    stream_rows = rows // n_streams
    assert n_streams % nseq == 0 and stream_rows % R == 0 and rows == n_streams * stream_rows
    z3 = z.reshape(n_streams, stream_rows, Z_W)
    if carried:
        grid = (n_streams // nseq, stream_rows // R)
        rmap = lambda c: (lambda b, t: (b, t, c))
        cmap = lambda b, t: (0, 0)
        smap = lambda b, t: (b, 0, 0)
        sem = ("arbitrary", "arbitrary")
    else:
        grid = (stream_rows // R,)
        rmap = lambda c: (lambda i: (0, i, c))
        cmap = lambda i: (0, 0)
        sem = ("arbitrary",)
        blk_per_step = R // lb
        seq_per_stream = n_seq // nseq

    in_specs = [pl.BlockSpec((nseq, R, RW), rmap(Z_R // RW)),
                pl.BlockSpec((nseq, R, RW), rmap(Z_K // RW)),
                pl.BlockSpec((nseq, R, RW), rmap(Z_VR // RW)),
                pl.BlockSpec((nseq, R, LORA), rmap(Z_L // LORA))]
    args = [z3, z3, z3, z3]
    if not carried:
        p3 = prev_rows.reshape(nseq, seq_per_stream, C_SHIFT)
        st4 = state_in.reshape(nseq, seq_per_stream, RW, HEAD)
        in_specs += [pl.BlockSpec((nseq, blk_per_step, RW), rmap(0)),
                     pl.BlockSpec((nseq, blk_per_step, RW), rmap(1)),
                     pl.BlockSpec((nseq, blk_per_step, RW), rmap(2)),
                     pl.BlockSpec((nseq, blk_per_step, LORA), rmap(3 * RW // LORA)),
                     pl.BlockSpec((nseq, blk_per_step, RW, HEAD), lambda i: (0, i, 0, 0))]
        args += [p3] * 4 + [st4]
    for a in vecs + mats:
        in_specs.append(pl.BlockSpec(a.shape, cmap))
        args.append(a)

    if carried:
        scratch = [pltpu.VMEM((nseq, N_GRP, GRP, GRP), F32), pltpu.VMEM((nseq, 1, RW), F32),
                   pltpu.VMEM((nseq, 1, RW), F32), pltpu.VMEM((nseq, 1, RW), F32),
                   pltpu.VMEM((nseq, 1, LORA), F32)]
        state_spec = pl.BlockSpec((nseq, RW, HEAD), smap)
        state_shape = (n_seq, RW, HEAD)
        shift_spec = pl.BlockSpec((nseq, 1, C_SHIFT), smap)
        shift_shape = (n_seq, 1, C_SHIFT)
    else:
        scratch = []
        state_spec = pl.BlockSpec((nseq, blk_per_step, RW, HEAD), lambda i: (0, i, 0, 0))
        state_shape = (nseq, seq_per_stream, RW, HEAD)
        shift_spec = pl.BlockSpec((nseq, blk_per_step, C_SHIFT), lambda i: (0, i, 0))
        shift_shape = (nseq, seq_per_stream, C_SHIFT)
    n_steps = math.prod(grid)
    step = (lambda b, t: b * grid[1] + t) if carried else (lambda i: i)
    cast_specs = [pl.BlockSpec((w.shape[0] // n_steps, w.shape[1]), lambda *g: (step(*g), 0)) for w in to_bf16]
    assert all(w.shape[0] % (16 * n_steps) == 0 for w in to_bf16)
    oa, st, sh, *casted = pl.pallas_call(
        functools.partial(_wkv_kernel, lb=lb, carried=carried, nch=nch, nseq=nseq, n_cast=len(to_bf16)),
        grid=grid,
        in_specs=in_specs + cast_specs,
        out_specs=[pl.BlockSpec((nseq, R, RW), rmap(0)), state_spec, shift_spec] + cast_specs,
        out_shape=[jax.ShapeDtypeStruct((n_streams, stream_rows, RW), BF16),
                   jax.ShapeDtypeStruct(state_shape, F32),
                   jax.ShapeDtypeStruct(shift_shape, F32)]
                  + [jax.ShapeDtypeStruct(w.shape, BF16) for w in to_bf16],
        scratch_shapes=scratch,
        compiler_params=_params(sem),
        name="wkv_carried" if carried else "wkv_blocks",
    )(*args, *to_bf16)
    return (oa.reshape(rows, RW), st.reshape(n_seq, RW, HEAD), sh.reshape(n_seq, C_SHIFT), *casted)


def _gmlp_kernel(*refs, seq, want_vn, n_cast):
    zu_ref, zv_ref, lng_ref, lnb_ref, ws_ref, bias_ref = refs[:6]
    cast_in = refs[6:6 + n_cast]
    ub_ref = refs[6 + n_cast]
    vn_ref = refs[7 + n_cast] if want_vn else None
    cast_out = refs[len(refs) - n_cast:]
    for src, dst in zip(cast_in, cast_out):
        dst[...] = src[...].astype(BF16)
    reps = CHUNK // seq
    tr = lax.broadcasted_iota(jnp.int32, (CHUNK, CHUNK), 0)
    tc = lax.broadcasted_iota(jnp.int32, (CHUNK, CHUNK), 1)
    causal = (tc <= tr) & ((tr // seq) == (tc // seq))
    ws_g = []
    for gi in range(MLP_GROUPS):
        w_rows = ws_ref[gi]
        w_full = jnp.concatenate([w_rows if s == 0 else pltpu.roll(w_rows, s * seq, axis=1)
                                  for s in range(reps)], axis=0)
        ws_g.append(jnp.where(causal, w_full, 0.0).astype(BF16))
    bias = jnp.concatenate([bias_ref[...]] * reps, axis=0)
    for c in range(zu_ref.shape[0] // CHUNK):
        rows = slice(c * CHUNK, (c + 1) * CHUNK)
        u = jax.nn.gelu(zu_ref[rows, :])
        vg = jax.nn.gelu(zv_ref[rows, :])
        mu = jnp.mean(vg, -1, keepdims=True)
        var = jnp.mean(jnp.square(vg - mu), -1, keepdims=True)
        vn = (vg - mu) * lax.rsqrt(var + LN_EPS) * lng_ref[...] + lnb_ref[...]
        if vn_ref is not None:
            vn_ref[rows, :] = vn
        vb = vn.astype(BF16)
        cols = [_dot(ws_g[gi], vb[:, gi * MLP_GD:(gi + 1) * MLP_GD]) for gi in range(MLP_GROUPS)]
        mixed = jnp.concatenate(cols, axis=1) + bias
        ub_ref[rows, :] = (u * mixed).astype(BF16)


def _gmlp(z, ln_g, ln_b, ws, bias, seq, want_vn, to_bf16=()):
    rows, tr = z.shape[0], 4 * CHUNK
    steps = rows // tr
    n_out = 2 if want_vn else 1
    cast_specs = [pl.BlockSpec((w.shape[0] // steps, w.shape[1]), lambda i: (i, 0)) for w in to_bf16]
    assert all(w.shape[0] % (16 * steps) == 0 for w in to_bf16)
    return pl.pallas_call(
        functools.partial(_gmlp_kernel, seq=seq, want_vn=want_vn, n_cast=len(to_bf16)),
        grid=(steps,),
        in_specs=[pl.BlockSpec((tr, MLP_W), lambda i: (i, Z_U // MLP_W)),
                  pl.BlockSpec((tr, MLP_W), lambda i: (i, Z_V // MLP_W)),
                  pl.BlockSpec((1, MLP_W), lambda i: (0, 0)),
                  pl.BlockSpec((1, MLP_W), lambda i: (0, 0)),
                  pl.BlockSpec((MLP_GROUPS, seq, CHUNK), lambda i: (0, 0, 0)),
                  pl.BlockSpec((seq, MLP_W), lambda i: (0, 0))] + cast_specs,
        out_specs=[pl.BlockSpec((tr, MLP_W), lambda i: (i, 0)),
                   pl.BlockSpec((tr, MLP_W), lambda i: (i, 0))][:n_out] + cast_specs,
        out_shape=[jax.ShapeDtypeStruct((rows, MLP_W), BF16),
                   jax.ShapeDtypeStruct((rows, MLP_W), F32)][:n_out]
                  + [jax.ShapeDtypeStruct(w.shape, BF16) for w in to_bf16],
        compiler_params=_params(("arbitrary",)),
        name="gmlp",
    )(z, z, ln_g, ln_b, ws, bias, *to_bf16)


def _merge_kernel(oa_ref, ub_ref, zga_ref, zgb_ref, x_ref, gm_ref, sc_ref, sh_ref, g_ref,
                  wa_ref, wb_ref, wo_ref, x1_ref, h2_ref):
    ya = _dot(oa_ref[...], wa_ref[...])
    yb = _dot(ub_ref[...], wb_ref[...])
    merged = jax.nn.sigmoid(zga_ref[...]) * ya + jax.nn.sigmoid(zgb_ref[...]) * yb
    mix = _dot(merged.astype(BF16), wo_ref[...])
    tm = x_ref.shape[0]
    x1 = x_ref[...] + _mod_rows(gm_ref, tm) * mix
    x1_ref[...] = x1
    h2_ref[...] = (_rms(x1, g_ref[...]) * (1.0 + _mod_rows(sc_ref, tm)) + _mod_rows(sh_ref, tm)).astype(BF16)


def _merge(oa, ub, z, x, mod, g_ffn, wa, wb, wo, rows_per_seq):
    m, tm = x.shape[0], 256
    const = lambda shape: pl.BlockSpec(shape, lambda i: (0, 0), pipeline_mode=pl.Buffered(1))
    mspec = lambda k: _mod_spec(k, tm, rows_per_seq)
    rows = lambda w, c: pl.BlockSpec((tm, w), lambda i: (i, c))
    return pl.pallas_call(
        _merge_kernel,
        grid=(m // tm,),
        in_specs=[rows(RW, 0), rows(MLP_W, 0),
                  rows(D_MODEL, Z_GA // D_MODEL), rows(D_MODEL, Z_GB // D_MODEL), rows(D_MODEL, 0),
                  mspec(MOD_GATE_M), mspec(MOD_SCALE_F), mspec(MOD_SHIFT_F),
                  pl.BlockSpec((1, D_MODEL), lambda i: (0, 0)),
                  const((RW, D_MODEL)), const((MLP_W, D_MODEL)), const((D_MODEL, D_MODEL))],
        out_specs=[rows(D_MODEL, 0), rows(D_MODEL, 0)],
        out_shape=[jax.ShapeDtypeStruct((m, D_MODEL), F32), jax.ShapeDtypeStruct((m, D_MODEL), BF16)],
        compiler_params=_params(("arbitrary",)),
        name="merge",
    )(oa, ub, z, z, x, mod, mod, mod, g_ffn, wa, wb, wo)


def _ffn_kernel(*refs, fused):
    if fused:
        h_ref, x_ref, gf_ref, gfin_ref, wg_ref, wu_ref, wo_ref, y_ref = refs
    else:
        h_ref, wg_ref, wu_ref, wo_ref, y_ref = refs
    j = pl.program_id(1)

    @pl.when(j == 0)
    def _():
        y_ref[...] = jnp.zeros_like(y_ref)

    hb = h_ref[...]
    gt = _dot(hb, wg_ref[...].astype(BF16))
    up = _dot(hb, wu_ref[...].astype(BF16))
    act = gt * jax.nn.sigmoid(gt) * up
    y_ref[...] += _dot(act.astype(BF16), wo_ref[...].astype(BF16))

    if fused:
        @pl.when(j == pl.num_programs(1) - 1)
        def _():
            x2 = x_ref[...] + gf_ref[...] * y_ref[...]
            y_ref[...] = _rms(x2, gfin_ref[...])


def _final_kernel(f_ref, x_ref, gf_ref, gfin_ref, y_ref):
    tm = x_ref.shape[0]

    def body(rows, i):
        x2 = x_ref[rows, :] + _mod_chunk(gf_ref, i, tm) * f_ref[rows, :]
        y_ref[rows, :] = _rms(x2, gfin_ref[...])

    _for_row_chunks(tm, body)


def _ffn(h2, x1, mod, g_final, w_in, w_out, rows_per_seq):
    m, tm, tf = x1.shape[0], 1024, (512 if w_in.dtype == BF16 else 256)
    nf = D_FF // tf
    fused = rows_per_seq >= tm
    rows = pl.BlockSpec((tm, D_MODEL), lambda i, j: (i, 0))
    vec = pl.BlockSpec((1, D_MODEL), lambda i, j: (0, 0))
    wspecs = [pl.BlockSpec((D_MODEL, tf), lambda i, j: (0, j)),
              pl.BlockSpec((D_MODEL, tf), lambda i, j: (0, nf + j)),
              pl.BlockSpec((tf, D_MODEL), lambda i, j: (j, 0))]
    if fused:
        in_specs = [rows, pl.BlockSpec((tm, D_MODEL), lambda i, j: (i, 0), pipeline_mode=pl.Buffered(1)),
                    _mod_spec(MOD_GATE_F, tm, rows_per_seq), vec] + wspecs
        args = (h2, x1, mod, g_final, w_in, w_in, w_out)
    else:
        in_specs = [rows] + wspecs
        args = (h2, w_in, w_in, w_out)
    y = pl.pallas_call(
        functools.partial(_ffn_kernel, fused=fused),
        grid=(m // tm, nf),
        in_specs=in_specs,
        out_specs=rows,
        out_shape=jax.ShapeDtypeStruct((m, D_MODEL), F32),
        compiler_params=_params(("arbitrary", "arbitrary"), FFN_VMEM_LIMIT),
        name="ffn",
    )(*args)
    if fused:
        return y
    te = 512
    erows = pl.BlockSpec((te, D_MODEL), lambda i: (i, 0))
    return pl.pallas_call(
        _final_kernel,
        grid=(m // te,),
        in_specs=[erows, erows, _mod_spec(MOD_GATE_F, te, rows_per_seq),
                  pl.BlockSpec((1, D_MODEL), lambda i: (0, 0))],
        out_specs=erows,
        out_shape=jax.ShapeDtypeStruct((m, D_MODEL), F32),
        compiler_params=_params(("arbitrary",)),
        name="final",
    )(y, x1, mod, g_final)


def kernel(x_prompt, x_sample, state_wkv, state_shift, c_prompt, c_sample, w_ada, b_ada, norm_mix_g, w_in, mu_shift, w0, w_decay_up, a0, w_aaa_up, w_gate_up, k_k, k_a, r_k, gn_g, gn_b, ln_v_g, ln_v_b, w_spatial, b_spatial, w_branch_a, w_branch_b, w_out, norm_ffn_g, w_ffn_in, w_ffn_out, norm_final_g):
    assert w_ada.shape[0] == 1, "single layer"
    bp, tp, _ = x_prompt.shape
    bs, ts, _ = x_sample.shape
    assert tp % CHUNK == 0 and WKV_ROWS % ts == 0 and CHUNK % ts == 0

    c_all = jnp.concatenate([c_sample, c_prompt], axis=0)
    c_all = jnp.pad(c_all, ((0, -c_all.shape[0] % 8), (0, 0)))
    mod = _ada(c_all, w_ada[0], b_ada[0])
    mod_s = mod
    mod_p = mod[bs:bs + bp].reshape(bp, 1, 6 * D_MODEL)


    row = lambda a: a.reshape(1, -1)
    mu = mu_shift[0]
    vecs = [row(mu[:RW]), row(mu[RW:2 * RW]), row(mu[2 * RW:3 * RW]), row(mu[3 * RW:]),
            row(w0[0]), row(a0[0]), row(k_k[0]), row(k_a[0]), row(r_k[0]), row(gn_g[0]), row(gn_b[0])]
    zpad = lambda w, lo: jnp.pad(w, ((lo, LORA - lo - w.shape[0]), (0, 0))).astype(BF16)
    mats = [zpad(w_decay_up[0], 0), zpad(w_aaa_up[0], DECAY_LORA),
            zpad(w_gate_up[0], DECAY_LORA + AAA_LORA)]

    bias = jnp.repeat(b_spatial[0].T, MLP_GD, axis=1)
    ws = w_spatial[0]

    xp = x_prompt.reshape(bp * tp, D_MODEL)
    xs = x_sample.reshape(bs * ts, D_MODEL)
    g_mix, g_ffn, g_fin = row(norm_mix_g[0]), row(norm_ffn_g[0]), row(norm_final_g)
    lng, lnb = row(ln_v_g[0]), row(ln_v_b[0])

    zp = _inproj_norm(xp, mod_p, g_mix, w_in[0], tp)
    f32_weights = (w_ffn_in[0], w_ffn_out[0], w_branch_a[0], w_branch_b[0], w_out[0], w_in[0])
    oa_p, st_p, shift_p, wfi, wfo, wa_b, wb_b, wo_b, w_in_b = _wkv(
        zp, vecs, mats, bp, tp, WKV_ROWS, WKV_NCH, WKV_NSEQ_PROMPT, to_bf16=f32_weights)
    (ub_p,) = _gmlp(zp, lng, lnb, ws, bias, CHUNK, False)
    x1_p, h2_p = _merge(oa_p, ub_p, zp, xp, mod_p, g_ffn, wa_b, wb_b, wo_b, tp)
    y_p = _ffn(h2_p, x1_p, mod_p, g_fin, wfi, wfo, tp)

    zs = _inproj(_hnorm(xs, mod_s, g_mix, ts), w_in_b)
    oa_s, st_s, shift_s = _wkv(zs, vecs, mats, bs, ts, ts, 1, WKV_NSEQ, prev_rows=state_shift[0],
                               state_in=state_wkv[0].reshape(bs, RW, HEAD))
    ub_s, vn_s = _gmlp(zs, lng, lnb, ws, bias, ts, True)
    x1_s, h2_s = _merge(oa_s, ub_s, zs, xs, mod_s, g_ffn, wa_b, wb_b, wo_b, ts)
    y_s = _ffn(h2_s, x1_s, mod_s, g_fin, wfi, wfo, ts)

    return (y_p.reshape(bp, tp, D_MODEL),
            y_s.reshape(bs, ts, D_MODEL),
            st_p.reshape(1, bp, N_HEADS, HEAD, HEAD),
            shift_p.reshape(1, bp, C_SHIFT),
            st_s.reshape(1, bs, N_HEADS, HEAD, HEAD),
            shift_s.reshape(1, bs, C_SHIFT),
            vn_s.reshape(bs, ts, MLP_W)[None])
```

```python
import functools
import math

import jax
import jax.numpy as jnp
from jax import lax
from jax.experimental import pallas as pl
from jax.experimental.pallas import tpu as pltpu

F32 = jnp.float32
BF16 = jnp.bfloat16

D_MODEL = 2048
HEAD = 64
RW = D_MODEL // 2
N_HEADS = RW // HEAD
DECAY_LORA = D_MODEL // 32
AAA_LORA = D_MODEL // 32
GATE_LORA = D_MODEL // 16
LORA = DECAY_LORA + AAA_LORA + GATE_LORA
CHUNK = 128
MLP_W = D_MODEL // 2
MLP_GROUPS = 8
MLP_GD = MLP_W // MLP_GROUPS
D_FF = ((-(-8 * D_MODEL // 3)) + 255) // 256 * 256
C_SHIFT = 3 * RW + LORA
C_IN = C_SHIFT + 2 * MLP_W + 2 * D_MODEL
NORM_EPS = 1e-6
GN_EPS = 64e-5
LN_EPS = 1e-5

Z_GA, Z_GB, Z_U, Z_V, Z_R, Z_K, Z_VR, Z_L = 0, 2048, 4096, 5120, 6144, 7168, 8192, 9216
Z_TN = 512
Z_W = -(-C_IN // Z_TN) * Z_TN

GRP = 256
HPG = GRP // HEAD
N_GRP = RW // GRP
WKV_ROWS = 64
WKV_NCH = 2
WKV_NSEQ_PROMPT = 2
WKV_NSEQ = 2
VMEM_LIMIT = 56 * 1024 * 1024
FFN_VMEM_LIMIT = 60 * 1024 * 1024


def _dot(a, b):
    return jnp.dot(a, b, preferred_element_type=F32)


def _dot_nt(a, b):
    return lax.dot_general(a, b, (((1,), (1,)), ((), ())), preferred_element_type=F32)


def _dot_tn(a, b):
    return lax.dot_general(a, b, (((0,), (0,)), ((), ())), preferred_element_type=F32)


def _rms(x, g):
    return x * lax.rsqrt(jnp.mean(x * x, -1, keepdims=True) + NORM_EPS) * g


def _split2(x):
    hi = x.astype(BF16)
    lo = (x - hi.astype(F32)).astype(BF16)
    return hi, lo


def _split3(x):
    hi = x.astype(BF16)
    r1 = x - hi.astype(F32)
    mid = r1.astype(BF16)
    lo = (r1 - mid.astype(F32)).astype(BF16)
    return hi, mid, lo


def _params(sem, vmem_limit=VMEM_LIMIT):
    return pltpu.CompilerParams(dimension_semantics=sem, vmem_limit_bytes=vmem_limit)


def _ada_kernel(c_ref, w_ref, b_ref, o_ref, s_ref):
    @pl.when(pl.program_id(0) == 0)
    def _():
        c = c_ref[...]
        s_ref[...] = (c * jax.nn.sigmoid(c)).astype(BF16)

    o_ref[...] = _dot(s_ref[...], w_ref[...].astype(BF16)) + b_ref[...]


def _ada(c, w_ada, b_ada):
    m, n, tn = c.shape[0], w_ada.shape[1], 1024
    return pl.pallas_call(
        _ada_kernel,
        grid=(n // tn,),
        in_specs=[pl.BlockSpec((m, D_MODEL), lambda j: (0, 0)),
                  pl.BlockSpec((D_MODEL, tn), lambda j: (0, j)),
                  pl.BlockSpec((1, tn), lambda j: (0, j))],
        out_specs=pl.BlockSpec((m, tn), lambda j: (0, j)),
        out_shape=jax.ShapeDtypeStruct((m, n), F32),
        scratch_shapes=[pltpu.VMEM((m, D_MODEL), BF16)],
        compiler_params=_params(("arbitrary",)),
        name="ada",
    )(c, w_ada, b_ada.reshape(1, n))


MOD_SHIFT_M, MOD_SCALE_M, MOD_GATE_M, MOD_SHIFT_F, MOD_SCALE_F, MOD_GATE_F = range(6)


def _mod_spec(k, tm, rows_per_seq):
    if rows_per_seq >= tm:
        return pl.BlockSpec((None, 1, D_MODEL), lambda i, *_: (i * tm // rows_per_seq, 0, k))
    return pl.BlockSpec((tm // rows_per_seq, D_MODEL), lambda i, *_: (i, k))


def _seq_rows(ref, first, n_seq, per_seq):
    return jnp.concatenate([jnp.broadcast_to(ref[pl.ds(first + s, 1), :], (per_seq, ref.shape[1]))
                            for s in range(n_seq)], axis=0)


def _mod_rows(ref, tm):
    n_seq = ref.shape[0]
    return ref[...] if n_seq == 1 else _seq_rows(ref, 0, n_seq, tm // n_seq)


ROW_CHUNK = 16


def _for_row_chunks(n_rows, body):
    def step(i, carry):
        body(pl.ds(pl.multiple_of(i * ROW_CHUNK, ROW_CHUNK), ROW_CHUNK), i)
        return carry
    lax.fori_loop(0, n_rows // ROW_CHUNK, step, 0, unroll=8)


def _mod_chunk(ref, i, tm):
    if ref.shape[0] == 1:
        return ref[...]
    per_seq = tm // ref.shape[0]
    n = ROW_CHUNK // per_seq
    return _seq_rows(ref, i * n, n, per_seq)


ZB = 256
_N_RW_B, _N_MLP_B, _N_GATE_B = C_SHIFT // ZB, 2 * MLP_W // ZB, 2 * D_MODEL // ZB


def _w_in_block(zb):
    return jnp.where(zb < _N_GATE_B, zb + _N_RW_B + _N_MLP_B,
                     jnp.where(zb < _N_GATE_B + _N_MLP_B, zb - _N_GATE_B + _N_RW_B,
                               jnp.minimum(zb - _N_GATE_B - _N_MLP_B, _N_RW_B - 1)))


def _hnorm_kernel(x_ref, sc_ref, sh_ref, g_ref, h_ref):
    tm = x_ref.shape[0]

    def body(rows, i):
        h = _rms(x_ref[rows, :], g_ref[...]) * (1.0 + _mod_chunk(sc_ref, i, tm)) + _mod_chunk(sh_ref, i, tm)
        h_ref[rows, :] = h.astype(BF16)

    _for_row_chunks(tm, body)


def _hnorm(x, mod, g, rows_per_seq, n_rows=None):
    m, tm = (n_rows or x.shape[0]), 1024
    mspec = lambda k: _mod_spec(k, tm, rows_per_seq)
    return pl.pallas_call(
        _hnorm_kernel,
        grid=(m // tm,),
        in_specs=[pl.BlockSpec((tm, D_MODEL), lambda i: (i, 0)),
                  mspec(MOD_SCALE_M), mspec(MOD_SHIFT_M),
                  pl.BlockSpec((1, D_MODEL), lambda i: (0, 0))],
        out_specs=pl.BlockSpec((tm, D_MODEL), lambda i: (i, 0)),
        out_shape=jax.ShapeDtypeStruct((m, D_MODEL), BF16),
        compiler_params=_params(("arbitrary",)),
        name="hnorm",
    )(x, mod, mod, g)


def _inproj_kernel(h_ref, wlo_ref, whi_ref, o_ref):
    h = h_ref[...]
    o_ref[:, :ZB] = _dot(h, wlo_ref[...].astype(BF16))
    o_ref[:, ZB:] = _dot(h, whi_ref[...].astype(BF16))


def _inproj(h, w):
    m = h.shape[0]
    tm = min(m, 2048)
    assert Z_TN == 2 * ZB and m % tm == 0
    return pl.pallas_call(
        _inproj_kernel,
        grid=(m // tm, Z_W // Z_TN),
        in_specs=[pl.BlockSpec((tm, D_MODEL), lambda i, j: (i, 0)),
                  pl.BlockSpec((D_MODEL, ZB), lambda i, j: (0, _w_in_block(2 * j))),
                  pl.BlockSpec((D_MODEL, ZB), lambda i, j: (0, _w_in_block(2 * j + 1)))],
        out_specs=pl.BlockSpec((tm, Z_TN), lambda i, j: (i, j)),
        out_shape=jax.ShapeDtypeStruct((m, Z_W), F32),
        compiler_params=_params(("arbitrary", "arbitrary")),
        name="inproj",
    )(h, w, w)


INPROJ_TM = 2048
INPROJ_CHUNK = 128


def _inproj_norm_kernel(h0_ref, x_ref, sc_ref, sh_ref, g_ref, wlo_ref, whi_ref, o_ref, ha_ref, hb_ref):
    i, j = pl.program_id(0), pl.program_id(1)
    ch = x_ref.shape[0]
    n_chunks = ha_ref.shape[0] // ch

    @pl.when((i == 0) & (j == 0))
    def _():
        ha_ref[...] = h0_ref[...]

    def step(cur_ref, nxt_ref):
        row0 = pl.multiple_of(jnp.minimum(j, n_chunks - 1) * ch, ch)
        h = _rms(x_ref[...], g_ref[...]) * (1.0 + sc_ref[...]) + sh_ref[...]
        nxt_ref[pl.ds(row0, ch), :] = h.astype(BF16)
        h_cur = cur_ref[...]
        o_ref[:, :ZB] = _dot(h_cur, wlo_ref[...].astype(BF16))
        o_ref[:, ZB:] = _dot(h_cur, whi_ref[...].astype(BF16))

    @pl.when(i % 2 == 0)
    def _():
        step(ha_ref, hb_ref)

    @pl.when(i % 2 == 1)
    def _():
        step(hb_ref, ha_ref)


def _inproj_norm(x, mod, g, w, rows_per_seq):
    m, tm, ch = x.shape[0], INPROJ_TM, INPROJ_CHUNK
    nt, n_chunks, n_col = m // tm, tm // ch, Z_W // Z_TN
    assert m % tm == 0 and rows_per_seq % tm == 0 and n_chunks <= n_col and Z_TN == 2 * ZB
    h0 = _hnorm(x, mod, g, rows_per_seq, n_rows=tm)
    nxt = lambda i: jnp.minimum(i + 1, nt - 1)
    mspec = lambda k: pl.BlockSpec((None, 1, D_MODEL), lambda i, j: (nxt(i) * tm // rows_per_seq, 0, k))
    return pl.pallas_call(
        _inproj_norm_kernel,
        grid=(nt, n_col),
        in_specs=[pl.BlockSpec((tm, D_MODEL), lambda i, j: (0, 0), pipeline_mode=pl.Buffered(1)),
                  pl.BlockSpec((ch, D_MODEL), lambda i, j: (nxt(i) * n_chunks + jnp.minimum(j, n_chunks - 1), 0)),
                  mspec(MOD_SCALE_M), mspec(MOD_SHIFT_M),
                  pl.BlockSpec((1, D_MODEL), lambda i, j: (0, 0)),
                  pl.BlockSpec((D_MODEL, ZB), lambda i, j: (0, _w_in_block(2 * j))),
                  pl.BlockSpec((D_MODEL, ZB), lambda i, j: (0, _w_in_block(2 * j + 1)))],
        out_specs=pl.BlockSpec((tm, Z_TN), lambda i, j: (i, j)),
        out_shape=jax.ShapeDtypeStruct((m, Z_W), F32),
        scratch_shapes=[pltpu.VMEM((tm, D_MODEL), BF16), pltpu.VMEM((tm, D_MODEL), BF16)],
        compiler_params=_params(("arbitrary", "arbitrary")),
        name="inproj_norm",
    )(h0, x, mod, mod, g, w, w)


def _bd(w, mask):
    return jnp.where(mask, jnp.concatenate([w] * HPG, axis=0), 0.0).astype(BF16)


def _bd_heads(nat, mask):
    return jnp.where(mask, jnp.concatenate([nat] * HPG, axis=1), 0.0).astype(BF16)


def _diag_blocks(m):
    return jnp.concatenate([m[h * HEAD:(h + 1) * HEAD, h * HEAD:(h + 1) * HEAD] for h in range(HPG)],
                           axis=0)


def _wkv_kernel(*refs, lb, carried, nch, nseq, n_cast):
    R = WKV_ROWS
    n_in = 18 if carried else 23
    refs = list(refs)
    cast_in = [refs.pop(n_in) for _ in range(n_cast)]
    cast_out = [refs.pop(n_in + 3) for _ in range(n_cast)]
    for src, dst in zip(cast_in, cast_out):
        dst[...] = src[...].astype(BF16)
    if carried:
        (zr_ref, zk_ref, zv_ref, zl_ref,
         mur_ref, muk_ref, muv_ref, mul_ref,
         w0_ref, a0_ref, kkp_ref, kap_ref, rkp_ref, gng_ref, gnb_ref,
         wd_ref, wa_ref, wg_ref,
         oa_ref, so_ref, sho_ref,
         s_ref, cr_ref, ck_ref, cv_ref, cl_ref) = refs
        carry = {id(zr_ref): cr_ref, id(zk_ref): ck_ref, id(zv_ref): cv_ref, id(zl_ref): cl_ref}
    else:
        (zr_ref, zk_ref, zv_ref, zl_ref,
         pr_ref, pk_ref, pv_ref, pl_ref, si_ref,
         mur_ref, muk_ref, muv_ref, mul_ref,
         w0_ref, a0_ref, kkp_ref, kap_ref, rkp_ref, gng_ref, gnb_ref,
         wd_ref, wa_ref, wg_ref,
         oa_ref, so_ref, sho_ref) = refs
        prev_of = {id(zr_ref): pr_ref, id(zk_ref): pk_ref, id(zv_ref): pv_ref, id(zl_ref): pl_ref}
    nblk = R // lb
    z_refs = (zr_ref, zk_ref, zv_ref, zl_ref)
    mu_refs = (mur_ref, muk_ref, muv_ref, mul_ref)

    if carried:
        @pl.when(pl.program_id(1) == 0)
        def _():
            s_ref[...] = jnp.zeros_like(s_ref)
            for ref in (cr_ref, ck_ref, cv_ref, cl_ref):
                ref[...] = jnp.zeros_like(ref)

    row1 = lax.broadcasted_iota(jnp.int32, (R, 1), 0)
    first = (row1 % lb) == 0
    ri = lax.broadcasted_iota(jnp.int32, (GRP, GRP), 0)
    ci = lax.broadcasted_iota(jnp.int32, (GRP, GRP), 1)
    bdm = (ri // HEAD) == (ci // HEAD)
    ones_bd = jnp.where(bdm, 1.0, 0.0).astype(BF16)
    tr = lax.broadcasted_iota(jnp.int32, (R, R), 0)
    tc = lax.broadcasted_iota(jnp.int32, (R, R), 1)
    same = (tr // lb) == (tc // lb)
    incl01 = jnp.where(same & (tc <= tr), 1.0, 0.0).astype(BF16)
    same01 = jnp.where(same, 1.0, 0.0).astype(BF16)
    wr = lax.broadcasted_iota(jnp.int32, (R, GRP), 0)
    wc = lax.broadcasted_iota(jnp.int32, (R, GRP), 1) % HEAD
    wsame = (wr // lb) == (wc // lb)
    strict_w = wsame & (wc < wr)
    incl_w = wsame & (wc <= wr)
    eye_w = jnp.where(wc == wr, 1.0, 0.0)
    sls = [slice(gi * GRP, (gi + 1) * GRP) for gi in range(N_GRP)]
    blks = [slice(b * lb, (b + 1) * lb) for b in range(nblk)]
    units = [(s, gi) for s in range(nseq) for gi in range(N_GRP)]
    uids = range(len(units))

    def seg_sums(xs):
        n = xs[0].shape[0]
        parts = []
        for x in xs:
            parts.extend(_split2(x))
        stacked = jnp.concatenate(parts, axis=0)
        cols = [_dot(stacked[:, sl], ones_bd) for sl in sls]
        full = jnp.concatenate(cols, axis=1)
        return [full[2 * i * n:(2 * i + 1) * n] + full[(2 * i + 1) * n:(2 * i + 2) * n]
                for i in range(len(xs))]

    def cum(m01, x):
        hi, mid, lo = _split3(x)
        return _dot(m01, hi) + _dot(m01, mid) + _dot(m01, lo)

    def prep(c, outs):
        rows = slice(c * R, (c + 1) * R)

        def shift(z_ref, mu_ref):
            parts = []
            for s in range(nseq):
                z = z_ref[s, rows, :]
                if not carried:
                    prev0 = _seq_rows(prev_of[id(z_ref)].at[s], c * nblk, nblk, lb)
                elif c == 0:
                    prev0 = carry[id(z_ref)][s]
                else:
                    prev0 = z_ref[s, c * R - 1:c * R, :]
                prev = jnp.where(first, prev0, pltpu.roll(z, 1, axis=0))
                parts.append(z + (prev - z) * mu_ref[...])
            return jnp.concatenate(parts, axis=0)

        per_stream = lambda x: [x[s * R:(s + 1) * R] for s in range(nseq)]
        r, k, v, l = [shift(z_ref, mu_ref) for z_ref, mu_ref in zip(z_refs, mu_refs)]
        w_raw = w0_ref[...] + _dot(jnp.tanh(l).astype(BF16), wd_ref[...])
        a_raw = a0_ref[...] + _dot(l.astype(BF16), wa_ref[...])
        g = _dot(jax.nn.sigmoid(l).astype(BF16), wg_ref[...])
        yield
        logw = jax.nn.sigmoid(w_raw) * (-math.exp(-0.5))
        a = jax.nn.sigmoid(a_raw)
        kk = k * kkp_ref[...]
        k2 = k * (1.0 + (a - 1.0) * kap_ref[...])
        ss, rk = seg_sums([kk * kk, r * k2 * rkp_ref[...]])
        cl_s = [cum(incl01, lw) for lw in per_stream(logw)]
        if nblk == 1:
            ce_s = [jnp.broadcast_to(x[R - 1:R, :], (R, RW)) for x in cl_s]
        else:
            ce_s = [cum(same01, lw) for lw in per_stream(logw)]
        yield
        cl = jnp.concatenate(cl_s, axis=0)
        cl_end = jnp.concatenate(ce_s, axis=0)
        kk = kk / jnp.maximum(jnp.sqrt(ss), 1e-12)
        av = -kk
        bv = kk * a
        e_neg = jnp.exp(-cl)
        e_end = jnp.exp(cl_end - cl)
        full = dict(v=v, g=g, bonus=rk * v,
                    at=av * jnp.exp(cl - logw), rt=r * jnp.exp(cl),
                    bt=bv * e_neg, kt=k2 * e_neg, bh=bv * e_end, kh=k2 * e_end)
        for name, x in full.items():
            for s, x_s in enumerate(per_stream(x)):
                outs[s][name] = x_s
        for s in range(nseq):
            outs[s]["p_end"] = jnp.exp(cl_s[s][R - 1:R, :] if nblk == 1 else ce_s[s])

    def main(c, ps):
        rows_c = slice(c * R, (c + 1) * R)
        col = lambda name, ui: ps[units[ui][0]][name][:, sls[units[ui][1]]]
        v_g = [col("v", ui) for ui in uids]
        lhs = [jnp.concatenate([col("at", ui), col("rt", ui)], axis=0).astype(BF16) for ui in uids]
        o1 = [_dot_nt(lhs[ui], jnp.concatenate([_bd(col("bt", ui), bdm), _bd(col("kt", ui), bdm)], axis=0))
              for ui in uids]
        yield
        w_ab = [jnp.where(strict_w, o[:R, :GRP], 0.0) for o in o1]
        w_ak = [jnp.where(strict_w, o[:R, GRP:], 0.0) for o in o1]
        w_rb = [jnp.where(incl_w, o[R:, :GRP], 0.0) for o in o1]
        w_rk = [jnp.where(incl_w, o[R:, GRP:], 0.0) for o in o1]

        def state_part(ui):
            s, gi = units[ui]
            if carried:
                o2 = _dot_nt(lhs[ui], s_ref[s, gi].astype(BF16))
                return o2[:R], o2[R:]
            at_u, rt_u = col("at", ui), col("rt", ui)
            a_rows, r_rows = [], []
            for b, rows in enumerate(blks):
                lhs_b = jnp.concatenate([at_u[rows], rt_u[rows]], axis=0).astype(BF16)
                o2 = _dot_nt(lhs_b, _bd_heads(si_ref[s, c * nblk + b, sls[gi], :], bdm))
                a_rows.append(o2[:lb])
                r_rows.append(o2[lb:])
            return jnp.concatenate(a_rows, axis=0), jnp.concatenate(r_rows, axis=0)

        x_w = list(w_ab)
        t_w = [eye_w + x for x in x_w]
        n_lvl = max(1, int(math.log2(lb)))
        a_s = r_s = o3 = None
        for j in range(n_lvl):
            for ui in uids:
                y_bd = _bd(x_w[ui], bdm)
                if j == 0:
                    x_w[ui] = _dot(x_w[ui].astype(BF16), y_bd)
                elif j == n_lvl - 1:
                    t_w[ui] = t_w[ui] + _dot(t_w[ui].astype(BF16), y_bd)
                else:
                    res = _dot(jnp.concatenate([t_w[ui], x_w[ui]], axis=0).astype(BF16), y_bd)
                    t_w[ui] = t_w[ui] + res[:R]
                    x_w[ui] = res[R:]
            yield
            if j == 0:
                parts = [state_part(ui) for ui in uids]
                a_s, r_s = [q[0] for q in parts], [q[1] for q in parts]
                yield
            elif j == 1:
                o3 = [_dot(jnp.concatenate([w_ak[ui], w_rk[ui]], axis=0).astype(BF16), _bd(v_g[ui], bdm))
                      for ui in uids]
                yield

        u = [_dot(t_w[ui].astype(BF16), _bd(a_s[ui] + o3[ui][:R], bdm)) for ui in uids]
        yield
        o_parts = [r_s[ui] + o3[ui][R:] + _dot(w_rb[ui].astype(BF16), _bd(u[ui], bdm)) for ui in uids]
        yield

        for ui in uids:
            s, gi = units[ui]
            sl = sls[gi]
            bh_u, kh_u, p_end = col("bh", ui), col("kh", ui), ps[s]["p_end"][:, sl]
            if carried:
                upd = _dot_tn(jnp.concatenate([u[ui], v_g[ui]], axis=0).astype(BF16),
                              jnp.concatenate([bh_u, kh_u], axis=0).astype(BF16))
                s_new = jnp.where(bdm, s_ref[s, gi] * p_end + upd, 0.0)
                s_ref[s, gi] = s_new
                if c == nch - 1:
                    so_ref[s, sl, :] = _diag_blocks(s_new)
            else:
                for b, rows in enumerate(blks):
                    upd = _dot_tn(jnp.concatenate([u[ui][rows], v_g[ui][rows]], axis=0).astype(BF16),
                                  jnp.concatenate([bh_u[rows], kh_u[rows]], axis=0).astype(BF16))
                    p_b = p_end[b * lb:b * lb + 1]
                    p_nat = jnp.concatenate(
                        [jnp.broadcast_to(p_b[:, h * HEAD:(h + 1) * HEAD], (HEAD, HEAD)) for h in range(HPG)],
                        axis=0)
                    so_ref[s, c * nblk + b, sl, :] = (si_ref[s, c * nblk + b, sl, :] * p_nat
                                                      + _diag_blocks(upd))
        yield

        o = [jnp.concatenate(o_parts[s * N_GRP:(s + 1) * N_GRP], axis=1) for s in range(nseq)]
        mu = seg_sums(o)
        yield
        dlt = [o[s] - mu[s] * (1.0 / HEAD) for s in range(nseq)]
        var = seg_sums([d * d for d in dlt])
        for s in range(nseq):
            on = dlt[s] * lax.rsqrt(var[s] * (1.0 / HEAD) + GN_EPS) * gng_ref[...] + gnb_ref[...]
            oa_ref[s, rows_c, :] = ((on + ps[s]["bonus"]) * ps[s]["g"]).astype(BF16)

    def run(gen):
        for _ in gen:
            pass

    cur = [{} for _ in range(nseq)]
    run(prep(0, cur))
    for c in range(nch):
        nxt = [{} for _ in range(nseq)]
        side = prep(c + 1, nxt) if c + 1 < nch else iter(())
        for tick, _ in enumerate(main(c, cur)):
            if tick % 3 == 1:
                next(side, None)
        run(side)
        cur = nxt

    n_rows = nch * R
    for s in range(nseq):
        if carried:
            last = lambda z_ref: z_ref[s, n_rows - 1:n_rows, :]
            for z_ref in z_refs:
                carry[id(z_ref)][s] = last(z_ref)
        else:
            last = lambda z_ref: jnp.concatenate(
                [z_ref[s, b * lb + lb - 1:b * lb + lb, :] for b in range(nch * nblk)], axis=0)
        sho_ref[s, :, 0:RW] = last(zr_ref)
        sho_ref[s, :, RW:2 * RW] = last(zk_ref)
        sho_ref[s, :, 2 * RW:3 * RW] = last(zv_ref)
        sho_ref[s, :, 3 * RW:] = last(zl_ref)


def _wkv(z, vecs, mats, n_seq, seq_len, lb, nch, nseq, prev_rows=None, state_in=None, to_bf16=()):
    R = nch * WKV_ROWS
    carried = prev_rows is None
    rows = z.shape[0]
    n_streams = n_seq if carried else nseq
    stream_rows = rows // n_streams
    assert n_streams % nseq == 0 and stream_rows % R == 0 and rows == n_streams * stream_rows
    z3 = z.reshape(n_streams, stream_rows, Z_W)
    if carried:
        grid = (n_streams // nseq, stream_rows // R)
        rmap = lambda c: (lambda b, t: (b, t, c))
        cmap = lambda b, t: (0, 0)
        smap = lambda b, t: (b, 0, 0)
        sem = ("arbitrary", "arbitrary")
    else:
        grid = (stream_rows // R,)
        rmap = lambda c: (lambda i: (0, i, c))
        cmap = lambda i: (0, 0)
        sem = ("arbitrary",)
        blk_per_step = R // lb
        seq_per_stream = n_seq // nseq

    in_specs = [pl.BlockSpec((nseq, R, RW), rmap(Z_R // RW)),
                pl.BlockSpec((nseq, R, RW), rmap(Z_K // RW)),
                pl.BlockSpec((nseq, R, RW), rmap(Z_VR // RW)),
                pl.BlockSpec((nseq, R, LORA), rmap(Z_L // LORA))]
    args = [z3, z3, z3, z3]
    if not carried:
        p3 = prev_rows.reshape(nseq, seq_per_stream, C_SHIFT)
        st4 = state_in.reshape(nseq, seq_per_stream, RW, HEAD)
        in_specs += [pl.BlockSpec((nseq, blk_per_step, RW), rmap(0)),
                     pl.BlockSpec((nseq, blk_per_step, RW), rmap(1)),
                     pl.BlockSpec((nseq, blk_per_step, RW), rmap(2)),
                     pl.BlockSpec((nseq, blk_per_step, LORA), rmap(3 * RW // LORA)),
                     pl.BlockSpec((nseq, blk_per_step, RW, HEAD), lambda i: (0, i, 0, 0))]
        args += [p3] * 4 + [st4]
    for a in vecs + mats:
        in_specs.append(pl.BlockSpec(a.shape, cmap))
        args.append(a)

    if carried:
        scratch = [pltpu.VMEM((nseq, N_GRP, GRP, GRP), F32), pltpu.VMEM((nseq, 1, RW), F32),
                   pltpu.VMEM((nseq, 1, RW), F32), pltpu.VMEM((nseq, 1, RW), F32),
                   pltpu.VMEM((nseq, 1, LORA), F32)]
        state_spec = pl.BlockSpec((nseq, RW, HEAD), smap)
        state_shape = (n_seq, RW, HEAD)
        shift_spec = pl.BlockSpec((nseq, 1, C_SHIFT), smap)
        shift_shape = (n_seq, 1, C_SHIFT)
    else:
        scratch = []
        state_spec = pl.BlockSpec((nseq, blk_per_step, RW, HEAD), lambda i: (0, i, 0, 0))
        state_shape = (nseq, seq_per_stream, RW, HEAD)
        shift_spec = pl.BlockSpec((nseq, blk_per_step, C_SHIFT), lambda i: (0, i, 0))
        shift_shape = (nseq, seq_per_stream, C_SHIFT)
    n_steps = math.prod(grid)
    step = (lambda b, t: b * grid[1] + t) if carried else (lambda i: i)
    cast_specs = [pl.BlockSpec((w.shape[0] // n_steps, w.shape[1]), lambda *g: (step(*g), 0)) for w in to_bf16]
    assert all(w.shape[0] % (16 * n_steps) == 0 for w in to_bf16)
    oa, st, sh, *casted = pl.pallas_call(
        functools.partial(_wkv_kernel, lb=lb, carried=carried, nch=nch, nseq=nseq, n_cast=len(to_bf16)),
        grid=grid,
        in_specs=in_specs + cast_specs,
        out_specs=[pl.BlockSpec((nseq, R, RW), rmap(0)), state_spec, shift_spec] + cast_specs,
        out_shape=[jax.ShapeDtypeStruct((n_streams, stream_rows, RW), BF16),
                   jax.ShapeDtypeStruct(state_shape, F32),
                   jax.ShapeDtypeStruct(shift_shape, F32)]
                  + [jax.ShapeDtypeStruct(w.shape, BF16) for w in to_bf16],
        scratch_shapes=scratch,
        compiler_params=_params(sem),
        name="wkv_carried" if carried else "wkv_blocks",
    )(*args, *to_bf16)
    return (oa.reshape(rows, RW), st.reshape(n_seq, RW, HEAD), sh.reshape(n_seq, C_SHIFT), *casted)


def _gmlp_kernel(*refs, seq, want_vn, n_cast):
    zu_ref, zv_ref, lng_ref, lnb_ref, ws_ref, bias_ref = refs[:6]
    cast_in = refs[6:6 + n_cast]
    ub_ref = refs[6 + n_cast]
    vn_ref = refs[7 + n_cast] if want_vn else None
    cast_out = refs[len(refs) - n_cast:]
    for src, dst in zip(cast_in, cast_out):
        dst[...] = src[...].astype(BF16)
    reps = CHUNK // seq
    tr = lax.broadcasted_iota(jnp.int32, (CHUNK, CHUNK), 0)
    tc = lax.broadcasted_iota(jnp.int32, (CHUNK, CHUNK), 1)
    causal = (tc <= tr) & ((tr // seq) == (tc // seq))
    ws_g = []
    for gi in range(MLP_GROUPS):
        w_rows = ws_ref[gi]
        w_full = jnp.concatenate([w_rows if s == 0 else pltpu.roll(w_rows, s * seq, axis=1)
                                  for s in range(reps)], axis=0)
        ws_g.append(jnp.where(causal, w_full, 0.0).astype(BF16))
    bias = jnp.concatenate([bias_ref[...]] * reps, axis=0)
    for c in range(zu_ref.shape[0] // CHUNK):
        rows = slice(c * CHUNK, (c + 1) * CHUNK)
        u = jax.nn.gelu(zu_ref[rows, :])
        vg = jax.nn.gelu(zv_ref[rows, :])
        mu = jnp.mean(vg, -1, keepdims=True)
        var = jnp.mean(jnp.square(vg - mu), -1, keepdims=True)
        vn = (vg - mu) * lax.rsqrt(var + LN_EPS) * lng_ref[...] + lnb_ref[...]
        if vn_ref is not None:
            vn_ref[rows, :] = vn
        vb = vn.astype(BF16)
        cols = [_dot(ws_g[gi], vb[:, gi * MLP_GD:(gi + 1) * MLP_GD]) for gi in range(MLP_GROUPS)]
        mixed = jnp.concatenate(cols, axis=1) + bias
        ub_ref[rows, :] = (u * mixed).astype(BF16)


def _gmlp(z, ln_g, ln_b, ws, bias, seq, want_vn, to_bf16=()):
    rows, tr = z.shape[0], 4 * CHUNK
    steps = rows // tr
    n_out = 2 if want_vn else 1
    cast_specs = [pl.BlockSpec((w.shape[0] // steps, w.shape[1]), lambda i: (i, 0)) for w in to_bf16]
    assert all(w.shape[0] % (16 * steps) == 0 for w in to_bf16)
    return pl.pallas_call(
        functools.partial(_gmlp_kernel, seq=seq, want_vn=want_vn, n_cast=len(to_bf16)),
        grid=(steps,),
        in_specs=[pl.BlockSpec((tr, MLP_W), lambda i: (i, Z_U // MLP_W)),
                  pl.BlockSpec((tr, MLP_W), lambda i: (i, Z_V // MLP_W)),
                  pl.BlockSpec((1, MLP_W), lambda i: (0, 0)),
                  pl.BlockSpec((1, MLP_W), lambda i: (0, 0)),
                  pl.BlockSpec((MLP_GROUPS, seq, CHUNK), lambda i: (0, 0, 0)),
                  pl.BlockSpec((seq, MLP_W), lambda i: (0, 0))] + cast_specs,
        out_specs=[pl.BlockSpec((tr, MLP_W), lambda i: (i, 0)),
                   pl.BlockSpec((tr, MLP_W), lambda i: (i, 0))][:n_out] + cast_specs,
        out_shape=[jax.ShapeDtypeStruct((rows, MLP_W), BF16),
                   jax.ShapeDtypeStruct((rows, MLP_W), F32)][:n_out]
                  + [jax.ShapeDtypeStruct(w.shape, BF16) for w in to_bf16],
        compiler_params=_params(("arbitrary",)),
        name="gmlp",
    )(z, z, ln_g, ln_b, ws, bias, *to_bf16)


def _merge_kernel(oa_ref, ub_ref, zga_ref, zgb_ref, x_ref, gm_ref, sc_ref, sh_ref, g_ref,
                  wa_ref, wb_ref, wo_ref, x1_ref, h2_ref):
    ya = _dot(oa_ref[...], wa_ref[...])
    yb = _dot(ub_ref[...], wb_ref[...])
    merged = jax.nn.sigmoid(zga_ref[...]) * ya + jax.nn.sigmoid(zgb_ref[...]) * yb
    mix = _dot(merged.astype(BF16), wo_ref[...])
    tm = x_ref.shape[0]
    x1 = x_ref[...] + _mod_rows(gm_ref, tm) * mix
    x1_ref[...] = x1
    h2_ref[...] = (_rms(x1, g_ref[...]) * (1.0 + _mod_rows(sc_ref, tm)) + _mod_rows(sh_ref, tm)).astype(BF16)


def _merge(oa, ub, z, x, mod, g_ffn, wa, wb, wo, rows_per_seq):
    m, tm = x.shape[0], 256
    const = lambda shape: pl.BlockSpec(shape, lambda i: (0, 0), pipeline_mode=pl.Buffered(1))
    mspec = lambda k: _mod_spec(k, tm, rows_per_seq)
    rows = lambda w, c: pl.BlockSpec((tm, w), lambda i: (i, c))
    return pl.pallas_call(
        _merge_kernel,
        grid=(m // tm,),
        in_specs=[rows(RW, 0), rows(MLP_W, 0),
                  rows(D_MODEL, Z_GA // D_MODEL), rows(D_MODEL, Z_GB // D_MODEL), rows(D_MODEL, 0),
                  mspec(MOD_GATE_M), mspec(MOD_SCALE_F), mspec(MOD_SHIFT_F),
                  pl.BlockSpec((1, D_MODEL), lambda i: (0, 0)),
                  const((RW, D_MODEL)), const((MLP_W, D_MODEL)), const((D_MODEL, D_MODEL))],
        out_specs=[rows(D_MODEL, 0), rows(D_MODEL, 0)],
        out_shape=[jax.ShapeDtypeStruct((m, D_MODEL), F32), jax.ShapeDtypeStruct((m, D_MODEL), BF16)],
        compiler_params=_params(("arbitrary",)),
        name="merge",
    )(oa, ub, z, z, x, mod, mod, mod, g_ffn, wa, wb, wo)


def _ffn_kernel(*refs, fused):
    if fused:
        h_ref, x_ref, gf_ref, gfin_ref, wg_ref, wu_ref, wo_ref, y_ref = refs
    else:
        h_ref, wg_ref, wu_ref, wo_ref, y_ref = refs
    j = pl.program_id(1)

    @pl.when(j == 0)
    def _():
        y_ref[...] = jnp.zeros_like(y_ref)

    hb = h_ref[...]
    gt = _dot(hb, wg_ref[...].astype(BF16))
    up = _dot(hb, wu_ref[...].astype(BF16))
    act = gt * jax.nn.sigmoid(gt) * up
    y_ref[...] += _dot(act.astype(BF16), wo_ref[...].astype(BF16))

    if fused:
        @pl.when(j == pl.num_programs(1) - 1)
        def _():
            x2 = x_ref[...] + gf_ref[...] * y_ref[...]
            y_ref[...] = _rms(x2, gfin_ref[...])


def _final_kernel(f_ref, x_ref, gf_ref, gfin_ref, y_ref):
    tm = x_ref.shape[0]

    def body(rows, i):
        x2 = x_ref[rows, :] + _mod_chunk(gf_ref, i, tm) * f_ref[rows, :]
        y_ref[rows, :] = _rms(x2, gfin_ref[...])

    _for_row_chunks(tm, body)


def _ffn(h2, x1, mod, g_final, w_in, w_out, rows_per_seq):
    m, tm, tf = x1.shape[0], 1024, (512 if w_in.dtype == BF16 else 256)
    nf = D_FF // tf
    fused = rows_per_seq >= tm
    rows = pl.BlockSpec((tm, D_MODEL), lambda i, j: (i, 0))
    vec = pl.BlockSpec((1, D_MODEL), lambda i, j: (0, 0))
    wspecs = [pl.BlockSpec((D_MODEL, tf), lambda i, j: (0, j)),
              pl.BlockSpec((D_MODEL, tf), lambda i, j: (0, nf + j)),
              pl.BlockSpec((tf, D_MODEL), lambda i, j: (j, 0))]
    if fused:
        in_specs = [rows, pl.BlockSpec((tm, D_MODEL), lambda i, j: (i, 0), pipeline_mode=pl.Buffered(1)),
                    _mod_spec(MOD_GATE_F, tm, rows_per_seq), vec] + wspecs
        args = (h2, x1, mod, g_final, w_in, w_in, w_out)
    else:
        in_specs = [rows] + wspecs
        args = (h2, w_in, w_in, w_out)
    y = pl.pallas_call(
        functools.partial(_ffn_kernel, fused=fused),
        grid=(m // tm, nf),
        in_specs=in_specs,
        out_specs=rows,
        out_shape=jax.ShapeDtypeStruct((m, D_MODEL), F32),
        compiler_params=_params(("arbitrary", "arbitrary"), FFN_VMEM_LIMIT),
        name="ffn",
    )(*args)
    if fused:
        return y
    te = 512
    erows = pl.BlockSpec((te, D_MODEL), lambda i: (i, 0))
    return pl.pallas_call(
        _final_kernel,
        grid=(m // te,),
        in_specs=[erows, erows, _mod_spec(MOD_GATE_F, te, rows_per_seq),
                  pl.BlockSpec((1, D_MODEL), lambda i: (0, 0))],
        out_specs=erows,
        out_shape=jax.ShapeDtypeStruct((m, D_MODEL), F32),
        compiler_params=_params(("arbitrary",)),
        name="final",
    )(y, x1, mod, g_final)


def kernel(x_prompt, x_sample, state_wkv, state_shift, c_prompt, c_sample, w_ada, b_ada, norm_mix_g, w_in, mu_shift, w0, w_decay_up, a0, w_aaa_up, w_gate_up, k_k, k_a, r_k, gn_g, gn_b, ln_v_g, ln_v_b, w_spatial, b_spatial, w_branch_a, w_branch_b, w_out, norm_ffn_g, w_ffn_in, w_ffn_out, norm_final_g):
    assert w_ada.shape[0] == 1, "single layer"
    bp, tp, _ = x_prompt.shape
    bs, ts, _ = x_sample.shape
    assert tp % CHUNK == 0 and WKV_ROWS % ts == 0 and CHUNK % ts == 0

    c_all = jnp.concatenate([c_sample, c_prompt], axis=0)
    c_all = jnp.pad(c_all, ((0, -c_all.shape[0] % 8), (0, 0)))
    mod = _ada(c_all, w_ada[0], b_ada[0])
    mod_s = mod
    mod_p = mod[bs:bs + bp].reshape(bp, 1, 6 * D_MODEL)


    row = lambda a: a.reshape(1, -1)
    mu = mu_shift[0]
    vecs = [row(mu[:RW]), row(mu[RW:2 * RW]), row(mu[2 * RW:3 * RW]), row(mu[3 * RW:]),
            row(w0[0]), row(a0[0]), row(k_k[0]), row(k_a[0]), row(r_k[0]), row(gn_g[0]), row(gn_b[0])]
    zpad = lambda w, lo: jnp.pad(w, ((lo, LORA - lo - w.shape[0]), (0, 0))).astype(BF16)
    mats = [zpad(w_decay_up[0], 0), zpad(w_aaa_up[0], DECAY_LORA),
            zpad(w_gate_up[0], DECAY_LORA + AAA_LORA)]

    bias = jnp.repeat(b_spatial[0].T, MLP_GD, axis=1)
    ws = w_spatial[0]

    xp = x_prompt.reshape(bp * tp, D_MODEL)
    xs = x_sample.reshape(bs * ts, D_MODEL)
    g_mix, g_ffn, g_fin = row(norm_mix_g[0]), row(norm_ffn_g[0]), row(norm_final_g)
    lng, lnb = row(ln_v_g[0]), row(ln_v_b[0])

    zp = _inproj_norm(xp, mod_p, g_mix, w_in[0], tp)
    (ub_p,) = _gmlp(zp, lng, lnb, ws, bias, CHUNK, False)
    f32_weights = (w_ffn_in[0], w_ffn_out[0], w_branch_a[0], w_branch_b[0], w_out[0], w_in[0])
    oa_p, st_p, shift_p, wfi, wfo, wa_b, wb_b, wo_b, w_in_b = _wkv(
        zp, vecs, mats, bp, tp, WKV_ROWS, WKV_NCH, WKV_NSEQ_PROMPT, to_bf16=f32_weights)
    x1_p, h2_p = _merge(oa_p, ub_p, zp, xp, mod_p, g_ffn, wa_b, wb_b, wo_b, tp)
    y_p = _ffn(h2_p, x1_p, mod_p, g_fin, wfi, wfo, tp)

    zs = _inproj(_hnorm(xs, mod_s, g_mix, ts), w_in_b)
    oa_s, st_s, shift_s = _wkv(zs, vecs, mats, bs, ts, ts, 1, WKV_NSEQ, prev_rows=state_shift[0],
                               state_in=state_wkv[0].reshape(bs, RW, HEAD))
    ub_s, vn_s = _gmlp(zs, lng, lnb, ws, bias, ts, True)
    x1_s, h2_s = _merge(oa_s, ub_s, zs, xs, mod_s, g_ffn, wa_b, wb_b, wo_b, ts)
    y_s = _ffn(h2_s, x1_s, mod_s, g_fin, wfi, wfo, ts)

    return (y_p.reshape(bp, tp, D_MODEL),
            y_s.reshape(bs, ts, D_MODEL),
            st_p.reshape(1, bp, N_HEADS, HEAD, HEAD),
            shift_p.reshape(1, bp, C_SHIFT),
            st_s.reshape(1, bs, N_HEADS, HEAD, HEAD),
            shift_s.reshape(1, bs, C_SHIFT),
            vn_s.reshape(bs, ts, MLP_W)[None])
```
